```python
import jax, jax.numpy as jnp
from jax import lax
import numpy as np

D_MODEL = 1024
BATCH = 8
SEQ = 8192
DEPTH = 1

HEAD_DIM = 64
CONV_WIDTH_CH = 512
N_HEADS = 8
ATTN_WIDTH = N_HEADS * HEAD_DIM
MIX_WIDTH = CONV_WIDTH_CH + ATTN_WIDTH
CONV_K = 31
DILATED_PATTERNS = ((128, 1), (512, 4), (2048, 16))
Q_BLOCK = 128
D_FF = 2816
FFN_CONV_K = 3
EPS = 1e-6

kernel_name = "hymba_conformer_dilated_alibi_convffn"


def rms_norm(x, g):
    xf = x.astype(jnp.float32)
    y = xf * lax.rsqrt(jnp.mean(xf * xf, axis=-1, keepdims=True) + EPS)
    return (y * g.astype(jnp.float32)).astype(x.dtype)


def layer_norm(x, g, b):
    xf = x.astype(jnp.float32)
    mu = jnp.mean(xf, axis=-1, keepdims=True)
    var = jnp.mean(jnp.square(xf - mu), axis=-1, keepdims=True)
    y = (xf - mu) * lax.rsqrt(var + EPS)
    return (y * g.astype(jnp.float32) + b.astype(jnp.float32)).astype(x.dtype)


def causal_depthwise_conv(x, w, b):
    K, C = w.shape
    y = lax.conv_general_dilated(
        x, w[:, None, :].astype(x.dtype), window_strides=(1,), padding=[(K - 1, 0)],
        dimension_numbers=("NWC", "WIO", "NWC"), feature_group_count=C)
    return y + b.astype(x.dtype)


def alibi_slopes(n_heads):
    return 2.0 ** (-8.0 * jnp.arange(1, n_heads + 1, dtype=jnp.float32) / n_heads)


def dilated_window_attention(q, k, v, slopes, window, dilation):
    B, S, H, E = q.shape
    d = dilation
    n_back = window // d
    L = S // d
    nb = -(-L // Q_BLOCK)
    Lp = nb * Q_BLOCK

    def to_streams(a):
        a = a.reshape(B, L, d, H, E)
        a = jnp.pad(a, ((0, 0), (0, Lp - L), (0, 0), (0, 0), (0, 0)))
        return a.reshape(B, nb, Q_BLOCK, d, H, E)

    def with_prev(a):
        prev = jnp.concatenate([jnp.zeros_like(a[:, :1]), a[:, :-1]], axis=1)
        return jnp.concatenate([prev, a], axis=2)

    qs = to_streams(q)
    kk = with_prev(to_streams(k))
    vv = with_prev(to_streams(v))

    s = jnp.einsum("bnqrhe,bnkrhe->bnrhqk", qs, kk, preferred_element_type=jnp.float32)
    qi = jnp.arange(Q_BLOCK)[:, None]
    ki = jnp.arange(2 * Q_BLOCK)[None, :]
    delta = qi + Q_BLOCK - ki
    band = (delta >= 0) & (delta <= n_back)
    key_pos = jnp.arange(nb)[:, None] * Q_BLOCK - Q_BLOCK + jnp.arange(2 * Q_BLOCK)[None, :]
    key_ok = key_pos >= 0
    mask = band[None, :, :] & key_ok[:, None, :]
    dist = (delta * d).astype(jnp.float32)
    bias = -slopes[:, None, None] * dist[None]
    s = s + bias[None, None, None]
    s = jnp.where(mask[None, :, None, None], s, -jnp.inf)
    lse = jax.nn.logsumexp(s, axis=-1, keepdims=True)
    p = jnp.exp(s - lse)
    o = jnp.einsum("bnrhqk,bnkrhe->bnqrhe", p.astype(vv.dtype), vv,
                   preferred_element_type=jnp.float32)
    o = o.reshape(B, Lp, d, H, E)[:, :L].reshape(B, S, H, E)
    lse = jnp.transpose(lse[..., 0], (0, 1, 4, 2, 3))
    lse = lse.reshape(B, Lp, d, H)[:, :L].reshape(B, S, H)
    return o, lse


def _fwd_setup_inputs(seed: int = 0) -> dict:
    key = jax.random.key(seed)
    ks = jax.random.split(key, 16)
    f32 = jnp.float32
    n_in = 2 * CONV_WIDTH_CH + 3 * ATTN_WIDTH
    nrm = lambda k, shape, fan: jax.random.normal(k, shape, f32) * (fan ** -0.5)
    gain = lambda k, n: 1.0 + 0.02 * jax.random.normal(k, (n,), f32)
    small = lambda k, n: 0.02 * jax.random.normal(k, (n,), f32)
    return {
        "x": jax.random.normal(ks[0], (BATCH, SEQ, D_MODEL), f32),
        "norm1_g": gain(ks[1], D_MODEL),
        "w_in": nrm(ks[2], (D_MODEL, n_in), D_MODEL),
        "conv_w": nrm(ks[3], (CONV_K, CONV_WIDTH_CH), CONV_K),
        "conv_b": small(ks[4], CONV_WIDTH_CH),
        "cn_g": gain(ks[5], CONV_WIDTH_CH),
        "cn_b": small(ks[6], CONV_WIDTH_CH),
        "q_norm_g": gain(ks[7], HEAD_DIM),
        "k_norm_g": gain(ks[8], HEAD_DIM),
        "w_out": nrm(ks[9], (MIX_WIDTH, D_MODEL), MIX_WIDTH),
        "norm2_g": gain(ks[10], D_MODEL),
        "w_up": nrm(ks[11], (D_MODEL, 2 * D_FF), D_MODEL),
        "ffconv_w": nrm(ks[12], (FFN_CONV_K, 2 * D_FF), FFN_CONV_K),
        "ffconv_b": small(ks[13], 2 * D_FF),
        "w_down": nrm(ks[14], (D_FF, D_MODEL), D_FF),
    }


def _fwd_reference(x, norm1_g, w_in, conv_w, conv_b, cn_g, cn_b, q_norm_g, k_norm_g, w_out,
              norm2_g, w_up, ffconv_w, ffconv_b, w_down):
    B, S, _ = x.shape
    slopes = alibi_slopes(N_HEADS)
    for _layer in range(DEPTH):
        h = rms_norm(x, norm1_g)
        proj = h @ w_in
        c = CONV_WIDTH_CH
        a_val, a_gate, q, k, v = jnp.split(
            proj, [c, 2 * c, 2 * c + ATTN_WIDTH, 2 * c + 2 * ATTN_WIDTH], axis=-1)

        u = a_val * jax.nn.sigmoid(a_gate)
        u = causal_depthwise_conv(u, conv_w, conv_b)
        u = jax.nn.silu(layer_norm(u, cn_g, cn_b))

        q = rms_norm(q.reshape(B, S, N_HEADS, HEAD_DIM), q_norm_g) * (HEAD_DIM ** -0.5)
        k = rms_norm(k.reshape(B, S, N_HEADS, HEAD_DIM), k_norm_g)
        v = v.reshape(B, S, N_HEADS, HEAD_DIM)
        outs, lses = [], []
        for window, dilation in DILATED_PATTERNS:
            o_i, lse_i = dilated_window_attention(q, k, v, slopes, window, dilation)
            outs.append(o_i)
            lses.append(lse_i)
        wts = jax.nn.softmax(jnp.stack(lses, axis=0), axis=0)
        o = jnp.sum(wts[..., None] * jnp.stack(outs, axis=0), axis=0)
        o = o.astype(x.dtype).reshape(B, S, ATTN_WIDTH)

        x = x + jnp.concatenate([u, o], axis=-1) @ w_out

        h2 = rms_norm(x, norm2_g)
        up = causal_depthwise_conv(h2 @ w_up, ffconv_w, ffconv_b)
        gate, val = jnp.split(up, 2, axis=-1)
        x = x + (jax.nn.silu(gate) * val) @ w_down
    return x


import jax as _jax
import jax.numpy as _jnp

TWIN_FORMAT = 'train_step'
FWD_PARAMS = ['x', 'norm1_g', 'w_in', 'conv_w', 'conv_b', 'cn_g', 'cn_b', 'q_norm_g', 'k_norm_g', 'w_out', 'norm2_g', 'w_up', 'ffconv_w', 'ffconv_b', 'w_down']
TWIN_WEIGHTS = ['norm1_g', 'w_in', 'conv_w', 'conv_b', 'cn_g', 'cn_b', 'q_norm_g', 'k_norm_g', 'w_out', 'norm2_g', 'w_up', 'ffconv_w', 'ffconv_b', 'w_down']
TWIN_DIFF_INPUT = 'x'
TWIN_INPUTS = ['x', 'norm1_g', 'w_in', 'conv_w', 'conv_b', 'cn_g', 'cn_b', 'q_norm_g', 'k_norm_g', 'w_out', 'norm2_g', 'w_up', 'ffconv_w', 'ffconv_b', 'w_down', 'loss_target', 'm_norm1_g', 'm_w_in', 'm_conv_w', 'm_conv_b', 'm_cn_g', 'm_cn_b', 'm_q_norm_g', 'm_k_norm_g', 'm_w_out', 'm_norm2_g', 'm_w_up', 'm_ffconv_w', 'm_ffconv_b', 'm_w_down', 'v_norm1_g', 'v_w_in', 'v_conv_w', 'v_conv_b', 'v_cn_g', 'v_cn_b', 'v_q_norm_g', 'v_k_norm_g', 'v_w_out', 'v_norm2_g', 'v_w_up', 'v_ffconv_w', 'v_ffconv_b', 'v_w_down']
TWIN_OUTPUTS = ['loss', 'grad_x', 'grad_norm1_g', 'grad_w_in', 'grad_conv_w', 'grad_conv_b', 'grad_cn_g', 'grad_cn_b', 'grad_q_norm_g', 'grad_k_norm_g', 'grad_w_out', 'grad_norm2_g', 'grad_w_up', 'grad_ffconv_w', 'grad_ffconv_b', 'grad_w_down', 'delta_norm1_g', 'delta_w_in', 'delta_conv_w', 'delta_conv_b', 'delta_cn_g', 'delta_cn_b', 'delta_q_norm_g', 'delta_k_norm_g', 'delta_w_out', 'delta_norm2_g', 'delta_w_up', 'delta_ffconv_w', 'delta_ffconv_b', 'delta_w_down', 'new_m_norm1_g', 'new_m_w_in', 'new_m_conv_w', 'new_m_conv_b', 'new_m_cn_g', 'new_m_cn_b', 'new_m_q_norm_g', 'new_m_k_norm_g', 'new_m_w_out', 'new_m_norm2_g', 'new_m_w_up', 'new_m_ffconv_w', 'new_m_ffconv_b', 'new_m_w_down', 'new_v_norm1_g', 'new_v_w_in', 'new_v_conv_w', 'new_v_conv_b', 'new_v_cn_g', 'new_v_cn_b', 'new_v_q_norm_g', 'new_v_k_norm_g', 'new_v_w_out', 'new_v_norm2_g', 'new_v_w_up', 'new_v_ffconv_w', 'new_v_ffconv_b', 'new_v_w_down']
TWIN_LEAF_KINDS = {'loss': 'loss', 'grad_x': 'grad_x', 'grad_norm1_g': 'grad_w', 'grad_w_in': 'grad_w', 'grad_conv_w': 'grad_w', 'grad_conv_b': 'grad_w', 'grad_cn_g': 'grad_w', 'grad_cn_b': 'grad_w', 'grad_q_norm_g': 'grad_w', 'grad_k_norm_g': 'grad_w', 'grad_w_out': 'grad_w', 'grad_norm2_g': 'grad_w', 'grad_w_up': 'grad_w', 'grad_ffconv_w': 'grad_w', 'grad_ffconv_b': 'grad_w', 'grad_w_down': 'grad_w', 'delta_norm1_g': 'delta_w', 'delta_w_in': 'delta_w', 'delta_conv_w': 'delta_w', 'delta_conv_b': 'delta_w', 'delta_cn_g': 'delta_w', 'delta_cn_b': 'delta_w', 'delta_q_norm_g': 'delta_w', 'delta_k_norm_g': 'delta_w', 'delta_w_out': 'delta_w', 'delta_norm2_g': 'delta_w', 'delta_w_up': 'delta_w', 'delta_ffconv_w': 'delta_w', 'delta_ffconv_b': 'delta_w', 'delta_w_down': 'delta_w', 'new_m_norm1_g': 'new_m', 'new_m_w_in': 'new_m', 'new_m_conv_w': 'new_m', 'new_m_conv_b': 'new_m', 'new_m_cn_g': 'new_m', 'new_m_cn_b': 'new_m', 'new_m_q_norm_g': 'new_m', 'new_m_k_norm_g': 'new_m', 'new_m_w_out': 'new_m', 'new_m_norm2_g': 'new_m', 'new_m_w_up': 'new_m', 'new_m_ffconv_w': 'new_m', 'new_m_ffconv_b': 'new_m', 'new_m_w_down': 'new_m', 'new_v_norm1_g': 'new_v', 'new_v_w_in': 'new_v', 'new_v_conv_w': 'new_v', 'new_v_conv_b': 'new_v', 'new_v_cn_g': 'new_v', 'new_v_cn_b': 'new_v', 'new_v_q_norm_g': 'new_v', 'new_v_k_norm_g': 'new_v', 'new_v_w_out': 'new_v', 'new_v_norm2_g': 'new_v', 'new_v_w_up': 'new_v', 'new_v_ffconv_w': 'new_v', 'new_v_ffconv_b': 'new_v', 'new_v_w_down': 'new_v'}


def _forward(args):
    return _fwd_reference(*[args[k] for k in FWD_PARAMS])


def _output_shape():
    out = _jax.eval_shape(lambda: _forward(_fwd_setup_inputs(0)))
    return out.shape, out.dtype

N_MICROBATCH = 1
ADAM_LR = 0.001
ADAM_B1 = 0.9
ADAM_B2 = 0.999
ADAM_EPS = 1e-08
ADAM_WD = 0.01
ADAM_STEP = 10
PER_EXAMPLE_BATCH_AXIS = {'x': 0, 'loss_target': 0}
SHARED_INPUTS = []
_WEIGHT_DTYPES = {'norm1_g': _jnp.float32, 'w_in': _jnp.float32, 'conv_w': _jnp.float32, 'conv_b': _jnp.float32, 'cn_g': _jnp.float32, 'cn_b': _jnp.float32, 'q_norm_g': _jnp.float32, 'k_norm_g': _jnp.float32, 'w_out': _jnp.float32, 'norm2_g': _jnp.float32, 'w_up': _jnp.float32, 'ffconv_w': _jnp.float32, 'ffconv_b': _jnp.float32, 'w_down': _jnp.float32}
MOMENT_SCALE = {'norm1_g': 3.089696e+00, 'w_in': 3.198511e-01, 'conv_w': 7.715784e-01, 'conv_b': 1.812585e+01, 'cn_g': 2.885011e+01, 'cn_b': 2.103534e+01, 'q_norm_g': 1.166689e+01, 'k_norm_g': 1.159118e+01, 'w_out': 2.572193e+00, 'norm2_g': 5.062024e+01, 'w_up': 8.551358e-01, 'ffconv_w': 7.223842e+00, 'ffconv_b': 6.489198e+00, 'w_down': 5.277681e-01}


def _to_microbatches(a, axis):
    t = _jnp.moveaxis(a, axis, 0)
    t = t.reshape((N_MICROBATCH, t.shape[0] // N_MICROBATCH) + t.shape[1:])
    return _jnp.moveaxis(t, 1, axis + 1)


def setup_inputs(seed: int = 0) -> dict:
    inp = _fwd_setup_inputs(seed)
    key = _jax.random.fold_in(_jax.random.key(seed), 7919)
    shape, _ = _output_shape()
    out = dict(inp)
    out["loss_target"] = _jax.random.normal(_jax.random.fold_in(key, 0), shape, _jnp.float32)
    for i, name in enumerate(TWIN_WEIGHTS):
        w = inp[name].astype(_jnp.float32)
        if MOMENT_SCALE is None:
            s = _jnp.sqrt(_jnp.mean(_jnp.square(w)) + 1e-30)
        else:
            s = MOMENT_SCALE[name]
        km, kv = _jax.random.split(_jax.random.fold_in(key, i + 1))
        out[name] = w
        out["m_" + name] = s * _jax.random.normal(km, w.shape, _jnp.float32)
        out["v_" + name] = (s * s) * _jax.random.uniform(kv, w.shape, _jnp.float32, 0.5, 1.5)
    if N_MICROBATCH > 1:
        for name, axis in PER_EXAMPLE_BATCH_AXIS.items():
            out[name] = _to_microbatches(out[name], axis)
    return {'x': out['x'], 'norm1_g': out['norm1_g'], 'w_in': out['w_in'], 'conv_w': out['conv_w'], 'conv_b': out['conv_b'], 'cn_g': out['cn_g'], 'cn_b': out['cn_b'], 'q_norm_g': out['q_norm_g'], 'k_norm_g': out['k_norm_g'], 'w_out': out['w_out'], 'norm2_g': out['norm2_g'], 'w_up': out['w_up'], 'ffconv_w': out['ffconv_w'], 'ffconv_b': out['ffconv_b'], 'w_down': out['w_down'], 'loss_target': out['loss_target'], 'm_norm1_g': out['m_norm1_g'], 'm_w_in': out['m_w_in'], 'm_conv_w': out['m_conv_w'], 'm_conv_b': out['m_conv_b'], 'm_cn_g': out['m_cn_g'], 'm_cn_b': out['m_cn_b'], 'm_q_norm_g': out['m_q_norm_g'], 'm_k_norm_g': out['m_k_norm_g'], 'm_w_out': out['m_w_out'], 'm_norm2_g': out['m_norm2_g'], 'm_w_up': out['m_w_up'], 'm_ffconv_w': out['m_ffconv_w'], 'm_ffconv_b': out['m_ffconv_b'], 'm_w_down': out['m_w_down'], 'v_norm1_g': out['v_norm1_g'], 'v_w_in': out['v_w_in'], 'v_conv_w': out['v_conv_w'], 'v_conv_b': out['v_conv_b'], 'v_cn_g': out['v_cn_g'], 'v_cn_b': out['v_cn_b'], 'v_q_norm_g': out['v_q_norm_g'], 'v_k_norm_g': out['v_k_norm_g'], 'v_w_out': out['v_w_out'], 'v_norm2_g': out['v_norm2_g'], 'v_w_up': out['v_w_up'], 'v_ffconv_w': out['v_ffconv_w'], 'v_ffconv_b': out['v_ffconv_b'], 'v_w_down': out['v_w_down']}


def _loss(weights, diff, rest, loss_target):
    with _jax.named_scope("forward"):
        args = {**rest, TWIN_DIFF_INPUT: diff, **{k: w.astype(_WEIGHT_DTYPES[k]) for k, w in weights.items()}}
        y = _forward(args)
    with _jax.named_scope("loss_head"):
        err = _jnp.square(y.astype(_jnp.float32) - loss_target)
        return 0.5 * _jnp.sum(_jnp.mean(err, axis=-1)) if err.ndim else 0.5 * err


def _adamw(w, g, m, v):
    m = ADAM_B1 * m + (1.0 - ADAM_B1) * g
    v = ADAM_B2 * v + (1.0 - ADAM_B2) * _jnp.square(g)
    m_hat = m / (1.0 - ADAM_B1 ** ADAM_STEP)
    v_hat = v / (1.0 - ADAM_B2 ** ADAM_STEP)
    delta = -ADAM_LR * (m_hat / (_jnp.sqrt(v_hat) + ADAM_EPS) + ADAM_WD * w)
    return delta, m, v


def reference(x, norm1_g, w_in, conv_w, conv_b, cn_g, cn_b, q_norm_g, k_norm_g, w_out, norm2_g, w_up, ffconv_w, ffconv_b, w_down, loss_target, m_norm1_g, m_w_in, m_conv_w, m_conv_b, m_cn_g, m_cn_b, m_q_norm_g, m_k_norm_g, m_w_out, m_norm2_g, m_w_up, m_ffconv_w, m_ffconv_b, m_w_down, v_norm1_g, v_w_in, v_conv_w, v_conv_b, v_cn_g, v_cn_b, v_q_norm_g, v_k_norm_g, v_w_out, v_norm2_g, v_w_up, v_ffconv_w, v_ffconv_b, v_w_down):
    given = dict(x=x, norm1_g=norm1_g, w_in=w_in, conv_w=conv_w, conv_b=conv_b, cn_g=cn_g, cn_b=cn_b, q_norm_g=q_norm_g, k_norm_g=k_norm_g, w_out=w_out, norm2_g=norm2_g, w_up=w_up, ffconv_w=ffconv_w, ffconv_b=ffconv_b, w_down=w_down, loss_target=loss_target, m_norm1_g=m_norm1_g, m_w_in=m_w_in, m_conv_w=m_conv_w, m_conv_b=m_conv_b, m_cn_g=m_cn_g, m_cn_b=m_cn_b, m_q_norm_g=m_q_norm_g, m_k_norm_g=m_k_norm_g, m_w_out=m_w_out, m_norm2_g=m_norm2_g, m_w_up=m_w_up, m_ffconv_w=m_ffconv_w, m_ffconv_b=m_ffconv_b, m_w_down=m_w_down, v_norm1_g=v_norm1_g, v_w_in=v_w_in, v_conv_w=v_conv_w, v_conv_b=v_conv_b, v_cn_g=v_cn_g, v_cn_b=v_cn_b, v_q_norm_g=v_q_norm_g, v_k_norm_g=v_k_norm_g, v_w_out=v_w_out, v_norm2_g=v_norm2_g, v_w_up=v_w_up, v_ffconv_w=v_ffconv_w, v_ffconv_b=v_ffconv_b, v_w_down=v_w_down)
    weights = {n: given[n] for n in TWIN_WEIGHTS}
    shared = {n: given[n] for n in SHARED_INPUTS}
    per_example = {n: given[n] for n in ['x']}
    grad_fn = _jax.value_and_grad(_loss, argnums=(0, 1))

    def one_microbatch(ex, loss_target):
        ex = dict(ex)
        diff = ex.pop(TWIN_DIFF_INPUT)
        return grad_fn(weights, diff, {**shared, **ex}, loss_target)

    if N_MICROBATCH == 1:
        loss, (grad_w, grad_x) = one_microbatch(per_example, given["loss_target"])
    else:
        def body(carry, xs):
            loss_sum, grad_sum = carry
            l_k, (gw_k, gx_k) = one_microbatch(xs[0], xs[1])
            with _jax.named_scope("update"):
                return (loss_sum + l_k, _jax.tree.map(_jnp.add, grad_sum, gw_k)), gx_k

        init = (_jnp.zeros((), _jnp.float32), _jax.tree.map(_jnp.zeros_like, weights))
        (loss, grad_w), grad_x = _jax.lax.scan(body, init, (per_example, given["loss_target"]))
    with _jax.named_scope("update"):
        delta_w, new_m, new_v = {}, {}, {}
        for n in TWIN_WEIGHTS:
            delta_w[n], new_m[n], new_v[n] = _adamw(weights[n], grad_w[n], given["m_" + n], given["v_" + n])
    return (loss, grad_x, *[grad_w[n] for n in TWIN_WEIGHTS], *[delta_w[n] for n in TWIN_WEIGHTS],
            *[new_m[n] for n in TWIN_WEIGHTS], *[new_v[n] for n in TWIN_WEIGHTS])
```

```python
import functools

import jax
import jax.numpy as jnp
from jax import lax
from jax.experimental import pallas as pl
from jax.experimental.pallas import tpu as pltpu

T = 8192
D = 1024
C = 512
NPROJ = 2560
DFF = 2816
NUP = 2 * DFF
CONV_K = 31
FF_K = 3
HEAD = 64
EPS = 1e-6
NEG = -1e30
N_CHIPS = 4
N_DEV = 8
PATTERN_DILATIONS = (1, 4, 16)
QB = 128

ADAM_LR = 0.001
ADAM_B1 = 0.9
ADAM_B2 = 0.999
ADAM_EPS = 1e-08
ADAM_WD = 0.01
ADAM_STEP = 10

F32 = jnp.float32
BF16 = jnp.bfloat16
MESH = pl.DeviceIdType.MESH
ANY = pl.BlockSpec(memory_space=pl.ANY)

VPACK_ROWS = 48
SPACK_ROWS = 24


def _params(sem, vmem_mb):
    return pltpu.CompilerParams(dimension_semantics=sem, vmem_limit_bytes=vmem_mb << 20)


def _nt(a, b):
    return lax.dot_general(a, b, (((1,), (1,)), ((), ())), preferred_element_type=F32)


def _tn_dot(a, b):
    return lax.dot_general(a, b, (((0,), (0,)), ((), ())), preferred_element_type=F32)


def _sigmoid(x):
    return 1.0 / (1.0 + jnp.exp(-x))


def _segsum(x, bd):
    hi = x.astype(BF16)
    lo = (x - hi.astype(F32)).astype(BF16)
    return (jnp.dot(hi, bd, preferred_element_type=F32)
            + jnp.dot(lo, bd, preferred_element_type=F32))


def _place():
    x, y, c = lax.axis_index("x"), lax.axis_index("y"), lax.axis_index("c")
    chips = [(1 - x, y), (x, 1 - y), (1 - x, 1 - y)]
    return x, y, c, chips


_MAT_SHAPES = ((D, NPROJ), (D, D), (D, NUP), (DFF, D))
_MAT_ROW_SHARDED = (False, True, False, True)


def _shard_shape(k):
    r, cdim = _MAT_SHAPES[k]
    return (r // N_CHIPS, cdim) if _MAT_ROW_SHARDED[k] else (r, cdim // N_CHIPS)


def _shard_of(ref, k, s):
    return _block_of(ref, _shard_shape(k), _MAT_ROW_SHARDED[k], s)


def _block_of(ref, shard_shape, row_sharded, s):
    r, cdim = shard_shape
    if row_sharded:
        return ref.at[pl.ds(s * r, r), :]
    return ref.at[:, pl.ds(s * cdim, cdim)]


def _gather_weights(shards, row_sharded):
    n = len(shards)
    shapes = [a.shape for a in shards]
    full = [(r * N_CHIPS, cd) if rs else (r, cd * N_CHIPS) for (r, cd), rs in zip(shapes, row_sharded)]

    def body(*refs):
        srcs, outs = refs[:n], refs[n:2 * n]
        send_sems, recv_sems, local_sems = refs[2 * n:]
        x, y, c, chips = _place()
        me = 2 * x + y
        place = lambda k, s: _block_of(outs[k], shapes[k], row_sharded[k], s)
        local = [pltpu.make_async_copy(srcs[k], place(k, me), local_sems.at[k]) for k in range(n)]
        for cp in local:
            cp.start()
        sends = []
        for k in range(n):
            for j, (px, py) in enumerate(chips):
                sends.append(pltpu.make_async_remote_copy(
                    src_ref=srcs[k], dst_ref=place(k, me),
                    send_sem=send_sems.at[3 * k + j], recv_sem=recv_sems.at[3 * k + j],
                    device_id=(px, py, c), device_id_type=MESH))
        for cp in sends:
            cp.start()
        for k in range(n):
            for j, (px, py) in enumerate(chips):
                pltpu.make_async_remote_copy(
                    src_ref=srcs[k], dst_ref=place(k, 2 * px + py),
                    send_sem=send_sems.at[3 * k + j], recv_sem=recv_sems.at[3 * k + j],
                    device_id=(px, py, c), device_id_type=MESH).wait_recv()
        for cp in sends:
            cp.wait_send()
        for cp in local:
            cp.wait()

    return pl.pallas_call(
        body, name="gather_weights",
        out_shape=[jax.ShapeDtypeStruct(f, a.dtype) for f, a in zip(full, shards)],
        in_specs=[ANY] * n, out_specs=[ANY] * n,
        scratch_shapes=[pltpu.SemaphoreType.DMA((3 * n,)), pltpu.SemaphoreType.DMA((3 * n,)),
                        pltpu.SemaphoreType.DMA((n,))],
    )(*shards)


def _grad_exchange(gbf, gf32, vpack):
    def body(b0, b1, b2, b3, f0, f1, f2, f3, v_ref, r0, r1, r2, r3, w0, w1, w2, w3, vr_ref,
             send_sems, recv_sems, local_sems, vsend_sems, vrecv_sems):
        gb, gf = (b0, b1, b2, b3), (f0, f1, f2, f3)
        rec, own = (r0, r1, r2, r3), (w0, w1, w2, w3)
        x, y, c, chips = _place()
        me_chip = 2 * x + y
        me = 4 * x + 2 * y + c
        local = [pltpu.make_async_copy(_shard_of(gf[k], k, me_chip), own[k], local_sems.at[k]) for k in range(4)]
        local.append(pltpu.make_async_copy(v_ref, vr_ref.at[me], local_sems.at[4]))
        for cp in local:
            cp.start()
        sends = []
        for k in range(4):
            for j, (px, py) in enumerate(chips):
                sends.append(pltpu.make_async_remote_copy(
                    src_ref=_shard_of(gb[k], k, 2 * px + py), dst_ref=rec[k].at[j],
                    send_sem=send_sems.at[3 * k + j], recv_sem=recv_sems.at[3 * k + j],
                    device_id=(px, py, c), device_id_type=MESH))
        flips = [(fx, fy, fc) for fx in (0, 1) for fy in (0, 1) for fc in (0, 1)][1:]
        for r, (fx, fy, fc) in enumerate(flips):
            sends.append(pltpu.make_async_remote_copy(
                src_ref=v_ref, dst_ref=vr_ref.at[me],
                send_sem=vsend_sems.at[r], recv_sem=vrecv_sems.at[r],
                device_id=(x ^ fx, y ^ fy, c ^ fc), device_id_type=MESH))
        for cp in sends:
            cp.start()
        for k in range(4):
            for j, (px, py) in enumerate(chips):
                pltpu.make_async_remote_copy(
                    src_ref=_shard_of(gb[k], k, me_chip), dst_ref=rec[k].at[j],
                    send_sem=send_sems.at[3 * k + j], recv_sem=recv_sems.at[3 * k + j],
                    device_id=(px, py, c), device_id_type=MESH).wait_recv()
        for r, (fx, fy, fc) in enumerate(flips):
            peer = 4 * (x ^ fx) + 2 * (y ^ fy) + (c ^ fc)
            pltpu.make_async_remote_copy(
                src_ref=v_ref, dst_ref=vr_ref.at[peer],
                send_sem=vsend_sems.at[r], recv_sem=vrecv_sems.at[r],
                device_id=(x ^ fx, y ^ fy, c ^ fc), device_id_type=MESH).wait_recv()
        for cp in sends:
            cp.wait_send()
        for cp in local:
            cp.wait()

    out_shape = ([jax.ShapeDtypeStruct((3,) + _shard_shape(k), BF16) for k in range(4)]
                 + [jax.ShapeDtypeStruct(_shard_shape(k), F32) for k in range(4)]
                 + [jax.ShapeDtypeStruct((N_DEV, VPACK_ROWS, D), F32)])
    res = pl.pallas_call(
        body, name="grad_exchange", out_shape=out_shape,
        in_specs=[ANY] * 9, out_specs=[ANY] * 9,
        scratch_shapes=[pltpu.SemaphoreType.DMA((12,)), pltpu.SemaphoreType.DMA((12,)),
                        pltpu.SemaphoreType.DMA((5,)), pltpu.SemaphoreType.DMA((7,)),
                        pltpu.SemaphoreType.DMA((7,))],
    )(*gbf, *gf32, vpack)
    return res[0:4], res[4:8], res[8]


def _sibling_exchange(parts):
    def body(p0, p1, p2, p3, o0, o1, o2, o3, send_sems, recv_sems):
        x, y, c, _ = _place()
        copies = [pltpu.make_async_remote_copy(
            src_ref=p, dst_ref=o, send_sem=send_sems.at[k], recv_sem=recv_sems.at[k],
            device_id=(x, y, 1 - c), device_id_type=MESH)
            for k, (p, o) in enumerate(zip((p0, p1, p2, p3), (o0, o1, o2, o3)))]
        for cp in copies:
            cp.start()
        for cp in copies:
            cp.wait()

    return pl.pallas_call(
        body, name="sibling_exchange",
        out_shape=[jax.ShapeDtypeStruct(_shard_shape(k), F32) for k in range(4)],
        in_specs=[ANY] * 4, out_specs=[ANY] * 4,
        scratch_shapes=[pltpu.SemaphoreType.DMA((4,)), pltpu.SemaphoreType.DMA((4,))],
    )(*parts)


def _proj_fwd(x, g1, w_in):
    tm, tn = 512, 640

    def body(x_ref, g_ref, w_ref, h_ref, p_ref):
        @pl.when(pl.program_id(1) == 0)
        def _():
            xv = x_ref[...]
            r = lax.rsqrt(jnp.mean(xv * xv, axis=-1, keepdims=True) + EPS)
            h_ref[...] = (xv * r * g_ref[...]).astype(BF16)
        p_ref[...] = jnp.dot(h_ref[...], w_ref[...], preferred_element_type=F32)

    return pl.pallas_call(
        body, name="proj_fwd", grid=(T // tm, NPROJ // tn),
        in_specs=[pl.BlockSpec((tm, D), lambda i, j: (i, 0)), pl.BlockSpec((1, D), lambda i, j: (0, 0)),
                  pl.BlockSpec((D, tn), lambda i, j: (0, j))],
        out_specs=[pl.BlockSpec((tm, D), lambda i, j: (i, 0)), pl.BlockSpec((tm, tn), lambda i, j: (i, j))],
        out_shape=[jax.ShapeDtypeStruct((T, D), BF16), jax.ShapeDtypeStruct((T, NPROJ), F32)],
        compiler_params=_params(("parallel", "arbitrary"), 40),
    )(x, g1, w_in)


CONV_TM = 512
CONV_HALO = 32
CONV_RB = 64


def _conv_fwd(proj, conv_w, conv_b, cn_g, cn_b):
    tm, hl, rb = CONV_TM, CONV_HALO, CONV_RB
    per = tm // hl

    def body(av_ref, ag_ref, hv_ref, hg_ref, w_ref, b_ref, g_ref, bb_ref, cat_ref, cv_ref, ext_ref):
        i = pl.program_id(0)
        glu_h = hv_ref[...] * _sigmoid(hg_ref[...])
        ext_ref[0:hl, :] = jnp.where(i > 0, glu_h, 0.0)
        ext_ref[hl:, :] = av_ref[...] * _sigmoid(ag_ref[...])
        for r0 in range(0, tm, rb):
            acc = jnp.zeros((rb, C), F32) + b_ref[...]
            for k in range(CONV_K):
                off = r0 + hl - (CONV_K - 1) + k
                acc = acc + w_ref[k:k + 1, :] * ext_ref[off:off + rb, :]
            mu = jnp.mean(acc, axis=-1, keepdims=True)
            xc = acc - mu
            var = jnp.mean(xc * xc, axis=-1, keepdims=True)
            ln = xc * lax.rsqrt(var + EPS) * g_ref[...] + bb_ref[...]
            cv_ref[r0:r0 + rb, :] = acc
            cat_ref[r0:r0 + rb, :] = (ln * _sigmoid(ln)).astype(BF16)

    halo = lambda col: pl.BlockSpec((hl, C), lambda i: (jnp.maximum(i * per - 1, 0), col))
    vec = pl.BlockSpec((1, C), lambda i: (0, 0))
    return pl.pallas_call(
        body, name="conv_fwd", grid=(T // tm,),
        in_specs=[pl.BlockSpec((tm, C), lambda i: (i, 0)), pl.BlockSpec((tm, C), lambda i: (i, 1)),
                  halo(0), halo(1), pl.BlockSpec((CONV_K, C), lambda i: (0, 0)), vec, vec, vec],
        out_specs=[pl.BlockSpec((tm, C), lambda i: (i, 0)), pl.BlockSpec((tm, C), lambda i: (i, 0))],
        out_shape=[jax.ShapeDtypeStruct((T, D), BF16), jax.ShapeDtypeStruct((T, C), F32)],
        scratch_shapes=[pltpu.VMEM((tm + hl, C), F32)],
        compiler_params=_params(("arbitrary",), 32),
    )(proj, proj, proj, proj, conv_w, conv_b, cn_g, cn_b)


def _qkv_prep(proj, qg, kg, bd):
    tm = 512

    def body(q_ref, k_ref, v_ref, qg_ref, kg_ref, bd_ref, qn_ref, kn_ref, vb_ref):
        for src, g, dst in ((q_ref, qg_ref, qn_ref), (k_ref, kg_ref, kn_ref)):
            xv = src[...]
            ms = _segsum(xv * xv, bd_ref[...]) * (1.0 / HEAD)
            dst[...] = (xv * lax.rsqrt(ms + EPS) * g[...]).astype(BF16)
        vb_ref[...] = v_ref[...].astype(BF16)

    col = lambda c: pl.BlockSpec((tm, C), lambda i: (i, c))
    vec = pl.BlockSpec((1, C), lambda i: (0, 0))
    out = pl.BlockSpec((tm, C), lambda i: (i, 0))
    return pl.pallas_call(
        body, name="qkv_prep", grid=(T // tm,),
        in_specs=[col(2), col(3), col(4), vec, vec, pl.BlockSpec((C, C), lambda i: (0, 0))],
        out_specs=[out, out, out],
        out_shape=[jax.ShapeDtypeStruct((T, C), BF16)] * 3,
        compiler_params=_params(("parallel",), 32),
    )(proj, proj, proj, qg, kg, bd)


def _stack_heads(a):
    lane = lax.broadcasted_iota(jnp.int32, a.shape, 1)
    zero = jnp.zeros_like(a)
    return jnp.concatenate([jnp.where(lane < HEAD, a, zero), jnp.where(lane >= HEAD, a, zero)], axis=0)


def _unstack_heads(a2):
    lane = lax.broadcasted_iota(jnp.int32, (QB, 2 * HEAD), 1)
    return jnp.where(lane < HEAD, a2[:QB], a2[QB:])


def _stack_cols(a):
    return jnp.concatenate([a[:, 0:1], a[:, HEAD:HEAD + 1]], axis=0)


def _attn_geometry(d):
    length = T // d
    width = d * C
    ch = min(length, 512)
    return length, width, ch, length // ch, ch // QB, width // (2 * HEAD)


def _alibi_tables(d):
    qi = jnp.arange(QB)[:, None]
    kj = jnp.arange(2 * QB)[None, :]
    delta = qi + QB - kj
    band = (delta >= 0) & (delta <= QB)
    dist = (delta * d).astype(F32)
    heads = jnp.arange(8, dtype=F32)
    slopes = 2.0 ** (-(heads + 1.0))
    t = jnp.where(band[None], -slopes[:, None, None] * dist[None], NEG)
    return t.reshape(4, 2 * QB, 2 * QB)


def _attn_fwd(qn, kn, vb, bias, d):
    length, width, ch, nch, nb, ncb = _attn_geometry(d)

    def body(q_ref, k_ref, v_ref, kh_ref, vh_ref, bias_ref, o_ref, l_ref):
        n = pl.program_id(1)
        col = lax.broadcasted_iota(jnp.int32, (2 * QB, 2 * QB), 1)
        for b in range(nb):
            qs = _stack_heads(q_ref[b * QB:(b + 1) * QB, :])
            if b == 0:
                kc = jnp.concatenate([kh_ref[...], k_ref[0:QB, :]], axis=0)
                vc = jnp.concatenate([vh_ref[...], v_ref[0:QB, :]], axis=0)
            else:
                kc = k_ref[(b - 1) * QB:(b + 1) * QB, :]
                vc = v_ref[(b - 1) * QB:(b + 1) * QB, :]
            s = _nt(qs, kc) + bias_ref[...]
            if b == 0:
                s = jnp.where((col < QB) & (n == 0), NEG, s)
            m = jnp.max(s, axis=-1, keepdims=True)
            p = jnp.exp(s - m)
            den = jnp.sum(p, axis=-1, keepdims=True)
            pv = jnp.dot(p.astype(BF16), vc, preferred_element_type=F32)
            o_ref[b * QB:(b + 1) * QB, :] = _unstack_heads(pv / den)
            lse = jnp.broadcast_to(m + jnp.log(den), (2 * QB, 2 * HEAD))
            l_ref[b * QB:(b + 1) * QB, :] = _unstack_heads(lse)

    main = pl.BlockSpec((ch, 2 * HEAD), lambda cb, n: (n, cb))
    halo = pl.BlockSpec((QB, 2 * HEAD), lambda cb, n: (jnp.maximum(n * nb - 1, 0), cb))
    q2, k2, v2 = (a.reshape(length, width) for a in (qn, kn, vb))
    o, l = pl.pallas_call(
        body, name=f"attn_fwd_d{d}", grid=(ncb, nch),
        in_specs=[main, main, main, halo, halo,
                  pl.BlockSpec((None, 2 * QB, 2 * QB), lambda cb, n: (cb % 4, 0, 0))],
        out_specs=[main, main],
        out_shape=[jax.ShapeDtypeStruct((length, width), F32)] * 2,
        compiler_params=_params(("parallel", "arbitrary"), 32),
    )(q2, k2, v2, k2, v2, bias)
    return o.reshape(T, C), l.reshape(T, C)


def _attn_merge(outs, lses, cat):
    tm = 512

    def body(o0, o1, o2, l0, l1, l2, cat_in, cat_ref, of_ref, lg_ref):
        del cat_in
        a, b, c = l0[...], l1[...], l2[...]
        m = jnp.maximum(jnp.maximum(a, b), c)
        e0, e1, e2 = jnp.exp(a - m), jnp.exp(b - m), jnp.exp(c - m)
        den = e0 + e1 + e2
        o = (e0 * o0[...] + e1 * o1[...] + e2 * o2[...]) / den
        of_ref[...] = o
        cat_ref[...] = o.astype(BF16)
        lg_ref[...] = m + jnp.log(den)

    blk = pl.BlockSpec((tm, C), lambda i: (i, 0))
    return pl.pallas_call(
        body, name="attn_merge", grid=(T // tm,),
        in_specs=[blk] * 6 + [ANY],
        out_specs=[pl.BlockSpec((tm, C), lambda i: (i, 1)), blk, blk],
        out_shape=[jax.ShapeDtypeStruct((T, D), BF16), jax.ShapeDtypeStruct((T, C), F32),
                   jax.ShapeDtypeStruct((T, C), F32)],
        input_output_aliases={6: 0},
        compiler_params=_params(("parallel",), 32),
    )(*outs, *lses, cat)


def _out_up(x, cat, w_out, g2, w_up):
    tm, tn = 512, NUP // 4

    def body(x_ref, cat_ref, wo_ref, g_ref, wu_ref, x1_ref, h2_ref, up_ref):
        @pl.when(pl.program_id(1) == 0)
        def _():
            x1 = x_ref[...] + jnp.dot(cat_ref[...], wo_ref[...], preferred_element_type=F32)
            x1_ref[...] = x1
            r = lax.rsqrt(jnp.mean(x1 * x1, axis=-1, keepdims=True) + EPS)
            h2_ref[...] = (x1 * r * g_ref[...]).astype(BF16)
        up_ref[...] = jnp.dot(h2_ref[...], wu_ref[...], preferred_element_type=F32)

    row = pl.BlockSpec((tm, D), lambda i, j: (i, 0))
    return pl.pallas_call(
        body, name="out_up", grid=(T // tm, 4),
        in_specs=[row, row, pl.BlockSpec((D, D), lambda i, j: (0, 0)), pl.BlockSpec((1, D), lambda i, j: (0, 0)),
                  pl.BlockSpec((D, tn), lambda i, j: (0, j))],
        out_specs=[row, row, pl.BlockSpec((tm, tn), lambda i, j: (i, j))],
        out_shape=[jax.ShapeDtypeStruct((T, D), F32), jax.ShapeDtypeStruct((T, D), BF16),
                   jax.ShapeDtypeStruct((T, NUP), F32)],
        compiler_params=_params(("parallel", "arbitrary"), 48),
    )(x, cat, w_out, g2, w_up)


FF_TM = 256
FF_HALO = 8
FF_CW = 256


def _ff_conv(ext_ref, fw_ref, fb_ref, col0, tm):
    cols = slice(col0, col0 + FF_CW)
    acc = fb_ref[:, cols] + fw_ref[0:1, cols] * ext_ref[FF_HALO - 2:FF_HALO - 2 + tm, cols]
    acc = acc + fw_ref[1:2, cols] * ext_ref[FF_HALO - 1:FF_HALO - 1 + tm, cols]
    return acc + fw_ref[2:3, cols] * ext_ref[FF_HALO:FF_HALO + tm, cols]


def _ffn_down(up, ffconv_w, ffconv_b, w_down, x1, target):
    tm, hl = FF_TM, FF_HALO
    per = tm // hl

    def body(up_ref, uh_ref, fw_ref, fb_ref, wd_ref, x1_ref, tg_ref, act_ref, dy_ref, loss_ref, ext_ref):
        i = pl.program_id(0)
        ext_ref[0:hl, :] = jnp.where(i > 0, uh_ref[...], 0.0)
        ext_ref[hl:, :] = up_ref[...]
        for c in range(DFF // FF_CW):
            gate = _ff_conv(ext_ref, fw_ref, fb_ref, c * FF_CW, tm)
            val = _ff_conv(ext_ref, fw_ref, fb_ref, DFF + c * FF_CW, tm)
            act_ref[:, c * FF_CW:(c + 1) * FF_CW] = (gate * _sigmoid(gate) * val).astype(BF16)
        y = x1_ref[...] + jnp.dot(act_ref[...], wd_ref[...], preferred_element_type=F32)
        err = y - tg_ref[...]
        dy_ref[...] = err * (1.0 / D)

        @pl.when(i == 0)
        def _():
            loss_ref[...] = jnp.zeros_like(loss_ref)
        loss_ref[...] += jnp.sum(err * err)

    row = pl.BlockSpec((tm, D), lambda i: (i, 0))
    return pl.pallas_call(
        body, name="ffn_down", grid=(T // tm,),
        in_specs=[pl.BlockSpec((tm, NUP), lambda i: (i, 0)),
                  pl.BlockSpec((hl, NUP), lambda i: (jnp.maximum(i * per - 1, 0), 0)),
                  pl.BlockSpec((FF_K, NUP), lambda i: (0, 0)), pl.BlockSpec((1, NUP), lambda i: (0, 0)),
                  pl.BlockSpec((DFF, D), lambda i: (0, 0)), row, row],
        out_specs=[pl.BlockSpec((tm, DFF), lambda i: (i, 0)), row, pl.BlockSpec((8, 128), lambda i: (0, 0))],
        out_shape=[jax.ShapeDtypeStruct((T, DFF), BF16), jax.ShapeDtypeStruct((T, D), F32),
                   jax.ShapeDtypeStruct((8, 128), F32)],
        scratch_shapes=[pltpu.VMEM((tm + hl, NUP), F32)],
        compiler_params=_params(("arbitrary",), 56),
    )(up, up, ffconv_w, ffconv_b, w_down, x1, target)


def _down_bwd(dy, w_down, up, ffconv_w, ffconv_b):
    tm, hl = FF_TM, FF_HALO
    per = tm // hl
    nt = T // tm

    def body(dy_ref, wd_ref, up_ref, uh_ref, fw_ref, fb_ref, dup_ref, gff_ref, ext_ref, dext_ref, dact_ref):
        i = pl.program_id(0)
        ti = nt - 1 - i

        @pl.when(i == 0)
        def _():
            gff_ref[...] = jnp.zeros_like(gff_ref)
            dext_ref[tm:tm + hl, :] = jnp.zeros((hl, NUP), F32)

        ext_ref[0:hl, :] = jnp.where(ti > 0, uh_ref[...], 0.0)
        ext_ref[hl:, :] = up_ref[...]
        dact_ref[...] = _nt(dy_ref[...].astype(BF16), wd_ref[...])
        for c in range(DFF // FF_CW):
            gate = _ff_conv(ext_ref, fw_ref, fb_ref, c * FF_CW, tm)
            val = _ff_conv(ext_ref, fw_ref, fb_ref, DFF + c * FF_CW, tm)
            sg = _sigmoid(gate)
            da = dact_ref[:, c * FF_CW:(c + 1) * FF_CW]
            dext_ref[0:tm, c * FF_CW:(c + 1) * FF_CW] = da * val * (sg + gate * sg * (1.0 - sg))
            dext_ref[0:tm, DFF + c * FF_CW:DFF + (c + 1) * FF_CW] = da * gate * sg
        for c in range(NUP // FF_CW):
            cols = slice(c * FF_CW, (c + 1) * FF_CW)
            d0 = dext_ref[0:tm, cols]
            d1 = dext_ref[1:tm + 1, cols]
            d2 = dext_ref[2:tm + 2, cols]
            dup_ref[:, cols] = (fw_ref[2:3, cols] * d0 + fw_ref[1:2, cols] * d1 + fw_ref[0:1, cols] * d2).astype(BF16)
            for k in range(FF_K):
                shifted = ext_ref[hl - 2 + k:hl - 2 + k + tm, cols]
                gff_ref[k:k + 1, cols] += jnp.sum(d0 * shifted, axis=0, keepdims=True)
            gff_ref[3:4, cols] += jnp.sum(d0, axis=0, keepdims=True)
        dext_ref[tm:tm + hl, :] = dext_ref[0:hl, :]

    rev = lambda i: (nt - 1 - i, 0)
    return pl.pallas_call(
        body, name="down_bwd", grid=(nt,),
        in_specs=[pl.BlockSpec((tm, D), rev), pl.BlockSpec((DFF, D), lambda i: (0, 0)),
                  pl.BlockSpec((tm, NUP), rev),
                  pl.BlockSpec((hl, NUP), lambda i: (jnp.maximum((nt - 1 - i) * per - 1, 0), 0)),
                  pl.BlockSpec((FF_K, NUP), lambda i: (0, 0)), pl.BlockSpec((1, NUP), lambda i: (0, 0))],
        out_specs=[pl.BlockSpec((tm, NUP), rev), pl.BlockSpec((8, NUP), lambda i: (0, 0))],
        out_shape=[jax.ShapeDtypeStruct((T, NUP), BF16), jax.ShapeDtypeStruct((8, NUP), F32)],
        scratch_shapes=[pltpu.VMEM((tm + hl, NUP), F32), pltpu.VMEM((tm + hl, NUP), F32),
                        pltpu.VMEM((tm, DFF), F32)],
        compiler_params=_params(("arbitrary",), 58),
    )(dy, w_down, up, up, ffconv_w, ffconv_b)


def _weight_grad(a, g, bm, bn, name):
    m, n = a.shape[1], g.shape[1]
    tk = 512

    def body(a_ref, g_ref, of_ref, ob_ref):
        k = pl.program_id(2)

        @pl.when(k == 0)
        def _():
            of_ref[...] = jnp.zeros_like(of_ref)
        of_ref[...] += _tn_dot(a_ref[...].astype(BF16), g_ref[...].astype(BF16))

        @pl.when(k == pl.num_programs(2) - 1)
        def _():
            ob_ref[...] = of_ref[...].astype(BF16)

    out = pl.BlockSpec((bm, bn), lambda i, j, k: (i, j))
    return pl.pallas_call(
        body, name=name, grid=(m // bm, n // bn, T // tk),
        in_specs=[pl.BlockSpec((tk, bm), lambda i, j, k: (k, i)), pl.BlockSpec((tk, bn), lambda i, j, k: (k, j))],
        out_specs=[out, out],
        out_shape=[jax.ShapeDtypeStruct((m, n), F32), jax.ShapeDtypeStruct((m, n), BF16)],
        compiler_params=_params(("parallel", "parallel", "arbitrary"), 56),
    )(a, g)


def _norm_bwd_mm(dz, w, xin, base, gain, kc, name):
    tm = 512
    nk = dz.shape[1] // kc

    def body(dz_ref, w_ref, x_ref, b_ref, g_ref, dx_ref, gg_ref, acc_ref):
        i, j = pl.program_id(0), pl.program_id(1)

        @pl.when(j == 0)
        def _():
            acc_ref[...] = jnp.zeros_like(acc_ref)
        acc_ref[...] += _nt(dz_ref[...], w_ref[...])

        @pl.when((i == 0) & (j == 0))
        def _():
            gg_ref[...] = jnp.zeros_like(gg_ref)

        @pl.when(j == nk - 1)
        def _():
            xv = x_ref[...]
            dh = acc_ref[...]
            r = lax.rsqrt(jnp.mean(xv * xv, axis=-1, keepdims=True) + EPS)
            t = dh * g_ref[...]
            dx_ref[...] = b_ref[...] + r * t - xv * (r * r * r) * jnp.mean(t * xv, axis=-1, keepdims=True)
            gg_ref[...] += jnp.sum(dh * xv * r, axis=0, keepdims=True)

    row = pl.BlockSpec((tm, D), lambda i, j: (i, 0))
    vec = pl.BlockSpec((1, D), lambda i, j: (0, 0))
    return pl.pallas_call(
        body, name=name, grid=(T // tm, nk),
        in_specs=[pl.BlockSpec((tm, kc), lambda i, j: (i, j)), pl.BlockSpec((D, kc), lambda i, j: (0, j)),
                  row, row, vec],
        out_specs=[row, vec],
        out_shape=[jax.ShapeDtypeStruct((T, D), F32), jax.ShapeDtypeStruct((1, D), F32)],
        scratch_shapes=[pltpu.VMEM((tm, D), F32)],
        compiler_params=_params(("arbitrary", "arbitrary"), 48),
    )(dz, w, xin, base, gain)


def _outproj_bwd(dx1, w_out):
    tm = 512

    def body(d_ref, w_ref, o_ref):
        o_ref[...] = _nt(d_ref[...].astype(BF16), w_ref[...])

    row = pl.BlockSpec((tm, D), lambda i: (i, 0))
    return pl.pallas_call(
        body, name="outproj_bwd", grid=(T // tm,),
        in_specs=[row, pl.BlockSpec((D, D), lambda i: (0, 0))], out_specs=row,
        out_shape=jax.ShapeDtypeStruct((T, D), F32),
        compiler_params=_params(("parallel",), 32),
    )(dx1, w_out)


def _conv_bwd(dcat, cv, proj, conv_w, cn_g, cn_b):
    tm, hl, rb = CONV_TM, CONV_HALO, CONV_RB
    per = tm // hl
    nt = T // tm

    def body(du_ref, dun_ref, cv_ref, cvn_ref, av_ref, ag_ref, hv_ref, hg_ref, w_ref, g_ref, bb_ref,
             dp_ref, gv_ref, gw_ref, dext_ref, gext_ref):
        i = pl.program_id(0)

        @pl.when(i == 0)
        def _():
            gv_ref[...] = jnp.zeros_like(gv_ref)
            gw_ref[...] = jnp.zeros_like(gw_ref)

        def ln_bwd(du, cvv):
            mu = jnp.mean(cvv, axis=-1, keepdims=True)
            xc = cvv - mu
            rs = lax.rsqrt(jnp.mean(xc * xc, axis=-1, keepdims=True) + EPS)
            xh = xc * rs
            ln = xh * g_ref[...] + bb_ref[...]
            sg = _sigmoid(ln)
            dln = du * (sg + ln * sg * (1.0 - sg))
            dxh = dln * g_ref[...]
            dcv = rs * (dxh - jnp.mean(dxh, axis=-1, keepdims=True)
                        - xh * jnp.mean(dxh * xh, axis=-1, keepdims=True))
            return dcv, dln, xh

        for r0 in range(0, tm, rb):
            dcv, dln, xh = ln_bwd(du_ref[r0:r0 + rb, :], cv_ref[r0:r0 + rb, :])
            dext_ref[r0:r0 + rb, :] = dcv
            gv_ref[0:1, :] += jnp.sum(dln * xh, axis=0, keepdims=True)
            gv_ref[1:2, :] += jnp.sum(dln, axis=0, keepdims=True)
            gv_ref[2:3, :] += jnp.sum(dcv, axis=0, keepdims=True)
        dcv_n, _, _ = ln_bwd(dun_ref[...], cvn_ref[...])
        dext_ref[tm:tm + hl, :] = jnp.where(i < nt - 1, dcv_n, 0.0)
        glu_h = hv_ref[...] * _sigmoid(hg_ref[...])
        gext_ref[0:hl, :] = jnp.where(i > 0, glu_h, 0.0)
        gext_ref[hl:, :] = av_ref[...] * _sigmoid(ag_ref[...])

        for r0 in range(0, tm, rb):
            dglu = jnp.zeros((rb, C), F32)
            for k in range(CONV_K):
                off = r0 + (CONV_K - 1) - k
                dglu = dglu + w_ref[k:k + 1, :] * dext_ref[off:off + rb, :]
            av = av_ref[r0:r0 + rb, :]
            sg = _sigmoid(ag_ref[r0:r0 + rb, :])
            dp_ref[r0:r0 + rb, 0:C] = (dglu * sg).astype(BF16)
            dp_ref[r0:r0 + rb, C:2 * C] = (dglu * av * sg * (1.0 - sg)).astype(BF16)
            dcv = dext_ref[r0:r0 + rb, :]
            for k in range(CONV_K):
                off = r0 + hl - (CONV_K - 1) + k
                gw_ref[k:k + 1, :] += jnp.sum(dcv * gext_ref[off:off + rb, :], axis=0, keepdims=True)

    main = lambda col: pl.BlockSpec((tm, C), lambda i: (i, col))
    prev = lambda col: pl.BlockSpec((hl, C), lambda i: (jnp.maximum(i * per - 1, 0), col))
    nxt = pl.BlockSpec((hl, C), lambda i: (jnp.minimum((i + 1) * per, T // hl - 1), 0))
    vec = pl.BlockSpec((1, C), lambda i: (0, 0))
    return pl.pallas_call(
        body, name="conv_bwd", grid=(nt,),
        in_specs=[main(0), nxt, main(0), nxt, main(0), main(1), prev(0), prev(1),
                  pl.BlockSpec((CONV_K, C), lambda i: (0, 0)), vec, vec],
        out_specs=[pl.BlockSpec((tm, 2 * C), lambda i: (i, 0)), pl.BlockSpec((8, C), lambda i: (0, 0)),
                   pl.BlockSpec((32, C), lambda i: (0, 0))],
        out_shape=[jax.ShapeDtypeStruct((T, NPROJ), BF16), jax.ShapeDtypeStruct((8, C), F32),
                   jax.ShapeDtypeStruct((32, C), F32)],
        scratch_shapes=[pltpu.VMEM((tm + hl, C), F32), pltpu.VMEM((tm + hl, C), F32)],
        compiler_params=_params(("arbitrary",), 40),
    )(dcat, dcat, cv, cv, proj, proj, proj, proj, conv_w, cn_g, cn_b)


def _attn_bwd_prep(dcat, o_f32, bd):
    tm = 512

    def body(do_ref, o_ref, bd_ref, dob_ref, dl_ref):
        do = do_ref[...]
        dob_ref[...] = do.astype(BF16)
        dl_ref[...] = _segsum(do * o_ref[...], bd_ref[...])

    blk = pl.BlockSpec((tm, C), lambda i: (i, 0))
    return pl.pallas_call(
        body, name="attn_bwd_prep", grid=(T // tm,),
        in_specs=[pl.BlockSpec((tm, C), lambda i: (i, 1)), blk, pl.BlockSpec((C, C), lambda i: (0, 0))],
        out_specs=[blk, blk],
        out_shape=[jax.ShapeDtypeStruct((T, C), BF16), jax.ShapeDtypeStruct((T, C), F32)],
        compiler_params=_params(("parallel",), 32),
    )(dcat, o_f32, bd)


def _attn_bwd(qn, kn, vb, dob, lg, dl, bias, d):
    length, width, ch, nch, nb, ncb = _attn_geometry(d)

    def body(q_ref, k_ref, v_ref, do_ref, lg_ref, dl_ref, kh_ref, vh_ref, qx_ref, dox_ref, lgx_ref, dlx_ref,
             bias_ref, dq_ref, dk_ref, dv_ref):
        n = pl.program_id(1)
        dk_ref[...] = jnp.zeros_like(dk_ref)
        dv_ref[...] = jnp.zeros_like(dv_ref)

        def unit(qb, dob_, lgb, dlb, kc, vc, biasv, invalid_prev):
            qs, dos = _stack_heads(qb), _stack_heads(dob_)
            s = _nt(qs, kc) + biasv
            if invalid_prev is not None:
                col = lax.broadcasted_iota(jnp.int32, s.shape, 1)
                s = jnp.where((col < QB) & invalid_prev, NEG, s)
            p = jnp.exp(s - _stack_cols(lgb))
            ds = p * (_nt(dos, vc) - _stack_cols(dlb))
            dsb = ds.astype(BF16)
            dq = _unstack_heads(jnp.dot(dsb, kc, preferred_element_type=F32))
            return dq, _tn_dot(dsb, qs), _tn_dot(p.astype(BF16), dos)

        for b in range(nb):
            rows = slice(b * QB, (b + 1) * QB)
            if b == 0:
                kc = jnp.concatenate([kh_ref[...], k_ref[0:QB, :]], axis=0)
                vc = jnp.concatenate([vh_ref[...], v_ref[0:QB, :]], axis=0)
            else:
                kc = k_ref[(b - 1) * QB:(b + 1) * QB, :]
                vc = v_ref[(b - 1) * QB:(b + 1) * QB, :]
            dq, dkc, dvc = unit(q_ref[rows, :], do_ref[rows, :], lg_ref[rows, :], dl_ref[rows, :], kc, vc,
                                bias_ref[...], (n == 0) if b == 0 else None)
            dq_ref[rows, :] = dq
            if b == 0:
                dk_ref[0:QB, :] += dkc[QB:]
                dv_ref[0:QB, :] += dvc[QB:]
            else:
                dk_ref[(b - 1) * QB:(b + 1) * QB, :] += dkc
                dv_ref[(b - 1) * QB:(b + 1) * QB, :] += dvc

        @pl.when(n < nch - 1)
        def _():
            last = slice((nb - 1) * QB, nb * QB)
            _, dkc, dvc = unit(qx_ref[...], dox_ref[...], lgx_ref[...], dlx_ref[...], k_ref[last, :], v_ref[last, :],
                               bias_ref[:, 0:QB], None)
            dk_ref[last, :] += dkc
            dv_ref[last, :] += dvc

    main = pl.BlockSpec((ch, 2 * HEAD), lambda cb, n: (n, cb))
    prev = pl.BlockSpec((QB, 2 * HEAD), lambda cb, n: (jnp.maximum(n * nb - 1, 0), cb))
    nxt = pl.BlockSpec((QB, 2 * HEAD), lambda cb, n: (jnp.minimum((n + 1) * nb, length // QB - 1), cb))
    q2, k2, v2, do2, lg2, dl2 = (a.reshape(length, width) for a in (qn, kn, vb, dob, lg, dl))
    res = pl.pallas_call(
        body, name=f"attn_bwd_d{d}", grid=(ncb, nch),
        in_specs=[main] * 6 + [prev, prev, nxt, nxt, nxt, nxt,
                               pl.BlockSpec((None, 2 * QB, 2 * QB), lambda cb, n: (cb % 4, 0, 0))],
        out_specs=[main] * 3,
        out_shape=[jax.ShapeDtypeStruct((length, width), F32)] * 3,
        compiler_params=_params(("parallel", "arbitrary"), 32),
    )(q2, k2, v2, do2, lg2, dl2, k2, v2, q2, do2, lg2, dl2, bias)
    return [a.reshape(T, C) for a in res]


def _qk_norm_bwd(d3, proj, col, gain, bd, dproj, name):
    tm = 512

    def body(d0, d1, d2, x_ref, g_ref, bd_ref, dp_in, dp_ref, gg_ref):
        del dp_in

        @pl.when(pl.program_id(0) == 0)
        def _():
            gg_ref[...] = jnp.zeros_like(gg_ref)
        dn = d0[...] + d1[...] + d2[...]
        xv = x_ref[...]
        r = lax.rsqrt(_segsum(xv * xv, bd_ref[...]) * (1.0 / HEAD) + EPS)
        t = dn * g_ref[...]
        mean_tx = _segsum(t * xv, bd_ref[...]) * (1.0 / HEAD)
        dp_ref[...] = (r * t - xv * (r * r * r) * mean_tx).astype(BF16)
        gg_ref[...] += jnp.sum(dn * xv * r, axis=0, keepdims=True)

    blk = pl.BlockSpec((tm, C), lambda i: (i, 0))
    vec = pl.BlockSpec((1, C), lambda i: (0, 0))
    return pl.pallas_call(
        body, name=name, grid=(T // tm,),
        in_specs=[blk, blk, blk, pl.BlockSpec((tm, C), lambda i: (i, col)), vec,
                  pl.BlockSpec((C, C), lambda i: (0, 0)), ANY],
        out_specs=[pl.BlockSpec((tm, C), lambda i: (i, col)), vec],
        out_shape=[jax.ShapeDtypeStruct((T, NPROJ), BF16), jax.ShapeDtypeStruct((1, C), F32)],
        input_output_aliases={6: 0},
        compiler_params=_params(("arbitrary",), 32),
    )(*d3, proj, gain, bd, dproj)


def _v_bwd(d3, dproj):
    tm = 512

    def body(d0, d1, d2, dp_in, dp_ref):
        del dp_in
        dp_ref[...] = (d0[...] + d1[...] + d2[...]).astype(BF16)

    blk = pl.BlockSpec((tm, C), lambda i: (i, 0))
    return pl.pallas_call(
        body, name="v_bwd", grid=(T // tm,),
        in_specs=[blk, blk, blk, ANY],
        out_specs=pl.BlockSpec((tm, C), lambda i: (i, 4)),
        out_shape=jax.ShapeDtypeStruct((T, NPROJ), BF16),
        input_output_aliases={3: 0},
        compiler_params=_params(("parallel",), 32),
    )(*d3, dproj)


def _adamw(w, g, m, v):
    m = ADAM_B1 * m + (1.0 - ADAM_B1) * g
    v = ADAM_B2 * v + (1.0 - ADAM_B2) * (g * g)
    m_hat = m / (1.0 - ADAM_B1 ** ADAM_STEP)
    v_hat = v / (1.0 - ADAM_B2 ** ADAM_STEP)
    delta = -ADAM_LR * (m_hat / (jnp.sqrt(v_hat) + ADAM_EPS) + ADAM_WD * w)
    return delta, m, v


def _row_block(shape):
    rows = shape[0]
    for cand in (256, 128, 64, 88, 32, 8):
        if rows % cand == 0 and cand * shape[1] * 4 <= (2 << 20):
            return cand
    return 8


def _partial_sum(own, recv, name):
    br = _row_block(own.shape)
    cols = own.shape[1]

    def body(o_ref, r_ref, p_ref):
        p_ref[...] = ((o_ref[...] + r_ref[0].astype(F32)) + r_ref[1].astype(F32)) + r_ref[2].astype(F32)

    blk = pl.BlockSpec((br, cols), lambda i: (i, 0))
    return pl.pallas_call(
        body, name=name, grid=(own.shape[0] // br,),
        in_specs=[blk, pl.BlockSpec((3, br, cols), lambda i: (0, i, 0))], out_specs=blk,
        out_shape=jax.ShapeDtypeStruct(own.shape, F32),
        compiler_params=_params(("parallel",), 32),
    )(own, recv)


def _adamw_mat(p_own, p_sib, w, m, v, name):
    br = _row_block(w.shape)
    cols = w.shape[1]

    def body(a_ref, b_ref, w_ref, m_ref, v_ref, g_ref, d_ref, nm_ref, nv_ref):
        g = a_ref[...] + b_ref[...]
        delta, nm, nv = _adamw(w_ref[...], g, m_ref[...], v_ref[...])
        g_ref[...] = g
        d_ref[...] = delta
        nm_ref[...] = nm
        nv_ref[...] = nv

    blk = pl.BlockSpec((br, cols), lambda i: (i, 0))
    return pl.pallas_call(
        body, name=name, grid=(w.shape[0] // br,),
        in_specs=[blk] * 5, out_specs=[blk] * 4,
        out_shape=[jax.ShapeDtypeStruct(w.shape, F32)] * 4,
        compiler_params=_params(("parallel",), 40),
    )(p_own, p_sib, w, m, v)


def _vec_reduce(vrecv):
    def body(v_ref, o_ref):
        acc = v_ref[0]
        for r in range(1, N_DEV):
            acc = acc + v_ref[r]
        o_ref[...] = acc

    return pl.pallas_call(
        body, name="vec_reduce",
        out_shape=jax.ShapeDtypeStruct((VPACK_ROWS, D), F32),
        compiler_params=_params((), 32),
    )(vrecv)


def _adamw_small(w, g, m, v):
    def body(w_ref, g_ref, m_ref, v_ref, d_ref, nm_ref, nv_ref):
        delta, nm, nv = _adamw(w_ref[...], g_ref[...], m_ref[...], v_ref[...])
        d_ref[...] = delta
        nm_ref[...] = nm
        nv_ref[...] = nv

    return pl.pallas_call(
        body, name="adamw_small",
        out_shape=[jax.ShapeDtypeStruct(w.shape, F32)] * 3,
        compiler_params=_params((), 32),
    )(w, g, m, v)


def _pack(parts, rows):
    flat = jnp.concatenate([p.reshape(-1) for p in parts])
    return jnp.pad(flat, (0, rows * D - flat.shape[0])).reshape(rows, D)


def _unpack(packed, shapes):
    flat = packed.reshape(-1)
    out, off = [], 0
    for shp in shapes:
        size = 1
        for s in shp:
            size *= s
        out.append(flat[off:off + size].reshape(shp))
        off += size
    return out


def _local_step(x, target, norm1_g, conv_w, conv_b, cn_g, cn_b, q_norm_g, k_norm_g, norm2_g, ffconv_w, ffconv_b,
                w_in, w_out, w_up, w_down):
    row = lambda a: a.reshape(1, -1)
    head_of = jnp.arange(C) // HEAD
    bd = (head_of[:, None] == head_of[None, :]).astype(BF16)
    qg = row(jnp.tile(q_norm_g, C // HEAD) * (HEAD ** -0.5))
    kg = row(jnp.tile(k_norm_g, C // HEAD))
    biases = [_alibi_tables(d) for d in PATTERN_DILATIONS]

    h, proj = _proj_fwd(x, row(norm1_g), w_in)
    cat, cv = _conv_fwd(proj, conv_w, row(conv_b), row(cn_g), row(cn_b))
    qn, kn, vb = _qkv_prep(proj, qg, kg, bd)
    fwd = [_attn_fwd(qn, kn, vb, biases[i], d) for i, d in enumerate(PATTERN_DILATIONS)]
    cat, o_f32, lg = _attn_merge([f[0] for f in fwd], [f[1] for f in fwd], cat)
    x1, h2, up = _out_up(x, cat, w_out, row(norm2_g), w_up)
    act, dy, loss_acc = _ffn_down(up, ffconv_w, row(ffconv_b), w_down, x1, target)

    dup, gff = _down_bwd(dy, w_down, up, ffconv_w, row(ffconv_b))
    gw_down = _weight_grad(act, dy, DFF // 2, D, "grad_w_down")
    dx1, g_norm2 = _norm_bwd_mm(dup, w_up, x1, dy, row(norm2_g), NUP // 4, "up_bwd")
    gw_up = _weight_grad(h2, dup, D, NUP // 4, "grad_w_up")
    dcat = _outproj_bwd(dx1, w_out)
    gw_out = _weight_grad(cat, dx1, D, D, "grad_w_out")
    dproj, gconv_vec, gconv_w = _conv_bwd(dcat, cv, proj, conv_w, row(cn_g), row(cn_b))
    dob, dl = _attn_bwd_prep(dcat, o_f32, bd)
    bwd = [_attn_bwd(qn, kn, vb, dob, lg, dl, biases[i], d) for i, d in enumerate(PATTERN_DILATIONS)]
    dproj, gq_lane = _qk_norm_bwd([b[0] for b in bwd], proj, 2, qg, bd, dproj, "q_norm_bwd")
    dproj, gk_lane = _qk_norm_bwd([b[1] for b in bwd], proj, 3, kg, bd, dproj, "k_norm_bwd")
    dproj = _v_bwd([b[2] for b in bwd], dproj)
    dx, g_norm1 = _norm_bwd_mm(dproj, w_in, x, dx1, row(norm1_g), NPROJ // 2, "in_bwd")
    gw_in = _weight_grad(h, dproj, D, NPROJ // 4, "grad_w_in")

    loss = loss_acc[0, 0] * (0.5 / D)
    g_qg = jnp.sum(gq_lane.reshape(C // HEAD, HEAD), axis=0) * (HEAD ** -0.5)
    g_kg = jnp.sum(gk_lane.reshape(C // HEAD, HEAD), axis=0)
    small = [g_norm1[0], gconv_vec[2], gconv_vec[0], gconv_vec[1], g_qg, g_kg, g_norm2[0], gff[3],
             gconv_w[:CONV_K], gff[:FF_K]]
    mats = [gw_in, gw_out, gw_up, gw_down]
    return loss, dx, [m[0] for m in mats], [m[1] for m in mats], small


_VEC_NAMES = ("norm1_g", "conv_b", "cn_g", "cn_b", "q_norm_g", "k_norm_g", "norm2_g", "ffconv_b")


def kernel(x, norm1_g, w_in, conv_w, conv_b, cn_g, cn_b, q_norm_g, k_norm_g, w_out, norm2_g, w_up, ffconv_w, ffconv_b, w_down, loss_target, m_norm1_g, m_w_in, m_conv_w, m_conv_b, m_cn_g, m_cn_b, m_q_norm_g, m_k_norm_g, m_w_out, m_norm2_g, m_w_up, m_ffconv_w, m_ffconv_b, m_w_down, v_norm1_g, v_w_in, v_conv_w, v_conv_b, v_cn_g, v_cn_b, v_q_norm_g, v_k_norm_g, v_w_out, v_norm2_g, v_w_up, v_ffconv_w, v_ffconv_b, v_w_down):
    chip = 2 * lax.axis_index("x") + lax.axis_index("y")

    gathered = _gather_weights([w.astype(BF16) for w in (w_in, w_out, w_up, w_down)] + [conv_w, ffconv_w],
                               _MAT_ROW_SHARDED + (False, False))
    full, conv_w_full, ffconv_w_full = gathered[:4], gathered[4], gathered[5]

    loss, dx, gf32, gbf, small = _local_step(
        x[0], loss_target[0], norm1_g, conv_w_full, conv_b, cn_g, cn_b, q_norm_g, k_norm_g, norm2_g,
        ffconv_w_full, ffconv_b, *full)

    recv, own, vrecv = _grad_exchange(gbf, gf32, _pack(small, VPACK_ROWS))
    names = ("w_in", "w_out", "w_up", "w_down")
    parts = [_partial_sum(own[k], recv[k], "partial_" + names[k]) for k in range(4)]
    sib = _sibling_exchange(parts)
    ws = (w_in, w_out, w_up, w_down)
    ms = (m_w_in, m_w_out, m_w_up, m_w_down)
    vs = (v_w_in, v_w_out, v_w_up, v_w_down)
    mat = [_adamw_mat(parts[k], sib[k], ws[k], ms[k], vs[k], "adamw_" + names[k]) for k in range(4)]

    vsum = _vec_reduce(vrecv)
    vec_shapes = [(D,), (C,), (C,), (C,), (HEAD,), (HEAD,), (D,), (NUP,), (CONV_K, C), (FF_K, NUP)]
    gsmall = _unpack(vsum, vec_shapes)
    g_conv_w = lax.dynamic_slice_in_dim(gsmall[8], chip * (C // N_CHIPS), C // N_CHIPS, axis=1)
    g_ffconv_w = lax.dynamic_slice_in_dim(gsmall[9], chip * (NUP // N_CHIPS), NUP // N_CHIPS, axis=1)
    gs = gsmall[:8] + [g_conv_w, g_ffconv_w]
    w_s = [norm1_g, conv_b, cn_g, cn_b, q_norm_g, k_norm_g, norm2_g, ffconv_b, conv_w, ffconv_w]
    m_s = [m_norm1_g, m_conv_b, m_cn_g, m_cn_b, m_q_norm_g, m_k_norm_g, m_norm2_g, m_ffconv_b, m_conv_w, m_ffconv_w]
    v_s = [v_norm1_g, v_conv_b, v_cn_g, v_cn_b, v_q_norm_g, v_k_norm_g, v_norm2_g, v_ffconv_b, v_conv_w, v_ffconv_w]
    shapes_s = [a.shape for a in w_s]
    d_p, m_p, v_p = _adamw_small(_pack(w_s, SPACK_ROWS), _pack(gs, SPACK_ROWS), _pack(m_s, SPACK_ROWS),
                                 _pack(v_s, SPACK_ROWS))
    d_s, nm_s, nv_s = _unpack(d_p, shapes_s), _unpack(m_p, shapes_s), _unpack(v_p, shapes_s)

    def ordered(sm, mt):
        return [sm[0], mt[0], sm[8], sm[1], sm[2], sm[3], sm[4], sm[5], mt[1], sm[6], mt[2], sm[9], sm[7], mt[3]]

    loss_all = lax.psum(loss, ("x", "y", "c"))
    grads = ordered(gs, [r[0] for r in mat])
    deltas = ordered(d_s, [r[1] for r in mat])
    new_m = ordered(nm_s, [r[2] for r in mat])
    new_v = ordered(nv_s, [r[3] for r in mat])
    return (loss_all, dx[None], *grads, *deltas, *new_m, *new_v)
```

```python
import types

import jax
import jax.numpy as jnp
from jax import lax
from jax.experimental import pallas as pl
from jax.experimental.pallas import tpu as pltpu

T = 8192
D = 1024
C = 512
NPROJ = 2560
DFF = 2816
NUP = 2 * DFF
CONV_K = 31
FF_K = 3
HEAD = 64
EPS = 1e-6
NEG = -1e30
N_CHIPS = 4
N_DEV = 8
PATTERN_DILATIONS = (1, 4, 16)
QB = 128

ADAM_LR = 0.001
ADAM_B1 = 0.9
ADAM_B2 = 0.999
ADAM_EPS = 1e-08
ADAM_WD = 0.01
ADAM_STEP = 10

F32 = jnp.float32
BF16 = jnp.bfloat16
MESH = pl.DeviceIdType.MESH
ANY = pl.BlockSpec(memory_space=pl.ANY)

VPACK_ROWS = 48
SPACK_ROWS = 24


def _params(sem, vmem_mb):
    return pltpu.CompilerParams(dimension_semantics=sem, vmem_limit_bytes=vmem_mb << 20)


def _nt(a, b):
    return lax.dot_general(a, b, (((1,), (1,)), ((), ())), preferred_element_type=F32)


def _tn_dot(a, b):
    return lax.dot_general(a, b, (((0,), (0,)), ((), ())), preferred_element_type=F32)


def _sigmoid(x):
    return 1.0 / (1.0 + jnp.exp(-x))


def _segsum(x, bd):
    hi = x.astype(BF16)
    lo = (x - hi.astype(F32)).astype(BF16)
    return (jnp.dot(hi, bd, preferred_element_type=F32)
            + jnp.dot(lo, bd, preferred_element_type=F32))


def _place():
    x, y, c = lax.axis_index("x"), lax.axis_index("y"), lax.axis_index("c")
    chips = [(1 - x, y), (x, 1 - y), (1 - x, 1 - y)]
    return x, y, c, chips


def _block_of(ref, shard_shape, row_sharded, s):
    r, cdim = shard_shape
    if row_sharded:
        return ref.at[pl.ds(s * r, r), :]
    return ref.at[:, pl.ds(s * cdim, cdim)]


def _full_shape(shard_shape, row_sharded):
    r, cdim = shard_shape
    return (r * N_CHIPS, cdim) if row_sharded else (r, cdim * N_CHIPS)


def _gather_rider(shards, row_sharded):
    n = len(shards)
    shapes = [a.shape for a in shards]

    def copies(ins, outs, sems):
        send_sems, recv_sems, local_sems = sems
        x, y, c, chips = _place()
        me = 2 * x + y
        place = lambda k, s: _block_of(outs[k], shapes[k], row_sharded[k], s)
        local = [pltpu.make_async_copy(ins[k], place(k, me), local_sems.at[k]) for k in range(n)]
        sends, recvs = [], []
        for k in range(n):
            for j, (px, py) in enumerate(chips):
                sem = dict(send_sem=send_sems.at[3 * k + j], recv_sem=recv_sems.at[3 * k + j],
                           device_id=(px, py, c), device_id_type=MESH)
                sends.append(pltpu.make_async_remote_copy(src_ref=ins[k], dst_ref=place(k, me), **sem))
                recvs.append(pltpu.make_async_remote_copy(src_ref=ins[k], dst_ref=place(k, 2 * px + py), **sem))
        return local, sends, recvs

    return types.SimpleNamespace(
        operands=list(shards), copies=copies,
        out_shape=[jax.ShapeDtypeStruct(_full_shape(s, rs), a.dtype) for s, rs, a in zip(shapes, row_sharded, shards)],
        sems=[pltpu.SemaphoreType.DMA((3 * n,)), pltpu.SemaphoreType.DMA((3 * n,)), pltpu.SemaphoreType.DMA((n,))])


def _grad_rider(g_bf16, g_f32, row_sharded):
    shard = tuple(d // N_CHIPS if (i == 0) == row_sharded else d for i, d in enumerate(g_f32.shape))

    def copies(ins, outs, sems):
        send_sems, recv_sems, local_sems = sems
        gb, gf = ins
        rec, own = outs
        x, y, c, chips = _place()
        me = 2 * x + y
        local = [pltpu.make_async_copy(_block_of(gf, shard, row_sharded, me), own, local_sems.at[0])]
        sends, recvs = [], []
        for j, (px, py) in enumerate(chips):
            sem = dict(send_sem=send_sems.at[j], recv_sem=recv_sems.at[j], device_id=(px, py, c), device_id_type=MESH)
            sends.append(pltpu.make_async_remote_copy(
                src_ref=_block_of(gb, shard, row_sharded, 2 * px + py), dst_ref=rec.at[j], **sem))
            recvs.append(pltpu.make_async_remote_copy(
                src_ref=_block_of(gb, shard, row_sharded, me), dst_ref=rec.at[j], **sem))
        return local, sends, recvs

    return types.SimpleNamespace(
        operands=[g_bf16, g_f32], copies=copies,
        out_shape=[jax.ShapeDtypeStruct((3,) + shard, BF16), jax.ShapeDtypeStruct(shard, F32)],
        sems=[pltpu.SemaphoreType.DMA((3,)), pltpu.SemaphoreType.DMA((3,)), pltpu.SemaphoreType.DMA((1,))])


def _rider_start(rider, ins, outs, sems):
    local, sends, _ = rider.copies(ins, outs, sems)
    for cp in local + sends:
        cp.start()


def _rider_wait(rider, ins, outs, sems):
    local, sends, recvs = rider.copies(ins, outs, sems)
    for cp in recvs:
        cp.wait_recv()
    for cp in sends:
        cp.wait_send()
    for cp in local:
        cp.wait()


def _call(body, *operands, rider=None, name, grid, in_specs, out_specs, out_shape, scratch_shapes=(),
          compiler_params, input_output_aliases=None):
    if rider is None:
        return pl.pallas_call(
            body, name=name, grid=grid, in_specs=list(in_specs), out_specs=list(out_specs), out_shape=list(out_shape),
            scratch_shapes=list(scratch_shapes), compiler_params=compiler_params,
            input_output_aliases=input_output_aliases or {})(*operands)
    n_in, n_out, n_scr = len(in_specs), len(out_specs), len(scratch_shapes)
    r_in, r_out = len(rider.operands), len(rider.out_shape)

    def riding(*refs):
        refs = list(refs)
        ins, refs = refs[:n_in], refs[n_in:]
        r_ins, refs = refs[:r_in], refs[r_in:]
        outs, refs = refs[:n_out], refs[n_out:]
        r_outs, refs = refs[:r_out], refs[r_out:]
        scr, sems = refs[:n_scr], refs[n_scr:]
        first = pl.program_id(0) == 0
        last = pl.program_id(0) == grid[0] - 1
        for axis in range(1, len(grid)):
            first = first & (pl.program_id(axis) == 0)
            last = last & (pl.program_id(axis) == grid[axis] - 1)

        @pl.when(first)
        def _():
            _rider_start(rider, r_ins, r_outs, sems)

        body(*ins, *outs, *scr)

        @pl.when(last)
        def _():
            _rider_wait(rider, r_ins, r_outs, sems)

    return pl.pallas_call(
        riding, name=name, grid=grid, in_specs=list(in_specs) + [ANY] * r_in,
        out_specs=list(out_specs) + [ANY] * r_out, out_shape=list(out_shape) + list(rider.out_shape),
        scratch_shapes=list(scratch_shapes) + list(rider.sems), compiler_params=compiler_params,
        input_output_aliases=input_output_aliases or {})(*operands, *rider.operands)


def _gather_now(shards, row_sharded):
    rider = _gather_rider(shards, row_sharded)
    n = len(shards)

    def body(*refs):
        ins, outs, sems = refs[:n], refs[n:2 * n], refs[2 * n:]
        _rider_start(rider, ins, outs, sems)
        _rider_wait(rider, ins, outs, sems)

    return pl.pallas_call(
        body, name="gather_first", out_shape=rider.out_shape, in_specs=[ANY] * n, out_specs=[ANY] * n,
        scratch_shapes=rider.sems)(*shards)


def _final_exchange(parts, vpack):
    def body(p0, p1, p2, p3, v_ref, o0, o1, o2, o3, vr_ref, send_sems, recv_sems, vsend_sems, vrecv_sems, local_sem):
        x, y, c, _ = _place()
        me = 4 * x + 2 * y + c
        mine = pltpu.make_async_copy(v_ref, vr_ref.at[me], local_sem)
        mine.start()
        copies = [pltpu.make_async_remote_copy(
            src_ref=p, dst_ref=o, send_sem=send_sems.at[k], recv_sem=recv_sems.at[k],
            device_id=(x, y, 1 - c), device_id_type=MESH)
            for k, (p, o) in enumerate(zip((p0, p1, p2, p3), (o0, o1, o2, o3)))]
        flips = [(fx, fy, fc) for fx in (0, 1) for fy in (0, 1) for fc in (0, 1)][1:]
        recvs = []
        for r, (fx, fy, fc) in enumerate(flips):
            peer = (x ^ fx, y ^ fy, c ^ fc)
            sem = dict(send_sem=vsend_sems.at[r], recv_sem=vrecv_sems.at[r], device_id=peer, device_id_type=MESH)
            copies.append(pltpu.make_async_remote_copy(src_ref=v_ref, dst_ref=vr_ref.at[me], **sem))
            recvs.append(pltpu.make_async_remote_copy(
                src_ref=v_ref, dst_ref=vr_ref.at[4 * peer[0] + 2 * peer[1] + peer[2]], **sem))
        for cp in copies:
            cp.start()
        for cp in copies[:4]:
            cp.wait_recv()
        for cp in recvs:
            cp.wait_recv()
        for cp in copies:
            cp.wait_send()
        mine.wait()

    res = pl.pallas_call(
        body, name="final_exchange",
        out_shape=[jax.ShapeDtypeStruct(p.shape, F32) for p in parts]
        + [jax.ShapeDtypeStruct((N_DEV, VPACK_ROWS, D), F32)],
        in_specs=[ANY] * 5, out_specs=[ANY] * 5,
        scratch_shapes=[pltpu.SemaphoreType.DMA((4,)), pltpu.SemaphoreType.DMA((4,)),
                        pltpu.SemaphoreType.DMA((7,)), pltpu.SemaphoreType.DMA((7,)), pltpu.SemaphoreType.DMA],
    )(*parts, vpack)
    return res[:4], res[4]


def _proj_fwd(x, g1, w_in, rider):
    tm, tn = 512, 640

    def body(x_ref, g_ref, w_ref, h_ref, p_ref):
        @pl.when(pl.program_id(1) == 0)
        def _():
            xv = x_ref[...]
            r = lax.rsqrt(jnp.mean(xv * xv, axis=-1, keepdims=True) + EPS)
            h_ref[...] = (xv * r * g_ref[...]).astype(BF16)
        p_ref[...] = jnp.dot(h_ref[...], w_ref[...], preferred_element_type=F32)

    return _call(
        body, x, g1, w_in, rider=rider, name="proj_fwd", grid=(T // tm, NPROJ // tn),
        in_specs=[pl.BlockSpec((tm, D), lambda i, j: (i, 0)), pl.BlockSpec((1, D), lambda i, j: (0, 0)),
                  pl.BlockSpec((D, tn), lambda i, j: (0, j))],
        out_specs=[pl.BlockSpec((tm, D), lambda i, j: (i, 0)), pl.BlockSpec((tm, tn), lambda i, j: (i, j))],
        out_shape=[jax.ShapeDtypeStruct((T, D), BF16), jax.ShapeDtypeStruct((T, NPROJ), F32)],
        compiler_params=_params(("arbitrary", "arbitrary"), 40))


CONV_TM = 512
CONV_HALO = 32
CONV_RB = 64


def _fill_shifts(sh_ref, rows):
    for b in range(1, 8):
        sh_ref[b, 0:rows - 8, :] = sh_ref[0, b:b + rows - 8, :]


def _shifted(sh_ref, off, rb):
    a, b = divmod(off, 8)
    return sh_ref[b, 8 * a:8 * a + rb, :]


def _conv_fwd(proj, conv_w, conv_b, cn_g, cn_b, rider):
    tm, hl, rb = CONV_TM, CONV_HALO, CONV_RB
    per = tm // hl

    def body(av_ref, ag_ref, hv_ref, hg_ref, w_ref, b_ref, g_ref, bb_ref, cat_ref, cv_ref, sh_ref):
        i = pl.program_id(0)
        glu_h = hv_ref[...] * _sigmoid(hg_ref[...])
        sh_ref[0, 0:hl, :] = jnp.where(i > 0, glu_h, 0.0)
        sh_ref[0, hl:, :] = av_ref[...] * _sigmoid(ag_ref[...])
        _fill_shifts(sh_ref, tm + hl)
        for r0 in range(0, tm, rb):
            acc = jnp.zeros((rb, C), F32) + b_ref[...]
            for k in range(CONV_K):
                acc = acc + w_ref[k:k + 1, :] * _shifted(sh_ref, r0 + hl - (CONV_K - 1) + k, rb)
            mu = jnp.mean(acc, axis=-1, keepdims=True)
            xc = acc - mu
            var = jnp.mean(xc * xc, axis=-1, keepdims=True)
            ln = xc * lax.rsqrt(var + EPS) * g_ref[...] + bb_ref[...]
            cv_ref[r0:r0 + rb, :] = acc
            cat_ref[r0:r0 + rb, :] = (ln * _sigmoid(ln)).astype(BF16)

    halo = lambda col: pl.BlockSpec((hl, C), lambda i: (jnp.maximum(i * per - 1, 0), col))
    vec = pl.BlockSpec((1, C), lambda i: (0, 0))
    return _call(
        body, proj, proj, proj, proj, conv_w, conv_b, cn_g, cn_b, rider=rider, name="conv_fwd", grid=(T // tm,),
        in_specs=[pl.BlockSpec((tm, C), lambda i: (i, 0)), pl.BlockSpec((tm, C), lambda i: (i, 1)),
                  halo(0), halo(1), pl.BlockSpec((CONV_K, C), lambda i: (0, 0)), vec, vec, vec],
        out_specs=[pl.BlockSpec((tm, C), lambda i: (i, 0)), pl.BlockSpec((tm, C), lambda i: (i, 0))],
        out_shape=[jax.ShapeDtypeStruct((T, D), BF16), jax.ShapeDtypeStruct((T, C), F32)],
        scratch_shapes=[pltpu.VMEM((8, tm + hl, C), F32)],
        compiler_params=_params(("arbitrary",), 40))


def _qkv_prep(proj, qg, kg, bd):
    tm = 512

    def body(q_ref, k_ref, qg_ref, kg_ref, bd_ref, qn_ref, kn_ref):
        for src, g, dst in ((q_ref, qg_ref, qn_ref), (k_ref, kg_ref, kn_ref)):
            xv = src[...]
            ms = _segsum(xv * xv, bd_ref[...]) * (1.0 / HEAD)
            dst[...] = xv * lax.rsqrt(ms + EPS) * g[...]

    col = lambda c: pl.BlockSpec((tm, C), lambda i: (i, c))
    vec = pl.BlockSpec((1, C), lambda i: (0, 0))
    out = pl.BlockSpec((tm, C), lambda i: (i, 0))
    return _call(
        body, proj, proj, qg, kg, bd, name="qkv_prep", grid=(T // tm,),
        in_specs=[col(2), col(3), vec, vec, pl.BlockSpec((C, C), lambda i: (0, 0))],
        out_specs=[out, out],
        out_shape=[jax.ShapeDtypeStruct((T, C), F32)] * 2,
        compiler_params=_params(("parallel",), 32))


def _stack_heads(a):
    lane = lax.broadcasted_iota(jnp.int32, a.shape, 1)
    zero = jnp.zeros_like(a)
    return jnp.concatenate([jnp.where(lane < HEAD, a, zero), jnp.where(lane >= HEAD, a, zero)], axis=0)


def _unstack_heads(a2):
    lane = lax.broadcasted_iota(jnp.int32, (QB, 2 * HEAD), 1)
    return jnp.where(lane < HEAD, a2[:QB], a2[QB:])


def _stack_cols(a):
    return jnp.concatenate([a[:, 0:1], a[:, HEAD:HEAD + 1]], axis=0)


ATT_WIN = 2048
V_COL = 4 * C // (2 * HEAD)
DO_COL = C // (2 * HEAD)


def _attn_geometry(d):
    sl = ATT_WIN // d
    return sl, sl // QB, QB * d


def _stream(ref, r, n, d):
    return ref[pl.ds(r, n, stride=d), :] if d > 1 else ref[pl.ds(r, n), :]


def _alibi_tables(d):
    qi = jnp.arange(QB)[:, None]
    kj = jnp.arange(2 * QB)[None, :]
    delta = qi + QB - kj
    band = (delta >= 0) & (delta <= QB)
    dist = (delta * d).astype(F32)
    heads = jnp.arange(8, dtype=F32)
    slopes = 2.0 ** (-(heads + 1.0))
    t = jnp.where(band[None], -slopes[:, None, None] * dist[None], NEG)
    return t.reshape(4, 2 * QB, 2 * QB)


def _attn_specs(d):
    _, _, hr = _attn_geometry(d)
    per = ATT_WIN // hr
    main = lambda off: pl.BlockSpec((ATT_WIN, 2 * HEAD), lambda cb, n: (n, off + cb))
    prev = lambda off: pl.BlockSpec((hr, 2 * HEAD), lambda cb, n: (jnp.maximum(n * per - 1, 0), off + cb))
    nxt = lambda off: pl.BlockSpec((hr, 2 * HEAD), lambda cb, n: (jnp.minimum((n + 1) * per, T // hr - 1), off + cb))
    bias = pl.BlockSpec((None, 2 * QB, 2 * QB), lambda cb, n: (cb, 0, 0))
    return main, prev, nxt, bias


def _attn_fwd(qn, kn, proj, bias, d):
    sl, nb, hr = _attn_geometry(d)
    slk = QB + sl

    def body(q_ref, k_ref, v_ref, kh_ref, vh_ref, bias_ref, o_ref, l_ref, qs, ks, vs, os_, ls):
        n = pl.program_id(1)
        for r in range(d):
            qs[r * sl:(r + 1) * sl, :] = _stream(q_ref, r, sl, d).astype(BF16)
            for dst, halo, src in ((ks, kh_ref, k_ref), (vs, vh_ref, v_ref)):
                dst[r * slk:r * slk + QB, :] = _stream(halo, r, QB, d).astype(BF16)
                dst[r * slk + QB:(r + 1) * slk, :] = _stream(src, r, sl, d).astype(BF16)
        col = lax.broadcasted_iota(jnp.int32, (2 * QB, 2 * QB), 1)
        for r in range(d):
            for b in range(nb):
                rows = slice(r * sl + b * QB, r * sl + (b + 1) * QB)
                keys = slice(r * slk + b * QB, r * slk + (b + 2) * QB)
                s = _nt(_stack_heads(qs[rows, :]), ks[keys, :]) + bias_ref[...]
                if b == 0:
                    s = jnp.where((col < QB) & (n == 0), NEG, s)
                m = jnp.max(s, axis=-1, keepdims=True)
                p = jnp.exp(s - m)
                den = jnp.sum(p, axis=-1, keepdims=True)
                pv = jnp.dot(p.astype(BF16), vs[keys, :], preferred_element_type=F32)
                os_[rows, :] = _unstack_heads(pv / den)
                ls[rows, :] = _unstack_heads(jnp.broadcast_to(m + jnp.log(den), (2 * QB, 2 * HEAD)))
        for r in range(d):
            if d > 1:
                o_ref[pl.ds(r, sl, stride=d), :] = os_[r * sl:(r + 1) * sl, :]
                l_ref[pl.ds(r, sl, stride=d), :] = ls[r * sl:(r + 1) * sl, :]
            else:
                o_ref[...] = os_[...]
                l_ref[...] = ls[...]

    main, prev, _, bias_spec = _attn_specs(d)
    lanes = 2 * HEAD
    return _call(
        body, qn, kn, proj, kn, proj, bias, name=f"attn_fwd_d{d}", grid=(C // lanes, T // ATT_WIN),
        in_specs=[main(0), main(0), main(V_COL), prev(0), prev(V_COL), bias_spec],
        out_specs=[main(0), main(0)],
        out_shape=[jax.ShapeDtypeStruct((T, C), F32)] * 2,
        scratch_shapes=[pltpu.VMEM((ATT_WIN, lanes), BF16), pltpu.VMEM((ATT_WIN + hr, lanes), BF16),
                        pltpu.VMEM((ATT_WIN + hr, lanes), BF16), pltpu.VMEM((ATT_WIN, lanes), F32),
                        pltpu.VMEM((ATT_WIN, lanes), F32)],
        compiler_params=_params(("parallel", "arbitrary"), 40))


def _attn_merge(outs, lses, cat):
    tm = 512

    def body(o0, o1, o2, l0, l1, l2, cat_in, cat_ref, of_ref, lg_ref):
        del cat_in
        a, b, c = l0[...], l1[...], l2[...]
        m = jnp.maximum(jnp.maximum(a, b), c)
        e0, e1, e2 = jnp.exp(a - m), jnp.exp(b - m), jnp.exp(c - m)
        den = e0 + e1 + e2
        o = (e0 * o0[...] + e1 * o1[...] + e2 * o2[...]) / den
        of_ref[...] = o
        cat_ref[...] = o.astype(BF16)
        lg_ref[...] = m + jnp.log(den)

    blk = pl.BlockSpec((tm, C), lambda i: (i, 0))
    return _call(
        body, *outs, *lses, cat, name="attn_merge", grid=(T // tm,),
        in_specs=[blk] * 6 + [ANY],
        out_specs=[pl.BlockSpec((tm, C), lambda i: (i, 1)), blk, blk],
        out_shape=[jax.ShapeDtypeStruct((T, D), BF16), jax.ShapeDtypeStruct((T, C), F32),
                   jax.ShapeDtypeStruct((T, C), F32)],
        input_output_aliases={6: 0},
        compiler_params=_params(("parallel",), 32))


def _out_up(x, cat, w_out, g2, w_up):
    tm, tn = 512, NUP // 4

    def body(x_ref, cat_ref, wo_ref, g_ref, wu_ref, x1_ref, h2_ref, up_ref):
        @pl.when(pl.program_id(1) == 0)
        def _():
            x1 = x_ref[...] + jnp.dot(cat_ref[...], wo_ref[...], preferred_element_type=F32)
            x1_ref[...] = x1
            r = lax.rsqrt(jnp.mean(x1 * x1, axis=-1, keepdims=True) + EPS)
            h2_ref[...] = (x1 * r * g_ref[...]).astype(BF16)
        up_ref[...] = jnp.dot(h2_ref[...], wu_ref[...], preferred_element_type=F32)

    row = pl.BlockSpec((tm, D), lambda i, j: (i, 0))
    return _call(
        body, x, cat, w_out, g2, w_up, name="out_up", grid=(T // tm, 4),
        in_specs=[row, row, pl.BlockSpec((D, D), lambda i, j: (0, 0)), pl.BlockSpec((1, D), lambda i, j: (0, 0)),
                  pl.BlockSpec((D, tn), lambda i, j: (0, j))],
        out_specs=[row, row, pl.BlockSpec((tm, tn), lambda i, j: (i, j))],
        out_shape=[jax.ShapeDtypeStruct((T, D), F32), jax.ShapeDtypeStruct((T, D), BF16),
                   jax.ShapeDtypeStruct((T, NUP), F32)],
        compiler_params=_params(("parallel", "arbitrary"), 48))


FF_TM = 256
FF_HALO = 8
FF_CW = 256


def _ff_conv(ext_ref, fw_ref, fb_ref, col0, tm):
    cols = slice(col0, col0 + FF_CW)
    acc = fb_ref[:, cols] + fw_ref[0:1, cols] * ext_ref[FF_HALO - 2:FF_HALO - 2 + tm, cols]
    acc = acc + fw_ref[1:2, cols] * ext_ref[FF_HALO - 1:FF_HALO - 1 + tm, cols]
    return acc + fw_ref[2:3, cols] * ext_ref[FF_HALO:FF_HALO + tm, cols]


def _ffn_down(up, ffconv_w, ffconv_b, w_down, x1, target):
    tm, hl = FF_TM, FF_HALO
    per = tm // hl

    def body(up_ref, uh_ref, fw_ref, fb_ref, wd_ref, x1_ref, tg_ref, upc_ref, act_ref, dy_ref, loss_ref, ext_ref):
        i = pl.program_id(0)
        ext_ref[0:hl, :] = jnp.where(i > 0, uh_ref[...], 0.0)
        ext_ref[hl:, :] = up_ref[...]
        for c in range(DFF // FF_CW):
            gcols = slice(c * FF_CW, (c + 1) * FF_CW)
            vcols = slice(DFF + c * FF_CW, DFF + (c + 1) * FF_CW)
            gate = _ff_conv(ext_ref, fw_ref, fb_ref, c * FF_CW, tm)
            val = _ff_conv(ext_ref, fw_ref, fb_ref, DFF + c * FF_CW, tm)
            upc_ref[:, gcols] = gate
            upc_ref[:, vcols] = val
            act_ref[:, gcols] = (gate * _sigmoid(gate) * val).astype(BF16)
        y = x1_ref[...] + jnp.dot(act_ref[...], wd_ref[...], preferred_element_type=F32)
        err = y - tg_ref[...]
        dy_ref[...] = err * (1.0 / D)

        @pl.when(i == 0)
        def _():
            loss_ref[...] = jnp.zeros_like(loss_ref)
        loss_ref[...] += jnp.sum(err * err)

    row = pl.BlockSpec((tm, D), lambda i: (i, 0))
    wide = pl.BlockSpec((tm, NUP), lambda i: (i, 0))
    return _call(
        body, up, up, ffconv_w, ffconv_b, w_down, x1, target, name="ffn_down", grid=(T // tm,),
        in_specs=[wide, pl.BlockSpec((hl, NUP), lambda i: (jnp.maximum(i * per - 1, 0), 0)),
                  pl.BlockSpec((FF_K, NUP), lambda i: (0, 0)), pl.BlockSpec((1, NUP), lambda i: (0, 0)),
                  pl.BlockSpec((DFF, D), lambda i: (0, 0)), row, row],
        out_specs=[wide, pl.BlockSpec((tm, DFF), lambda i: (i, 0)), row, pl.BlockSpec((8, 128), lambda i: (0, 0))],
        out_shape=[jax.ShapeDtypeStruct((T, NUP), F32), jax.ShapeDtypeStruct((T, DFF), BF16),
                   jax.ShapeDtypeStruct((T, D), F32), jax.ShapeDtypeStruct((8, 128), F32)],
        scratch_shapes=[pltpu.VMEM((tm + hl, NUP), F32)],
        compiler_params=_params(("arbitrary",), 58))


def _down_bwd(dy, w_down, up, upc, ffconv_w):
    tm, hl = FF_TM, FF_HALO
    nt = T // tm

    def body(dy_ref, wd_ref, up_ref, upc_ref, fw_ref, dup_ref, gff_ref, dext_ref, dact_ref):
        i = pl.program_id(0)

        @pl.when(i == 0)
        def _():
            gff_ref[...] = jnp.zeros_like(gff_ref)
            dext_ref[tm:tm + hl, :] = jnp.zeros((hl, NUP), F32)

        dact_ref[...] = _nt(dy_ref[...].astype(BF16), wd_ref[...])
        for c in range(DFF // FF_CW):
            gcols = slice(c * FF_CW, (c + 1) * FF_CW)
            vcols = slice(DFF + c * FF_CW, DFF + (c + 1) * FF_CW)
            gate, val = upc_ref[:, gcols], upc_ref[:, vcols]
            sg = _sigmoid(gate)
            da = dact_ref[:, gcols]
            dext_ref[0:tm, gcols] = da * val * (sg + gate * sg * (1.0 - sg))
            dext_ref[0:tm, vcols] = da * gate * sg
        for c in range(NUP // FF_CW):
            cols = slice(c * FF_CW, (c + 1) * FF_CW)
            shifted = [dext_ref[k:tm + k, cols] for k in range(FF_K)]
            dup = fw_ref[2:3, cols] * shifted[0]
            u = up_ref[:, cols]
            for k in range(FF_K):
                if k:
                    dup = dup + fw_ref[2 - k:3 - k, cols] * shifted[k]
                gff_ref[2 - k:3 - k, cols] += jnp.sum(shifted[k] * u, axis=0, keepdims=True)
            dup_ref[:, cols] = dup.astype(BF16)
            gff_ref[3:4, cols] += jnp.sum(shifted[0], axis=0, keepdims=True)
        dext_ref[tm:tm + hl, :] = dext_ref[0:hl, :]

    rev = lambda i: (nt - 1 - i, 0)
    wide = pl.BlockSpec((tm, NUP), rev)
    return _call(
        body, dy, w_down, up, upc, ffconv_w, name="down_bwd", grid=(nt,),
        in_specs=[pl.BlockSpec((tm, D), rev), pl.BlockSpec((DFF, D), lambda i: (0, 0)), wide, wide,
                  pl.BlockSpec((FF_K, NUP), lambda i: (0, 0))],
        out_specs=[wide, pl.BlockSpec((8, NUP), lambda i: (0, 0))],
        out_shape=[jax.ShapeDtypeStruct((T, NUP), BF16), jax.ShapeDtypeStruct((8, NUP), F32)],
        scratch_shapes=[pltpu.VMEM((tm + hl, NUP), F32), pltpu.VMEM((tm, DFF), F32)],
        compiler_params=_params(("arbitrary",), 58))


def _weight_grad(a, g, bm, bn, name):
    m, n = a.shape[1], g.shape[1]
    tk = 512
    nk = T // tk

    def body(a_ref, g_ref, of_ref, ob_ref):
        k = pl.program_id(2)

        @pl.when(k == 0)
        def _():
            of_ref[...] = jnp.zeros_like(of_ref)
        of_ref[...] += _tn_dot(a_ref[...].astype(BF16), g_ref[...].astype(BF16))

        @pl.when(k == nk - 1)
        def _():
            ob_ref[...] = of_ref[...].astype(BF16)

    out = pl.BlockSpec((bm, bn), lambda i, j, k: (i, j))
    return _call(
        body, a, g, name=name, grid=(m // bm, n // bn, nk),
        in_specs=[pl.BlockSpec((tk, bm), lambda i, j, k: (k, i)), pl.BlockSpec((tk, bn), lambda i, j, k: (k, j))],
        out_specs=[out, out],
        out_shape=[jax.ShapeDtypeStruct((m, n), F32), jax.ShapeDtypeStruct((m, n), BF16)],
        compiler_params=_params(("parallel", "parallel", "arbitrary"), 56))


def _norm_bwd_mm(dz, w, xin, base, gain, kc, name, rider):
    tm = 512
    nk = dz.shape[1] // kc

    def body(dz_ref, w_ref, x_ref, b_ref, g_ref, dx_ref, gg_ref, acc_ref):
        i, j = pl.program_id(0), pl.program_id(1)

        @pl.when(j == 0)
        def _():
            acc_ref[...] = jnp.zeros_like(acc_ref)
        acc_ref[...] += _nt(dz_ref[...], w_ref[...])

        @pl.when((i == 0) & (j == 0))
        def _():
            gg_ref[...] = jnp.zeros_like(gg_ref)

        @pl.when(j == nk - 1)
        def _():
            xv = x_ref[...]
            dh = acc_ref[...]
            r = lax.rsqrt(jnp.mean(xv * xv, axis=-1, keepdims=True) + EPS)
            t = dh * g_ref[...]
            dx_ref[...] = b_ref[...] + r * t - xv * (r * r * r) * jnp.mean(t * xv, axis=-1, keepdims=True)
            gg_ref[...] += jnp.sum(dh * xv * r, axis=0, keepdims=True)

    row = pl.BlockSpec((tm, D), lambda i, j: (i, 0))
    vec = pl.BlockSpec((1, D), lambda i, j: (0, 0))
    return _call(
        body, dz, w, xin, base, gain, rider=rider, name=name, grid=(T // tm, nk),
        in_specs=[pl.BlockSpec((tm, kc), lambda i, j: (i, j)), pl.BlockSpec((D, kc), lambda i, j: (0, j)),
                  row, row, vec],
        out_specs=[row, vec],
        out_shape=[jax.ShapeDtypeStruct((T, D), F32), jax.ShapeDtypeStruct((1, D), F32)],
        scratch_shapes=[pltpu.VMEM((tm, D), F32)],
        compiler_params=_params(("arbitrary", "arbitrary"), 48))


def _outproj_bwd(dx1, w_out):
    tm = 512

    def body(d_ref, w_ref, o_ref):
        o_ref[...] = _nt(d_ref[...].astype(BF16), w_ref[...])

    row = pl.BlockSpec((tm, D), lambda i: (i, 0))
    return _call(
        body, dx1, w_out, name="outproj_bwd", grid=(T // tm,),
        in_specs=[row, pl.BlockSpec((D, D), lambda i: (0, 0))], out_specs=[row],
        out_shape=[jax.ShapeDtypeStruct((T, D), F32)],
        compiler_params=_params(("parallel",), 32))[0]


def _conv_bwd(dcat, cv, proj, conv_w, cn_g, cn_b, rider):
    tm, hl, rb = CONV_TM, CONV_HALO, CONV_RB
    per = tm // hl
    nt = T // tm

    def body(du_ref, dun_ref, cv_ref, cvn_ref, av_ref, ag_ref, hv_ref, hg_ref, w_ref, g_ref, bb_ref,
             dp_ref, gv_ref, gw_ref, dsh_ref, gsh_ref):
        i = pl.program_id(0)

        @pl.when(i == 0)
        def _():
            gv_ref[...] = jnp.zeros_like(gv_ref)
            gw_ref[...] = jnp.zeros_like(gw_ref)

        def ln_bwd(du, cvv):
            mu = jnp.mean(cvv, axis=-1, keepdims=True)
            xc = cvv - mu
            rs = lax.rsqrt(jnp.mean(xc * xc, axis=-1, keepdims=True) + EPS)
            xh = xc * rs
            ln = xh * g_ref[...] + bb_ref[...]
            sg = _sigmoid(ln)
            dln = du * (sg + ln * sg * (1.0 - sg))
            dxh = dln * g_ref[...]
            dcv = rs * (dxh - jnp.mean(dxh, axis=-1, keepdims=True)
                        - xh * jnp.mean(dxh * xh, axis=-1, keepdims=True))
            return dcv, dln, xh

        for r0 in range(0, tm, rb):
            dcv, dln, xh = ln_bwd(du_ref[r0:r0 + rb, :], cv_ref[r0:r0 + rb, :])
            dsh_ref[0, r0:r0 + rb, :] = dcv
            gv_ref[0:1, :] += jnp.sum(dln * xh, axis=0, keepdims=True)
            gv_ref[1:2, :] += jnp.sum(dln, axis=0, keepdims=True)
            gv_ref[2:3, :] += jnp.sum(dcv, axis=0, keepdims=True)
        dcv_n, _, _ = ln_bwd(dun_ref[...], cvn_ref[...])
        dsh_ref[0, tm:tm + hl, :] = jnp.where(i < nt - 1, dcv_n, 0.0)
        glu_h = hv_ref[...] * _sigmoid(hg_ref[...])
        gsh_ref[0, 0:hl, :] = jnp.where(i > 0, glu_h, 0.0)
        gsh_ref[0, hl:, :] = av_ref[...] * _sigmoid(ag_ref[...])
        _fill_shifts(dsh_ref, tm + hl)
        _fill_shifts(gsh_ref, tm + hl)

        for r0 in range(0, tm, rb):
            dglu = jnp.zeros((rb, C), F32)
            for k in range(CONV_K):
                dglu = dglu + w_ref[k:k + 1, :] * _shifted(dsh_ref, r0 + (CONV_K - 1) - k, rb)
            av = av_ref[r0:r0 + rb, :]
            sg = _sigmoid(ag_ref[r0:r0 + rb, :])
            dp_ref[r0:r0 + rb, 0:C] = (dglu * sg).astype(BF16)
            dp_ref[r0:r0 + rb, C:2 * C] = (dglu * av * sg * (1.0 - sg)).astype(BF16)
        for k in range(CONV_K):
            part = jnp.zeros((8, C), F32)
            for r0 in range(0, tm, rb):
                prod = dsh_ref[0, r0:r0 + rb, :] * _shifted(gsh_ref, r0 + hl - (CONV_K - 1) + k, rb)
                part = part + jnp.sum(prod.reshape(rb // 8, 8, C), axis=0)
            gw_ref[k:k + 1, :] += jnp.sum(part, axis=0, keepdims=True)

    main = lambda col: pl.BlockSpec((tm, C), lambda i: (i, col))
    prev = lambda col: pl.BlockSpec((hl, C), lambda i: (jnp.maximum(i * per - 1, 0), col))
    nxt = pl.BlockSpec((hl, C), lambda i: (jnp.minimum((i + 1) * per, T // hl - 1), 0))
    vec = pl.BlockSpec((1, C), lambda i: (0, 0))
    return _call(
        body, dcat, dcat, cv, cv, proj, proj, proj, proj, conv_w, cn_g, cn_b, rider=rider, name="conv_bwd",
        grid=(nt,),
        in_specs=[main(0), nxt, main(0), nxt, main(0), main(1), prev(0), prev(1),
                  pl.BlockSpec((CONV_K, C), lambda i: (0, 0)), vec, vec],
        out_specs=[pl.BlockSpec((tm, 2 * C), lambda i: (i, 0)), pl.BlockSpec((8, C), lambda i: (0, 0)),
                   pl.BlockSpec((32, C), lambda i: (0, 0))],
        out_shape=[jax.ShapeDtypeStruct((T, NPROJ), BF16), jax.ShapeDtypeStruct((8, C), F32),
                   jax.ShapeDtypeStruct((32, C), F32)],
        scratch_shapes=[pltpu.VMEM((8, tm + hl, C), F32), pltpu.VMEM((8, tm + hl, C), F32)],
        compiler_params=_params(("arbitrary",), 48))


def _attn_bwd_prep(dcat, o_f32, bd):
    tm = 512

    def body(do_ref, o_ref, bd_ref, dl_ref):
        dl_ref[...] = _segsum(do_ref[...] * o_ref[...], bd_ref[...])

    blk = pl.BlockSpec((tm, C), lambda i: (i, 0))
    return _call(
        body, dcat, o_f32, bd, name="attn_bwd_prep", grid=(T // tm,),
        in_specs=[pl.BlockSpec((tm, C), lambda i: (i, 1)), blk, pl.BlockSpec((C, C), lambda i: (0, 0))],
        out_specs=[blk],
        out_shape=[jax.ShapeDtypeStruct((T, C), F32)],
        compiler_params=_params(("parallel",), 32))[0]


def _attn_bwd(qn, kn, proj, dcat, lg, dl, bias, d, rider=None):
    sl, nb, hr = _attn_geometry(d)
    slk = QB + sl
    slq = sl + QB
    n_win = T // ATT_WIN

    def body(q_ref, k_ref, v_ref, do_ref, lg_ref, dl_ref, kh_ref, vh_ref, qx_ref, dox_ref, lgx_ref, dlx_ref,
             bias_ref, dq_ref, dk_ref, dv_ref, qs, dos, lgs, dls, ks, vs, dqs, dks, dvs):
        n = pl.program_id(1)
        for r in range(d):
            for dst, src, nx, dt in ((qs, q_ref, qx_ref, BF16), (dos, do_ref, dox_ref, BF16),
                                     (lgs, lg_ref, lgx_ref, F32), (dls, dl_ref, dlx_ref, F32)):
                dst[r * slq:r * slq + sl, :] = _stream(src, r, sl, d).astype(dt)
                dst[r * slq + sl:(r + 1) * slq, :] = _stream(nx, r, QB, d).astype(dt)
            for dst, halo, src in ((ks, kh_ref, k_ref), (vs, vh_ref, v_ref)):
                dst[r * slk:r * slk + QB, :] = _stream(halo, r, QB, d).astype(BF16)
                dst[r * slk + QB:(r + 1) * slk, :] = _stream(src, r, sl, d).astype(BF16)
        dks[...] = jnp.zeros_like(dks)
        dvs[...] = jnp.zeros_like(dvs)

        def unit(rows, kc, vc, biasv, invalid_prev):
            qst, dost = _stack_heads(qs[rows, :]), _stack_heads(dos[rows, :])
            s = _nt(qst, kc) + biasv
            if invalid_prev is not None:
                col = lax.broadcasted_iota(jnp.int32, s.shape, 1)
                s = jnp.where((col < QB) & invalid_prev, NEG, s)
            p = jnp.exp(s - _stack_cols(lgs[rows, :]))
            ds = p * (_nt(dost, vc) - _stack_cols(dls[rows, :]))
            dsb = ds.astype(BF16)
            dq = _unstack_heads(jnp.dot(dsb, kc, preferred_element_type=F32))
            return dq, _tn_dot(dsb, qst), _tn_dot(p.astype(BF16), dost)

        for r in range(d):
            for b in range(nb):
                rows = slice(r * slq + b * QB, r * slq + (b + 1) * QB)
                keys = slice(r * slk + b * QB, r * slk + (b + 2) * QB)
                dq, dkc, dvc = unit(rows, ks[keys, :], vs[keys, :], bias_ref[...], (n == 0) if b == 0 else None)
                dqs[r * sl + b * QB:r * sl + (b + 1) * QB, :] = dq
                if b == 0:
                    dks[r * sl:r * sl + QB, :] += dkc[QB:]
                    dvs[r * sl:r * sl + QB, :] += dvc[QB:]
                else:
                    dks[r * sl + (b - 1) * QB:r * sl + (b + 1) * QB, :] += dkc
                    dvs[r * sl + (b - 1) * QB:r * sl + (b + 1) * QB, :] += dvc

        @pl.when(n < n_win - 1)
        def _():
            for r in range(d):
                rows = slice(r * slq + sl, (r + 1) * slq)
                keys = slice(r * slk + sl, (r + 1) * slk)
                _, dkc, dvc = unit(rows, ks[keys, :], vs[keys, :], bias_ref[:, 0:QB], None)
                dks[(r + 1) * sl - QB:(r + 1) * sl, :] += dkc
                dvs[(r + 1) * sl - QB:(r + 1) * sl, :] += dvc

        for dst, src in ((dq_ref, dqs), (dk_ref, dks), (dv_ref, dvs)):
            for r in range(d):
                if d > 1:
                    dst[pl.ds(r, sl, stride=d), :] = src[r * sl:(r + 1) * sl, :]
                else:
                    dst[...] = src[...]

    main, prev, nxt, bias_spec = _attn_specs(d)
    lanes = 2 * HEAD
    return _call(
        body, qn, kn, proj, dcat, lg, dl, kn, proj, qn, dcat, lg, dl, bias, rider=rider, name=f"attn_bwd_d{d}",
        grid=(C // lanes, n_win),
        in_specs=[main(0), main(0), main(V_COL), main(DO_COL), main(0), main(0), prev(0), prev(V_COL),
                  nxt(0), nxt(DO_COL), nxt(0), nxt(0), bias_spec],
        out_specs=[main(0)] * 3,
        out_shape=[jax.ShapeDtypeStruct((T, C), F32)] * 3,
        scratch_shapes=[pltpu.VMEM((ATT_WIN + hr, lanes), BF16), pltpu.VMEM((ATT_WIN + hr, lanes), BF16),
                        pltpu.VMEM((ATT_WIN + hr, lanes), F32), pltpu.VMEM((ATT_WIN + hr, lanes), F32),
                        pltpu.VMEM((ATT_WIN + hr, lanes), BF16), pltpu.VMEM((ATT_WIN + hr, lanes), BF16),
                        pltpu.VMEM((ATT_WIN, lanes), F32), pltpu.VMEM((ATT_WIN, lanes), F32),
                        pltpu.VMEM((ATT_WIN, lanes), F32)],
        compiler_params=_params(("arbitrary", "arbitrary"), 48))


def _qk_norm_bwd(d3, proj, col, gain, bd, dproj, name):
    tm = 512

    def body(d0, d1, d2, x_ref, g_ref, bd_ref, dp_in, dp_ref, gg_ref):
        del dp_in

        @pl.when(pl.program_id(0) == 0)
        def _():
            gg_ref[...] = jnp.zeros_like(gg_ref)
        dn = d0[...] + d1[...] + d2[...]
        xv = x_ref[...]
        r = lax.rsqrt(_segsum(xv * xv, bd_ref[...]) * (1.0 / HEAD) + EPS)
        t = dn * g_ref[...]
        mean_tx = _segsum(t * xv, bd_ref[...]) * (1.0 / HEAD)
        dp_ref[...] = (r * t - xv * (r * r * r) * mean_tx).astype(BF16)
        gg_ref[...] += jnp.sum(dn * xv * r, axis=0, keepdims=True)

    blk = pl.BlockSpec((tm, C), lambda i: (i, 0))
    vec = pl.BlockSpec((1, C), lambda i: (0, 0))
    return _call(
        body, *d3, proj, gain, bd, dproj, name=name, grid=(T // tm,),
        in_specs=[blk, blk, blk, pl.BlockSpec((tm, C), lambda i: (i, col)), vec,
                  pl.BlockSpec((C, C), lambda i: (0, 0)), ANY],
        out_specs=[pl.BlockSpec((tm, C), lambda i: (i, col)), vec],
        out_shape=[jax.ShapeDtypeStruct((T, NPROJ), BF16), jax.ShapeDtypeStruct((1, C), F32)],
        input_output_aliases={6: 0},
        compiler_params=_params(("arbitrary",), 32))


def _v_bwd(d3, dproj):
    tm = 512

    def body(d0, d1, d2, dp_in, dp_ref):
        del dp_in
        dp_ref[...] = (d0[...] + d1[...] + d2[...]).astype(BF16)

    blk = pl.BlockSpec((tm, C), lambda i: (i, 0))
    return _call(
        body, *d3, dproj, name="v_bwd", grid=(T // tm,),
        in_specs=[blk, blk, blk, ANY],
        out_specs=[pl.BlockSpec((tm, C), lambda i: (i, 4))],
        out_shape=[jax.ShapeDtypeStruct((T, NPROJ), BF16)],
        input_output_aliases={3: 0},
        compiler_params=_params(("parallel",), 32))[0]


def _adamw(w, g, m, v):
    m = ADAM_B1 * m + (1.0 - ADAM_B1) * g
    v = ADAM_B2 * v + (1.0 - ADAM_B2) * (g * g)
    m_hat = m / (1.0 - ADAM_B1 ** ADAM_STEP)
    v_hat = v / (1.0 - ADAM_B2 ** ADAM_STEP)
    delta = -ADAM_LR * (m_hat / (jnp.sqrt(v_hat) + ADAM_EPS) + ADAM_WD * w)
    return delta, m, v


def _row_block(shape):
    rows = shape[0]
    for cand in (256, 128, 64, 88, 32, 8):
        if rows % cand == 0 and cand * shape[1] * 4 <= (2 << 20):
            return cand
    return 8


def _partial_sum(own, recv, name):
    br = _row_block(own.shape)
    cols = own.shape[1]

    def body(o_ref, r_ref, p_ref):
        p_ref[...] = ((o_ref[...] + r_ref[0].astype(F32)) + r_ref[1].astype(F32)) + r_ref[2].astype(F32)

    blk = pl.BlockSpec((br, cols), lambda i: (i, 0))
    return _call(
        body, own, recv, name=name, grid=(own.shape[0] // br,),
        in_specs=[blk, pl.BlockSpec((3, br, cols), lambda i: (0, i, 0))], out_specs=[blk],
        out_shape=[jax.ShapeDtypeStruct(own.shape, F32)],
        compiler_params=_params(("parallel",), 32))[0]


def _adamw_mat(p_own, p_sib, w, m, v, name):
    br = _row_block(w.shape)
    cols = w.shape[1]

    def body(a_ref, b_ref, w_ref, m_ref, v_ref, g_ref, d_ref, nm_ref, nv_ref):
        g = a_ref[...] + b_ref[...]
        delta, nm, nv = _adamw(w_ref[...], g, m_ref[...], v_ref[...])
        g_ref[...] = g
        d_ref[...] = delta
        nm_ref[...] = nm
        nv_ref[...] = nv

    blk = pl.BlockSpec((br, cols), lambda i: (i, 0))
    return _call(
        body, p_own, p_sib, w, m, v, name=name, grid=(w.shape[0] // br,),
        in_specs=[blk] * 5, out_specs=[blk] * 4,
        out_shape=[jax.ShapeDtypeStruct(w.shape, F32)] * 4,
        compiler_params=_params(("parallel",), 40))


def _vec_reduce(vrecv):
    def body(v_ref, o_ref):
        acc = v_ref[0]
        for r in range(1, N_DEV):
            acc = acc + v_ref[r]
        o_ref[...] = acc

    return pl.pallas_call(
        body, name="vec_reduce",
        out_shape=jax.ShapeDtypeStruct((VPACK_ROWS, D), F32),
        compiler_params=_params((), 32),
    )(vrecv)


def _adamw_small(w, g, m, v):
    def body(w_ref, g_ref, m_ref, v_ref, d_ref, nm_ref, nv_ref):
        delta, nm, nv = _adamw(w_ref[...], g_ref[...], m_ref[...], v_ref[...])
        d_ref[...] = delta
        nm_ref[...] = nm
        nv_ref[...] = nv

    return pl.pallas_call(
        body, name="adamw_small",
        out_shape=[jax.ShapeDtypeStruct(w.shape, F32)] * 3,
        compiler_params=_params((), 32),
    )(w, g, m, v)


def _pack(parts, rows):
    flat = jnp.concatenate([p.reshape(-1) for p in parts])
    return jnp.pad(flat, (0, rows * D - flat.shape[0])).reshape(rows, D)


def _unpack(packed, shapes):
    flat = packed.reshape(-1)
    out, off = [], 0
    for shp in shapes:
        size = 1
        for s in shp:
            size *= s
        out.append(flat[off:off + size].reshape(shp))
        off += size
    return out


def _no_comm(shards, row_sharded):
    del row_sharded
    return None, lambda res, n: (res, shards)


def _with_comm(shards, row_sharded):
    rider = _gather_rider(shards, row_sharded)
    return rider, lambda res, n: (res[:n], res[n:])


def _local_step(x, target, norm1_g, conv_w, conv_b, cn_g, cn_b, q_norm_g, k_norm_g, norm2_g, ffconv_w, ffconv_b,
                w_in, late_weights, comm=True):
    row = lambda a: a.reshape(1, -1)
    head_of = jnp.arange(C) // HEAD
    bd = (head_of[:, None] == head_of[None, :]).astype(BF16)
    qg = row(jnp.tile(q_norm_g, C // HEAD) * (HEAD ** -0.5))
    kg = row(jnp.tile(k_norm_g, C // HEAD))
    biases = [_alibi_tables(d) for d in PATTERN_DILATIONS]
    gather = _with_comm if comm else _no_comm
    grad_rider = (lambda g, rs: _grad_rider(g[1], g[0], rs)) if comm else (lambda g, rs: None)

    rider, split = gather(late_weights[:2], (True, False))
    (h, proj), (w_out, w_up) = split(_proj_fwd(x, row(norm1_g), w_in, rider), 2)
    rider, split = gather(late_weights[2:], (True,))
    (cat, cv), (w_down,) = split(_conv_fwd(proj, conv_w, row(conv_b), row(cn_g), row(cn_b), rider), 2)
    qn, kn = _qkv_prep(proj, qg, kg, bd)
    fwd = [_attn_fwd(qn, kn, proj, biases[i], d) for i, d in enumerate(PATTERN_DILATIONS)]
    cat, o_f32, lg = _attn_merge([f[0] for f in fwd], [f[1] for f in fwd], cat)
    x1, h2, up = _out_up(x, cat, w_out, row(norm2_g), w_up)
    upc, act, dy, loss_acc = _ffn_down(up, ffconv_w, row(ffconv_b), w_down, x1, target)

    dup, gff = _down_bwd(dy, w_down, up, upc, ffconv_w)
    gw_down = _weight_grad(act, dy, DFF // 2, D, "grad_w_down")
    res = _norm_bwd_mm(dup, w_up, x1, dy, row(norm2_g), NUP // 4, "up_bwd", grad_rider(gw_down, True))
    (dx1, g_norm2), ex_down = res[:2], res[2:]
    gw_up = _weight_grad(h2, dup, D, NUP // 4, "grad_w_up")
    dcat = _outproj_bwd(dx1, w_out)
    gw_out = _weight_grad(cat, dx1, D, D, "grad_w_out")
    res = _conv_bwd(dcat, cv, proj, conv_w, row(cn_g), row(cn_b), grad_rider(gw_up, False))
    (dproj, gconv_vec, gconv_w), ex_up = res[:3], res[3:]
    dl = _attn_bwd_prep(dcat, o_f32, bd)
    bwd, ex_out = [], []
    for i, d in enumerate(PATTERN_DILATIONS):
        res = _attn_bwd(qn, kn, proj, dcat, lg, dl, biases[i], d, grad_rider(gw_out, True) if i == 0 else None)
        bwd.append(res[:3])
        ex_out = res[3:] if i == 0 else ex_out
    dproj, gq_lane = _qk_norm_bwd([b[0] for b in bwd], proj, 2, qg, bd, dproj, "q_norm_bwd")
    dproj, gk_lane = _qk_norm_bwd([b[1] for b in bwd], proj, 3, kg, bd, dproj, "k_norm_bwd")
    dproj = _v_bwd([b[2] for b in bwd], dproj)
    gw_in = _weight_grad(h, dproj, D, NPROJ // 4, "grad_w_in")
    res = _norm_bwd_mm(dproj, w_in, x, dx1, row(norm1_g), NPROJ // 2, "in_bwd", grad_rider(gw_in, False))
    (dx, g_norm1), ex_in = res[:2], res[2:]

    loss = loss_acc[0, 0] * (0.5 / D)
    g_qg = jnp.sum(gq_lane.reshape(C // HEAD, HEAD), axis=0) * (HEAD ** -0.5)
    g_kg = jnp.sum(gk_lane.reshape(C // HEAD, HEAD), axis=0)
    small = [g_norm1[0], gconv_vec[2], gconv_vec[0], gconv_vec[1], g_qg, g_kg, g_norm2[0], gff[3],
             gconv_w[:CONV_K], gff[:FF_K]]
    mats = [ex_in, ex_out, ex_up, ex_down] if comm else [gw_in, gw_out, gw_up, gw_down]
    return loss, dx, mats, small


def kernel(x, norm1_g, w_in, conv_w, conv_b, cn_g, cn_b, q_norm_g, k_norm_g, w_out, norm2_g, w_up, ffconv_w, ffconv_b, w_down, loss_target, m_norm1_g, m_w_in, m_conv_w, m_conv_b, m_cn_g, m_cn_b, m_q_norm_g, m_k_norm_g, m_w_out, m_norm2_g, m_w_up, m_ffconv_w, m_ffconv_b, m_w_down, v_norm1_g, v_w_in, v_conv_w, v_conv_b, v_cn_g, v_cn_b, v_q_norm_g, v_k_norm_g, v_w_out, v_norm2_g, v_w_up, v_ffconv_w, v_ffconv_b, v_w_down):
    chip = 2 * lax.axis_index("x") + lax.axis_index("y")

    w_in_full, conv_w_full, ffconv_w_full = _gather_now([w_in.astype(BF16), conv_w, ffconv_w], (False, False, False))
    loss, dx, mats, small = _local_step(
        x[0], loss_target[0], norm1_g, conv_w_full, conv_b, cn_g, cn_b, q_norm_g, k_norm_g, norm2_g,
        ffconv_w_full, ffconv_b, w_in_full, [w.astype(BF16) for w in (w_out, w_up, w_down)])

    names = ("w_in", "w_out", "w_up", "w_down")
    parts = [_partial_sum(own, recv, "partial_" + names[k]) for k, (recv, own) in enumerate(mats)]
    sib, vrecv = _final_exchange(parts, _pack(small, VPACK_ROWS))
    ws = (w_in, w_out, w_up, w_down)
    ms = (m_w_in, m_w_out, m_w_up, m_w_down)
    vs = (v_w_in, v_w_out, v_w_up, v_w_down)
    mat = [_adamw_mat(parts[k], sib[k], ws[k], ms[k], vs[k], "adamw_" + names[k]) for k in range(4)]

    vsum = _vec_reduce(vrecv)
    vec_shapes = [(D,), (C,), (C,), (C,), (HEAD,), (HEAD,), (D,), (NUP,), (CONV_K, C), (FF_K, NUP)]
    gsmall = _unpack(vsum, vec_shapes)
    g_conv_w = lax.dynamic_slice_in_dim(gsmall[8], chip * (C // N_CHIPS), C // N_CHIPS, axis=1)
    g_ffconv_w = lax.dynamic_slice_in_dim(gsmall[9], chip * (NUP // N_CHIPS), NUP // N_CHIPS, axis=1)
    gs = gsmall[:8] + [g_conv_w, g_ffconv_w]
    w_s = [norm1_g, conv_b, cn_g, cn_b, q_norm_g, k_norm_g, norm2_g, ffconv_b, conv_w, ffconv_w]
    m_s = [m_norm1_g, m_conv_b, m_cn_g, m_cn_b, m_q_norm_g, m_k_norm_g, m_norm2_g, m_ffconv_b, m_conv_w, m_ffconv_w]
    v_s = [v_norm1_g, v_conv_b, v_cn_g, v_cn_b, v_q_norm_g, v_k_norm_g, v_norm2_g, v_ffconv_b, v_conv_w, v_ffconv_w]
    shapes_s = [a.shape for a in w_s]
    d_p, m_p, v_p = _adamw_small(_pack(w_s, SPACK_ROWS), _pack(gs, SPACK_ROWS), _pack(m_s, SPACK_ROWS),
                                 _pack(v_s, SPACK_ROWS))
    d_s, nm_s, nv_s = _unpack(d_p, shapes_s), _unpack(m_p, shapes_s), _unpack(v_p, shapes_s)

    def ordered(sm, mt):
        return [sm[0], mt[0], sm[8], sm[1], sm[2], sm[3], sm[4], sm[5], mt[1], sm[6], mt[2], sm[9], sm[7], mt[3]]

    loss_all = lax.psum(loss, ("x", "y", "c"))
    grads = ordered(gs, [r[0] for r in mat])
    deltas = ordered(d_s, [r[1] for r in mat])
    new_m = ordered(nm_s, [r[2] for r in mat])
    new_v = ordered(nv_s, [r[3] for r in mat])
    return (loss_all, dx[None], *grads, *deltas, *new_m, *new_v)
```

```python
import types

import jax
import jax.numpy as jnp
from jax import lax
from jax.experimental import pallas as pl
from jax.experimental.pallas import tpu as pltpu

T = 8192
D = 1024
C = 512
NPROJ = 2560
DFF = 2816
NUP = 2 * DFF
CONV_K = 31
FF_K = 3
HEAD = 64
EPS = 1e-6
NEG = -1e30
N_CHIPS = 4
N_DEV = 8
PATTERN_DILATIONS = (1, 4, 16)
QB = 128

ADAM_LR = 0.001
ADAM_B1 = 0.9
ADAM_B2 = 0.999
ADAM_EPS = 1e-08
ADAM_WD = 0.01
ADAM_STEP = 10

F32 = jnp.float32
BF16 = jnp.bfloat16
MESH = pl.DeviceIdType.MESH
ANY = pl.BlockSpec(memory_space=pl.ANY)

VPACK_ROWS = 48
SPACK_ROWS = 24


def _params(sem, vmem_mb):
    return pltpu.CompilerParams(dimension_semantics=sem, vmem_limit_bytes=vmem_mb << 20)


def _resident(shape):
    return pl.BlockSpec(shape, lambda i: (0, 0), pipeline_mode=pl.Buffered(1))


def _nt(a, b):
    return lax.dot_general(a, b, (((1,), (1,)), ((), ())), preferred_element_type=F32)


def _tn_dot(a, b):
    return lax.dot_general(a, b, (((0,), (0,)), ((), ())), preferred_element_type=F32)


def _sigmoid(x):
    return 1.0 / (1.0 + jnp.exp(-x))


def _segsum(x, bd):
    hi = x.astype(BF16)
    lo = (x - hi.astype(F32)).astype(BF16)
    return (jnp.dot(hi, bd, preferred_element_type=F32)
            + jnp.dot(lo, bd, preferred_element_type=F32))


def _place():
    x, y, c = lax.axis_index("x"), lax.axis_index("y"), lax.axis_index("c")
    chips = [(1 - x, y), (x, 1 - y), (1 - x, 1 - y)]
    return x, y, c, chips


def _block_of(ref, shard_shape, row_sharded, s):
    r, cdim = shard_shape
    if row_sharded:
        return ref.at[pl.ds(s * r, r), :]
    return ref.at[:, pl.ds(s * cdim, cdim)]


def _full_shape(shard_shape, row_sharded):
    r, cdim = shard_shape
    return (r * N_CHIPS, cdim) if row_sharded else (r, cdim * N_CHIPS)


def _gather_rider(shards, row_sharded, peers=(0, 1, 2), into=None):
    n = len(shards)
    shapes = [a.shape for a in shards]

    def copies(ins, outs, sems):
        send_sems, recv_sems, local_sems = sems
        x, y, c, chips = _place()
        me = 2 * x + y
        place = lambda k, s: _block_of(outs[k], shapes[k], row_sharded[k], s)
        local = []
        if into is None:
            local = [pltpu.make_async_copy(ins[k], place(k, me), local_sems.at[k]) for k in range(n)]
        sends, recvs = [], []
        for k in range(n):
            for j in peers:
                px, py = chips[j]
                sem = dict(send_sem=send_sems.at[3 * k + j], recv_sem=recv_sems.at[3 * k + j],
                           device_id=(px, py, c), device_id_type=MESH)
                sends.append(pltpu.make_async_remote_copy(src_ref=ins[k], dst_ref=place(k, me), **sem))
                recvs.append(pltpu.make_async_remote_copy(src_ref=ins[k], dst_ref=place(k, 2 * px + py), **sem))
        return local, sends, recvs

    return types.SimpleNamespace(
        operands=list(shards) + list(into or []), copies=copies,
        aliases={n + k: k for k in range(n)} if into is not None else {},
        out_shape=[jax.ShapeDtypeStruct(_full_shape(s, rs), a.dtype) for s, rs, a in zip(shapes, row_sharded, shards)],
        sems=[pltpu.SemaphoreType.DMA((3 * n,)), pltpu.SemaphoreType.DMA((3 * n,)), pltpu.SemaphoreType.DMA((n,))])


def _grad_rider(g_bf16, g_f32, row_sharded):
    shard = tuple(d // N_CHIPS if (i == 0) == row_sharded else d for i, d in enumerate(g_f32.shape))

    def copies(ins, outs, sems):
        send_sems, recv_sems, local_sems = sems
        gb, gf = ins
        rec, own = outs
        x, y, c, chips = _place()
        me = 2 * x + y
        local = [pltpu.make_async_copy(_block_of(gf, shard, row_sharded, me), own, local_sems.at[0])]
        sends, recvs = [], []
        for j, (px, py) in enumerate(chips):
            sem = dict(send_sem=send_sems.at[j], recv_sem=recv_sems.at[j], device_id=(px, py, c), device_id_type=MESH)
            sends.append(pltpu.make_async_remote_copy(
                src_ref=_block_of(gb, shard, row_sharded, 2 * px + py), dst_ref=rec.at[j], **sem))
            recvs.append(pltpu.make_async_remote_copy(
                src_ref=_block_of(gb, shard, row_sharded, me), dst_ref=rec.at[j], **sem))
        return local, sends, recvs

    return types.SimpleNamespace(
        operands=[g_bf16, g_f32], copies=copies, aliases={},
        out_shape=[jax.ShapeDtypeStruct((3,) + shard, BF16), jax.ShapeDtypeStruct(shard, F32)],
        sems=[pltpu.SemaphoreType.DMA((3,)), pltpu.SemaphoreType.DMA((3,)), pltpu.SemaphoreType.DMA((1,))])


def _rider_start(rider, ins, outs, sems):
    local, sends, _ = rider.copies(ins, outs, sems)
    for cp in local + sends:
        cp.start()


def _rider_wait(rider, ins, outs, sems):
    local, sends, recvs = rider.copies(ins, outs, sems)
    for cp in recvs:
        cp.wait_recv()
    for cp in sends:
        cp.wait_send()
    for cp in local:
        cp.wait()


def _call(body, *operands, rider=None, name, grid, in_specs, out_specs, out_shape, scratch_shapes=(),
          compiler_params, input_output_aliases=None):
    operands = [pltpu.with_memory_space_constraint(a, pltpu.HBM) for a in operands]
    if rider is None:
        return pl.pallas_call(
            body, name=name, grid=grid, in_specs=list(in_specs), out_specs=list(out_specs), out_shape=list(out_shape),
            scratch_shapes=list(scratch_shapes), compiler_params=compiler_params,
            input_output_aliases=input_output_aliases or {})(*operands)
    n_in, n_out, n_scr = len(in_specs), len(out_specs), len(scratch_shapes)
    r_in, r_out = len(rider.operands), len(rider.out_shape)

    def riding(*refs):
        refs = list(refs)
        ins, refs = refs[:n_in], refs[n_in:]
        r_ins, refs = refs[:r_in], refs[r_in:]
        outs, refs = refs[:n_out], refs[n_out:]
        r_outs, refs = refs[:r_out], refs[r_out:]
        scr, sems = refs[:n_scr], refs[n_scr:]
        first = pl.program_id(0) == 0
        last = pl.program_id(0) == grid[0] - 1
        for axis in range(1, len(grid)):
            first = first & (pl.program_id(axis) == 0)
            last = last & (pl.program_id(axis) == grid[axis] - 1)

        @pl.when(first)
        def _():
            _rider_start(rider, r_ins, r_outs, sems)

        body(*ins, *outs, *scr)

        @pl.when(last)
        def _():
            _rider_wait(rider, r_ins, r_outs, sems)

    return pl.pallas_call(
        riding, name=name, grid=grid, in_specs=list(in_specs) + [ANY] * r_in,
        out_specs=list(out_specs) + [ANY] * r_out, out_shape=list(out_shape) + list(rider.out_shape),
        scratch_shapes=list(scratch_shapes) + list(rider.sems), compiler_params=compiler_params,
        input_output_aliases={**(input_output_aliases or {}),
                              **{n_in + i: n_out + o for i, o in rider.aliases.items()}})(
            *operands, *[pltpu.with_memory_space_constraint(a, pltpu.HBM) for a in rider.operands])


def _gather_now(shards, row_sharded):
    rider = _gather_rider(shards, row_sharded)
    n = len(shards)

    def body(*refs):
        ins, outs, sems = refs[:n], refs[n:2 * n], refs[2 * n:]
        _rider_start(rider, ins, outs, sems)
        _rider_wait(rider, ins, outs, sems)

    return pl.pallas_call(
        body, name="gather_first", out_shape=rider.out_shape, in_specs=[ANY] * n, out_specs=[ANY] * n,
        scratch_shapes=rider.sems)(*shards)


def _final_exchange(parts, vpack):
    def body(p0, p1, p2, p3, v_ref, o0, o1, o2, o3, vr_ref, send_sems, recv_sems, vsend_sems, vrecv_sems, local_sem):
        x, y, c, _ = _place()
        me = 4 * x + 2 * y + c
        mine = pltpu.make_async_copy(v_ref, vr_ref.at[me], local_sem)
        mine.start()
        copies = [pltpu.make_async_remote_copy(
            src_ref=p, dst_ref=o, send_sem=send_sems.at[k], recv_sem=recv_sems.at[k],
            device_id=(x, y, 1 - c), device_id_type=MESH)
            for k, (p, o) in enumerate(zip((p0, p1, p2, p3), (o0, o1, o2, o3)))]
        flips = [(fx, fy, fc) for fx in (0, 1) for fy in (0, 1) for fc in (0, 1)][1:]
        recvs = []
        for r, (fx, fy, fc) in enumerate(flips):
            peer = (x ^ fx, y ^ fy, c ^ fc)
            sem = dict(send_sem=vsend_sems.at[r], recv_sem=vrecv_sems.at[r], device_id=peer, device_id_type=MESH)
            copies.append(pltpu.make_async_remote_copy(src_ref=v_ref, dst_ref=vr_ref.at[me], **sem))
            recvs.append(pltpu.make_async_remote_copy(
                src_ref=v_ref, dst_ref=vr_ref.at[4 * peer[0] + 2 * peer[1] + peer[2]], **sem))
        for cp in copies:
            cp.start()
        for cp in copies[:4]:
            cp.wait_recv()
        for cp in recvs:
            cp.wait_recv()
        for cp in copies:
            cp.wait_send()
        mine.wait()

    res = pl.pallas_call(
        body, name="final_exchange",
        out_shape=[jax.ShapeDtypeStruct(p.shape, F32) for p in parts]
        + [jax.ShapeDtypeStruct((N_DEV, VPACK_ROWS, D), F32)],
        in_specs=[ANY] * 5, out_specs=[ANY] * 5,
        scratch_shapes=[pltpu.SemaphoreType.DMA((4,)), pltpu.SemaphoreType.DMA((4,)),
                        pltpu.SemaphoreType.DMA((7,)), pltpu.SemaphoreType.DMA((7,)), pltpu.SemaphoreType.DMA],
    )(*parts, vpack)
    return res[:4], res[4]


def _proj_fwd(x, g1, w_in, rider):
    tm, tn = 512, 640

    def body(x_ref, g_ref, w_ref, h_ref, p_ref):
        xv = x_ref[...]
        r = lax.rsqrt(jnp.mean(xv * xv, axis=-1, keepdims=True) + EPS)
        h_ref[...] = (xv * r * g_ref[...]).astype(BF16)
        for j in range(NPROJ // tn):
            cols = slice(j * tn, (j + 1) * tn)
            p_ref[:, cols] = jnp.dot(h_ref[...], w_ref[:, cols], preferred_element_type=F32)

    return _call(
        body, x, g1, w_in, rider=rider, name="proj_fwd", grid=(T // tm,),
        in_specs=[pl.BlockSpec((tm, D), lambda i: (i, 0)), pl.BlockSpec((1, D), lambda i: (0, 0)),
                  _resident((D, NPROJ))],
        out_specs=[pl.BlockSpec((tm, D), lambda i: (i, 0)), pl.BlockSpec((tm, NPROJ), lambda i: (i, 0))],
        out_shape=[jax.ShapeDtypeStruct((T, D), BF16), jax.ShapeDtypeStruct((T, NPROJ), F32)],
        compiler_params=_params(("arbitrary",), 40))


CONV_TM = 512
CONV_HALO = 32
CONV_RB = 64


def _fill_shifts(sh_ref, rows):
    for b in range(1, 8):
        sh_ref[b, 0:rows - 8, :] = sh_ref[0, b:b + rows - 8, :]


def _shifted(sh_ref, off, rb):
    a, b = divmod(off, 8)
    return sh_ref[b, 8 * a:8 * a + rb, :]


def _conv_fwd(proj, conv_w, conv_b, cn_g, cn_b, rider):
    tm, hl, rb = CONV_TM, CONV_HALO, CONV_RB
    per = tm // hl

    def body(av_ref, ag_ref, hv_ref, hg_ref, w_ref, b_ref, g_ref, bb_ref, cat_ref, cv_ref, sh_ref):
        i = pl.program_id(0)
        glu_h = hv_ref[...] * _sigmoid(hg_ref[...])
        sh_ref[0, 0:hl, :] = jnp.where(i > 0, glu_h, 0.0)
        sh_ref[0, hl:, :] = av_ref[...] * _sigmoid(ag_ref[...])
        _fill_shifts(sh_ref, tm + hl)
        for r0 in range(0, tm, rb):
            acc = jnp.zeros((rb, C), F32) + b_ref[...]
            for k in range(CONV_K):
                acc = acc + w_ref[k:k + 1, :] * _shifted(sh_ref, r0 + hl - (CONV_K - 1) + k, rb)
            mu = jnp.mean(acc, axis=-1, keepdims=True)
            xc = acc - mu
            var = jnp.mean(xc * xc, axis=-1, keepdims=True)
            ln = xc * lax.rsqrt(var + EPS) * g_ref[...] + bb_ref[...]
            cv_ref[r0:r0 + rb, :] = acc
            cat_ref[r0:r0 + rb, :] = (ln * _sigmoid(ln)).astype(BF16)

    halo = lambda col: pl.BlockSpec((hl, C), lambda i: (jnp.maximum(i * per - 1, 0), col))
    vec = pl.BlockSpec((1, C), lambda i: (0, 0))
    return _call(
        body, proj, proj, proj, proj, conv_w, conv_b, cn_g, cn_b, rider=rider, name="conv_fwd", grid=(T // tm,),
        in_specs=[pl.BlockSpec((tm, C), lambda i: (i, 0)), pl.BlockSpec((tm, C), lambda i: (i, 1)),
                  halo(0), halo(1), pl.BlockSpec((CONV_K, C), lambda i: (0, 0)), vec, vec, vec],
        out_specs=[pl.BlockSpec((tm, C), lambda i: (i, 0)), pl.BlockSpec((tm, C), lambda i: (i, 0))],
        out_shape=[jax.ShapeDtypeStruct((T, D), BF16), jax.ShapeDtypeStruct((T, C), F32)],
        scratch_shapes=[pltpu.VMEM((8, tm + hl, C), F32)],
        compiler_params=_params(("arbitrary",), 40))


def _qkv_prep(proj, qg, kg, bd):
    tm = 512

    def body(q_ref, k_ref, qg_ref, kg_ref, bd_ref, qn_ref, kn_ref):
        for src, g, dst in ((q_ref, qg_ref, qn_ref), (k_ref, kg_ref, kn_ref)):
            xv = src[...]
            ms = _segsum(xv * xv, bd_ref[...]) * (1.0 / HEAD)
            dst[...] = xv * lax.rsqrt(ms + EPS) * g[...]

    col = lambda c: pl.BlockSpec((tm, C), lambda i: (i, c))
    vec = pl.BlockSpec((1, C), lambda i: (0, 0))
    out = pl.BlockSpec((tm, C), lambda i: (i, 0))
    return _call(
        body, proj, proj, qg, kg, bd, name="qkv_prep", grid=(T // tm,),
        in_specs=[col(2), col(3), vec, vec, pl.BlockSpec((C, C), lambda i: (0, 0))],
        out_specs=[out, out],
        out_shape=[jax.ShapeDtypeStruct((T, C), F32)] * 2,
        compiler_params=_params(("parallel",), 32))


def _stack_heads(a):
    lane = lax.broadcasted_iota(jnp.int32, a.shape, 1)
    zero = jnp.zeros_like(a)
    return jnp.concatenate([jnp.where(lane < HEAD, a, zero), jnp.where(lane >= HEAD, a, zero)], axis=0)


def _unstack_heads(a2):
    lane = lax.broadcasted_iota(jnp.int32, (QB, 2 * HEAD), 1)
    return jnp.where(lane < HEAD, a2[:QB], a2[QB:])


def _stack_cols(a):
    return jnp.concatenate([a[:, 0:1], a[:, HEAD:HEAD + 1]], axis=0)


ATT_WIN = 2048
V_COL = 4 * C // (2 * HEAD)
DO_COL = C // (2 * HEAD)


def _attn_geometry(d):
    sl = ATT_WIN // d
    return sl, sl // QB, QB * d


def _stream(ref, r, n, d):
    return ref[pl.ds(r, n, stride=d), :] if d > 1 else ref[pl.ds(r, n), :]


def _alibi_tables(d):
    qi = jnp.arange(QB)[:, None]
    kj = jnp.arange(2 * QB)[None, :]
    delta = qi + QB - kj
    band = (delta >= 0) & (delta <= QB)
    dist = (delta * d).astype(F32)
    heads = jnp.arange(8, dtype=F32)
    slopes = 2.0 ** (-(heads + 1.0))
    t = jnp.where(band[None], -slopes[:, None, None] * dist[None], NEG)
    return t.reshape(4, 2 * QB, 2 * QB)


def _attn_specs(d):
    _, _, hr = _attn_geometry(d)
    per = ATT_WIN // hr
    main = lambda off: pl.BlockSpec((ATT_WIN, 2 * HEAD), lambda cb, n: (n, off + cb))
    prev = lambda off: pl.BlockSpec((hr, 2 * HEAD), lambda cb, n: (jnp.maximum(n * per - 1, 0), off + cb))
    nxt = lambda off: pl.BlockSpec((hr, 2 * HEAD), lambda cb, n: (jnp.minimum((n + 1) * per, T // hr - 1), off + cb))
    bias = pl.BlockSpec((None, 2 * QB, 2 * QB), lambda cb, n: (cb, 0, 0))
    return main, prev, nxt, bias


def _attn_fwd(qn, kn, proj, bias, d, rider=None):
    sl, nb, hr = _attn_geometry(d)
    slk = QB + sl

    def body(q_ref, k_ref, v_ref, kh_ref, vh_ref, bias_ref, o_ref, l_ref, qs, ks, vs, os_, ls):
        n = pl.program_id(1)
        for r in range(d):
            qs[r * sl:(r + 1) * sl, :] = _stream(q_ref, r, sl, d).astype(BF16)
            for dst, halo, src in ((ks, kh_ref, k_ref), (vs, vh_ref, v_ref)):
                dst[r * slk:r * slk + QB, :] = _stream(halo, r, QB, d).astype(BF16)
                dst[r * slk + QB:(r + 1) * slk, :] = _stream(src, r, sl, d).astype(BF16)
        col = lax.broadcasted_iota(jnp.int32, (2 * QB, 2 * QB), 1)
        for r in range(d):
            for b in range(nb):
                rows = slice(r * sl + b * QB, r * sl + (b + 1) * QB)
                keys = slice(r * slk + b * QB, r * slk + (b + 2) * QB)
                s = _nt(_stack_heads(qs[rows, :]), ks[keys, :]) + bias_ref[...]
                if b == 0:
                    s = jnp.where((col < QB) & (n == 0), NEG, s)
                m = jnp.max(s, axis=-1, keepdims=True)
                p = jnp.exp(s - m)
                den = jnp.sum(p, axis=-1, keepdims=True)
                pv = jnp.dot(p.astype(BF16), vs[keys, :], preferred_element_type=F32)
                os_[rows, :] = _unstack_heads(pv / den)
                ls[rows, :] = _unstack_heads(jnp.broadcast_to(m + jnp.log(den), (2 * QB, 2 * HEAD)))
        for r in range(d):
            if d > 1:
                o_ref[pl.ds(r, sl, stride=d), :] = os_[r * sl:(r + 1) * sl, :]
                l_ref[pl.ds(r, sl, stride=d), :] = ls[r * sl:(r + 1) * sl, :]
            else:
                o_ref[...] = os_[...]
                l_ref[...] = ls[...]

    main, prev, _, bias_spec = _attn_specs(d)
    lanes = 2 * HEAD
    return _call(
        body, qn, kn, proj, kn, proj, bias, rider=rider, name=f"attn_fwd_d{d}", grid=(C // lanes, T // ATT_WIN),
        in_specs=[main(0), main(0), main(V_COL), prev(0), prev(V_COL), bias_spec],
        out_specs=[main(0), main(0)],
        out_shape=[jax.ShapeDtypeStruct((T, C), F32)] * 2,
        scratch_shapes=[pltpu.VMEM((ATT_WIN, lanes), BF16), pltpu.VMEM((ATT_WIN + hr, lanes), BF16),
                        pltpu.VMEM((ATT_WIN + hr, lanes), BF16), pltpu.VMEM((ATT_WIN, lanes), F32),
                        pltpu.VMEM((ATT_WIN, lanes), F32)],
        compiler_params=_params(("arbitrary", "arbitrary"), 40))


def _attn_merge(outs, lses, cat):
    tm = 512

    def body(o0, o1, o2, l0, l1, l2, cat_in, cat_ref, of_ref, lg_ref):
        del cat_in
        a, b, c = l0[...], l1[...], l2[...]
        m = jnp.maximum(jnp.maximum(a, b), c)
        e0, e1, e2 = jnp.exp(a - m), jnp.exp(b - m), jnp.exp(c - m)
        den = e0 + e1 + e2
        o = (e0 * o0[...] + e1 * o1[...] + e2 * o2[...]) / den
        of_ref[...] = o
        cat_ref[...] = o.astype(BF16)
        lg_ref[...] = m + jnp.log(den)

    blk = pl.BlockSpec((tm, C), lambda i: (i, 0))
    return _call(
        body, *outs, *lses, cat, name="attn_merge", grid=(T // tm,),
        in_specs=[blk] * 6 + [ANY],
        out_specs=[pl.BlockSpec((tm, C), lambda i: (i, 1)), blk, blk],
        out_shape=[jax.ShapeDtypeStruct((T, D), BF16), jax.ShapeDtypeStruct((T, C), F32),
                   jax.ShapeDtypeStruct((T, C), F32)],
        input_output_aliases={6: 0},
        compiler_params=_params(("parallel",), 32))


def _out_up(x, cat, w_out, g2, w_up, rider):
    tm, tn = 256, NUP // 4

    def body(x_ref, cat_ref, wo_ref, g_ref, wu_ref, x1_ref, h2_ref, up_ref):
        x1 = x_ref[...] + jnp.dot(cat_ref[...], wo_ref[...], preferred_element_type=F32)
        x1_ref[...] = x1
        r = lax.rsqrt(jnp.mean(x1 * x1, axis=-1, keepdims=True) + EPS)
        h2_ref[...] = (x1 * r * g_ref[...]).astype(BF16)
        for j in range(NUP // tn):
            cols = slice(j * tn, (j + 1) * tn)
            up_ref[:, cols] = jnp.dot(h2_ref[...], wu_ref[:, cols], preferred_element_type=F32)

    row = pl.BlockSpec((tm, D), lambda i: (i, 0))
    return _call(
        body, x, cat, w_out, g2, w_up, rider=rider, name="out_up", grid=(T // tm,),
        in_specs=[row, row, _resident((D, D)), pl.BlockSpec((1, D), lambda i: (0, 0)), _resident((D, NUP))],
        out_specs=[row, row, pl.BlockSpec((tm, NUP), lambda i: (i, 0))],
        out_shape=[jax.ShapeDtypeStruct((T, D), F32), jax.ShapeDtypeStruct((T, D), BF16),
                   jax.ShapeDtypeStruct((T, NUP), F32)],
        compiler_params=_params(("arbitrary",), 48))


FF_TM = 256
FF_HALO = 8
FF_CW = 256


def _ff_conv(ext_ref, fw_ref, fb_ref, col0, tm):
    cols = slice(col0, col0 + FF_CW)
    acc = fb_ref[:, cols] + fw_ref[0:1, cols] * ext_ref[FF_HALO - 2:FF_HALO - 2 + tm, cols]
    acc = acc + fw_ref[1:2, cols] * ext_ref[FF_HALO - 1:FF_HALO - 1 + tm, cols]
    return acc + fw_ref[2:3, cols] * ext_ref[FF_HALO:FF_HALO + tm, cols]


def _ffn_down(up, ffconv_w, ffconv_b, w_down, x1, target):
    tm, hl = FF_TM, FF_HALO
    per = tm // hl

    def body(up_ref, uh_ref, fw_ref, fb_ref, wd_ref, x1_ref, tg_ref, upc_ref, act_ref, dy_ref, loss_ref, ext_ref):
        i = pl.program_id(0)
        ext_ref[0:hl, :] = jnp.where(i > 0, uh_ref[...], 0.0)
        ext_ref[hl:, :] = up_ref[...]
        for c in range(DFF // FF_CW):
            gcols = slice(c * FF_CW, (c + 1) * FF_CW)
            vcols = slice(DFF + c * FF_CW, DFF + (c + 1) * FF_CW)
            gate = _ff_conv(ext_ref, fw_ref, fb_ref, c * FF_CW, tm)
            val = _ff_conv(ext_ref, fw_ref, fb_ref, DFF + c * FF_CW, tm)
            upc_ref[:, gcols] = gate
            upc_ref[:, vcols] = val
            act_ref[:, gcols] = (gate * _sigmoid(gate) * val).astype(BF16)
        y = x1_ref[...] + jnp.dot(act_ref[...], wd_ref[...], preferred_element_type=F32)
        err = y - tg_ref[...]
        dy_ref[...] = err * (1.0 / D)

        @pl.when(i == 0)
        def _():
            loss_ref[...] = jnp.zeros_like(loss_ref)
        loss_ref[...] += jnp.sum(err * err)

    row = pl.BlockSpec((tm, D), lambda i: (i, 0))
    wide = pl.BlockSpec((tm, NUP), lambda i: (i, 0))
    return _call(
        body, up, up, ffconv_w, ffconv_b, w_down, x1, target, name="ffn_down", grid=(T // tm,),
        in_specs=[wide, pl.BlockSpec((hl, NUP), lambda i: (jnp.maximum(i * per - 1, 0), 0)),
                  pl.BlockSpec((FF_K, NUP), lambda i: (0, 0)), pl.BlockSpec((1, NUP), lambda i: (0, 0)),
                  _resident((DFF, D)), row, row],
        out_specs=[wide, pl.BlockSpec((tm, DFF), lambda i: (i, 0)), row, pl.BlockSpec((8, 128), lambda i: (0, 0))],
        out_shape=[jax.ShapeDtypeStruct((T, NUP), F32), jax.ShapeDtypeStruct((T, DFF), BF16),
                   jax.ShapeDtypeStruct((T, D), F32), jax.ShapeDtypeStruct((8, 128), F32)],
        scratch_shapes=[pltpu.VMEM((tm + hl, NUP), F32)],
        compiler_params=_params(("arbitrary",), 58))


def _down_bwd(dy, w_down, up, upc, ffconv_w):
    tm, hl = FF_TM, FF_HALO
    nt = T // tm

    def body(dy_ref, wd_ref, up_ref, upc_ref, fw_ref, dup_ref, gff_ref, dext_ref, dact_ref):
        i = pl.program_id(0)

        @pl.when(i == 0)
        def _():
            gff_ref[...] = jnp.zeros_like(gff_ref)
            dext_ref[tm:tm + hl, :] = jnp.zeros((hl, NUP), F32)

        dact_ref[...] = _nt(dy_ref[...].astype(BF16), wd_ref[...])
        for c in range(DFF // FF_CW):
            gcols = slice(c * FF_CW, (c + 1) * FF_CW)
            vcols = slice(DFF + c * FF_CW, DFF + (c + 1) * FF_CW)
            gate, val = upc_ref[:, gcols], upc_ref[:, vcols]
            sg = _sigmoid(gate)
            da = dact_ref[:, gcols]
            dext_ref[0:tm, gcols] = da * val * (sg + gate * sg * (1.0 - sg))
            dext_ref[0:tm, vcols] = da * gate * sg
        for c in range(NUP // FF_CW):
            cols = slice(c * FF_CW, (c + 1) * FF_CW)
            shifted = [dext_ref[k:tm + k, cols] for k in range(FF_K)]
            dup = fw_ref[2:3, cols] * shifted[0]
            u = up_ref[:, cols]
            for k in range(FF_K):
                if k:
                    dup = dup + fw_ref[2 - k:3 - k, cols] * shifted[k]
                gff_ref[2 - k:3 - k, cols] += jnp.sum(shifted[k] * u, axis=0, keepdims=True)
            dup_ref[:, cols] = dup.astype(BF16)
            gff_ref[3:4, cols] += jnp.sum(shifted[0], axis=0, keepdims=True)
        dext_ref[tm:tm + hl, :] = dext_ref[0:hl, :]

    rev = lambda i: (nt - 1 - i, 0)
    wide = pl.BlockSpec((tm, NUP), rev)
    return _call(
        body, dy, w_down, up, upc, ffconv_w, name="down_bwd", grid=(nt,),
        in_specs=[pl.BlockSpec((tm, D), rev), _resident((DFF, D)), wide, wide,
                  pl.BlockSpec((FF_K, NUP), lambda i: (0, 0))],
        out_specs=[wide, pl.BlockSpec((8, NUP), lambda i: (0, 0))],
        out_shape=[jax.ShapeDtypeStruct((T, NUP), BF16), jax.ShapeDtypeStruct((8, NUP), F32)],
        scratch_shapes=[pltpu.VMEM((tm + hl, NUP), F32), pltpu.VMEM((tm, DFF), F32)],
        compiler_params=_params(("arbitrary",), 58))


def _weight_grad(a, g, bm, bn, name):
    m, n = a.shape[1], g.shape[1]
    tk = 512
    nk = T // tk

    def body(a_ref, g_ref, of_ref, ob_ref):
        k = pl.program_id(2)

        @pl.when(k == 0)
        def _():
            of_ref[...] = jnp.zeros_like(of_ref)
        of_ref[...] += _tn_dot(a_ref[...].astype(BF16), g_ref[...].astype(BF16))

        @pl.when(k == nk - 1)
        def _():
            ob_ref[...] = of_ref[...].astype(BF16)

    out = pl.BlockSpec((bm, bn), lambda i, j, k: (i, j))
    return _call(
        body, a, g, name=name, grid=(m // bm, n // bn, nk),
        in_specs=[pl.BlockSpec((tk, bm), lambda i, j, k: (k, i)), pl.BlockSpec((tk, bn), lambda i, j, k: (k, j))],
        out_specs=[out, out],
        out_shape=[jax.ShapeDtypeStruct((m, n), F32), jax.ShapeDtypeStruct((m, n), BF16)],
        compiler_params=_params(("parallel", "parallel", "arbitrary"), 56))


def _norm_bwd_mm(dz, w, xin, base, gain, name, rider):
    kdim = dz.shape[1]
    tm = 256

    def body(dz_ref, w_ref, x_ref, b_ref, g_ref, dx_ref, gg_ref):
        @pl.when(pl.program_id(0) == 0)
        def _():
            gg_ref[...] = jnp.zeros_like(gg_ref)

        xv = x_ref[...]
        dh = _nt(dz_ref[...], w_ref[...])
        r = lax.rsqrt(jnp.mean(xv * xv, axis=-1, keepdims=True) + EPS)
        t = dh * g_ref[...]
        dx_ref[...] = b_ref[...] + r * t - xv * (r * r * r) * jnp.mean(t * xv, axis=-1, keepdims=True)
        gg_ref[...] += jnp.sum(dh * xv * r, axis=0, keepdims=True)

    row = pl.BlockSpec((tm, D), lambda i: (i, 0))
    vec = pl.BlockSpec((1, D), lambda i: (0, 0))
    return _call(
        body, dz, w, xin, base, gain, rider=rider, name=name, grid=(T // tm,),
        in_specs=[pl.BlockSpec((tm, kdim), lambda i: (i, 0)), _resident((D, kdim)), row, row, vec],
        out_specs=[row, vec],
        out_shape=[jax.ShapeDtypeStruct((T, D), F32), jax.ShapeDtypeStruct((1, D), F32)],
        compiler_params=_params(("arbitrary",), 48))


def _outproj_bwd(dx1, w_out):
    tm = 512

    def body(d_ref, w_ref, o_ref):
        o_ref[...] = _nt(d_ref[...].astype(BF16), w_ref[...])

    row = pl.BlockSpec((tm, D), lambda i: (i, 0))
    return _call(
        body, dx1, w_out, name="outproj_bwd", grid=(T // tm,),
        in_specs=[row, pl.BlockSpec((D, D), lambda i: (0, 0))], out_specs=[row],
        out_shape=[jax.ShapeDtypeStruct((T, D), F32)],
        compiler_params=_params(("parallel",), 32))[0]


def _conv_bwd(dcat, cv, proj, conv_w, cn_g, cn_b, rider):
    tm, hl, rb = CONV_TM, CONV_HALO, CONV_RB
    per = tm // hl
    nt = T // tm

    def body(du_ref, dun_ref, cv_ref, cvn_ref, av_ref, ag_ref, hv_ref, hg_ref, w_ref, g_ref, bb_ref,
             dp_ref, gv_ref, gw_ref, dsh_ref, gsh_ref):
        i = pl.program_id(0)

        @pl.when(i == 0)
        def _():
            gv_ref[...] = jnp.zeros_like(gv_ref)
            gw_ref[...] = jnp.zeros_like(gw_ref)

        def ln_bwd(du, cvv):
            mu = jnp.mean(cvv, axis=-1, keepdims=True)
            xc = cvv - mu
            rs = lax.rsqrt(jnp.mean(xc * xc, axis=-1, keepdims=True) + EPS)
            xh = xc * rs
            ln = xh * g_ref[...] + bb_ref[...]
            sg = _sigmoid(ln)
            dln = du * (sg + ln * sg * (1.0 - sg))
            dxh = dln * g_ref[...]
            dcv = rs * (dxh - jnp.mean(dxh, axis=-1, keepdims=True)
                        - xh * jnp.mean(dxh * xh, axis=-1, keepdims=True))
            return dcv, dln, xh

        for r0 in range(0, tm, rb):
            dcv, dln, xh = ln_bwd(du_ref[r0:r0 + rb, :], cv_ref[r0:r0 + rb, :])
            dsh_ref[0, r0:r0 + rb, :] = dcv
            gv_ref[0:1, :] += jnp.sum(dln * xh, axis=0, keepdims=True)
            gv_ref[1:2, :] += jnp.sum(dln, axis=0, keepdims=True)
            gv_ref[2:3, :] += jnp.sum(dcv, axis=0, keepdims=True)
        dcv_n, _, _ = ln_bwd(dun_ref[...], cvn_ref[...])
        dsh_ref[0, tm:tm + hl, :] = jnp.where(i < nt - 1, dcv_n, 0.0)
        glu_h = hv_ref[...] * _sigmoid(hg_ref[...])
        gsh_ref[0, 0:hl, :] = jnp.where(i > 0, glu_h, 0.0)
        gsh_ref[0, hl:, :] = av_ref[...] * _sigmoid(ag_ref[...])
        _fill_shifts(dsh_ref, tm + hl)
        _fill_shifts(gsh_ref, tm + hl)

        for r0 in range(0, tm, rb):
            dglu = jnp.zeros((rb, C), F32)
            for k in range(CONV_K):
                dglu = dglu + w_ref[k:k + 1, :] * _shifted(dsh_ref, r0 + (CONV_K - 1) - k, rb)
            av = av_ref[r0:r0 + rb, :]
            sg = _sigmoid(ag_ref[r0:r0 + rb, :])
            dp_ref[r0:r0 + rb, 0:C] = (dglu * sg).astype(BF16)
            dp_ref[r0:r0 + rb, C:2 * C] = (dglu * av * sg * (1.0 - sg)).astype(BF16)
        for k in range(CONV_K):
            part = jnp.zeros((8, C), F32)
            for r0 in range(0, tm, rb):
                prod = dsh_ref[0, r0:r0 + rb, :] * _shifted(gsh_ref, r0 + hl - (CONV_K - 1) + k, rb)
                part = part + jnp.sum(prod.reshape(rb // 8, 8, C), axis=0)
            gw_ref[k:k + 1, :] += jnp.sum(part, axis=0, keepdims=True)

    main = lambda col: pl.BlockSpec((tm, C), lambda i: (i, col))
    prev = lambda col: pl.BlockSpec((hl, C), lambda i: (jnp.maximum(i * per - 1, 0), col))
    nxt = pl.BlockSpec((hl, C), lambda i: (jnp.minimum((i + 1) * per, T // hl - 1), 0))
    vec = pl.BlockSpec((1, C), lambda i: (0, 0))
    return _call(
        body, dcat, dcat, cv, cv, proj, proj, proj, proj, conv_w, cn_g, cn_b, rider=rider, name="conv_bwd",
        grid=(nt,),
        in_specs=[main(0), nxt, main(0), nxt, main(0), main(1), prev(0), prev(1),
                  pl.BlockSpec((CONV_K, C), lambda i: (0, 0)), vec, vec],
        out_specs=[pl.BlockSpec((tm, 2 * C), lambda i: (i, 0)), pl.BlockSpec((8, C), lambda i: (0, 0)),
                   pl.BlockSpec((32, C), lambda i: (0, 0))],
        out_shape=[jax.ShapeDtypeStruct((T, NPROJ), BF16), jax.ShapeDtypeStruct((8, C), F32),
                   jax.ShapeDtypeStruct((32, C), F32)],
        scratch_shapes=[pltpu.VMEM((8, tm + hl, C), F32), pltpu.VMEM((8, tm + hl, C), F32)],
        compiler_params=_params(("arbitrary",), 48))


def _attn_bwd_prep(dcat, o_f32, bd):
    tm = 512

    def body(do_ref, o_ref, bd_ref, dl_ref):
        dl_ref[...] = _segsum(do_ref[...] * o_ref[...], bd_ref[...])

    blk = pl.BlockSpec((tm, C), lambda i: (i, 0))
    return _call(
        body, dcat, o_f32, bd, name="attn_bwd_prep", grid=(T // tm,),
        in_specs=[pl.BlockSpec((tm, C), lambda i: (i, 1)), blk, pl.BlockSpec((C, C), lambda i: (0, 0))],
        out_specs=[blk],
        out_shape=[jax.ShapeDtypeStruct((T, C), F32)],
        compiler_params=_params(("parallel",), 32))[0]


def _attn_bwd(qn, kn, proj, dcat, lg, dl, bias, d, rider=None):
    sl, nb, hr = _attn_geometry(d)
    slk = QB + sl
    slq = sl + QB
    n_win = T // ATT_WIN

    def body(q_ref, k_ref, v_ref, do_ref, lg_ref, dl_ref, kh_ref, vh_ref, qx_ref, dox_ref, lgx_ref, dlx_ref,
             bias_ref, dq_ref, dk_ref, dv_ref, qs, dos, lgs, dls, ks, vs, dqs, dks, dvs):
        n = pl.program_id(1)
        for r in range(d):
            for dst, src, nx, dt in ((qs, q_ref, qx_ref, BF16), (dos, do_ref, dox_ref, BF16),
                                     (lgs, lg_ref, lgx_ref, F32), (dls, dl_ref, dlx_ref, F32)):
                dst[r * slq:r * slq + sl, :] = _stream(src, r, sl, d).astype(dt)
                dst[r * slq + sl:(r + 1) * slq, :] = _stream(nx, r, QB, d).astype(dt)
            for dst, halo, src in ((ks, kh_ref, k_ref), (vs, vh_ref, v_ref)):
                dst[r * slk:r * slk + QB, :] = _stream(halo, r, QB, d).astype(BF16)
                dst[r * slk + QB:(r + 1) * slk, :] = _stream(src, r, sl, d).astype(BF16)
        dks[...] = jnp.zeros_like(dks)
        dvs[...] = jnp.zeros_like(dvs)

        def unit(rows, kc, vc, biasv, invalid_prev):
            qst, dost = _stack_heads(qs[rows, :]), _stack_heads(dos[rows, :])
            s = _nt(qst, kc) + biasv
            if invalid_prev is not None:
                col = lax.broadcasted_iota(jnp.int32, s.shape, 1)
                s = jnp.where((col < QB) & invalid_prev, NEG, s)
            p = jnp.exp(s - _stack_cols(lgs[rows, :]))
            ds = p * (_nt(dost, vc) - _stack_cols(dls[rows, :]))
            dsb = ds.astype(BF16)
            dq = _unstack_heads(jnp.dot(dsb, kc, preferred_element_type=F32))
            return dq, _tn_dot(dsb, qst), _tn_dot(p.astype(BF16), dost)

        for r in range(d):
            for b in range(nb):
                rows = slice(r * slq + b * QB, r * slq + (b + 1) * QB)
                keys = slice(r * slk + b * QB, r * slk + (b + 2) * QB)
                dq, dkc, dvc = unit(rows, ks[keys, :], vs[keys, :], bias_ref[...], (n == 0) if b == 0 else None)
                dqs[r * sl + b * QB:r * sl + (b + 1) * QB, :] = dq
                if b == 0:
                    dks[r * sl:r * sl + QB, :] += dkc[QB:]
                    dvs[r * sl:r * sl + QB, :] += dvc[QB:]
                else:
                    dks[r * sl + (b - 1) * QB:r * sl + (b + 1) * QB, :] += dkc
                    dvs[r * sl + (b - 1) * QB:r * sl + (b + 1) * QB, :] += dvc

        @pl.when(n < n_win - 1)
        def _():
            for r in range(d):
                rows = slice(r * slq + sl, (r + 1) * slq)
                keys = slice(r * slk + sl, (r + 1) * slk)
                _, dkc, dvc = unit(rows, ks[keys, :], vs[keys, :], bias_ref[:, 0:QB], None)
                dks[(r + 1) * sl - QB:(r + 1) * sl, :] += dkc
                dvs[(r + 1) * sl - QB:(r + 1) * sl, :] += dvc

        for dst, src in ((dq_ref, dqs), (dk_ref, dks), (dv_ref, dvs)):
            for r in range(d):
                if d > 1:
                    dst[pl.ds(r, sl, stride=d), :] = src[r * sl:(r + 1) * sl, :]
                else:
                    dst[...] = src[...]

    main, prev, nxt, bias_spec = _attn_specs(d)
    lanes = 2 * HEAD
    return _call(
        body, qn, kn, proj, dcat, lg, dl, kn, proj, qn, dcat, lg, dl, bias, rider=rider, name=f"attn_bwd_d{d}",
        grid=(C // lanes, n_win),
        in_specs=[main(0), main(0), main(V_COL), main(DO_COL), main(0), main(0), prev(0), prev(V_COL),
                  nxt(0), nxt(DO_COL), nxt(0), nxt(0), bias_spec],
        out_specs=[main(0)] * 3,
        out_shape=[jax.ShapeDtypeStruct((T, C), F32)] * 3,
        scratch_shapes=[pltpu.VMEM((ATT_WIN + hr, lanes), BF16), pltpu.VMEM((ATT_WIN + hr, lanes), BF16),
                        pltpu.VMEM((ATT_WIN + hr, lanes), F32), pltpu.VMEM((ATT_WIN + hr, lanes), F32),
                        pltpu.VMEM((ATT_WIN + hr, lanes), BF16), pltpu.VMEM((ATT_WIN + hr, lanes), BF16),
                        pltpu.VMEM((ATT_WIN, lanes), F32), pltpu.VMEM((ATT_WIN, lanes), F32),
                        pltpu.VMEM((ATT_WIN, lanes), F32)],
        compiler_params=_params(("arbitrary", "arbitrary"), 48))


def _qk_norm_bwd(d3, proj, col, gain, bd, dproj, name):
    tm = 512

    def body(d0, d1, d2, x_ref, g_ref, bd_ref, dp_in, dp_ref, gg_ref):
        del dp_in

        @pl.when(pl.program_id(0) == 0)
        def _():
            gg_ref[...] = jnp.zeros_like(gg_ref)
        dn = d0[...] + d1[...] + d2[...]
        xv = x_ref[...]
        r = lax.rsqrt(_segsum(xv * xv, bd_ref[...]) * (1.0 / HEAD) + EPS)
        t = dn * g_ref[...]
        mean_tx = _segsum(t * xv, bd_ref[...]) * (1.0 / HEAD)
        dp_ref[...] = (r * t - xv * (r * r * r) * mean_tx).astype(BF16)
        gg_ref[...] += jnp.sum(dn * xv * r, axis=0, keepdims=True)

    blk = pl.BlockSpec((tm, C), lambda i: (i, 0))
    vec = pl.BlockSpec((1, C), lambda i: (0, 0))
    return _call(
        body, *d3, proj, gain, bd, dproj, name=name, grid=(T // tm,),
        in_specs=[blk, blk, blk, pl.BlockSpec((tm, C), lambda i: (i, col)), vec,
                  pl.BlockSpec((C, C), lambda i: (0, 0)), ANY],
        out_specs=[pl.BlockSpec((tm, C), lambda i: (i, col)), vec],
        out_shape=[jax.ShapeDtypeStruct((T, NPROJ), BF16), jax.ShapeDtypeStruct((1, C), F32)],
        input_output_aliases={6: 0},
        compiler_params=_params(("arbitrary",), 32))


def _v_bwd(d3, dproj):
    tm = 512

    def body(d0, d1, d2, dp_in, dp_ref):
        del dp_in
        dp_ref[...] = (d0[...] + d1[...] + d2[...]).astype(BF16)

    blk = pl.BlockSpec((tm, C), lambda i: (i, 0))
    return _call(
        body, *d3, dproj, name="v_bwd", grid=(T // tm,),
        in_specs=[blk, blk, blk, ANY],
        out_specs=[pl.BlockSpec((tm, C), lambda i: (i, 4))],
        out_shape=[jax.ShapeDtypeStruct((T, NPROJ), BF16)],
        input_output_aliases={3: 0},
        compiler_params=_params(("parallel",), 32))[0]


def _adamw(w, g, m, v):
    m = ADAM_B1 * m + (1.0 - ADAM_B1) * g
    v = ADAM_B2 * v + (1.0 - ADAM_B2) * (g * g)
    m_hat = m / (1.0 - ADAM_B1 ** ADAM_STEP)
    v_hat = v / (1.0 - ADAM_B2 ** ADAM_STEP)
    delta = -ADAM_LR * (m_hat / (jnp.sqrt(v_hat) + ADAM_EPS) + ADAM_WD * w)
    return delta, m, v


def _row_block(shape):
    rows = shape[0]
    for cand in (256, 128, 64, 88, 32, 8):
        if rows % cand == 0 and cand * shape[1] * 4 <= (2 << 20):
            return cand
    return 8


def _partial_sum(own, recv, name):
    br = _row_block(own.shape)
    cols = own.shape[1]

    def body(o_ref, r_ref, p_ref):
        p_ref[...] = ((o_ref[...] + r_ref[0].astype(F32)) + r_ref[1].astype(F32)) + r_ref[2].astype(F32)

    blk = pl.BlockSpec((br, cols), lambda i: (i, 0))
    return _call(
        body, own, recv, name=name, grid=(own.shape[0] // br,),
        in_specs=[blk, pl.BlockSpec((3, br, cols), lambda i: (0, i, 0))], out_specs=[blk],
        out_shape=[jax.ShapeDtypeStruct(own.shape, F32)],
        compiler_params=_params(("parallel",), 32))[0]


def _adamw_mat(p_own, p_sib, w, m, v, name):
    br = _row_block(w.shape)
    cols = w.shape[1]

    def body(a_ref, b_ref, w_ref, m_ref, v_ref, g_ref, d_ref, nm_ref, nv_ref):
        g = a_ref[...] + b_ref[...]
        delta, nm, nv = _adamw(w_ref[...], g, m_ref[...], v_ref[...])
        g_ref[...] = g
        d_ref[...] = delta
        nm_ref[...] = nm
        nv_ref[...] = nv

    blk = pl.BlockSpec((br, cols), lambda i: (i, 0))
    return _call(
        body, p_own, p_sib, w, m, v, name=name, grid=(w.shape[0] // br,),
        in_specs=[blk] * 5, out_specs=[blk] * 4,
        out_shape=[jax.ShapeDtypeStruct(w.shape, F32)] * 4,
        compiler_params=_params(("parallel",), 40))


def _vec_reduce(vrecv):
    def body(v_ref, o_ref):
        acc = v_ref[0]
        for r in range(1, N_DEV):
            acc = acc + v_ref[r]
        o_ref[...] = acc

    return pl.pallas_call(
        body, name="vec_reduce",
        out_shape=jax.ShapeDtypeStruct((VPACK_ROWS, D), F32),
        compiler_params=_params((), 32),
    )(vrecv)


def _adamw_small(w, g, m, v):
    def body(w_ref, g_ref, m_ref, v_ref, d_ref, nm_ref, nv_ref):
        delta, nm, nv = _adamw(w_ref[...], g_ref[...], m_ref[...], v_ref[...])
        d_ref[...] = delta
        nm_ref[...] = nm
        nv_ref[...] = nv

    return pl.pallas_call(
        body, name="adamw_small",
        out_shape=[jax.ShapeDtypeStruct(w.shape, F32)] * 3,
        compiler_params=_params((), 32),
    )(w, g, m, v)


def _pack(parts, rows):
    flat = jnp.concatenate([p.reshape(-1) for p in parts])
    return jnp.pad(flat, (0, rows * D - flat.shape[0])).reshape(rows, D)


def _unpack(packed, shapes):
    flat = packed.reshape(-1)
    out, off = [], 0
    for shp in shapes:
        size = 1
        for s in shp:
            size *= s
        out.append(flat[off:off + size].reshape(shp))
        off += size
    return out


def _no_comm(shards, row_sharded, peers=(0, 1, 2), into=None):
    del row_sharded, peers, into
    return None, lambda res, n: (res, shards)


def _with_comm(shards, row_sharded, peers=(0, 1, 2), into=None):
    rider = _gather_rider(shards, row_sharded, peers, into)
    return rider, lambda res, n: (res[:n], res[n:])


def _local_step(x, target, norm1_g, conv_w, conv_b, cn_g, cn_b, q_norm_g, k_norm_g, norm2_g, ffconv_w, ffconv_b,
                w_in, late_weights, comm=True):
    row = lambda a: a.reshape(1, -1)
    head_of = jnp.arange(C) // HEAD
    bd = (head_of[:, None] == head_of[None, :]).astype(BF16)
    qg = row(jnp.tile(q_norm_g, C // HEAD) * (HEAD ** -0.5))
    kg = row(jnp.tile(k_norm_g, C // HEAD))
    biases = [_alibi_tables(d) for d in PATTERN_DILATIONS]
    gather = _with_comm if comm else _no_comm
    grad_rider = (lambda g, rs: _grad_rider(g[1], g[0], rs)) if comm else (lambda g, rs: None)

    rider, split = gather(late_weights[0:1], (True,))
    (h, proj), (w_out,) = split(_proj_fwd(x, row(norm1_g), w_in, rider), 2)
    rider, split = gather(late_weights[1:2], (False,), (0, 1))
    (cat, cv), w_up_part = split(_conv_fwd(proj, conv_w, row(conv_b), row(cn_g), row(cn_b), rider), 2)
    qn, kn = _qkv_prep(proj, qg, kg, bd)
    fwd = []
    for i, d in enumerate(PATTERN_DILATIONS):
        rider, split = (gather(late_weights[1:2], (False,), (2,), w_up_part) if i == 2 else
                        (None, lambda res, n: (res, None)))
        res, last = split(_attn_fwd(qn, kn, proj, biases[i], d, rider), 2)
        fwd.append(res)
    (w_up,) = last
    cat, o_f32, lg = _attn_merge([f[0] for f in fwd], [f[1] for f in fwd], cat)
    rider, split = gather(late_weights[2:3], (True,))
    (x1, h2, up), (w_down,) = split(_out_up(x, cat, w_out, row(norm2_g), w_up, rider), 3)
    upc, act, dy, loss_acc = _ffn_down(up, ffconv_w, row(ffconv_b), w_down, x1, target)

    dup, gff = _down_bwd(dy, w_down, up, upc, ffconv_w)
    gw_down = _weight_grad(act, dy, DFF // 2, D, "grad_w_down")
    res = _norm_bwd_mm(dup, w_up, x1, dy, row(norm2_g), "up_bwd", grad_rider(gw_down, True))
    (dx1, g_norm2), ex_down = res[:2], res[2:]
    gw_up = _weight_grad(h2, dup, D, NUP // 4, "grad_w_up")
    dcat = _outproj_bwd(dx1, w_out)
    gw_out = _weight_grad(cat, dx1, D, D, "grad_w_out")
    res = _conv_bwd(dcat, cv, proj, conv_w, row(cn_g), row(cn_b), grad_rider(gw_up, False))
    (dproj, gconv_vec, gconv_w), ex_up = res[:3], res[3:]
    dl = _attn_bwd_prep(dcat, o_f32, bd)
    bwd, ex_out = [], []
    for i, d in enumerate(PATTERN_DILATIONS):
        res = _attn_bwd(qn, kn, proj, dcat, lg, dl, biases[i], d, grad_rider(gw_out, True) if i == 0 else None)
        bwd.append(res[:3])
        ex_out = res[3:] if i == 0 else ex_out
    dproj, gq_lane = _qk_norm_bwd([b[0] for b in bwd], proj, 2, qg, bd, dproj, "q_norm_bwd")
    dproj, gk_lane = _qk_norm_bwd([b[1] for b in bwd], proj, 3, kg, bd, dproj, "k_norm_bwd")
    dproj = _v_bwd([b[2] for b in bwd], dproj)
    gw_in = _weight_grad(h, dproj, D, NPROJ // 4, "grad_w_in")
    res = _norm_bwd_mm(dproj, w_in, x, dx1, row(norm1_g), "in_bwd", grad_rider(gw_in, False))
    (dx, g_norm1), ex_in = res[:2], res[2:]

    loss = loss_acc[0, 0] * (0.5 / D)
    g_qg = jnp.sum(gq_lane.reshape(C // HEAD, HEAD), axis=0) * (HEAD ** -0.5)
    g_kg = jnp.sum(gk_lane.reshape(C // HEAD, HEAD), axis=0)
    small = [g_norm1[0], gconv_vec[2], gconv_vec[0], gconv_vec[1], g_qg, g_kg, g_norm2[0], gff[3],
             gconv_w[:CONV_K], gff[:FF_K]]
    mats = [ex_in, ex_out, ex_up, ex_down] if comm else [gw_in, gw_out, gw_up, gw_down]
    return loss, dx, mats, small


def kernel(x, norm1_g, w_in, conv_w, conv_b, cn_g, cn_b, q_norm_g, k_norm_g, w_out, norm2_g, w_up, ffconv_w, ffconv_b, w_down, loss_target, m_norm1_g, m_w_in, m_conv_w, m_conv_b, m_cn_g, m_cn_b, m_q_norm_g, m_k_norm_g, m_w_out, m_norm2_g, m_w_up, m_ffconv_w, m_ffconv_b, m_w_down, v_norm1_g, v_w_in, v_conv_w, v_conv_b, v_cn_g, v_cn_b, v_q_norm_g, v_k_norm_g, v_w_out, v_norm2_g, v_w_up, v_ffconv_w, v_ffconv_b, v_w_down):
    chip = 2 * lax.axis_index("x") + lax.axis_index("y")

    w_in_full, conv_w_full, ffconv_w_full = _gather_now([w_in.astype(BF16), conv_w, ffconv_w], (False, False, False))
    loss, dx, mats, small = _local_step(
        x[0], loss_target[0], norm1_g, conv_w_full, conv_b, cn_g, cn_b, q_norm_g, k_norm_g, norm2_g,
        ffconv_w_full, ffconv_b, w_in_full, [w.astype(BF16) for w in (w_out, w_up, w_down)])

    names = ("w_in", "w_out", "w_up", "w_down")
    parts = [_partial_sum(own, recv, "partial_" + names[k]) for k, (recv, own) in enumerate(mats)]
    sib, vrecv = _final_exchange(parts, _pack(small, VPACK_ROWS))
    ws = (w_in, w_out, w_up, w_down)
    ms = (m_w_in, m_w_out, m_w_up, m_w_down)
    vs = (v_w_in, v_w_out, v_w_up, v_w_down)
    mat = [_adamw_mat(parts[k], sib[k], ws[k], ms[k], vs[k], "adamw_" + names[k]) for k in range(4)]

    vsum = _vec_reduce(vrecv)
    vec_shapes = [(D,), (C,), (C,), (C,), (HEAD,), (HEAD,), (D,), (NUP,), (CONV_K, C), (FF_K, NUP)]
    gsmall = _unpack(vsum, vec_shapes)
    g_conv_w = lax.dynamic_slice_in_dim(gsmall[8], chip * (C // N_CHIPS), C // N_CHIPS, axis=1)
    g_ffconv_w = lax.dynamic_slice_in_dim(gsmall[9], chip * (NUP // N_CHIPS), NUP // N_CHIPS, axis=1)
    gs = gsmall[:8] + [g_conv_w, g_ffconv_w]
    w_s = [norm1_g, conv_b, cn_g, cn_b, q_norm_g, k_norm_g, norm2_g, ffconv_b, conv_w, ffconv_w]
    m_s = [m_norm1_g, m_conv_b, m_cn_g, m_cn_b, m_q_norm_g, m_k_norm_g, m_norm2_g, m_ffconv_b, m_conv_w, m_ffconv_w]
    v_s = [v_norm1_g, v_conv_b, v_cn_g, v_cn_b, v_q_norm_g, v_k_norm_g, v_norm2_g, v_ffconv_b, v_conv_w, v_ffconv_w]
    shapes_s = [a.shape for a in w_s]
    d_p, m_p, v_p = _adamw_small(_pack(w_s, SPACK_ROWS), _pack(gs, SPACK_ROWS), _pack(m_s, SPACK_ROWS),
                                 _pack(v_s, SPACK_ROWS))
    d_s, nm_s, nv_s = _unpack(d_p, shapes_s), _unpack(m_p, shapes_s), _unpack(v_p, shapes_s)

    def ordered(sm, mt):
        return [sm[0], mt[0], sm[8], sm[1], sm[2], sm[3], sm[4], sm[5], mt[1], sm[6], mt[2], sm[9], sm[7], mt[3]]

    loss_all = lax.psum(loss, ("x", "y", "c"))
    grads = ordered(gs, [r[0] for r in mat])
    deltas = ordered(d_s, [r[1] for r in mat])
    new_m = ordered(nm_s, [r[2] for r in mat])
    new_v = ordered(nv_s, [r[3] for r in mat])
    return (loss_all, dx[None], *grads, *deltas, *new_m, *new_v)
```

```python
import types

import jax
import jax.numpy as jnp
from jax import lax
from jax.experimental import pallas as pl
from jax.experimental.pallas import tpu as pltpu

T = 8192
D = 1024
C = 512
NPROJ = 2560
DFF = 2816
NUP = 2 * DFF
CONV_K = 31
FF_K = 3
HEAD = 64
EPS = 1e-6
NEG = -1e30
N_CHIPS = 4
N_DEV = 8
PATTERN_DILATIONS = (1, 4, 16)
QB = 128

ADAM_LR = 0.001
ADAM_B1 = 0.9
ADAM_B2 = 0.999
ADAM_EPS = 1e-08
ADAM_WD = 0.01
ADAM_STEP = 10

F32 = jnp.float32
BF16 = jnp.bfloat16
MESH = pl.DeviceIdType.MESH
ANY = pl.BlockSpec(memory_space=pl.ANY)

VPACK_ROWS = 48
SPACK_ROWS = 24


def _params(sem, vmem_mb):
    return pltpu.CompilerParams(dimension_semantics=sem, vmem_limit_bytes=vmem_mb << 20)


def _resident(shape):
    return pl.BlockSpec(shape, lambda i: (0, 0), pipeline_mode=pl.Buffered(1))


def _nt(a, b):
    return lax.dot_general(a, b, (((1,), (1,)), ((), ())), preferred_element_type=F32)


def _tn_dot(a, b):
    return lax.dot_general(a, b, (((0,), (0,)), ((), ())), preferred_element_type=F32)


def _sigmoid(x):
    return 1.0 / (1.0 + jnp.exp(-x))


def _segsum(x, bd):
    hi = x.astype(BF16)
    lo = (x - hi.astype(F32)).astype(BF16)
    return (jnp.dot(hi, bd, preferred_element_type=F32)
            + jnp.dot(lo, bd, preferred_element_type=F32))


def _place():
    x, y, c = lax.axis_index("x"), lax.axis_index("y"), lax.axis_index("c")
    chips = [(1 - x, y), (x, 1 - y), (1 - x, 1 - y)]
    return x, y, c, chips


def _block_of(ref, shard_shape, row_sharded, s):
    r, cdim = shard_shape
    if row_sharded:
        return ref.at[pl.ds(s * r, r), :]
    return ref.at[:, pl.ds(s * cdim, cdim)]


def _full_shape(shard_shape, row_sharded):
    r, cdim = shard_shape
    return (r * N_CHIPS, cdim) if row_sharded else (r, cdim * N_CHIPS)


def _gather_rider(shards, row_sharded, peers=(0, 1, 2), into=None):
    n = len(shards)
    shapes = [a.shape for a in shards]

    def copies(ins, outs, sems):
        send_sems, recv_sems, local_sems = sems
        x, y, c, chips = _place()
        me = 2 * x + y
        place = lambda k, s: _block_of(outs[k], shapes[k], row_sharded[k], s)
        local = []
        if into is None:
            local = [pltpu.make_async_copy(ins[k], place(k, me), local_sems.at[k]) for k in range(n)]
        sends, recvs = [], []
        for k in range(n):
            for j in peers:
                px, py = chips[j]
                sem = dict(send_sem=send_sems.at[3 * k + j], recv_sem=recv_sems.at[3 * k + j],
                           device_id=(px, py, c), device_id_type=MESH)
                sends.append(pltpu.make_async_remote_copy(src_ref=ins[k], dst_ref=place(k, me), **sem))
                recvs.append(pltpu.make_async_remote_copy(src_ref=ins[k], dst_ref=place(k, 2 * px + py), **sem))
        return local, sends, recvs

    return types.SimpleNamespace(
        operands=list(shards) + list(into or []), copies=copies,
        aliases={n + k: k for k in range(n)} if into is not None else {},
        out_shape=[jax.ShapeDtypeStruct(_full_shape(s, rs), a.dtype) for s, rs, a in zip(shapes, row_sharded, shards)],
        sems=[pltpu.SemaphoreType.DMA((3 * n,)), pltpu.SemaphoreType.DMA((3 * n,)), pltpu.SemaphoreType.DMA((n,))])


def _grad_rider(g_bf16, g_f32, row_sharded):
    shard = tuple(d // N_CHIPS if (i == 0) == row_sharded else d for i, d in enumerate(g_f32.shape))

    def copies(ins, outs, sems):
        send_sems, recv_sems, local_sems = sems
        gb, gf = ins
        rec, own = outs
        x, y, c, chips = _place()
        me = 2 * x + y
        local = [pltpu.make_async_copy(_block_of(gf, shard, row_sharded, me), own, local_sems.at[0])]
        sends, recvs = [], []
        for j, (px, py) in enumerate(chips):
            sem = dict(send_sem=send_sems.at[j], recv_sem=recv_sems.at[j], device_id=(px, py, c), device_id_type=MESH)
            sends.append(pltpu.make_async_remote_copy(
                src_ref=_block_of(gb, shard, row_sharded, 2 * px + py), dst_ref=rec.at[j], **sem))
            recvs.append(pltpu.make_async_remote_copy(
                src_ref=_block_of(gb, shard, row_sharded, me), dst_ref=rec.at[j], **sem))
        return local, sends, recvs

    return types.SimpleNamespace(
        operands=[g_bf16, g_f32], copies=copies, aliases={},
        out_shape=[jax.ShapeDtypeStruct((3,) + shard, BF16), jax.ShapeDtypeStruct(shard, F32)],
        sems=[pltpu.SemaphoreType.DMA((3,)), pltpu.SemaphoreType.DMA((3,)), pltpu.SemaphoreType.DMA((1,))])


def _rider_start(rider, ins, outs, sems):
    local, sends, _ = rider.copies(ins, outs, sems)
    for cp in local + sends:
        cp.start()


def _rider_wait(rider, ins, outs, sems):
    local, sends, recvs = rider.copies(ins, outs, sems)
    for cp in recvs:
        cp.wait_recv()
    for cp in sends:
        cp.wait_send()
    for cp in local:
        cp.wait()


def _call(body, *operands, rider=None, name, grid, in_specs, out_specs, out_shape, scratch_shapes=(),
          compiler_params, input_output_aliases=None):
    operands = [pltpu.with_memory_space_constraint(a, pltpu.HBM) for a in operands]
    if rider is None:
        return pl.pallas_call(
            body, name=name, grid=grid, in_specs=list(in_specs), out_specs=list(out_specs), out_shape=list(out_shape),
            scratch_shapes=list(scratch_shapes), compiler_params=compiler_params,
            input_output_aliases=input_output_aliases or {})(*operands)
    n_in, n_out, n_scr = len(in_specs), len(out_specs), len(scratch_shapes)
    r_in, r_out = len(rider.operands), len(rider.out_shape)

    def riding(*refs):
        refs = list(refs)
        ins, refs = refs[:n_in], refs[n_in:]
        r_ins, refs = refs[:r_in], refs[r_in:]
        outs, refs = refs[:n_out], refs[n_out:]
        r_outs, refs = refs[:r_out], refs[r_out:]
        scr, sems = refs[:n_scr], refs[n_scr:]
        first = pl.program_id(0) == 0
        last = pl.program_id(0) == grid[0] - 1
        for axis in range(1, len(grid)):
            first = first & (pl.program_id(axis) == 0)
            last = last & (pl.program_id(axis) == grid[axis] - 1)

        @pl.when(first)
        def _():
            _rider_start(rider, r_ins, r_outs, sems)

        body(*ins, *outs, *scr)

        @pl.when(last)
        def _():
            _rider_wait(rider, r_ins, r_outs, sems)

    return pl.pallas_call(
        riding, name=name, grid=grid, in_specs=list(in_specs) + [ANY] * r_in,
        out_specs=list(out_specs) + [ANY] * r_out, out_shape=list(out_shape) + list(rider.out_shape),
        scratch_shapes=list(scratch_shapes) + list(rider.sems), compiler_params=compiler_params,
        input_output_aliases={**(input_output_aliases or {}),
                              **{n_in + i: n_out + o for i, o in rider.aliases.items()}})(
            *operands, *[pltpu.with_memory_space_constraint(a, pltpu.HBM) for a in rider.operands])


def _gather_now(shards, row_sharded):
    rider = _gather_rider(shards, row_sharded)
    n = len(shards)

    def body(*refs):
        ins, outs, sems = refs[:n], refs[n:2 * n], refs[2 * n:]
        _rider_start(rider, ins, outs, sems)
        _rider_wait(rider, ins, outs, sems)

    return pl.pallas_call(
        body, name="gather_first", out_shape=rider.out_shape, in_specs=[ANY] * n, out_specs=[ANY] * n,
        scratch_shapes=rider.sems)(*shards)


def _final_exchange(parts, vpack):
    def body(p0, p1, p2, p3, v_ref, o0, o1, o2, o3, vr_ref, send_sems, recv_sems, vsend_sems, vrecv_sems, local_sem):
        x, y, c, _ = _place()
        me = 4 * x + 2 * y + c
        mine = pltpu.make_async_copy(v_ref, vr_ref.at[me], local_sem)
        mine.start()
        copies = [pltpu.make_async_remote_copy(
            src_ref=p, dst_ref=o, send_sem=send_sems.at[k], recv_sem=recv_sems.at[k],
            device_id=(x, y, 1 - c), device_id_type=MESH)
            for k, (p, o) in enumerate(zip((p0, p1, p2, p3), (o0, o1, o2, o3)))]
        flips = [(fx, fy, fc) for fx in (0, 1) for fy in (0, 1) for fc in (0, 1)][1:]
        recvs = []
        for r, (fx, fy, fc) in enumerate(flips):
            peer = (x ^ fx, y ^ fy, c ^ fc)
            sem = dict(send_sem=vsend_sems.at[r], recv_sem=vrecv_sems.at[r], device_id=peer, device_id_type=MESH)
            copies.append(pltpu.make_async_remote_copy(src_ref=v_ref, dst_ref=vr_ref.at[me], **sem))
            recvs.append(pltpu.make_async_remote_copy(
                src_ref=v_ref, dst_ref=vr_ref.at[4 * peer[0] + 2 * peer[1] + peer[2]], **sem))
        for cp in copies:
            cp.start()
        for cp in copies[:4]:
            cp.wait_recv()
        for cp in recvs:
            cp.wait_recv()
        for cp in copies:
            cp.wait_send()
        mine.wait()

    res = pl.pallas_call(
        body, name="final_exchange",
        out_shape=[jax.ShapeDtypeStruct(p.shape, F32) for p in parts]
        + [jax.ShapeDtypeStruct((N_DEV, VPACK_ROWS, D), F32)],
        in_specs=[ANY] * 5, out_specs=[ANY] * 5,
        scratch_shapes=[pltpu.SemaphoreType.DMA((4,)), pltpu.SemaphoreType.DMA((4,)),
                        pltpu.SemaphoreType.DMA((7,)), pltpu.SemaphoreType.DMA((7,)), pltpu.SemaphoreType.DMA],
    )(*parts, vpack)
    return res[:4], res[4]


def _proj_fwd(x, g1, w_in, rider):
    tm, tn = 512, 640

    def body(x_ref, g_ref, w_ref, h_ref, p_ref):
        xv = x_ref[...]
        r = lax.rsqrt(jnp.mean(xv * xv, axis=-1, keepdims=True) + EPS)
        h_ref[...] = (xv * r * g_ref[...]).astype(BF16)
        for j in range(NPROJ // tn):
            cols = slice(j * tn, (j + 1) * tn)
            p_ref[:, cols] = jnp.dot(h_ref[...], w_ref[:, cols], preferred_element_type=F32)

    return _call(
        body, x, g1, w_in, rider=rider, name="proj_fwd", grid=(T // tm,),
        in_specs=[pl.BlockSpec((tm, D), lambda i: (i, 0)), pl.BlockSpec((1, D), lambda i: (0, 0)),
                  _resident((D, NPROJ))],
        out_specs=[pl.BlockSpec((tm, D), lambda i: (i, 0)), pl.BlockSpec((tm, NPROJ), lambda i: (i, 0))],
        out_shape=[jax.ShapeDtypeStruct((T, D), BF16), jax.ShapeDtypeStruct((T, NPROJ), F32)],
        compiler_params=_params(("arbitrary",), 40))


CONV_TM = 512
CONV_HALO = 32
CONV_RB = 32
CONV_CB = 64


LANES = 128


def _sp(start, n):
    return (pl.ds(2 * start, n, stride=2), slice(None))


def _lanes(tile):
    return slice(tile * LANES, (tile + 1) * LANES)


def _conv_fwd(proj, conv_w, conv_b, cn_g, cn_b, rider):
    tm, hl, rb, cb = CONV_TM, CONV_HALO, CONV_RB, CONV_CB
    per = tm // hl

    def body(av_ref, ag_ref, hv_ref, hg_ref, w_ref, b_ref, g_ref, bb_ref, cat_ref, cv_ref, sh_ref):
        i = pl.program_id(0)
        for j in range(C // LANES):
            ln_ = _lanes(j)
            glu_h = hv_ref[:, ln_] * _sigmoid(hg_ref[:, ln_])
            sh_ref.at[j][_sp(0, hl)] = jnp.where(i > 0, glu_h, 0.0)
            for r0 in range(0, tm, cb):
                sh_ref.at[j][_sp(hl + r0, cb)] = av_ref[r0:r0 + cb, ln_] * _sigmoid(ag_ref[r0:r0 + cb, ln_])
            for r0 in range(0, tm, cb):
                acc = jnp.zeros((cb, LANES), F32) + b_ref[:, ln_]
                for k in range(CONV_K):
                    acc = acc + w_ref[k:k + 1, ln_] * sh_ref.at[j][_sp(r0 + hl - (CONV_K - 1) + k, cb)]
                cv_ref[r0:r0 + cb, ln_] = acc
        for r0 in range(0, tm, rb):
            acc = cv_ref[r0:r0 + rb, :]
            mu = jnp.mean(acc, axis=-1, keepdims=True)
            xc = acc - mu
            var = jnp.mean(xc * xc, axis=-1, keepdims=True)
            ln = xc * lax.rsqrt(var + EPS) * g_ref[...] + bb_ref[...]
            cat_ref[r0:r0 + rb, :] = (ln * _sigmoid(ln)).astype(BF16)

    halo = lambda col: pl.BlockSpec((hl, C), lambda i: (jnp.maximum(i * per - 1, 0), col))
    vec = pl.BlockSpec((1, C), lambda i: (0, 0))
    return _call(
        body, proj, proj, proj, proj, conv_w, conv_b, cn_g, cn_b, rider=rider, name="conv_fwd", grid=(T // tm,),
        in_specs=[pl.BlockSpec((tm, C), lambda i: (i, 0)), pl.BlockSpec((tm, C), lambda i: (i, 1)),
                  halo(0), halo(1), pl.BlockSpec((CONV_K, C), lambda i: (0, 0)), vec, vec, vec],
        out_specs=[pl.BlockSpec((tm, C), lambda i: (i, 0)), pl.BlockSpec((tm, C), lambda i: (i, 0))],
        out_shape=[jax.ShapeDtypeStruct((T, D), BF16), jax.ShapeDtypeStruct((T, C), F32)],
        scratch_shapes=[pltpu.VMEM((C // LANES, 2 * (tm + hl), LANES), F32)],
        compiler_params=_params(("arbitrary",), 40))


def _qkv_prep(proj, qg, kg, bd):
    tm = 512

    def body(q_ref, k_ref, qg_ref, kg_ref, bd_ref, qn_ref, kn_ref):
        for src, g, dst in ((q_ref, qg_ref, qn_ref), (k_ref, kg_ref, kn_ref)):
            xv = src[...]
            ms = _segsum(xv * xv, bd_ref[...]) * (1.0 / HEAD)
            dst[...] = xv * lax.rsqrt(ms + EPS) * g[...]

    col = lambda c: pl.BlockSpec((tm, C), lambda i: (i, c))
    vec = pl.BlockSpec((1, C), lambda i: (0, 0))
    out = pl.BlockSpec((tm, C), lambda i: (i, 0))
    return _call(
        body, proj, proj, qg, kg, bd, name="qkv_prep", grid=(T // tm,),
        in_specs=[col(2), col(3), vec, vec, pl.BlockSpec((C, C), lambda i: (0, 0))],
        out_specs=[out, out],
        out_shape=[jax.ShapeDtypeStruct((T, C), F32)] * 2,
        compiler_params=_params(("parallel",), 32))


def _stack_heads(a):
    lane = lax.broadcasted_iota(jnp.int32, a.shape, 1)
    zero = jnp.zeros_like(a)
    return jnp.concatenate([jnp.where(lane < HEAD, a, zero), jnp.where(lane >= HEAD, a, zero)], axis=0)


def _unstack_heads(a2):
    lane = lax.broadcasted_iota(jnp.int32, (QB, 2 * HEAD), 1)
    return jnp.where(lane < HEAD, a2[:QB], a2[QB:])


def _stack_cols(a):
    return jnp.concatenate([a[:, 0:1], a[:, HEAD:HEAD + 1]], axis=0)


ATT_WIN = 2048
V_COL = 4 * C // (2 * HEAD)
DO_COL = C // (2 * HEAD)


def _attn_geometry(d):
    sl = ATT_WIN // d
    return sl, sl // QB, QB * d


def _stream(ref, r, n, d):
    return ref[pl.ds(r, n, stride=d), :] if d > 1 else ref[pl.ds(r, n), :]


def _alibi_tables(d):
    qi = jnp.arange(QB)[:, None]
    kj = jnp.arange(2 * QB)[None, :]
    delta = qi + QB - kj
    band = (delta >= 0) & (delta <= QB)
    dist = (delta * d).astype(F32)
    heads = jnp.arange(8, dtype=F32)
    slopes = 2.0 ** (-(heads + 1.0))
    t = jnp.where(band[None], -slopes[:, None, None] * dist[None], NEG)
    return t.reshape(4, 2 * QB, 2 * QB)


def _attn_specs(d):
    _, _, hr = _attn_geometry(d)
    per = ATT_WIN // hr
    main = lambda off: pl.BlockSpec((ATT_WIN, 2 * HEAD), lambda cb, n: (n, off + cb))
    prev = lambda off: pl.BlockSpec((hr, 2 * HEAD), lambda cb, n: (jnp.maximum(n * per - 1, 0), off + cb))
    nxt = lambda off: pl.BlockSpec((hr, 2 * HEAD), lambda cb, n: (jnp.minimum((n + 1) * per, T // hr - 1), off + cb))
    bias = pl.BlockSpec((None, 2 * QB, 2 * QB), lambda cb, n: (cb, 0, 0))
    return main, prev, nxt, bias


def _attn_fwd(qn, kn, proj, bias, d, rider=None):
    sl, nb, hr = _attn_geometry(d)
    slk = QB + sl

    def body(q_ref, k_ref, v_ref, kh_ref, vh_ref, bias_ref, o_ref, l_ref, qs, ks, vs, os_, ls):
        n = pl.program_id(1)
        for r in range(d):
            qs[r * sl:(r + 1) * sl, :] = _stream(q_ref, r, sl, d).astype(BF16)
            for dst, halo, src in ((ks, kh_ref, k_ref), (vs, vh_ref, v_ref)):
                dst[r * slk:r * slk + QB, :] = _stream(halo, r, QB, d).astype(BF16)
                dst[r * slk + QB:(r + 1) * slk, :] = _stream(src, r, sl, d).astype(BF16)
        col = lax.broadcasted_iota(jnp.int32, (2 * QB, 2 * QB), 1)
        for r in range(d):
            for b in range(nb):
                rows = slice(r * sl + b * QB, r * sl + (b + 1) * QB)
                keys = slice(r * slk + b * QB, r * slk + (b + 2) * QB)
                s = _nt(_stack_heads(qs[rows, :]), ks[keys, :]) + bias_ref[...]
                if b == 0:
                    s = jnp.where((col < QB) & (n == 0), NEG, s)
                m = jnp.max(s, axis=-1, keepdims=True)
                p = jnp.exp(s - m)
                den = jnp.sum(p, axis=-1, keepdims=True)
                pv = jnp.dot(p.astype(BF16), vs[keys, :], preferred_element_type=F32)
                os_[rows, :] = _unstack_heads(pv / den)
                ls[rows, :] = _unstack_heads(jnp.broadcast_to(m + jnp.log(den), (2 * QB, 2 * HEAD)))
        for r in range(d):
            if d > 1:
                o_ref[pl.ds(r, sl, stride=d), :] = os_[r * sl:(r + 1) * sl, :]
                l_ref[pl.ds(r, sl, stride=d), :] = ls[r * sl:(r + 1) * sl, :]
            else:
                o_ref[...] = os_[...]
                l_ref[...] = ls[...]

    main, prev, _, bias_spec = _attn_specs(d)
    lanes = 2 * HEAD
    return _call(
        body, qn, kn, proj, kn, proj, bias, rider=rider, name=f"attn_fwd_d{d}", grid=(C // lanes, T // ATT_WIN),
        in_specs=[main(0), main(0), main(V_COL), prev(0), prev(V_COL), bias_spec],
        out_specs=[main(0), main(0)],
        out_shape=[jax.ShapeDtypeStruct((T, C), F32)] * 2,
        scratch_shapes=[pltpu.VMEM((ATT_WIN, lanes), BF16), pltpu.VMEM((ATT_WIN + hr, lanes), BF16),
                        pltpu.VMEM((ATT_WIN + hr, lanes), BF16), pltpu.VMEM((ATT_WIN, lanes), F32),
                        pltpu.VMEM((ATT_WIN, lanes), F32)],
        compiler_params=_params(("arbitrary", "arbitrary"), 40))


def _attn_merge(outs, lses, cat):
    tm = 512

    def body(o0, o1, o2, l0, l1, l2, cat_in, cat_ref, of_ref, lg_ref):
        del cat_in
        a, b, c = l0[...], l1[...], l2[...]
        m = jnp.maximum(jnp.maximum(a, b), c)
        e0, e1, e2 = jnp.exp(a - m), jnp.exp(b - m), jnp.exp(c - m)
        den = e0 + e1 + e2
        o = (e0 * o0[...] + e1 * o1[...] + e2 * o2[...]) / den
        of_ref[...] = o
        cat_ref[...] = o.astype(BF16)
        lg_ref[...] = m + jnp.log(den)

    blk = pl.BlockSpec((tm, C), lambda i: (i, 0))
    return _call(
        body, *outs, *lses, cat, name="attn_merge", grid=(T // tm,),
        in_specs=[blk] * 6 + [ANY],
        out_specs=[pl.BlockSpec((tm, C), lambda i: (i, 1)), blk, blk],
        out_shape=[jax.ShapeDtypeStruct((T, D), BF16), jax.ShapeDtypeStruct((T, C), F32),
                   jax.ShapeDtypeStruct((T, C), F32)],
        input_output_aliases={6: 0},
        compiler_params=_params(("parallel",), 32))


def _out_up(x, cat, w_out, g2, w_up, rider):
    tm, tn = 512, NUP // 4

    def body(x_ref, cat_ref, wo_ref, g_ref, wu_ref, x1_ref, h2_ref, up_ref):
        x1 = x_ref[...] + jnp.dot(cat_ref[...], wo_ref[...], preferred_element_type=F32)
        x1_ref[...] = x1
        r = lax.rsqrt(jnp.mean(x1 * x1, axis=-1, keepdims=True) + EPS)
        h2_ref[...] = (x1 * r * g_ref[...]).astype(BF16)
        for j in range(NUP // tn):
            cols = slice(j * tn, (j + 1) * tn)
            up_ref[:, cols] = jnp.dot(h2_ref[...], wu_ref[:, cols], preferred_element_type=F32)

    row = pl.BlockSpec((tm, D), lambda i: (i, 0))
    return _call(
        body, x, cat, w_out, g2, w_up, rider=rider, name="out_up", grid=(T // tm,),
        in_specs=[row, row, _resident((D, D)), pl.BlockSpec((1, D), lambda i: (0, 0)), _resident((D, NUP))],
        out_specs=[row, row, pl.BlockSpec((tm, NUP), lambda i: (i, 0))],
        out_shape=[jax.ShapeDtypeStruct((T, D), F32), jax.ShapeDtypeStruct((T, D), BF16),
                   jax.ShapeDtypeStruct((T, NUP), F32)],
        compiler_params=_params(("arbitrary",), 58))


FF_TM = 256
FF_HALO = 8
FF_RB = 64
FF_TILES = DFF // LANES


def _ff_conv(ext_ref, fw_ref, fb_ref, tile, r0):
    cols = _lanes(tile)
    base = FF_HALO + r0
    acc = fb_ref[:, cols] + fw_ref[0:1, cols] * ext_ref.at[tile][_sp(base - 2, FF_RB)]
    acc = acc + fw_ref[1:2, cols] * ext_ref.at[tile][_sp(base - 1, FF_RB)]
    return acc + fw_ref[2:3, cols] * ext_ref.at[tile][_sp(base, FF_RB)]


def _ffn_down(up, ffconv_w, ffconv_b, w_down, x1, target):
    tm, hl = FF_TM, FF_HALO
    per = tm // hl

    def body(up_ref, uh_ref, fw_ref, fb_ref, wd_ref, x1_ref, tg_ref, upc_ref, act_ref, dy_ref, loss_ref, ext_ref):
        i = pl.program_id(0)
        for j in range(2 * FF_TILES):
            ext_ref.at[j][_sp(0, hl)] = jnp.where(i > 0, uh_ref[:, _lanes(j)], 0.0)
            for r0 in range(0, tm, FF_RB):
                ext_ref.at[j][_sp(hl + r0, FF_RB)] = up_ref[r0:r0 + FF_RB, _lanes(j)]
        for c in range(FF_TILES):
            gcols, vcols = _lanes(c), _lanes(FF_TILES + c)
            for r0 in range(0, tm, FF_RB):
                rows = slice(r0, r0 + FF_RB)
                gate = _ff_conv(ext_ref, fw_ref, fb_ref, c, r0)
                val = _ff_conv(ext_ref, fw_ref, fb_ref, FF_TILES + c, r0)
                upc_ref[rows, gcols] = gate
                upc_ref[rows, vcols] = val
                act_ref[rows, gcols] = (gate * _sigmoid(gate) * val).astype(BF16)
        y = x1_ref[...] + jnp.dot(act_ref[...], wd_ref[...], preferred_element_type=F32)
        err = y - tg_ref[...]
        dy_ref[...] = err * (1.0 / D)

        @pl.when(i == 0)
        def _():
            loss_ref[...] = jnp.zeros_like(loss_ref)
        loss_ref[...] += jnp.sum(err * err)

    row = pl.BlockSpec((tm, D), lambda i: (i, 0))
    wide = pl.BlockSpec((tm, NUP), lambda i: (i, 0))
    return _call(
        body, up, up, ffconv_w, ffconv_b, w_down, x1, target, name="ffn_down", grid=(T // tm,),
        in_specs=[wide, pl.BlockSpec((hl, NUP), lambda i: (jnp.maximum(i * per - 1, 0), 0)),
                  pl.BlockSpec((FF_K, NUP), lambda i: (0, 0)), pl.BlockSpec((1, NUP), lambda i: (0, 0)),
                  _resident((DFF, D)), row, row],
        out_specs=[wide, pl.BlockSpec((tm, DFF), lambda i: (i, 0)), row, pl.BlockSpec((8, 128), lambda i: (0, 0))],
        out_shape=[jax.ShapeDtypeStruct((T, NUP), F32), jax.ShapeDtypeStruct((T, DFF), BF16),
                   jax.ShapeDtypeStruct((T, D), F32), jax.ShapeDtypeStruct((8, 128), F32)],
        scratch_shapes=[pltpu.VMEM((2 * FF_TILES, 2 * (tm + hl), LANES), F32)],
        compiler_params=_params(("arbitrary",), 58))


def _down_bwd(dy, w_down, up, upc, ffconv_w):
    tm, hl = FF_TM, FF_HALO
    nt = T // tm

    def body(dy_ref, wd_ref, up_ref, upc_ref, fw_ref, dup_ref, gff_ref, dext_ref, dact_ref):
        i = pl.program_id(0)

        @pl.when(i == 0)
        def _():
            gff_ref[...] = jnp.zeros_like(gff_ref)
            for j in range(2 * FF_TILES):
                dext_ref.at[j][_sp(tm, hl)] = jnp.zeros((hl, LANES), F32)

        dact_ref[...] = _nt(dy_ref[...].astype(BF16), wd_ref[...])
        for c in range(FF_TILES):
            gcols, vcols = _lanes(c), _lanes(FF_TILES + c)
            for r0 in range(0, tm, FF_RB):
                rows = slice(r0, r0 + FF_RB)
                gate, val = upc_ref[rows, gcols], upc_ref[rows, vcols]
                sg = _sigmoid(gate)
                da = dact_ref[rows, gcols]
                dext_ref.at[c][_sp(r0, FF_RB)] = da * val * (sg + gate * sg * (1.0 - sg))
                dext_ref.at[FF_TILES + c][_sp(r0, FF_RB)] = da * gate * sg
        fold = lambda a: jnp.sum(a.reshape(FF_RB // 8, 8, LANES), axis=0)
        for c in range(2 * FF_TILES):
            cols = _lanes(c)
            taps = [fw_ref[k:k + 1, cols] for k in range(FF_K)]
            acc = [jnp.zeros((8, LANES), F32) for _ in range(FF_K + 1)]
            for r0 in range(0, tm, FF_RB):
                shifted = [dext_ref.at[c][_sp(r0 + k, FF_RB)] for k in range(FF_K)]
                u = up_ref[r0:r0 + FF_RB, cols]
                dup = taps[2] * shifted[0] + taps[1] * shifted[1] + taps[0] * shifted[2]
                dup_ref[r0:r0 + FF_RB, cols] = dup.astype(BF16)
                for k in range(FF_K):
                    acc[2 - k] = acc[2 - k] + fold(shifted[k] * u)
                acc[FF_K] = acc[FF_K] + fold(shifted[0])
            for k in range(FF_K + 1):
                gff_ref[k:k + 1, cols] += jnp.sum(acc[k], axis=0, keepdims=True)
        for j in range(2 * FF_TILES):
            dext_ref.at[j][_sp(tm, hl)] = dext_ref.at[j][_sp(0, hl)]

    rev = lambda i: (nt - 1 - i, 0)
    wide = pl.BlockSpec((tm, NUP), rev)
    return _call(
        body, dy, w_down, up, upc, ffconv_w, name="down_bwd", grid=(nt,),
        in_specs=[pl.BlockSpec((tm, D), rev), _resident((DFF, D)), wide, wide,
                  pl.BlockSpec((FF_K, NUP), lambda i: (0, 0))],
        out_specs=[wide, pl.BlockSpec((8, NUP), lambda i: (0, 0))],
        out_shape=[jax.ShapeDtypeStruct((T, NUP), BF16), jax.ShapeDtypeStruct((8, NUP), F32)],
        scratch_shapes=[pltpu.VMEM((2 * FF_TILES, 2 * (tm + hl), LANES), F32), pltpu.VMEM((tm, DFF), F32)],
        compiler_params=_params(("arbitrary",), 58))


def _weight_grad(a, g, bm, bn, tk, name):
    m, n = a.shape[1], g.shape[1]
    nk = T // tk

    def body(a_ref, g_ref, of_ref, ob_ref):
        k = pl.program_id(2)

        @pl.when(k == 0)
        def _():
            of_ref[...] = jnp.zeros_like(of_ref)
        of_ref[...] += _tn_dot(a_ref[...].astype(BF16), g_ref[...].astype(BF16))

        @pl.when(k == nk - 1)
        def _():
            ob_ref[...] = of_ref[...].astype(BF16)

    out = pl.BlockSpec((bm, bn), lambda i, j, k: (i, j))
    return _call(
        body, a, g, name=name, grid=(m // bm, n // bn, nk),
        in_specs=[pl.BlockSpec((tk, bm), lambda i, j, k: (k, i)), pl.BlockSpec((tk, bn), lambda i, j, k: (k, j))],
        out_specs=[out, out],
        out_shape=[jax.ShapeDtypeStruct((m, n), F32), jax.ShapeDtypeStruct((m, n), BF16)],
        compiler_params=_params(("parallel", "parallel", "arbitrary"), 56))


def _norm_bwd_mm(dz, w, xin, base, gain, name, rider):
    kdim = dz.shape[1]
    tm = 512

    def body(dz_ref, w_ref, x_ref, b_ref, g_ref, dx_ref, gg_ref):
        @pl.when(pl.program_id(0) == 0)
        def _():
            gg_ref[...] = jnp.zeros_like(gg_ref)

        xv = x_ref[...]
        dh = _nt(dz_ref[...], w_ref[...])
        r = lax.rsqrt(jnp.mean(xv * xv, axis=-1, keepdims=True) + EPS)
        t = dh * g_ref[...]
        dx_ref[...] = b_ref[...] + r * t - xv * (r * r * r) * jnp.mean(t * xv, axis=-1, keepdims=True)
        gg_ref[...] += jnp.sum(dh * xv * r, axis=0, keepdims=True)

    row = pl.BlockSpec((tm, D), lambda i: (i, 0))
    vec = pl.BlockSpec((1, D), lambda i: (0, 0))
    return _call(
        body, dz, w, xin, base, gain, rider=rider, name=name, grid=(T // tm,),
        in_specs=[pl.BlockSpec((tm, kdim), lambda i: (i, 0)), _resident((D, kdim)), row, row, vec],
        out_specs=[row, vec],
        out_shape=[jax.ShapeDtypeStruct((T, D), F32), jax.ShapeDtypeStruct((1, D), F32)],
        compiler_params=_params(("arbitrary",), 48))


def _outproj_bwd(dx1, w_out):
    tm = 512

    def body(d_ref, w_ref, o_ref):
        o_ref[...] = _nt(d_ref[...].astype(BF16), w_ref[...])

    row = pl.BlockSpec((tm, D), lambda i: (i, 0))
    return _call(
        body, dx1, w_out, name="outproj_bwd", grid=(T // tm,),
        in_specs=[row, pl.BlockSpec((D, D), lambda i: (0, 0))], out_specs=[row],
        out_shape=[jax.ShapeDtypeStruct((T, D), F32)],
        compiler_params=_params(("parallel",), 32))[0]


def _conv_bwd(dcat, cv, proj, conv_w, cn_g, cn_b, rider):
    tm, hl, rb, cb = CONV_TM, CONV_HALO, CONV_RB, CONV_CB
    per = tm // hl
    nt = T // tm
    tiles = C // LANES

    def body(du_ref, dun_ref, cv_ref, cvn_ref, av_ref, ag_ref, hv_ref, hg_ref, w_ref, g_ref, bb_ref,
             dp_ref, gv_ref, gw_ref, dsh_ref, gsh_ref):
        i = pl.program_id(0)

        @pl.when(i == 0)
        def _():
            gv_ref[...] = jnp.zeros_like(gv_ref)
            gw_ref[...] = jnp.zeros_like(gw_ref)

        def ln_bwd(du, cvv):
            mu = jnp.mean(cvv, axis=-1, keepdims=True)
            xc = cvv - mu
            rs = lax.rsqrt(jnp.mean(xc * xc, axis=-1, keepdims=True) + EPS)
            xh = xc * rs
            ln = xh * g_ref[...] + bb_ref[...]
            sg = _sigmoid(ln)
            dln = du * (sg + ln * sg * (1.0 - sg))
            dxh = dln * g_ref[...]
            dcv = rs * (dxh - jnp.mean(dxh, axis=-1, keepdims=True)
                        - xh * jnp.mean(dxh * xh, axis=-1, keepdims=True))
            return dcv, dln, xh

        for r0 in range(0, tm, rb):
            dcv, dln, xh = ln_bwd(du_ref[r0:r0 + rb, :], cv_ref[r0:r0 + rb, :])
            for j in range(tiles):
                dsh_ref.at[j][_sp(r0, rb)] = dcv[:, _lanes(j)]
            gv_ref[0:1, :] += jnp.sum(dln * xh, axis=0, keepdims=True)
            gv_ref[1:2, :] += jnp.sum(dln, axis=0, keepdims=True)
            gv_ref[2:3, :] += jnp.sum(dcv, axis=0, keepdims=True)
        dcv_n, _, _ = ln_bwd(dun_ref[...], cvn_ref[...])
        dcv_n = jnp.where(i < nt - 1, dcv_n, 0.0)
        for j in range(tiles):
            ln_ = _lanes(j)
            dsh_ref.at[j][_sp(tm, hl)] = dcv_n[:, ln_]
            glu_h = hv_ref[:, ln_] * _sigmoid(hg_ref[:, ln_])
            gsh_ref.at[j][_sp(0, hl)] = jnp.where(i > 0, glu_h, 0.0)
            for r0 in range(0, tm, cb):
                gsh_ref.at[j][_sp(hl + r0, cb)] = av_ref[r0:r0 + cb, ln_] * _sigmoid(ag_ref[r0:r0 + cb, ln_])

        for j in range(tiles):
            ln_ = _lanes(j)
            for r0 in range(0, tm, cb):
                dglu = jnp.zeros((cb, LANES), F32)
                for k in range(CONV_K):
                    dglu = dglu + w_ref[k:k + 1, ln_] * dsh_ref.at[j][_sp(r0 + (CONV_K - 1) - k, cb)]
                av = av_ref[r0:r0 + cb, ln_]
                sg = _sigmoid(ag_ref[r0:r0 + cb, ln_])
                dp_ref[r0:r0 + cb, ln_] = (dglu * sg).astype(BF16)
                dp_ref[r0:r0 + cb, _lanes(tiles + j)] = (dglu * av * sg * (1.0 - sg)).astype(BF16)
            for k in range(CONV_K):
                part = jnp.zeros((8, LANES), F32)
                for r0 in range(0, tm, cb):
                    prod = dsh_ref.at[j][_sp(r0, cb)] * gsh_ref.at[j][_sp(r0 + hl - (CONV_K - 1) + k, cb)]
                    part = part + jnp.sum(prod.reshape(cb // 8, 8, LANES), axis=0)
                gw_ref[k:k + 1, ln_] += jnp.sum(part, axis=0, keepdims=True)

    main = lambda col: pl.BlockSpec((tm, C), lambda i: (i, col))
    prev = lambda col: pl.BlockSpec((hl, C), lambda i: (jnp.maximum(i * per - 1, 0), col))
    nxt = pl.BlockSpec((hl, C), lambda i: (jnp.minimum((i + 1) * per, T // hl - 1), 0))
    vec = pl.BlockSpec((1, C), lambda i: (0, 0))
    return _call(
        body, dcat, dcat, cv, cv, proj, proj, proj, proj, conv_w, cn_g, cn_b, rider=rider, name="conv_bwd",
        grid=(nt,),
        in_specs=[main(0), nxt, main(0), nxt, main(0), main(1), prev(0), prev(1),
                  pl.BlockSpec((CONV_K, C), lambda i: (0, 0)), vec, vec],
        out_specs=[pl.BlockSpec((tm, 2 * C), lambda i: (i, 0)), pl.BlockSpec((8, C), lambda i: (0, 0)),
                   pl.BlockSpec((32, C), lambda i: (0, 0))],
        out_shape=[jax.ShapeDtypeStruct((T, NPROJ), BF16), jax.ShapeDtypeStruct((8, C), F32),
                   jax.ShapeDtypeStruct((32, C), F32)],
        scratch_shapes=[pltpu.VMEM((C // LANES, 2 * (tm + hl), LANES), F32)] * 2,
        compiler_params=_params(("arbitrary",), 48))


def _attn_bwd_prep(dcat, o_f32, bd):
    tm = 512

    def body(do_ref, o_ref, bd_ref, dl_ref):
        dl_ref[...] = _segsum(do_ref[...] * o_ref[...], bd_ref[...])

    blk = pl.BlockSpec((tm, C), lambda i: (i, 0))
    return _call(
        body, dcat, o_f32, bd, name="attn_bwd_prep", grid=(T // tm,),
        in_specs=[pl.BlockSpec((tm, C), lambda i: (i, 1)), blk, pl.BlockSpec((C, C), lambda i: (0, 0))],
        out_specs=[blk],
        out_shape=[jax.ShapeDtypeStruct((T, C), F32)],
        compiler_params=_params(("parallel",), 32))[0]


def _attn_bwd_unit(qs, dos, lgs, dls, rows, kc, vc, biasv, invalid_prev):
    qst, dost = _stack_heads(qs[rows, :]), _stack_heads(dos[rows, :])
    s = _nt(qst, kc) + biasv
    if invalid_prev is not None:
        col = lax.broadcasted_iota(jnp.int32, s.shape, 1)
        s = jnp.where((col < QB) & invalid_prev, NEG, s)
    p = jnp.exp(s - _stack_cols(lgs[rows, :]))
    ds = p * (_nt(dost, vc) - _stack_cols(dls[rows, :]))
    dsb = ds.astype(BF16)
    dq = _unstack_heads(jnp.dot(dsb, kc, preferred_element_type=F32))
    return dq, _tn_dot(dsb, qst), _tn_dot(p.astype(BF16), dost)


def _attn_bwd_lagged(qn, kn, proj, dcat, lg, dl, bias, d):
    assert QB * d == ATT_WIN
    n_win = T // ATT_WIN
    lanes = 2 * HEAD

    def body(q_ref, k_ref, v_ref, do_ref, lg_ref, dl_ref, kh_ref, vh_ref, bias_ref, dq_ref, dk_ref, dv_ref,
             qs, dos, lgs, dls, ks, vs, dqs, ck, cv, ok, ov):
        n = pl.program_id(1)

        @pl.when(n == 0)
        def _():
            ck[...] = jnp.zeros_like(ck)
            cv[...] = jnp.zeros_like(cv)

        @pl.when(n < n_win)
        def _():
            for r in range(d):
                rows = slice(r * QB, (r + 1) * QB)
                for dst, src, dt in ((qs, q_ref, BF16), (dos, do_ref, BF16), (lgs, lg_ref, F32), (dls, dl_ref, F32)):
                    dst[rows, :] = _stream(src, r, QB, d).astype(dt)
                for dst, halo, src in ((ks, kh_ref, k_ref), (vs, vh_ref, v_ref)):
                    dst[2 * r * QB:(2 * r + 1) * QB, :] = _stream(halo, r, QB, d).astype(BF16)
                    dst[(2 * r + 1) * QB:(2 * r + 2) * QB, :] = _stream(src, r, QB, d).astype(BF16)
            for r in range(d):
                rows = slice(r * QB, (r + 1) * QB)
                keys = slice(2 * r * QB, (2 * r + 2) * QB)
                dq, dkc, dvc = _attn_bwd_unit(qs, dos, lgs, dls, rows, ks[keys, :], vs[keys, :], bias_ref[...], n == 0)
                dqs[rows, :] = dq
                ok[rows, :] = ck[rows, :] + dkc[:QB]
                ov[rows, :] = cv[rows, :] + dvc[:QB]
                ck[rows, :] = dkc[QB:]
                cv[rows, :] = dvc[QB:]
            for r in range(d):
                rows = slice(r * QB, (r + 1) * QB)
                dq_ref[pl.ds(r, QB, stride=d), :] = dqs[rows, :]
                dk_ref[pl.ds(r, QB, stride=d), :] = ok[rows, :]
                dv_ref[pl.ds(r, QB, stride=d), :] = ov[rows, :]

        @pl.when(n == n_win)
        def _():
            for r in range(d):
                rows = slice(r * QB, (r + 1) * QB)
                dk_ref[pl.ds(r, QB, stride=d), :] = ck[rows, :]
                dv_ref[pl.ds(r, QB, stride=d), :] = cv[rows, :]

    cur = lambda off: pl.BlockSpec((ATT_WIN, lanes), lambda cb, n: (jnp.minimum(n, n_win - 1), off + cb))
    prev = lambda off: pl.BlockSpec(
        (ATT_WIN, lanes), lambda cb, n: (jnp.maximum(jnp.minimum(n, n_win - 1) - 1, 0), off + cb))
    late = pl.BlockSpec((ATT_WIN, lanes), lambda cb, n: (jnp.maximum(n - 1, 0), cb))
    buf = lambda rows, dt: pltpu.VMEM((rows, lanes), dt)
    return _call(
        body, qn, kn, proj, dcat, lg, dl, kn, proj, bias, name=f"attn_bwd_d{d}", grid=(C // lanes, n_win + 1),
        in_specs=[cur(0), cur(0), cur(V_COL), cur(DO_COL), cur(0), cur(0), prev(0), prev(V_COL),
                  pl.BlockSpec((None, 2 * QB, 2 * QB), lambda cb, n: (cb, 0, 0))],
        out_specs=[cur(0), late, late],
        out_shape=[jax.ShapeDtypeStruct((T, C), F32)] * 3,
        scratch_shapes=[buf(ATT_WIN, BF16), buf(ATT_WIN, BF16), buf(ATT_WIN, F32), buf(ATT_WIN, F32),
                        buf(2 * ATT_WIN, BF16), buf(2 * ATT_WIN, BF16)] + [buf(ATT_WIN, F32)] * 5,
        compiler_params=_params(("arbitrary", "arbitrary"), 48))


def _attn_bwd(qn, kn, proj, dcat, lg, dl, bias, d, rider=None):
    sl, nb, hr = _attn_geometry(d)
    slk = QB + sl
    slq = sl + QB
    n_win = T // ATT_WIN

    def body(q_ref, k_ref, v_ref, do_ref, lg_ref, dl_ref, kh_ref, vh_ref, qx_ref, dox_ref, lgx_ref, dlx_ref,
             bias_ref, dq_ref, dk_ref, dv_ref, qs, dos, lgs, dls, ks, vs, dqs, dks, dvs):
        n = pl.program_id(1)
        for r in range(d):
            for dst, src, nx, dt in ((qs, q_ref, qx_ref, BF16), (dos, do_ref, dox_ref, BF16),
                                     (lgs, lg_ref, lgx_ref, F32), (dls, dl_ref, dlx_ref, F32)):
                dst[r * slq:r * slq + sl, :] = _stream(src, r, sl, d).astype(dt)
                dst[r * slq + sl:(r + 1) * slq, :] = _stream(nx, r, QB, d).astype(dt)
            for dst, halo, src in ((ks, kh_ref, k_ref), (vs, vh_ref, v_ref)):
                dst[r * slk:r * slk + QB, :] = _stream(halo, r, QB, d).astype(BF16)
                dst[r * slk + QB:(r + 1) * slk, :] = _stream(src, r, sl, d).astype(BF16)
        dks[...] = jnp.zeros_like(dks)
        dvs[...] = jnp.zeros_like(dvs)

        def unit(rows, kc, vc, biasv, invalid_prev):
            return _attn_bwd_unit(qs, dos, lgs, dls, rows, kc, vc, biasv, invalid_prev)

        for r in range(d):
            for b in range(nb):
                rows = slice(r * slq + b * QB, r * slq + (b + 1) * QB)
                keys = slice(r * slk + b * QB, r * slk + (b + 2) * QB)
                dq, dkc, dvc = unit(rows, ks[keys, :], vs[keys, :], bias_ref[...], (n == 0) if b == 0 else None)
                dqs[r * sl + b * QB:r * sl + (b + 1) * QB, :] = dq
                if b == 0:
                    dks[r * sl:r * sl + QB, :] += dkc[QB:]
                    dvs[r * sl:r * sl + QB, :] += dvc[QB:]
                else:
                    dks[r * sl + (b - 1) * QB:r * sl + (b + 1) * QB, :] += dkc
                    dvs[r * sl + (b - 1) * QB:r * sl + (b + 1) * QB, :] += dvc

        @pl.when(n < n_win - 1)
        def _():
            for r in range(d):
                rows = slice(r * slq + sl, (r + 1) * slq)
                keys = slice(r * slk + sl, (r + 1) * slk)
                _, dkc, dvc = unit(rows, ks[keys, :], vs[keys, :], bias_ref[:, 0:QB], None)
                dks[(r + 1) * sl - QB:(r + 1) * sl, :] += dkc
                dvs[(r + 1) * sl - QB:(r + 1) * sl, :] += dvc

        for dst, src in ((dq_ref, dqs), (dk_ref, dks), (dv_ref, dvs)):
            for r in range(d):
                if d > 1:
                    dst[pl.ds(r, sl, stride=d), :] = src[r * sl:(r + 1) * sl, :]
                else:
                    dst[...] = src[...]

    main, prev, nxt, bias_spec = _attn_specs(d)
    lanes = 2 * HEAD
    return _call(
        body, qn, kn, proj, dcat, lg, dl, kn, proj, qn, dcat, lg, dl, bias, rider=rider, name=f"attn_bwd_d{d}",
        grid=(C // lanes, n_win),
        in_specs=[main(0), main(0), main(V_COL), main(DO_COL), main(0), main(0), prev(0), prev(V_COL),
                  nxt(0), nxt(DO_COL), nxt(0), nxt(0), bias_spec],
        out_specs=[main(0)] * 3,
        out_shape=[jax.ShapeDtypeStruct((T, C), F32)] * 3,
        scratch_shapes=[pltpu.VMEM((ATT_WIN + hr, lanes), BF16), pltpu.VMEM((ATT_WIN + hr, lanes), BF16),
                        pltpu.VMEM((ATT_WIN + hr, lanes), F32), pltpu.VMEM((ATT_WIN + hr, lanes), F32),
                        pltpu.VMEM((ATT_WIN + hr, lanes), BF16), pltpu.VMEM((ATT_WIN + hr, lanes), BF16),
                        pltpu.VMEM((ATT_WIN, lanes), F32), pltpu.VMEM((ATT_WIN, lanes), F32),
                        pltpu.VMEM((ATT_WIN, lanes), F32)],
        compiler_params=_params(("arbitrary", "arbitrary"), 48))


def _qk_norm_bwd(d3, proj, col, gain, bd, dproj, name):
    tm = 512

    def body(d0, d1, d2, x_ref, g_ref, bd_ref, dp_in, dp_ref, gg_ref):
        del dp_in

        @pl.when(pl.program_id(0) == 0)
        def _():
            gg_ref[...] = jnp.zeros_like(gg_ref)
        dn = d0[...] + d1[...] + d2[...]
        xv = x_ref[...]
        r = lax.rsqrt(_segsum(xv * xv, bd_ref[...]) * (1.0 / HEAD) + EPS)
        t = dn * g_ref[...]
        mean_tx = _segsum(t * xv, bd_ref[...]) * (1.0 / HEAD)
        dp_ref[...] = (r * t - xv * (r * r * r) * mean_tx).astype(BF16)
        gg_ref[...] += jnp.sum(dn * xv * r, axis=0, keepdims=True)

    blk = pl.BlockSpec((tm, C), lambda i: (i, 0))
    vec = pl.BlockSpec((1, C), lambda i: (0, 0))
    return _call(
        body, *d3, proj, gain, bd, dproj, name=name, grid=(T // tm,),
        in_specs=[blk, blk, blk, pl.BlockSpec((tm, C), lambda i: (i, col)), vec,
                  pl.BlockSpec((C, C), lambda i: (0, 0)), ANY],
        out_specs=[pl.BlockSpec((tm, C), lambda i: (i, col)), vec],
        out_shape=[jax.ShapeDtypeStruct((T, NPROJ), BF16), jax.ShapeDtypeStruct((1, C), F32)],
        input_output_aliases={6: 0},
        compiler_params=_params(("arbitrary",), 32))


def _v_bwd(d3, dproj):
    tm = 512

    def body(d0, d1, d2, dp_in, dp_ref):
        del dp_in
        dp_ref[...] = (d0[...] + d1[...] + d2[...]).astype(BF16)

    blk = pl.BlockSpec((tm, C), lambda i: (i, 0))
    return _call(
        body, *d3, dproj, name="v_bwd", grid=(T // tm,),
        in_specs=[blk, blk, blk, ANY],
        out_specs=[pl.BlockSpec((tm, C), lambda i: (i, 4))],
        out_shape=[jax.ShapeDtypeStruct((T, NPROJ), BF16)],
        input_output_aliases={3: 0},
        compiler_params=_params(("parallel",), 32))[0]


def _adamw(w, g, m, v):
    m = ADAM_B1 * m + (1.0 - ADAM_B1) * g
    v = ADAM_B2 * v + (1.0 - ADAM_B2) * (g * g)
    m_hat = m / (1.0 - ADAM_B1 ** ADAM_STEP)
    v_hat = v / (1.0 - ADAM_B2 ** ADAM_STEP)
    delta = -ADAM_LR * (m_hat / (jnp.sqrt(v_hat) + ADAM_EPS) + ADAM_WD * w)
    return delta, m, v


def _row_block(shape):
    rows = shape[0]
    for cand in (256, 128, 64, 88, 32, 8):
        if rows % cand == 0 and cand * shape[1] * 4 <= (2 << 20):
            return cand
    return 8


def _partial_sum(own, recv, name):
    br = _row_block(own.shape)
    cols = own.shape[1]

    def body(o_ref, r_ref, p_ref):
        p_ref[...] = ((o_ref[...] + r_ref[0].astype(F32)) + r_ref[1].astype(F32)) + r_ref[2].astype(F32)

    blk = pl.BlockSpec((br, cols), lambda i: (i, 0))
    return _call(
        body, own, recv, name=name, grid=(own.shape[0] // br,),
        in_specs=[blk, pl.BlockSpec((3, br, cols), lambda i: (0, i, 0))], out_specs=[blk],
        out_shape=[jax.ShapeDtypeStruct(own.shape, F32)],
        compiler_params=_params(("parallel",), 32))[0]


def _adamw_mat(p_own, p_sib, w, m, v, name):
    br = _row_block(w.shape)
    cols = w.shape[1]

    def body(a_ref, b_ref, w_ref, m_ref, v_ref, g_ref, d_ref, nm_ref, nv_ref):
        g = a_ref[...] + b_ref[...]
        delta, nm, nv = _adamw(w_ref[...], g, m_ref[...], v_ref[...])
        g_ref[...] = g
        d_ref[...] = delta
        nm_ref[...] = nm
        nv_ref[...] = nv

    blk = pl.BlockSpec((br, cols), lambda i: (i, 0))
    return _call(
        body, p_own, p_sib, w, m, v, name=name, grid=(w.shape[0] // br,),
        in_specs=[blk] * 5, out_specs=[blk] * 4,
        out_shape=[jax.ShapeDtypeStruct(w.shape, F32)] * 4,
        compiler_params=_params(("parallel",), 40))


def _vec_reduce(vrecv):
    def body(v_ref, o_ref):
        acc = v_ref[0]
        for r in range(1, N_DEV):
            acc = acc + v_ref[r]
        o_ref[...] = acc

    return pl.pallas_call(
        body, name="vec_reduce",
        out_shape=jax.ShapeDtypeStruct((VPACK_ROWS, D), F32),
        compiler_params=_params((), 32),
    )(vrecv)


def _adamw_small(w, g, m, v):
    def body(w_ref, g_ref, m_ref, v_ref, d_ref, nm_ref, nv_ref):
        delta, nm, nv = _adamw(w_ref[...], g_ref[...], m_ref[...], v_ref[...])
        d_ref[...] = delta
        nm_ref[...] = nm
        nv_ref[...] = nv

    return pl.pallas_call(
        body, name="adamw_small",
        out_shape=[jax.ShapeDtypeStruct(w.shape, F32)] * 3,
        compiler_params=_params((), 32),
    )(w, g, m, v)


def _pack(parts, rows):
    flat = jnp.concatenate([p.reshape(-1) for p in parts])
    return jnp.pad(flat, (0, rows * D - flat.shape[0])).reshape(rows, D)


def _unpack(packed, shapes):
    flat = packed.reshape(-1)
    out, off = [], 0
    for shp in shapes:
        size = 1
        for s in shp:
            size *= s
        out.append(flat[off:off + size].reshape(shp))
        off += size
    return out


def _no_comm(shards, row_sharded, peers=(0, 1, 2), into=None):
    del row_sharded, peers, into
    return None, lambda res, n: (res, shards)


def _with_comm(shards, row_sharded, peers=(0, 1, 2), into=None):
    rider = _gather_rider(shards, row_sharded, peers, into)
    return rider, lambda res, n: (res[:n], res[n:])


def _local_step(x, target, norm1_g, conv_w, conv_b, cn_g, cn_b, q_norm_g, k_norm_g, norm2_g, ffconv_w, ffconv_b,
                w_in, late_weights, comm=True):
    row = lambda a: a.reshape(1, -1)
    head_of = jnp.arange(C) // HEAD
    bd = (head_of[:, None] == head_of[None, :]).astype(BF16)
    qg = row(jnp.tile(q_norm_g, C // HEAD) * (HEAD ** -0.5))
    kg = row(jnp.tile(k_norm_g, C // HEAD))
    biases = [_alibi_tables(d) for d in PATTERN_DILATIONS]
    gather = _with_comm if comm else _no_comm
    grad_rider = (lambda g, rs: _grad_rider(g[1], g[0], rs)) if comm else (lambda g, rs: None)

    rider, split = gather(late_weights[0:1], (True,))
    (h, proj), (w_out,) = split(_proj_fwd(x, row(norm1_g), w_in, rider), 2)
    rider, split = gather(late_weights[1:2], (False,), (0, 1))
    (cat, cv), w_up_part = split(_conv_fwd(proj, conv_w, row(conv_b), row(cn_g), row(cn_b), rider), 2)
    qn, kn = _qkv_prep(proj, qg, kg, bd)
    fwd = []
    for i, d in enumerate(PATTERN_DILATIONS):
        rider, split = (gather(late_weights[1:2], (False,), (2,), w_up_part) if i == 2 else
                        (None, lambda res, n: (res, None)))
        res, last = split(_attn_fwd(qn, kn, proj, biases[i], d, rider), 2)
        fwd.append(res)
    (w_up,) = last
    cat, o_f32, lg = _attn_merge([f[0] for f in fwd], [f[1] for f in fwd], cat)
    rider, split = gather(late_weights[2:3], (True,))
    (x1, h2, up), (w_down,) = split(_out_up(x, cat, w_out, row(norm2_g), w_up, rider), 3)
    upc, act, dy, loss_acc = _ffn_down(up, ffconv_w, row(ffconv_b), w_down, x1, target)

    dup, gff = _down_bwd(dy, w_down, up, upc, ffconv_w)
    gw_down = _weight_grad(act, dy, DFF // 2, D, 1024, "grad_w_down")
    res = _norm_bwd_mm(dup, w_up, x1, dy, row(norm2_g), "up_bwd", grad_rider(gw_down, True))
    (dx1, g_norm2), ex_down = res[:2], res[2:]
    gw_up = _weight_grad(h2, dup, D, NUP // 4, 2048, "grad_w_up")
    dcat = _outproj_bwd(dx1, w_out)
    gw_out = _weight_grad(cat, dx1, D, D, 2048, "grad_w_out")
    res = _conv_bwd(dcat, cv, proj, conv_w, row(cn_g), row(cn_b), grad_rider(gw_up, False))
    (dproj, gconv_vec, gconv_w), ex_up = res[:3], res[3:]
    dl = _attn_bwd_prep(dcat, o_f32, bd)
    bwd, ex_out = [], []
    for i, d in enumerate(PATTERN_DILATIONS):
        if QB * d == ATT_WIN:
            res = _attn_bwd_lagged(qn, kn, proj, dcat, lg, dl, biases[i], d)
        else:
            res = _attn_bwd(qn, kn, proj, dcat, lg, dl, biases[i], d, grad_rider(gw_out, True) if i == 0 else None)
        bwd.append(res[:3])
        ex_out = res[3:] if i == 0 else ex_out
    dproj, gq_lane = _qk_norm_bwd([b[0] for b in bwd], proj, 2, qg, bd, dproj, "q_norm_bwd")
    dproj, gk_lane = _qk_norm_bwd([b[1] for b in bwd], proj, 3, kg, bd, dproj, "k_norm_bwd")
    dproj = _v_bwd([b[2] for b in bwd], dproj)
    gw_in = _weight_grad(h, dproj, D, NPROJ // 4, 2048, "grad_w_in")
    res = _norm_bwd_mm(dproj, w_in, x, dx1, row(norm1_g), "in_bwd", grad_rider(gw_in, False))
    (dx, g_norm1), ex_in = res[:2], res[2:]

    loss = loss_acc[0, 0] * (0.5 / D)
    g_qg = jnp.sum(gq_lane.reshape(C // HEAD, HEAD), axis=0) * (HEAD ** -0.5)
    g_kg = jnp.sum(gk_lane.reshape(C // HEAD, HEAD), axis=0)
    small = [g_norm1[0], gconv_vec[2], gconv_vec[0], gconv_vec[1], g_qg, g_kg, g_norm2[0], gff[3],
             gconv_w[:CONV_K], gff[:FF_K]]
    mats = [ex_in, ex_out, ex_up, ex_down] if comm else [gw_in, gw_out, gw_up, gw_down]
    return loss, dx, mats, small


def kernel(x, norm1_g, w_in, conv_w, conv_b, cn_g, cn_b, q_norm_g, k_norm_g, w_out, norm2_g, w_up, ffconv_w, ffconv_b, w_down, loss_target, m_norm1_g, m_w_in, m_conv_w, m_conv_b, m_cn_g, m_cn_b, m_q_norm_g, m_k_norm_g, m_w_out, m_norm2_g, m_w_up, m_ffconv_w, m_ffconv_b, m_w_down, v_norm1_g, v_w_in, v_conv_w, v_conv_b, v_cn_g, v_cn_b, v_q_norm_g, v_k_norm_g, v_w_out, v_norm2_g, v_w_up, v_ffconv_w, v_ffconv_b, v_w_down):
    chip = 2 * lax.axis_index("x") + lax.axis_index("y")

    w_in_full, conv_w_full, ffconv_w_full = _gather_now([w_in.astype(BF16), conv_w, ffconv_w], (False, False, False))
    loss, dx, mats, small = _local_step(
        x[0], loss_target[0], norm1_g, conv_w_full, conv_b, cn_g, cn_b, q_norm_g, k_norm_g, norm2_g,
        ffconv_w_full, ffconv_b, w_in_full, [w.astype(BF16) for w in (w_out, w_up, w_down)])

    names = ("w_in", "w_out", "w_up", "w_down")
    parts = [_partial_sum(own, recv, "partial_" + names[k]) for k, (recv, own) in enumerate(mats)]
    sib, vrecv = _final_exchange(parts, _pack(small, VPACK_ROWS))
    ws = (w_in, w_out, w_up, w_down)
    ms = (m_w_in, m_w_out, m_w_up, m_w_down)
    vs = (v_w_in, v_w_out, v_w_up, v_w_down)
    mat = [_adamw_mat(parts[k], sib[k], ws[k], ms[k], vs[k], "adamw_" + names[k]) for k in range(4)]

    vsum = _vec_reduce(vrecv)
    vec_shapes = [(D,), (C,), (C,), (C,), (HEAD,), (HEAD,), (D,), (NUP,), (CONV_K, C), (FF_K, NUP)]
    gsmall = _unpack(vsum, vec_shapes)
    g_conv_w = lax.dynamic_slice_in_dim(gsmall[8], chip * (C // N_CHIPS), C // N_CHIPS, axis=1)
    g_ffconv_w = lax.dynamic_slice_in_dim(gsmall[9], chip * (NUP // N_CHIPS), NUP // N_CHIPS, axis=1)
    gs = gsmall[:8] + [g_conv_w, g_ffconv_w]
    w_s = [norm1_g, conv_b, cn_g, cn_b, q_norm_g, k_norm_g, norm2_g, ffconv_b, conv_w, ffconv_w]
    m_s = [m_norm1_g, m_conv_b, m_cn_g, m_cn_b, m_q_norm_g, m_k_norm_g, m_norm2_g, m_ffconv_b, m_conv_w, m_ffconv_w]
    v_s = [v_norm1_g, v_conv_b, v_cn_g, v_cn_b, v_q_norm_g, v_k_norm_g, v_norm2_g, v_ffconv_b, v_conv_w, v_ffconv_w]
    shapes_s = [a.shape for a in w_s]
    d_p, m_p, v_p = _adamw_small(_pack(w_s, SPACK_ROWS), _pack(gs, SPACK_ROWS), _pack(m_s, SPACK_ROWS),
                                 _pack(v_s, SPACK_ROWS))
    d_s, nm_s, nv_s = _unpack(d_p, shapes_s), _unpack(m_p, shapes_s), _unpack(v_p, shapes_s)

    def ordered(sm, mt):
        return [sm[0], mt[0], sm[8], sm[1], sm[2], sm[3], sm[4], sm[5], mt[1], sm[6], mt[2], sm[9], sm[7], mt[3]]

    loss_all = lax.psum(loss, ("x", "y", "c"))
    grads = ordered(gs, [r[0] for r in mat])
    deltas = ordered(d_s, [r[1] for r in mat])
    new_m = ordered(nm_s, [r[2] for r in mat])
    new_v = ordered(nv_s, [r[3] for r in mat])
    return (loss_all, dx[None], *grads, *deltas, *new_m, *new_v)
```

```python
import types

import jax
import jax.numpy as jnp
from jax import lax
from jax.experimental import pallas as pl
from jax.experimental.pallas import tpu as pltpu

T = 8192
D = 1024
C = 512
NPROJ = 2560
DFF = 2816
NUP = 2 * DFF
CONV_K = 31
FF_K = 3
HEAD = 64
EPS = 1e-6
NEG = -1e30
N_CHIPS = 4
N_DEV = 8
PATTERN_DILATIONS = (1, 4, 16)
QB = 128

ADAM_LR = 0.001
ADAM_B1 = 0.9
ADAM_B2 = 0.999
ADAM_EPS = 1e-08
ADAM_WD = 0.01
ADAM_STEP = 10

F32 = jnp.float32
BF16 = jnp.bfloat16
MESH = pl.DeviceIdType.MESH
ANY = pl.BlockSpec(memory_space=pl.ANY)

VPACK_ROWS = 48
SPACK_ROWS = 24


def _params(sem, vmem_mb):
    return pltpu.CompilerParams(dimension_semantics=sem, vmem_limit_bytes=vmem_mb << 20)


def _resident(shape):
    return pl.BlockSpec(shape, lambda i: (0, 0), pipeline_mode=pl.Buffered(1))


def _nt(a, b):
    return lax.dot_general(a, b, (((1,), (1,)), ((), ())), preferred_element_type=F32)


def _tn_dot(a, b):
    return lax.dot_general(a, b, (((0,), (0,)), ((), ())), preferred_element_type=F32)


def _sigmoid(x):
    return 1.0 / (1.0 + jnp.exp(-x))


def _segsum(x, bd):
    hi = x.astype(BF16)
    lo = (x - hi.astype(F32)).astype(BF16)
    return (jnp.dot(hi, bd, preferred_element_type=F32)
            + jnp.dot(lo, bd, preferred_element_type=F32))


def _place():
    x, y, c = lax.axis_index("x"), lax.axis_index("y"), lax.axis_index("c")
    chips = [(1 - x, y), (x, 1 - y), (1 - x, 1 - y)]
    return x, y, c, chips


def _block_of(ref, shard_shape, row_sharded, s):
    r, cdim = shard_shape
    if row_sharded:
        return ref.at[pl.ds(s * r, r), :]
    return ref.at[:, pl.ds(s * cdim, cdim)]


def _full_shape(shard_shape, row_sharded):
    r, cdim = shard_shape
    return (r * N_CHIPS, cdim) if row_sharded else (r, cdim * N_CHIPS)


def _gather_rider(shards, row_sharded, peers=(0, 1, 2), into=None):
    n = len(shards)
    shapes = [a.shape for a in shards]

    def copies(ins, outs, sems):
        send_sems, recv_sems, local_sems = sems
        x, y, c, chips = _place()
        me = 2 * x + y
        place = lambda k, s: _block_of(outs[k], shapes[k], row_sharded[k], s)
        local = []
        if into is None:
            local = [pltpu.make_async_copy(ins[k], place(k, me), local_sems.at[k]) for k in range(n)]
        sends, recvs = [], []
        for k in range(n):
            for j in peers:
                px, py = chips[j]
                sem = dict(send_sem=send_sems.at[3 * k + j], recv_sem=recv_sems.at[3 * k + j],
                           device_id=(px, py, c), device_id_type=MESH)
                sends.append(pltpu.make_async_remote_copy(src_ref=ins[k], dst_ref=place(k, me), **sem))
                recvs.append(pltpu.make_async_remote_copy(src_ref=ins[k], dst_ref=place(k, 2 * px + py), **sem))
        return local, sends, recvs

    return types.SimpleNamespace(
        operands=list(shards) + list(into or []), copies=copies,
        aliases={n + k: k for k in range(n)} if into is not None else {},
        out_shape=[jax.ShapeDtypeStruct(_full_shape(s, rs), a.dtype) for s, rs, a in zip(shapes, row_sharded, shards)],
        sems=[pltpu.SemaphoreType.DMA((3 * n,)), pltpu.SemaphoreType.DMA((3 * n,)), pltpu.SemaphoreType.DMA((n,))])


def _grad_rider(g_bf16, g_f32, row_sharded):
    shard = tuple(d // N_CHIPS if (i == 0) == row_sharded else d for i, d in enumerate(g_f32.shape))

    def copies(ins, outs, sems):
        send_sems, recv_sems, local_sems = sems
        gb, gf = ins
        rec, own = outs
        x, y, c, chips = _place()
        me = 2 * x + y
        local = [pltpu.make_async_copy(_block_of(gf, shard, row_sharded, me), own, local_sems.at[0])]
        sends, recvs = [], []
        for j, (px, py) in enumerate(chips):
            sem = dict(send_sem=send_sems.at[j], recv_sem=recv_sems.at[j], device_id=(px, py, c), device_id_type=MESH)
            sends.append(pltpu.make_async_remote_copy(
                src_ref=_block_of(gb, shard, row_sharded, 2 * px + py), dst_ref=rec.at[j], **sem))
            recvs.append(pltpu.make_async_remote_copy(
                src_ref=_block_of(gb, shard, row_sharded, me), dst_ref=rec.at[j], **sem))
        return local, sends, recvs

    return types.SimpleNamespace(
        operands=[g_bf16, g_f32], copies=copies, aliases={},
        out_shape=[jax.ShapeDtypeStruct((3,) + shard, BF16), jax.ShapeDtypeStruct(shard, F32)],
        sems=[pltpu.SemaphoreType.DMA((3,)), pltpu.SemaphoreType.DMA((3,)), pltpu.SemaphoreType.DMA((1,))])


def _rider_start(rider, ins, outs, sems):
    local, sends, _ = rider.copies(ins, outs, sems)
    for cp in local + sends:
        cp.start()


def _rider_wait(rider, ins, outs, sems):
    local, sends, recvs = rider.copies(ins, outs, sems)
    for cp in recvs:
        cp.wait_recv()
    for cp in sends:
        cp.wait_send()
    for cp in local:
        cp.wait()


def _call(body, *operands, rider=None, name, grid, in_specs, out_specs, out_shape, scratch_shapes=(),
          compiler_params, input_output_aliases=None):
    operands = [pltpu.with_memory_space_constraint(a, pltpu.HBM) for a in operands]
    if rider is None:
        return pl.pallas_call(
            body, name=name, grid=grid, in_specs=list(in_specs), out_specs=list(out_specs), out_shape=list(out_shape),
            scratch_shapes=list(scratch_shapes), compiler_params=compiler_params,
            input_output_aliases=input_output_aliases or {})(*operands)
    n_in, n_out, n_scr = len(in_specs), len(out_specs), len(scratch_shapes)
    r_in, r_out = len(rider.operands), len(rider.out_shape)

    def riding(*refs):
        refs = list(refs)
        ins, refs = refs[:n_in], refs[n_in:]
        r_ins, refs = refs[:r_in], refs[r_in:]
        outs, refs = refs[:n_out], refs[n_out:]
        r_outs, refs = refs[:r_out], refs[r_out:]
        scr, sems = refs[:n_scr], refs[n_scr:]
        first = pl.program_id(0) == 0
        last = pl.program_id(0) == grid[0] - 1
        for axis in range(1, len(grid)):
            first = first & (pl.program_id(axis) == 0)
            last = last & (pl.program_id(axis) == grid[axis] - 1)

        @pl.when(first)
        def _():
            _rider_start(rider, r_ins, r_outs, sems)

        body(*ins, *outs, *scr)

        @pl.when(last)
        def _():
            _rider_wait(rider, r_ins, r_outs, sems)

    return pl.pallas_call(
        riding, name=name, grid=grid, in_specs=list(in_specs) + [ANY] * r_in,
        out_specs=list(out_specs) + [ANY] * r_out, out_shape=list(out_shape) + list(rider.out_shape),
        scratch_shapes=list(scratch_shapes) + list(rider.sems), compiler_params=compiler_params,
        input_output_aliases={**(input_output_aliases or {}),
                              **{n_in + i: n_out + o for i, o in rider.aliases.items()}})(
            *operands, *[pltpu.with_memory_space_constraint(a, pltpu.HBM) for a in rider.operands])


def _gather_now(shards, row_sharded):
    rider = _gather_rider(shards, row_sharded)
    n = len(shards)

    def body(*refs):
        ins, outs, sems = refs[:n], refs[n:2 * n], refs[2 * n:]
        _rider_start(rider, ins, outs, sems)
        _rider_wait(rider, ins, outs, sems)

    return pl.pallas_call(
        body, name="gather_first", out_shape=rider.out_shape, in_specs=[ANY] * n, out_specs=[ANY] * n,
        scratch_shapes=rider.sems)(*shards)


def _final_exchange(parts, vpack):
    def body(p0, p1, p2, p3, v_ref, o0, o1, o2, o3, vr_ref, send_sems, recv_sems, vsend_sems, vrecv_sems, local_sem):
        x, y, c, _ = _place()
        me = 4 * x + 2 * y + c
        mine = pltpu.make_async_copy(v_ref, vr_ref.at[me], local_sem)
        mine.start()
        copies = [pltpu.make_async_remote_copy(
            src_ref=p, dst_ref=o, send_sem=send_sems.at[k], recv_sem=recv_sems.at[k],
            device_id=(x, y, 1 - c), device_id_type=MESH)
            for k, (p, o) in enumerate(zip((p0, p1, p2, p3), (o0, o1, o2, o3)))]
        flips = [(fx, fy, fc) for fx in (0, 1) for fy in (0, 1) for fc in (0, 1)][1:]
        recvs = []
        for r, (fx, fy, fc) in enumerate(flips):
            peer = (x ^ fx, y ^ fy, c ^ fc)
            sem = dict(send_sem=vsend_sems.at[r], recv_sem=vrecv_sems.at[r], device_id=peer, device_id_type=MESH)
            copies.append(pltpu.make_async_remote_copy(src_ref=v_ref, dst_ref=vr_ref.at[me], **sem))
            recvs.append(pltpu.make_async_remote_copy(
                src_ref=v_ref, dst_ref=vr_ref.at[4 * peer[0] + 2 * peer[1] + peer[2]], **sem))
        for cp in copies:
            cp.start()
        for cp in copies[:4]:
            cp.wait_recv()
        for cp in recvs:
            cp.wait_recv()
        for cp in copies:
            cp.wait_send()
        mine.wait()

    res = pl.pallas_call(
        body, name="final_exchange",
        out_shape=[jax.ShapeDtypeStruct(p.shape, F32) for p in parts]
        + [jax.ShapeDtypeStruct((N_DEV, VPACK_ROWS, D), F32)],
        in_specs=[ANY] * 5, out_specs=[ANY] * 5,
        scratch_shapes=[pltpu.SemaphoreType.DMA((4,)), pltpu.SemaphoreType.DMA((4,)),
                        pltpu.SemaphoreType.DMA((7,)), pltpu.SemaphoreType.DMA((7,)), pltpu.SemaphoreType.DMA],
    )(*parts, vpack)
    return res[:4], res[4]


def _proj_fwd(x, g1, w_in, rider):
    tm, tn = 512, 640

    def body(x_ref, g_ref, w_ref, h_ref, p_ref):
        xv = x_ref[...]
        r = lax.rsqrt(jnp.mean(xv * xv, axis=-1, keepdims=True) + EPS)
        h_ref[...] = (xv * r * g_ref[...]).astype(BF16)
        for j in range(NPROJ // tn):
            cols = slice(j * tn, (j + 1) * tn)
            p_ref[:, cols] = jnp.dot(h_ref[...], w_ref[:, cols], preferred_element_type=F32)

    return _call(
        body, x, g1, w_in, rider=rider, name="proj_fwd", grid=(T // tm,),
        in_specs=[pl.BlockSpec((tm, D), lambda i: (i, 0)), pl.BlockSpec((1, D), lambda i: (0, 0)),
                  _resident((D, NPROJ))],
        out_specs=[pl.BlockSpec((tm, D), lambda i: (i, 0)), pl.BlockSpec((tm, NPROJ), lambda i: (i, 0))],
        out_shape=[jax.ShapeDtypeStruct((T, D), BF16), jax.ShapeDtypeStruct((T, NPROJ), F32)],
        compiler_params=_params(("arbitrary",), 40))


CONV_TM = 512
CONV_HALO = 32
CONV_RB = 32
CONV_CB = 64


LANES = 128


def _sp(start, n):
    return (pl.ds(2 * start, n, stride=2), slice(None))


def _lanes(tile):
    return slice(tile * LANES, (tile + 1) * LANES)


def _conv_fwd(proj, conv_w, conv_b, cn_g, cn_b, rider):
    tm, hl, rb, cb = CONV_TM, CONV_HALO, CONV_RB, CONV_CB
    per = tm // hl

    def body(av_ref, ag_ref, hv_ref, hg_ref, w_ref, b_ref, g_ref, bb_ref, cat_ref, cv_ref, sh_ref):
        i = pl.program_id(0)
        for j in range(C // LANES):
            ln_ = _lanes(j)
            glu_h = hv_ref[:, ln_] * _sigmoid(hg_ref[:, ln_])
            sh_ref.at[j][_sp(0, hl)] = jnp.where(i > 0, glu_h, 0.0)
            for r0 in range(0, tm, cb):
                sh_ref.at[j][_sp(hl + r0, cb)] = av_ref[r0:r0 + cb, ln_] * _sigmoid(ag_ref[r0:r0 + cb, ln_])
            for r0 in range(0, tm, cb):
                acc = jnp.zeros((cb, LANES), F32) + b_ref[:, ln_]
                for k in range(CONV_K):
                    acc = acc + w_ref[k:k + 1, ln_] * sh_ref.at[j][_sp(r0 + hl - (CONV_K - 1) + k, cb)]
                cv_ref[r0:r0 + cb, ln_] = acc
        for r0 in range(0, tm, rb):
            acc = cv_ref[r0:r0 + rb, :]
            mu = jnp.mean(acc, axis=-1, keepdims=True)
            xc = acc - mu
            var = jnp.mean(xc * xc, axis=-1, keepdims=True)
            ln = xc * lax.rsqrt(var + EPS) * g_ref[...] + bb_ref[...]
            cat_ref[r0:r0 + rb, :] = (ln * _sigmoid(ln)).astype(BF16)

    halo = lambda col: pl.BlockSpec((hl, C), lambda i: (jnp.maximum(i * per - 1, 0), col))
    vec = pl.BlockSpec((1, C), lambda i: (0, 0))
    return _call(
        body, proj, proj, proj, proj, conv_w, conv_b, cn_g, cn_b, rider=rider, name="conv_fwd", grid=(T // tm,),
        in_specs=[pl.BlockSpec((tm, C), lambda i: (i, 0)), pl.BlockSpec((tm, C), lambda i: (i, 1)),
                  halo(0), halo(1), pl.BlockSpec((CONV_K, C), lambda i: (0, 0)), vec, vec, vec],
        out_specs=[pl.BlockSpec((tm, C), lambda i: (i, 0)), pl.BlockSpec((tm, C), lambda i: (i, 0))],
        out_shape=[jax.ShapeDtypeStruct((T, D), BF16), jax.ShapeDtypeStruct((T, C), F32)],
        scratch_shapes=[pltpu.VMEM((C // LANES, 2 * (tm + hl), LANES), F32)],
        compiler_params=_params(("arbitrary",), 40))


def _qkv_prep(proj, qg, kg, bd):
    tm = 512

    def body(q_ref, k_ref, qg_ref, kg_ref, bd_ref, qn_ref, kn_ref):
        for src, g, dst in ((q_ref, qg_ref, qn_ref), (k_ref, kg_ref, kn_ref)):
            xv = src[...]
            ms = _segsum(xv * xv, bd_ref[...]) * (1.0 / HEAD)
            dst[...] = xv * lax.rsqrt(ms + EPS) * g[...]

    col = lambda c: pl.BlockSpec((tm, C), lambda i: (i, c))
    vec = pl.BlockSpec((1, C), lambda i: (0, 0))
    out = pl.BlockSpec((tm, C), lambda i: (i, 0))
    return _call(
        body, proj, proj, qg, kg, bd, name="qkv_prep", grid=(T // tm,),
        in_specs=[col(2), col(3), vec, vec, pl.BlockSpec((C, C), lambda i: (0, 0))],
        out_specs=[out, out],
        out_shape=[jax.ShapeDtypeStruct((T, C), F32)] * 2,
        compiler_params=_params(("parallel",), 32))


def _stack_heads(a):
    lane = lax.broadcasted_iota(jnp.int32, a.shape, 1)
    zero = jnp.zeros_like(a)
    return jnp.concatenate([jnp.where(lane < HEAD, a, zero), jnp.where(lane >= HEAD, a, zero)], axis=0)


def _unstack_heads(a2):
    lane = lax.broadcasted_iota(jnp.int32, (QB, 2 * HEAD), 1)
    return jnp.where(lane < HEAD, a2[:QB], a2[QB:])


def _stack_cols(a):
    return jnp.concatenate([a[:, 0:1], a[:, HEAD:HEAD + 1]], axis=0)


ATT_WIN = 2048
V_COL = 4 * C // (2 * HEAD)
DO_COL = C // (2 * HEAD)


def _attn_geometry(d):
    sl = ATT_WIN // d
    return sl, sl // QB, QB * d


def _stream(ref, r, n, d):
    return ref[pl.ds(r, n, stride=d), :] if d > 1 else ref[pl.ds(r, n), :]


def _alibi_tables(d):
    qi = jnp.arange(QB)[:, None]
    kj = jnp.arange(2 * QB)[None, :]
    delta = qi + QB - kj
    band = (delta >= 0) & (delta <= QB)
    dist = (delta * d).astype(F32)
    heads = jnp.arange(8, dtype=F32)
    slopes = 2.0 ** (-(heads + 1.0))
    t = jnp.where(band[None], -slopes[:, None, None] * dist[None], NEG)
    return t.reshape(4, 2 * QB, 2 * QB)


def _attn_specs(d):
    _, _, hr = _attn_geometry(d)
    per = ATT_WIN // hr
    main = lambda off: pl.BlockSpec((ATT_WIN, 2 * HEAD), lambda cb, n: (n, off + cb))
    prev = lambda off: pl.BlockSpec((hr, 2 * HEAD), lambda cb, n: (jnp.maximum(n * per - 1, 0), off + cb))
    nxt = lambda off: pl.BlockSpec((hr, 2 * HEAD), lambda cb, n: (jnp.minimum((n + 1) * per, T // hr - 1), off + cb))
    bias = pl.BlockSpec((None, 2 * QB, 2 * QB), lambda cb, n: (cb, 0, 0))
    return main, prev, nxt, bias


def _attn_fwd(qn, kn, proj, bias, d, rider=None, merge=None):
    sl, nb, hr = _attn_geometry(d)
    slk = QB + sl
    mrows = 256

    def body(q_ref, k_ref, v_ref, kh_ref, vh_ref, bias_ref, *rest):
        if merge is None:
            o_ref, l_ref, qs, ks, vs, os_, ls = rest
        else:
            oa_ref, la_ref, ob_ref, lb_ref, _, cat_ref, of_ref, lg_ref, qs, ks, vs, os_, ls, o_ref, l_ref = rest
        n = pl.program_id(1)
        for r in range(d):
            qs[r * sl:(r + 1) * sl, :] = _stream(q_ref, r, sl, d).astype(BF16)
            for dst, halo, src in ((ks, kh_ref, k_ref), (vs, vh_ref, v_ref)):
                dst[r * slk:r * slk + QB, :] = _stream(halo, r, QB, d).astype(BF16)
                dst[r * slk + QB:(r + 1) * slk, :] = _stream(src, r, sl, d).astype(BF16)
        col = lax.broadcasted_iota(jnp.int32, (2 * QB, 2 * QB), 1)
        for r in range(d):
            for b in range(nb):
                rows = slice(r * sl + b * QB, r * sl + (b + 1) * QB)
                keys = slice(r * slk + b * QB, r * slk + (b + 2) * QB)
                s = _nt(_stack_heads(qs[rows, :]), ks[keys, :]) + bias_ref[...]
                if b == 0:
                    s = jnp.where((col < QB) & (n == 0), NEG, s)
                m = jnp.max(s, axis=-1, keepdims=True)
                p = jnp.exp(s - m)
                den = jnp.sum(p, axis=-1, keepdims=True)
                pv = jnp.dot(p.astype(BF16), vs[keys, :], preferred_element_type=F32)
                os_[rows, :] = _unstack_heads(pv / den)
                ls[rows, :] = _unstack_heads(jnp.broadcast_to(m + jnp.log(den), (2 * QB, 2 * HEAD)))
        for r in range(d):
            if d > 1:
                o_ref[pl.ds(r, sl, stride=d), :] = os_[r * sl:(r + 1) * sl, :]
                l_ref[pl.ds(r, sl, stride=d), :] = ls[r * sl:(r + 1) * sl, :]
            else:
                o_ref[...] = os_[...]
                l_ref[...] = ls[...]
        if merge is not None:
            for r0 in range(0, ATT_WIN, mrows):
                rows = slice(r0, r0 + mrows)
                a, b, c = la_ref[rows, :], lb_ref[rows, :], l_ref[rows, :]
                m = jnp.maximum(jnp.maximum(a, b), c)
                e0, e1, e2 = jnp.exp(a - m), jnp.exp(b - m), jnp.exp(c - m)
                den = e0 + e1 + e2
                o = (e0 * oa_ref[rows, :] + e1 * ob_ref[rows, :] + e2 * o_ref[rows, :]) / den
                of_ref[rows, :] = o
                cat_ref[rows, :] = o.astype(BF16)
                lg_ref[rows, :] = m + jnp.log(den)

    main, prev, _, bias_spec = _attn_specs(d)
    lanes = 2 * HEAD
    operands = [qn, kn, proj, kn, proj, bias]
    in_specs = [main(0), main(0), main(V_COL), prev(0), prev(V_COL), bias_spec]
    scratch = [pltpu.VMEM((ATT_WIN, lanes), BF16), pltpu.VMEM((ATT_WIN + hr, lanes), BF16),
               pltpu.VMEM((ATT_WIN + hr, lanes), BF16), pltpu.VMEM((ATT_WIN, lanes), F32),
               pltpu.VMEM((ATT_WIN, lanes), F32)]
    if merge is None:
        out_specs = [main(0), main(0)]
        out_shape = [jax.ShapeDtypeStruct((T, C), F32)] * 2
        aliases = None
    else:
        operands += list(merge)
        in_specs += [main(0)] * 4 + [ANY]
        out_specs = [main(C // lanes), main(0), main(0)]
        out_shape = [jax.ShapeDtypeStruct((T, D), BF16), jax.ShapeDtypeStruct((T, C), F32),
                     jax.ShapeDtypeStruct((T, C), F32)]
        scratch += [pltpu.VMEM((ATT_WIN, lanes), F32)] * 2
        aliases = {len(operands) - 1: 0}
    return _call(
        body, *operands, rider=rider, name=f"attn_fwd_d{d}", grid=(C // lanes, T // ATT_WIN),
        in_specs=in_specs, out_specs=out_specs, out_shape=out_shape, scratch_shapes=scratch,
        input_output_aliases=aliases, compiler_params=_params(("arbitrary", "arbitrary"), 48))


def _out_up(x, cat, w_out, g2, w_up, rider):
    tm, tn = 512, NUP // 4

    def body(x_ref, cat_ref, wo_ref, g_ref, wu_ref, x1_ref, h2_ref, up_ref):
        x1 = x_ref[...] + jnp.dot(cat_ref[...], wo_ref[...], preferred_element_type=F32)
        x1_ref[...] = x1
        r = lax.rsqrt(jnp.mean(x1 * x1, axis=-1, keepdims=True) + EPS)
        h2_ref[...] = (x1 * r * g_ref[...]).astype(BF16)
        for j in range(NUP // tn):
            cols = slice(j * tn, (j + 1) * tn)
            up_ref[:, cols] = jnp.dot(h2_ref[...], wu_ref[:, cols], preferred_element_type=F32)

    row = pl.BlockSpec((tm, D), lambda i: (i, 0))
    return _call(
        body, x, cat, w_out, g2, w_up, rider=rider, name="out_up", grid=(T // tm,),
        in_specs=[row, row, _resident((D, D)), pl.BlockSpec((1, D), lambda i: (0, 0)), _resident((D, NUP))],
        out_specs=[row, row, pl.BlockSpec((tm, NUP), lambda i: (i, 0))],
        out_shape=[jax.ShapeDtypeStruct((T, D), F32), jax.ShapeDtypeStruct((T, D), BF16),
                   jax.ShapeDtypeStruct((T, NUP), F32)],
        compiler_params=_params(("arbitrary",), 58))


FF_TM = 256
FF_HALO = 8
FF_RB = 64
FF_TILES = DFF // LANES


def _ff_conv(ext_ref, fw_ref, fb_ref, tile, r0):
    cols = _lanes(tile)
    base = FF_HALO + r0
    acc = fb_ref[:, cols] + fw_ref[0:1, cols] * ext_ref.at[tile][_sp(base - 2, FF_RB)]
    acc = acc + fw_ref[1:2, cols] * ext_ref.at[tile][_sp(base - 1, FF_RB)]
    return acc + fw_ref[2:3, cols] * ext_ref.at[tile][_sp(base, FF_RB)]


def _fill_ext(ext_ref, up_ref, uh_ref, has_prev, tm):
    for j in range(2 * FF_TILES):
        ext_ref.at[j][_sp(0, FF_HALO)] = jnp.where(has_prev, uh_ref[:, _lanes(j)], 0.0)
        for r0 in range(0, tm, FF_RB):
            ext_ref.at[j][_sp(FF_HALO + r0, FF_RB)] = up_ref[r0:r0 + FF_RB, _lanes(j)]


def _ffn_down(up, ffconv_w, ffconv_b, w_down, x1, target):
    tm, hl = FF_TM, FF_HALO
    per = tm // hl

    def body(up_ref, uh_ref, fw_ref, fb_ref, wd_ref, x1_ref, tg_ref, act_ref, dy_ref, loss_ref, ext_ref):
        i = pl.program_id(0)
        _fill_ext(ext_ref, up_ref, uh_ref, i > 0, tm)
        for c in range(FF_TILES):
            for r0 in range(0, tm, FF_RB):
                gate = _ff_conv(ext_ref, fw_ref, fb_ref, c, r0)
                val = _ff_conv(ext_ref, fw_ref, fb_ref, FF_TILES + c, r0)
                act_ref[r0:r0 + FF_RB, _lanes(c)] = (gate * _sigmoid(gate) * val).astype(BF16)
        y = x1_ref[...] + jnp.dot(act_ref[...], wd_ref[...], preferred_element_type=F32)
        err = y - tg_ref[...]
        dy_ref[...] = err * (1.0 / D)

        @pl.when(i == 0)
        def _():
            loss_ref[...] = jnp.zeros_like(loss_ref)
        loss_ref[...] += jnp.sum(err * err)

    row = pl.BlockSpec((tm, D), lambda i: (i, 0))
    wide = pl.BlockSpec((tm, NUP), lambda i: (i, 0))
    return _call(
        body, up, up, ffconv_w, ffconv_b, w_down, x1, target, name="ffn_down", grid=(T // tm,),
        in_specs=[wide, pl.BlockSpec((hl, NUP), lambda i: (jnp.maximum(i * per - 1, 0), 0)),
                  pl.BlockSpec((FF_K, NUP), lambda i: (0, 0)), pl.BlockSpec((1, NUP), lambda i: (0, 0)),
                  _resident((DFF, D)), row, row],
        out_specs=[pl.BlockSpec((tm, DFF), lambda i: (i, 0)), row, pl.BlockSpec((8, 128), lambda i: (0, 0))],
        out_shape=[jax.ShapeDtypeStruct((T, DFF), BF16), jax.ShapeDtypeStruct((T, D), F32),
                   jax.ShapeDtypeStruct((8, 128), F32)],
        scratch_shapes=[pltpu.VMEM((2 * FF_TILES, 2 * (tm + hl), LANES), F32)],
        compiler_params=_params(("arbitrary",), 58))


def _down_bwd(dy, w_down, up, ffconv_w, ffconv_b):
    tm, hl = FF_TM, FF_HALO
    per = tm // hl
    nt = T // tm

    def body(dy_ref, wd_ref, up_ref, uh_ref, fw_ref, fb_ref, dup_ref, gff_ref, ext_ref, dext_ref, dact_ref):
        i = pl.program_id(0)
        _fill_ext(ext_ref, up_ref, uh_ref, i < nt - 1, tm)

        @pl.when(i == 0)
        def _():
            gff_ref[...] = jnp.zeros_like(gff_ref)
            for j in range(2 * FF_TILES):
                dext_ref.at[j][_sp(tm, hl)] = jnp.zeros((hl, LANES), F32)

        dact_ref[...] = _nt(dy_ref[...].astype(BF16), wd_ref[...])
        for c in range(FF_TILES):
            for r0 in range(0, tm, FF_RB):
                gate = _ff_conv(ext_ref, fw_ref, fb_ref, c, r0)
                val = _ff_conv(ext_ref, fw_ref, fb_ref, FF_TILES + c, r0)
                sg = _sigmoid(gate)
                da = dact_ref[r0:r0 + FF_RB, _lanes(c)]
                dext_ref.at[c][_sp(r0, FF_RB)] = da * val * (sg + gate * sg * (1.0 - sg))
                dext_ref.at[FF_TILES + c][_sp(r0, FF_RB)] = da * gate * sg
        fold = lambda a: jnp.sum(a.reshape(FF_RB // 8, 8, LANES), axis=0)
        for c in range(2 * FF_TILES):
            cols = _lanes(c)
            taps = [fw_ref[k:k + 1, cols] for k in range(FF_K)]
            acc = [jnp.zeros((8, LANES), F32) for _ in range(FF_K + 1)]
            for r0 in range(0, tm, FF_RB):
                shifted = [dext_ref.at[c][_sp(r0 + k, FF_RB)] for k in range(FF_K)]
                u = up_ref[r0:r0 + FF_RB, cols]
                dup = taps[2] * shifted[0] + taps[1] * shifted[1] + taps[0] * shifted[2]
                dup_ref[r0:r0 + FF_RB, cols] = dup.astype(BF16)
                for k in range(FF_K):
                    acc[2 - k] = acc[2 - k] + fold(shifted[k] * u)
                acc[FF_K] = acc[FF_K] + fold(shifted[0])
            for k in range(FF_K + 1):
                gff_ref[k:k + 1, cols] += jnp.sum(acc[k], axis=0, keepdims=True)
        for j in range(2 * FF_TILES):
            dext_ref.at[j][_sp(tm, hl)] = dext_ref.at[j][_sp(0, hl)]

    rev = lambda i: (nt - 1 - i, 0)
    wide = pl.BlockSpec((tm, NUP), rev)
    return _call(
        body, dy, w_down, up, up, ffconv_w, ffconv_b, name="down_bwd", grid=(nt,),
        in_specs=[pl.BlockSpec((tm, D), rev), _resident((DFF, D)), wide,
                  pl.BlockSpec((hl, NUP), lambda i: (jnp.maximum((nt - 1 - i) * per - 1, 0), 0)),
                  pl.BlockSpec((FF_K, NUP), lambda i: (0, 0)), pl.BlockSpec((1, NUP), lambda i: (0, 0))],
        out_specs=[wide, pl.BlockSpec((8, NUP), lambda i: (0, 0))],
        out_shape=[jax.ShapeDtypeStruct((T, NUP), BF16), jax.ShapeDtypeStruct((8, NUP), F32)],
        scratch_shapes=[pltpu.VMEM((2 * FF_TILES, 2 * (tm + hl), LANES), F32)] * 2 + [pltpu.VMEM((tm, DFF), F32)],
        compiler_params=_params(("arbitrary",), 58))


def _weight_grad(a, g, bm, bn, tk, name):
    m, n = a.shape[1], g.shape[1]
    nk = T // tk

    def body(a_ref, g_ref, of_ref, ob_ref):
        k = pl.program_id(2)

        @pl.when(k == 0)
        def _():
            of_ref[...] = jnp.zeros_like(of_ref)
        of_ref[...] += _tn_dot(a_ref[...].astype(BF16), g_ref[...].astype(BF16))

        @pl.when(k == nk - 1)
        def _():
            ob_ref[...] = of_ref[...].astype(BF16)

    out = pl.BlockSpec((bm, bn), lambda i, j, k: (i, j))
    return _call(
        body, a, g, name=name, grid=(m // bm, n // bn, nk),
        in_specs=[pl.BlockSpec((tk, bm), lambda i, j, k: (k, i)), pl.BlockSpec((tk, bn), lambda i, j, k: (k, j))],
        out_specs=[out, out],
        out_shape=[jax.ShapeDtypeStruct((m, n), F32), jax.ShapeDtypeStruct((m, n), BF16)],
        compiler_params=_params(("parallel", "parallel", "arbitrary"), 56))


def _norm_bwd_mm(dz, w, xin, base, gain, name, rider):
    kdim = dz.shape[1]
    tm = 512

    def body(dz_ref, w_ref, x_ref, b_ref, g_ref, dx_ref, gg_ref):
        @pl.when(pl.program_id(0) == 0)
        def _():
            gg_ref[...] = jnp.zeros_like(gg_ref)

        xv = x_ref[...]
        dh = _nt(dz_ref[...], w_ref[...])
        r = lax.rsqrt(jnp.mean(xv * xv, axis=-1, keepdims=True) + EPS)
        t = dh * g_ref[...]
        dx_ref[...] = b_ref[...] + r * t - xv * (r * r * r) * jnp.mean(t * xv, axis=-1, keepdims=True)
        gg_ref[...] += jnp.sum(dh * xv * r, axis=0, keepdims=True)

    row = pl.BlockSpec((tm, D), lambda i: (i, 0))
    vec = pl.BlockSpec((1, D), lambda i: (0, 0))
    return _call(
        body, dz, w, xin, base, gain, rider=rider, name=name, grid=(T // tm,),
        in_specs=[pl.BlockSpec((tm, kdim), lambda i: (i, 0)), _resident((D, kdim)), row, row, vec],
        out_specs=[row, vec],
        out_shape=[jax.ShapeDtypeStruct((T, D), F32), jax.ShapeDtypeStruct((1, D), F32)],
        compiler_params=_params(("arbitrary",), 48))


def _outproj_bwd(dx1, w_out):
    tm = 512

    def body(d_ref, w_ref, o_ref):
        o_ref[...] = _nt(d_ref[...].astype(BF16), w_ref[...])

    row = pl.BlockSpec((tm, D), lambda i: (i, 0))
    return _call(
        body, dx1, w_out, name="outproj_bwd", grid=(T // tm,),
        in_specs=[row, pl.BlockSpec((D, D), lambda i: (0, 0))], out_specs=[row],
        out_shape=[jax.ShapeDtypeStruct((T, D), F32)],
        compiler_params=_params(("parallel",), 32))[0]


def _conv_bwd(dcat, cv, proj, conv_w, cn_g, cn_b, rider):
    tm, hl, rb, cb = CONV_TM, CONV_HALO, CONV_RB, CONV_CB
    per = tm // hl
    nt = T // tm
    tiles = C // LANES

    def body(du_ref, dun_ref, cv_ref, cvn_ref, av_ref, ag_ref, hv_ref, hg_ref, w_ref, g_ref, bb_ref,
             dp_ref, gv_ref, gw_ref, dsh_ref, gsh_ref):
        i = pl.program_id(0)

        @pl.when(i == 0)
        def _():
            gv_ref[...] = jnp.zeros_like(gv_ref)
            gw_ref[...] = jnp.zeros_like(gw_ref)

        def ln_bwd(du, cvv):
            mu = jnp.mean(cvv, axis=-1, keepdims=True)
            xc = cvv - mu
            rs = lax.rsqrt(jnp.mean(xc * xc, axis=-1, keepdims=True) + EPS)
            xh = xc * rs
            ln = xh * g_ref[...] + bb_ref[...]
            sg = _sigmoid(ln)
            dln = du * (sg + ln * sg * (1.0 - sg))
            dxh = dln * g_ref[...]
            dcv = rs * (dxh - jnp.mean(dxh, axis=-1, keepdims=True)
                        - xh * jnp.mean(dxh * xh, axis=-1, keepdims=True))
            return dcv, dln, xh

        for r0 in range(0, tm, rb):
            dcv, dln, xh = ln_bwd(du_ref[r0:r0 + rb, :], cv_ref[r0:r0 + rb, :])
            for j in range(tiles):
                dsh_ref.at[j][_sp(r0, rb)] = dcv[:, _lanes(j)]
            gv_ref[0:1, :] += jnp.sum(dln * xh, axis=0, keepdims=True)
            gv_ref[1:2, :] += jnp.sum(dln, axis=0, keepdims=True)
            gv_ref[2:3, :] += jnp.sum(dcv, axis=0, keepdims=True)
        dcv_n, _, _ = ln_bwd(dun_ref[...], cvn_ref[...])
        dcv_n = jnp.where(i < nt - 1, dcv_n, 0.0)
        for j in range(tiles):
            ln_ = _lanes(j)
            dsh_ref.at[j][_sp(tm, hl)] = dcv_n[:, ln_]
            glu_h = hv_ref[:, ln_] * _sigmoid(hg_ref[:, ln_])
            gsh_ref.at[j][_sp(0, hl)] = jnp.where(i > 0, glu_h, 0.0)
            for r0 in range(0, tm, cb):
                gsh_ref.at[j][_sp(hl + r0, cb)] = av_ref[r0:r0 + cb, ln_] * _sigmoid(ag_ref[r0:r0 + cb, ln_])

        for j in range(tiles):
            ln_ = _lanes(j)
            for r0 in range(0, tm, cb):
                dglu = jnp.zeros((cb, LANES), F32)
                for k in range(CONV_K):
                    dglu = dglu + w_ref[k:k + 1, ln_] * dsh_ref.at[j][_sp(r0 + (CONV_K - 1) - k, cb)]
                av = av_ref[r0:r0 + cb, ln_]
                sg = _sigmoid(ag_ref[r0:r0 + cb, ln_])
                dp_ref[r0:r0 + cb, ln_] = (dglu * sg).astype(BF16)
                dp_ref[r0:r0 + cb, _lanes(tiles + j)] = (dglu * av * sg * (1.0 - sg)).astype(BF16)
            for k in range(CONV_K):
                part = jnp.zeros((8, LANES), F32)
                for r0 in range(0, tm, cb):
                    prod = dsh_ref.at[j][_sp(r0, cb)] * gsh_ref.at[j][_sp(r0 + hl - (CONV_K - 1) + k, cb)]
                    part = part + jnp.sum(prod.reshape(cb // 8, 8, LANES), axis=0)
                gw_ref[k:k + 1, ln_] += jnp.sum(part, axis=0, keepdims=True)

    main = lambda col: pl.BlockSpec((tm, C), lambda i: (i, col))
    prev = lambda col: pl.BlockSpec((hl, C), lambda i: (jnp.maximum(i * per - 1, 0), col))
    nxt = pl.BlockSpec((hl, C), lambda i: (jnp.minimum((i + 1) * per, T // hl - 1), 0))
    vec = pl.BlockSpec((1, C), lambda i: (0, 0))
    return _call(
        body, dcat, dcat, cv, cv, proj, proj, proj, proj, conv_w, cn_g, cn_b, rider=rider, name="conv_bwd",
        grid=(nt,),
        in_specs=[main(0), nxt, main(0), nxt, main(0), main(1), prev(0), prev(1),
                  pl.BlockSpec((CONV_K, C), lambda i: (0, 0)), vec, vec],
        out_specs=[pl.BlockSpec((tm, 2 * C), lambda i: (i, 0)), pl.BlockSpec((8, C), lambda i: (0, 0)),
                   pl.BlockSpec((32, C), lambda i: (0, 0))],
        out_shape=[jax.ShapeDtypeStruct((T, NPROJ), BF16), jax.ShapeDtypeStruct((8, C), F32),
                   jax.ShapeDtypeStruct((32, C), F32)],
        scratch_shapes=[pltpu.VMEM((C // LANES, 2 * (tm + hl), LANES), F32)] * 2,
        compiler_params=_params(("arbitrary",), 48))


def _attn_bwd_prep(dcat, o_f32, bd):
    tm = 512

    def body(do_ref, o_ref, bd_ref, dl_ref):
        dl_ref[...] = _segsum(do_ref[...] * o_ref[...], bd_ref[...])

    blk = pl.BlockSpec((tm, C), lambda i: (i, 0))
    return _call(
        body, dcat, o_f32, bd, name="attn_bwd_prep", grid=(T // tm,),
        in_specs=[pl.BlockSpec((tm, C), lambda i: (i, 1)), blk, pl.BlockSpec((C, C), lambda i: (0, 0))],
        out_specs=[blk],
        out_shape=[jax.ShapeDtypeStruct((T, C), F32)],
        compiler_params=_params(("parallel",), 32))[0]


def _attn_bwd_unit(qs, dos, lgs, dls, rows, kc, vc, biasv, invalid_prev):
    qst, dost = _stack_heads(qs[rows, :]), _stack_heads(dos[rows, :])
    s = _nt(qst, kc) + biasv
    if invalid_prev is not None:
        col = lax.broadcasted_iota(jnp.int32, s.shape, 1)
        s = jnp.where((col < QB) & invalid_prev, NEG, s)
    p = jnp.exp(s - _stack_cols(lgs[rows, :]))
    ds = p * (_nt(dost, vc) - _stack_cols(dls[rows, :]))
    dsb = ds.astype(BF16)
    dq = _unstack_heads(jnp.dot(dsb, kc, preferred_element_type=F32))
    return dq, _tn_dot(dsb, qst), _tn_dot(p.astype(BF16), dost)


def _attn_bwd_lagged(qn, kn, proj, dcat, lg, dl, bias, d):
    assert QB * d == ATT_WIN
    n_win = T // ATT_WIN
    lanes = 2 * HEAD

    def body(q_ref, k_ref, v_ref, do_ref, lg_ref, dl_ref, kh_ref, vh_ref, bias_ref, dq_ref, dk_ref, dv_ref,
             qs, dos, lgs, dls, ks, vs, dqs, ck, cv, ok, ov):
        n = pl.program_id(1)

        @pl.when(n == 0)
        def _():
            ck[...] = jnp.zeros_like(ck)
            cv[...] = jnp.zeros_like(cv)

        @pl.when(n < n_win)
        def _():
            for r in range(d):
                rows = slice(r * QB, (r + 1) * QB)
                for dst, src, dt in ((qs, q_ref, BF16), (dos, do_ref, BF16), (lgs, lg_ref, F32), (dls, dl_ref, F32)):
                    dst[rows, :] = _stream(src, r, QB, d).astype(dt)
                for dst, halo, src in ((ks, kh_ref, k_ref), (vs, vh_ref, v_ref)):
                    dst[2 * r * QB:(2 * r + 1) * QB, :] = _stream(halo, r, QB, d).astype(BF16)
                    dst[(2 * r + 1) * QB:(2 * r + 2) * QB, :] = _stream(src, r, QB, d).astype(BF16)
            for r in range(d):
                rows = slice(r * QB, (r + 1) * QB)
                keys = slice(2 * r * QB, (2 * r + 2) * QB)
                dq, dkc, dvc = _attn_bwd_unit(qs, dos, lgs, dls, rows, ks[keys, :], vs[keys, :], bias_ref[...], n == 0)
                dqs[rows, :] = dq
                ok[rows, :] = ck[rows, :] + dkc[:QB]
                ov[rows, :] = cv[rows, :] + dvc[:QB]
                ck[rows, :] = dkc[QB:]
                cv[rows, :] = dvc[QB:]
            for r in range(d):
                rows = slice(r * QB, (r + 1) * QB)
                dq_ref[pl.ds(r, QB, stride=d), :] = dqs[rows, :]
                dk_ref[pl.ds(r, QB, stride=d), :] = ok[rows, :]
                dv_ref[pl.ds(r, QB, stride=d), :] = ov[rows, :]

        @pl.when(n == n_win)
        def _():
            for r in range(d):
                rows = slice(r * QB, (r + 1) * QB)
                dk_ref[pl.ds(r, QB, stride=d), :] = ck[rows, :]
                dv_ref[pl.ds(r, QB, stride=d), :] = cv[rows, :]

    cur = lambda off: pl.BlockSpec((ATT_WIN, lanes), lambda cb, n: (jnp.minimum(n, n_win - 1), off + cb))
    prev = lambda off: pl.BlockSpec(
        (ATT_WIN, lanes), lambda cb, n: (jnp.maximum(jnp.minimum(n, n_win - 1) - 1, 0), off + cb))
    late = pl.BlockSpec((ATT_WIN, lanes), lambda cb, n: (jnp.maximum(n - 1, 0), cb))
    buf = lambda rows, dt: pltpu.VMEM((rows, lanes), dt)
    return _call(
        body, qn, kn, proj, dcat, lg, dl, kn, proj, bias, name=f"attn_bwd_d{d}", grid=(C // lanes, n_win + 1),
        in_specs=[cur(0), cur(0), cur(V_COL), cur(DO_COL), cur(0), cur(0), prev(0), prev(V_COL),
                  pl.BlockSpec((None, 2 * QB, 2 * QB), lambda cb, n: (cb, 0, 0))],
        out_specs=[cur(0), late, late],
        out_shape=[jax.ShapeDtypeStruct((T, C), F32)] * 3,
        scratch_shapes=[buf(ATT_WIN, BF16), buf(ATT_WIN, BF16), buf(ATT_WIN, F32), buf(ATT_WIN, F32),
                        buf(2 * ATT_WIN, BF16), buf(2 * ATT_WIN, BF16)] + [buf(ATT_WIN, F32)] * 5,
        compiler_params=_params(("arbitrary", "arbitrary"), 48))


def _attn_bwd(qn, kn, proj, dcat, lg, dl, bias, d, rider=None):
    sl, nb, hr = _attn_geometry(d)
    slk = QB + sl
    slq = sl + QB
    n_win = T // ATT_WIN

    def body(q_ref, k_ref, v_ref, do_ref, lg_ref, dl_ref, kh_ref, vh_ref, qx_ref, dox_ref, lgx_ref, dlx_ref,
             bias_ref, dq_ref, dk_ref, dv_ref, qs, dos, lgs, dls, ks, vs, dqs, dks, dvs):
        n = pl.program_id(1)
        for r in range(d):
            for dst, src, nx, dt in ((qs, q_ref, qx_ref, BF16), (dos, do_ref, dox_ref, BF16),
                                     (lgs, lg_ref, lgx_ref, F32), (dls, dl_ref, dlx_ref, F32)):
                dst[r * slq:r * slq + sl, :] = _stream(src, r, sl, d).astype(dt)
                dst[r * slq + sl:(r + 1) * slq, :] = _stream(nx, r, QB, d).astype(dt)
            for dst, halo, src in ((ks, kh_ref, k_ref), (vs, vh_ref, v_ref)):
                dst[r * slk:r * slk + QB, :] = _stream(halo, r, QB, d).astype(BF16)
                dst[r * slk + QB:(r + 1) * slk, :] = _stream(src, r, sl, d).astype(BF16)
        dks[...] = jnp.zeros_like(dks)
        dvs[...] = jnp.zeros_like(dvs)

        def unit(rows, kc, vc, biasv, invalid_prev):
            return _attn_bwd_unit(qs, dos, lgs, dls, rows, kc, vc, biasv, invalid_prev)

        for r in range(d):
            for b in range(nb):
                rows = slice(r * slq + b * QB, r * slq + (b + 1) * QB)
                keys = slice(r * slk + b * QB, r * slk + (b + 2) * QB)
                dq, dkc, dvc = unit(rows, ks[keys, :], vs[keys, :], bias_ref[...], (n == 0) if b == 0 else None)
                dqs[r * sl + b * QB:r * sl + (b + 1) * QB, :] = dq
                if b == 0:
                    dks[r * sl:r * sl + QB, :] += dkc[QB:]
                    dvs[r * sl:r * sl + QB, :] += dvc[QB:]
                else:
                    dks[r * sl + (b - 1) * QB:r * sl + (b + 1) * QB, :] += dkc
                    dvs[r * sl + (b - 1) * QB:r * sl + (b + 1) * QB, :] += dvc

        @pl.when(n < n_win - 1)
        def _():
            for r in range(d):
                rows = slice(r * slq + sl, (r + 1) * slq)
                keys = slice(r * slk + sl, (r + 1) * slk)
                _, dkc, dvc = unit(rows, ks[keys, :], vs[keys, :], bias_ref[:, 0:QB], None)
                dks[(r + 1) * sl - QB:(r + 1) * sl, :] += dkc
                dvs[(r + 1) * sl - QB:(r + 1) * sl, :] += dvc

        for dst, src in ((dq_ref, dqs), (dk_ref, dks), (dv_ref, dvs)):
            for r in range(d):
                if d > 1:
                    dst[pl.ds(r, sl, stride=d), :] = src[r * sl:(r + 1) * sl, :]
                else:
                    dst[...] = src[...]

    main, prev, nxt, bias_spec = _attn_specs(d)
    lanes = 2 * HEAD
    return _call(
        body, qn, kn, proj, dcat, lg, dl, kn, proj, qn, dcat, lg, dl, bias, rider=rider, name=f"attn_bwd_d{d}",
        grid=(C // lanes, n_win),
        in_specs=[main(0), main(0), main(V_COL), main(DO_COL), main(0), main(0), prev(0), prev(V_COL),
                  nxt(0), nxt(DO_COL), nxt(0), nxt(0), bias_spec],
        out_specs=[main(0)] * 3,
        out_shape=[jax.ShapeDtypeStruct((T, C), F32)] * 3,
        scratch_shapes=[pltpu.VMEM((ATT_WIN + hr, lanes), BF16), pltpu.VMEM((ATT_WIN + hr, lanes), BF16),
                        pltpu.VMEM((ATT_WIN + hr, lanes), F32), pltpu.VMEM((ATT_WIN + hr, lanes), F32),
                        pltpu.VMEM((ATT_WIN + hr, lanes), BF16), pltpu.VMEM((ATT_WIN + hr, lanes), BF16),
                        pltpu.VMEM((ATT_WIN, lanes), F32), pltpu.VMEM((ATT_WIN, lanes), F32),
                        pltpu.VMEM((ATT_WIN, lanes), F32)],
        compiler_params=_params(("arbitrary", "arbitrary"), 48))


def _qk_norm_bwd(d3, proj, col, gain, bd, dproj, name):
    tm = 512

    def body(d0, d1, d2, x_ref, g_ref, bd_ref, dp_in, dp_ref, gg_ref):
        del dp_in

        @pl.when(pl.program_id(0) == 0)
        def _():
            gg_ref[...] = jnp.zeros_like(gg_ref)
        dn = d0[...] + d1[...] + d2[...]
        xv = x_ref[...]
        r = lax.rsqrt(_segsum(xv * xv, bd_ref[...]) * (1.0 / HEAD) + EPS)
        t = dn * g_ref[...]
        mean_tx = _segsum(t * xv, bd_ref[...]) * (1.0 / HEAD)
        dp_ref[...] = (r * t - xv * (r * r * r) * mean_tx).astype(BF16)
        gg_ref[...] += jnp.sum(dn * xv * r, axis=0, keepdims=True)

    blk = pl.BlockSpec((tm, C), lambda i: (i, 0))
    vec = pl.BlockSpec((1, C), lambda i: (0, 0))
    return _call(
        body, *d3, proj, gain, bd, dproj, name=name, grid=(T // tm,),
        in_specs=[blk, blk, blk, pl.BlockSpec((tm, C), lambda i: (i, col)), vec,
                  pl.BlockSpec((C, C), lambda i: (0, 0)), ANY],
        out_specs=[pl.BlockSpec((tm, C), lambda i: (i, col)), vec],
        out_shape=[jax.ShapeDtypeStruct((T, NPROJ), BF16), jax.ShapeDtypeStruct((1, C), F32)],
        input_output_aliases={6: 0},
        compiler_params=_params(("arbitrary",), 32))


def _v_bwd(d3, dproj):
    tm = 512

    def body(d0, d1, d2, dp_in, dp_ref):
        del dp_in
        dp_ref[...] = (d0[...] + d1[...] + d2[...]).astype(BF16)

    blk = pl.BlockSpec((tm, C), lambda i: (i, 0))
    return _call(
        body, *d3, dproj, name="v_bwd", grid=(T // tm,),
        in_specs=[blk, blk, blk, ANY],
        out_specs=[pl.BlockSpec((tm, C), lambda i: (i, 4))],
        out_shape=[jax.ShapeDtypeStruct((T, NPROJ), BF16)],
        input_output_aliases={3: 0},
        compiler_params=_params(("parallel",), 32))[0]


def _adamw(w, g, m, v):
    m = ADAM_B1 * m + (1.0 - ADAM_B1) * g
    v = ADAM_B2 * v + (1.0 - ADAM_B2) * (g * g)
    m_hat = m / (1.0 - ADAM_B1 ** ADAM_STEP)
    v_hat = v / (1.0 - ADAM_B2 ** ADAM_STEP)
    delta = -ADAM_LR * (m_hat / (jnp.sqrt(v_hat) + ADAM_EPS) + ADAM_WD * w)
    return delta, m, v


def _row_block(shape):
    rows = shape[0]
    for cand in (256, 128, 64, 88, 32, 8):
        if rows % cand == 0 and cand * shape[1] * 4 <= (2 << 20):
            return cand
    return 8


def _partial_sum(own, recv, name):
    br = _row_block(own.shape)
    cols = own.shape[1]

    def body(o_ref, r_ref, p_ref):
        p_ref[...] = ((o_ref[...] + r_ref[0].astype(F32)) + r_ref[1].astype(F32)) + r_ref[2].astype(F32)

    blk = pl.BlockSpec((br, cols), lambda i: (i, 0))
    return _call(
        body, own, recv, name=name, grid=(own.shape[0] // br,),
        in_specs=[blk, pl.BlockSpec((3, br, cols), lambda i: (0, i, 0))], out_specs=[blk],
        out_shape=[jax.ShapeDtypeStruct(own.shape, F32)],
        compiler_params=_params(("parallel",), 32))[0]


def _adamw_mat(p_own, p_sib, w, m, v, name):
    br = _row_block(w.shape)
    cols = w.shape[1]

    def body(a_ref, b_ref, w_ref, m_ref, v_ref, g_ref, d_ref, nm_ref, nv_ref):
        g = a_ref[...] + b_ref[...]
        delta, nm, nv = _adamw(w_ref[...], g, m_ref[...], v_ref[...])
        g_ref[...] = g
        d_ref[...] = delta
        nm_ref[...] = nm
        nv_ref[...] = nv

    blk = pl.BlockSpec((br, cols), lambda i: (i, 0))
    return _call(
        body, p_own, p_sib, w, m, v, name=name, grid=(w.shape[0] // br,),
        in_specs=[blk] * 5, out_specs=[blk] * 4,
        out_shape=[jax.ShapeDtypeStruct(w.shape, F32)] * 4,
        compiler_params=_params(("parallel",), 40))


def _vec_reduce(vrecv):
    def body(v_ref, o_ref):
        acc = v_ref[0]
        for r in range(1, N_DEV):
            acc = acc + v_ref[r]
        o_ref[...] = acc

    return pl.pallas_call(
        body, name="vec_reduce",
        out_shape=jax.ShapeDtypeStruct((VPACK_ROWS, D), F32),
        compiler_params=_params((), 32),
    )(vrecv)


def _adamw_small(w, g, m, v):
    def body(w_ref, g_ref, m_ref, v_ref, d_ref, nm_ref, nv_ref):
        delta, nm, nv = _adamw(w_ref[...], g_ref[...], m_ref[...], v_ref[...])
        d_ref[...] = delta
        nm_ref[...] = nm
        nv_ref[...] = nv

    return pl.pallas_call(
        body, name="adamw_small",
        out_shape=[jax.ShapeDtypeStruct(w.shape, F32)] * 3,
        compiler_params=_params((), 32),
    )(w, g, m, v)


def _pack(parts, rows):
    flat = jnp.concatenate([p.reshape(-1) for p in parts])
    return jnp.pad(flat, (0, rows * D - flat.shape[0])).reshape(rows, D)


def _unpack(packed, shapes):
    flat = packed.reshape(-1)
    out, off = [], 0
    for shp in shapes:
        size = 1
        for s in shp:
            size *= s
        out.append(flat[off:off + size].reshape(shp))
        off += size
    return out


def _no_comm(shards, row_sharded, peers=(0, 1, 2), into=None):
    del row_sharded, peers, into
    return None, lambda res, n: (res, shards)


def _with_comm(shards, row_sharded, peers=(0, 1, 2), into=None):
    rider = _gather_rider(shards, row_sharded, peers, into)
    return rider, lambda res, n: (res[:n], res[n:])


def _local_step(x, target, norm1_g, conv_w, conv_b, cn_g, cn_b, q_norm_g, k_norm_g, norm2_g, ffconv_w, ffconv_b,
                w_in, late_weights, comm=True):
    row = lambda a: a.reshape(1, -1)
    head_of = jnp.arange(C) // HEAD
    bd = (head_of[:, None] == head_of[None, :]).astype(BF16)
    qg = row(jnp.tile(q_norm_g, C // HEAD) * (HEAD ** -0.5))
    kg = row(jnp.tile(k_norm_g, C // HEAD))
    biases = [_alibi_tables(d) for d in PATTERN_DILATIONS]
    gather = _with_comm if comm else _no_comm
    grad_rider = (lambda g, rs: _grad_rider(g[1], g[0], rs)) if comm else (lambda g, rs: None)

    rider, split = gather(late_weights[0:1], (True,))
    (h, proj), (w_out,) = split(_proj_fwd(x, row(norm1_g), w_in, rider), 2)
    rider, split = gather(late_weights[1:2], (False,), (0, 1))
    (cat, cv), w_up_part = split(_conv_fwd(proj, conv_w, row(conv_b), row(cn_g), row(cn_b), rider), 2)
    qn, kn = _qkv_prep(proj, qg, kg, bd)
    fwd = [_attn_fwd(qn, kn, proj, biases[i], d) for i, d in enumerate(PATTERN_DILATIONS[:-1])]
    rider, split = gather(late_weights[1:2], (False,), (2,), w_up_part)
    merge = (fwd[0][0], fwd[0][1], fwd[1][0], fwd[1][1], cat)
    (cat, o_f32, lg), (w_up,) = split(
        _attn_fwd(qn, kn, proj, biases[-1], PATTERN_DILATIONS[-1], rider, merge), 3)
    rider, split = gather(late_weights[2:3], (True,))
    (x1, h2, up), (w_down,) = split(_out_up(x, cat, w_out, row(norm2_g), w_up, rider), 3)
    act, dy, loss_acc = _ffn_down(up, ffconv_w, row(ffconv_b), w_down, x1, target)

    dup, gff = _down_bwd(dy, w_down, up, ffconv_w, row(ffconv_b))
    gw_down = _weight_grad(act, dy, DFF // 2, D, 1024, "grad_w_down")
    res = _norm_bwd_mm(dup, w_up, x1, dy, row(norm2_g), "up_bwd", grad_rider(gw_down, True))
    (dx1, g_norm2), ex_down = res[:2], res[2:]
    gw_up = _weight_grad(h2, dup, D, NUP // 4, 2048, "grad_w_up")
    dcat = _outproj_bwd(dx1, w_out)
    gw_out = _weight_grad(cat, dx1, D, D, 2048, "grad_w_out")
    res = _conv_bwd(dcat, cv, proj, conv_w, row(cn_g), row(cn_b), grad_rider(gw_up, False))
    (dproj, gconv_vec, gconv_w), ex_up = res[:3], res[3:]
    dl = _attn_bwd_prep(dcat, o_f32, bd)
    bwd, ex_out = [], []
    for i, d in enumerate(PATTERN_DILATIONS):
        if QB * d == ATT_WIN:
            res = _attn_bwd_lagged(qn, kn, proj, dcat, lg, dl, biases[i], d)
        else:
            res = _attn_bwd(qn, kn, proj, dcat, lg, dl, biases[i], d, grad_rider(gw_out, True) if i == 0 else None)
        bwd.append(res[:3])
        ex_out = res[3:] if i == 0 else ex_out
    dproj, gq_lane = _qk_norm_bwd([b[0] for b in bwd], proj, 2, qg, bd, dproj, "q_norm_bwd")
    dproj, gk_lane = _qk_norm_bwd([b[1] for b in bwd], proj, 3, kg, bd, dproj, "k_norm_bwd")
    dproj = _v_bwd([b[2] for b in bwd], dproj)
    gw_in = _weight_grad(h, dproj, D, NPROJ // 4, 2048, "grad_w_in")
    res = _norm_bwd_mm(dproj, w_in, x, dx1, row(norm1_g), "in_bwd", grad_rider(gw_in, False))
    (dx, g_norm1), ex_in = res[:2], res[2:]

    loss = loss_acc[0, 0] * (0.5 / D)
    g_qg = jnp.sum(gq_lane.reshape(C // HEAD, HEAD), axis=0) * (HEAD ** -0.5)
    g_kg = jnp.sum(gk_lane.reshape(C // HEAD, HEAD), axis=0)
    small = [g_norm1[0], gconv_vec[2], gconv_vec[0], gconv_vec[1], g_qg, g_kg, g_norm2[0], gff[3],
             gconv_w[:CONV_K], gff[:FF_K]]
    mats = [ex_in, ex_out, ex_up, ex_down] if comm else [gw_in, gw_out, gw_up, gw_down]
    return loss, dx, mats, small


def kernel(x, norm1_g, w_in, conv_w, conv_b, cn_g, cn_b, q_norm_g, k_norm_g, w_out, norm2_g, w_up, ffconv_w, ffconv_b, w_down, loss_target, m_norm1_g, m_w_in, m_conv_w, m_conv_b, m_cn_g, m_cn_b, m_q_norm_g, m_k_norm_g, m_w_out, m_norm2_g, m_w_up, m_ffconv_w, m_ffconv_b, m_w_down, v_norm1_g, v_w_in, v_conv_w, v_conv_b, v_cn_g, v_cn_b, v_q_norm_g, v_k_norm_g, v_w_out, v_norm2_g, v_w_up, v_ffconv_w, v_ffconv_b, v_w_down):
    chip = 2 * lax.axis_index("x") + lax.axis_index("y")

    w_in_full, conv_w_full, ffconv_w_full = _gather_now([w_in.astype(BF16), conv_w, ffconv_w], (False, False, False))
    loss, dx, mats, small = _local_step(
        x[0], loss_target[0], norm1_g, conv_w_full, conv_b, cn_g, cn_b, q_norm_g, k_norm_g, norm2_g,
        ffconv_w_full, ffconv_b, w_in_full, [w.astype(BF16) for w in (w_out, w_up, w_down)])

    names = ("w_in", "w_out", "w_up", "w_down")
    parts = [_partial_sum(own, recv, "partial_" + names[k]) for k, (recv, own) in enumerate(mats)]
    sib, vrecv = _final_exchange(parts, _pack(small + [loss.reshape(1)], VPACK_ROWS))
    ws = (w_in, w_out, w_up, w_down)
    ms = (m_w_in, m_w_out, m_w_up, m_w_down)
    vs = (v_w_in, v_w_out, v_w_up, v_w_down)
    mat = [_adamw_mat(parts[k], sib[k], ws[k], ms[k], vs[k], "adamw_" + names[k]) for k in range(4)]

    vsum = _vec_reduce(vrecv)
    vec_shapes = [(D,), (C,), (C,), (C,), (HEAD,), (HEAD,), (D,), (NUP,), (CONV_K, C), (FF_K, NUP), (1,)]
    gsmall = _unpack(vsum, vec_shapes)
    g_conv_w = lax.dynamic_slice_in_dim(gsmall[8], chip * (C // N_CHIPS), C // N_CHIPS, axis=1)
    g_ffconv_w = lax.dynamic_slice_in_dim(gsmall[9], chip * (NUP // N_CHIPS), NUP // N_CHIPS, axis=1)
    gs = gsmall[:8] + [g_conv_w, g_ffconv_w]
    w_s = [norm1_g, conv_b, cn_g, cn_b, q_norm_g, k_norm_g, norm2_g, ffconv_b, conv_w, ffconv_w]
    m_s = [m_norm1_g, m_conv_b, m_cn_g, m_cn_b, m_q_norm_g, m_k_norm_g, m_norm2_g, m_ffconv_b, m_conv_w, m_ffconv_w]
    v_s = [v_norm1_g, v_conv_b, v_cn_g, v_cn_b, v_q_norm_g, v_k_norm_g, v_norm2_g, v_ffconv_b, v_conv_w, v_ffconv_w]
    shapes_s = [a.shape for a in w_s]
    d_p, m_p, v_p = _adamw_small(_pack(w_s, SPACK_ROWS), _pack(gs, SPACK_ROWS), _pack(m_s, SPACK_ROWS),
                                 _pack(v_s, SPACK_ROWS))
    d_s, nm_s, nv_s = _unpack(d_p, shapes_s), _unpack(m_p, shapes_s), _unpack(v_p, shapes_s)

    def ordered(sm, mt):
        return [sm[0], mt[0], sm[8], sm[1], sm[2], sm[3], sm[4], sm[5], mt[1], sm[6], mt[2], sm[9], sm[7], mt[3]]

    loss_all = gsmall[10][0]
    grads = ordered(gs, [r[0] for r in mat])
    deltas = ordered(d_s, [r[1] for r in mat])
    new_m = ordered(nm_s, [r[2] for r in mat])
    new_v = ordered(nv_s, [r[3] for r in mat])
    return (loss_all, dx[None], *grads, *deltas, *new_m, *new_v)
```

```python
import types

import jax
import jax.numpy as jnp
from jax import lax
from jax.experimental import pallas as pl
from jax.experimental.pallas import tpu as pltpu

T = 8192
D = 1024
C = 512
NPROJ = 2560
DFF = 2816
NUP = 2 * DFF
CONV_K = 31
FF_K = 3
HEAD = 64
EPS = 1e-6
NEG = -1e30
N_CHIPS = 4
N_DEV = 8
PATTERN_DILATIONS = (1, 4, 16)
QB = 128

ADAM_LR = 0.001
ADAM_B1 = 0.9
ADAM_B2 = 0.999
ADAM_EPS = 1e-08
ADAM_WD = 0.01
ADAM_STEP = 10

F32 = jnp.float32
BF16 = jnp.bfloat16
MESH = pl.DeviceIdType.MESH
ANY = pl.BlockSpec(memory_space=pl.ANY)

VPACK_ROWS = 48
SPACK_ROWS = 24


def _params(sem, vmem_mb):
    return pltpu.CompilerParams(dimension_semantics=sem, vmem_limit_bytes=vmem_mb << 20)


def _resident(shape):
    return pl.BlockSpec(shape, lambda i: (0, 0), pipeline_mode=pl.Buffered(1))


def _nt(a, b):
    return lax.dot_general(a, b, (((1,), (1,)), ((), ())), preferred_element_type=F32)


def _tn_dot(a, b):
    return lax.dot_general(a, b, (((0,), (0,)), ((), ())), preferred_element_type=F32)


def _sigmoid(x):
    return 1.0 / (1.0 + jnp.exp(-x))


def _segsum(x, bd):
    hi = x.astype(BF16)
    lo = (x - hi.astype(F32)).astype(BF16)
    return (jnp.dot(hi, bd, preferred_element_type=F32)
            + jnp.dot(lo, bd, preferred_element_type=F32))


def _place():
    x, y, c = lax.axis_index("x"), lax.axis_index("y"), lax.axis_index("c")
    chips = [(1 - x, y), (x, 1 - y), (1 - x, 1 - y)]
    return x, y, c, chips


def _block_of(ref, shard_shape, row_sharded, s):
    r, cdim = shard_shape
    if row_sharded:
        return ref.at[pl.ds(s * r, r), :]
    return ref.at[:, pl.ds(s * cdim, cdim)]


def _full_shape(shard_shape, row_sharded):
    r, cdim = shard_shape
    return (r * N_CHIPS, cdim) if row_sharded else (r, cdim * N_CHIPS)


def _gather_rider(shards, row_sharded, peers=(0, 1, 2), into=None):
    n = len(shards)
    shapes = [a.shape for a in shards]

    def copies(ins, outs, sems):
        send_sems, recv_sems, local_sems = sems
        x, y, c, chips = _place()
        me = 2 * x + y
        place = lambda k, s: _block_of(outs[k], shapes[k], row_sharded[k], s)
        local = []
        if into is None:
            local = [pltpu.make_async_copy(ins[k], place(k, me), local_sems.at[k]) for k in range(n)]
        sends, recvs = [], []
        for k in range(n):
            for j in peers:
                px, py = chips[j]
                sem = dict(send_sem=send_sems.at[3 * k + j], recv_sem=recv_sems.at[3 * k + j],
                           device_id=(px, py, c), device_id_type=MESH)
                sends.append(pltpu.make_async_remote_copy(src_ref=ins[k], dst_ref=place(k, me), **sem))
                recvs.append(pltpu.make_async_remote_copy(src_ref=ins[k], dst_ref=place(k, 2 * px + py), **sem))
        return local, sends, recvs

    return types.SimpleNamespace(
        operands=list(shards) + list(into or []), copies=copies,
        aliases={n + k: k for k in range(n)} if into is not None else {},
        out_shape=[jax.ShapeDtypeStruct(_full_shape(s, rs), a.dtype) for s, rs, a in zip(shapes, row_sharded, shards)],
        sems=[pltpu.SemaphoreType.DMA((3 * n,)), pltpu.SemaphoreType.DMA((3 * n,)), pltpu.SemaphoreType.DMA((n,))])


def _grad_rider(g_bf16, g_f32, row_sharded):
    shard = tuple(d // N_CHIPS if (i == 0) == row_sharded else d for i, d in enumerate(g_f32.shape))

    def copies(ins, outs, sems):
        send_sems, recv_sems, local_sems = sems
        gb, gf = ins
        rec, own = outs
        x, y, c, chips = _place()
        me = 2 * x + y
        local = [pltpu.make_async_copy(_block_of(gf, shard, row_sharded, me), own, local_sems.at[0])]
        sends, recvs = [], []
        for j, (px, py) in enumerate(chips):
            sem = dict(send_sem=send_sems.at[j], recv_sem=recv_sems.at[j], device_id=(px, py, c), device_id_type=MESH)
            sends.append(pltpu.make_async_remote_copy(
                src_ref=_block_of(gb, shard, row_sharded, 2 * px + py), dst_ref=rec.at[j], **sem))
            recvs.append(pltpu.make_async_remote_copy(
                src_ref=_block_of(gb, shard, row_sharded, me), dst_ref=rec.at[j], **sem))
        return local, sends, recvs

    return types.SimpleNamespace(
        operands=[g_bf16, g_f32], copies=copies, aliases={},
        out_shape=[jax.ShapeDtypeStruct((3,) + shard, BF16), jax.ShapeDtypeStruct(shard, F32)],
        sems=[pltpu.SemaphoreType.DMA((3,)), pltpu.SemaphoreType.DMA((3,)), pltpu.SemaphoreType.DMA((1,))])


def _rider_start(rider, ins, outs, sems):
    local, sends, _ = rider.copies(ins, outs, sems)
    for cp in local + sends:
        cp.start()


def _rider_wait(rider, ins, outs, sems):
    local, sends, recvs = rider.copies(ins, outs, sems)
    for cp in recvs:
        cp.wait_recv()
    for cp in sends:
        cp.wait_send()
    for cp in local:
        cp.wait()


def _call(body, *operands, rider=None, name, grid, in_specs, out_specs, out_shape, scratch_shapes=(),
          compiler_params, input_output_aliases=None):
    operands = [pltpu.with_memory_space_constraint(a, pltpu.HBM) for a in operands]
    if rider is None:
        return pl.pallas_call(
            body, name=name, grid=grid, in_specs=list(in_specs), out_specs=list(out_specs), out_shape=list(out_shape),
            scratch_shapes=list(scratch_shapes), compiler_params=compiler_params,
            input_output_aliases=input_output_aliases or {})(*operands)
    n_in, n_out, n_scr = len(in_specs), len(out_specs), len(scratch_shapes)
    r_in, r_out = len(rider.operands), len(rider.out_shape)

    def riding(*refs):
        refs = list(refs)
        ins, refs = refs[:n_in], refs[n_in:]
        r_ins, refs = refs[:r_in], refs[r_in:]
        outs, refs = refs[:n_out], refs[n_out:]
        r_outs, refs = refs[:r_out], refs[r_out:]
        scr, sems = refs[:n_scr], refs[n_scr:]
        first = pl.program_id(0) == 0
        last = pl.program_id(0) == grid[0] - 1
        for axis in range(1, len(grid)):
            first = first & (pl.program_id(axis) == 0)
            last = last & (pl.program_id(axis) == grid[axis] - 1)

        @pl.when(first)
        def _():
            _rider_start(rider, r_ins, r_outs, sems)

        body(*ins, *outs, *scr)

        @pl.when(last)
        def _():
            _rider_wait(rider, r_ins, r_outs, sems)

    return pl.pallas_call(
        riding, name=name, grid=grid, in_specs=list(in_specs) + [ANY] * r_in,
        out_specs=list(out_specs) + [ANY] * r_out, out_shape=list(out_shape) + list(rider.out_shape),
        scratch_shapes=list(scratch_shapes) + list(rider.sems), compiler_params=compiler_params,
        input_output_aliases={**(input_output_aliases or {}),
                              **{n_in + i: n_out + o for i, o in rider.aliases.items()}})(
            *operands, *[pltpu.with_memory_space_constraint(a, pltpu.HBM) for a in rider.operands])


def _gather_now(shards, row_sharded):
    rider = _gather_rider(shards, row_sharded)
    n = len(shards)

    def body(*refs):
        ins, outs, sems = refs[:n], refs[n:2 * n], refs[2 * n:]
        _rider_start(rider, ins, outs, sems)
        _rider_wait(rider, ins, outs, sems)

    return pl.pallas_call(
        body, name="gather_first", out_shape=rider.out_shape, in_specs=[ANY] * n, out_specs=[ANY] * n,
        scratch_shapes=rider.sems)(*shards)


def _final_exchange(parts, vpack):
    def body(p0, p1, p2, p3, v_ref, o0, o1, o2, o3, vr_ref, send_sems, recv_sems, vsend_sems, vrecv_sems, local_sem):
        x, y, c, _ = _place()
        me = 4 * x + 2 * y + c
        mine = pltpu.make_async_copy(v_ref, vr_ref.at[me], local_sem)
        mine.start()
        copies = [pltpu.make_async_remote_copy(
            src_ref=p, dst_ref=o, send_sem=send_sems.at[k], recv_sem=recv_sems.at[k],
            device_id=(x, y, 1 - c), device_id_type=MESH)
            for k, (p, o) in enumerate(zip((p0, p1, p2, p3), (o0, o1, o2, o3)))]
        flips = [(fx, fy, fc) for fx in (0, 1) for fy in (0, 1) for fc in (0, 1)][1:]
        recvs = []
        for r, (fx, fy, fc) in enumerate(flips):
            peer = (x ^ fx, y ^ fy, c ^ fc)
            sem = dict(send_sem=vsend_sems.at[r], recv_sem=vrecv_sems.at[r], device_id=peer, device_id_type=MESH)
            copies.append(pltpu.make_async_remote_copy(src_ref=v_ref, dst_ref=vr_ref.at[me], **sem))
            recvs.append(pltpu.make_async_remote_copy(
                src_ref=v_ref, dst_ref=vr_ref.at[4 * peer[0] + 2 * peer[1] + peer[2]], **sem))
        for cp in copies:
            cp.start()
        for cp in copies[:4]:
            cp.wait_recv()
        for cp in recvs:
            cp.wait_recv()
        for cp in copies:
            cp.wait_send()
        mine.wait()

    res = pl.pallas_call(
        body, name="final_exchange",
        out_shape=[jax.ShapeDtypeStruct(p.shape, F32) for p in parts]
        + [jax.ShapeDtypeStruct((N_DEV, VPACK_ROWS, D), F32)],
        in_specs=[ANY] * 5, out_specs=[ANY] * 5,
        scratch_shapes=[pltpu.SemaphoreType.DMA((4,)), pltpu.SemaphoreType.DMA((4,)),
                        pltpu.SemaphoreType.DMA((7,)), pltpu.SemaphoreType.DMA((7,)), pltpu.SemaphoreType.DMA],
    )(*parts, vpack)
    return res[:4], res[4]


def _proj_fwd(x, g1, w_in, rider):
    tm, tn = 512, 640

    def body(x_ref, g_ref, w_ref, h_ref, p_ref):
        xv = x_ref[...]
        r = lax.rsqrt(jnp.mean(xv * xv, axis=-1, keepdims=True) + EPS)
        h_ref[...] = (xv * r * g_ref[...]).astype(BF16)
        for j in range(NPROJ // tn):
            cols = slice(j * tn, (j + 1) * tn)
            p_ref[:, cols] = jnp.dot(h_ref[...], w_ref[:, cols], preferred_element_type=F32)

    return _call(
        body, x, g1, w_in, rider=rider, name="proj_fwd", grid=(T // tm,),
        in_specs=[pl.BlockSpec((tm, D), lambda i: (i, 0)), pl.BlockSpec((1, D), lambda i: (0, 0)),
                  _resident((D, NPROJ))],
        out_specs=[pl.BlockSpec((tm, D), lambda i: (i, 0)), pl.BlockSpec((tm, NPROJ), lambda i: (i, 0))],
        out_shape=[jax.ShapeDtypeStruct((T, D), BF16), jax.ShapeDtypeStruct((T, NPROJ), F32)],
        compiler_params=_params(("arbitrary",), 40))


CONV_TM = 512
CONV_HALO = 32
CONV_RB = 32
CONV_CB = 64


LANES = 128


def _sp(start, n):
    return (pl.ds(2 * start, n, stride=2), slice(None))


def _lanes(tile):
    return slice(tile * LANES, (tile + 1) * LANES)


def _conv_fwd(proj, conv_w, conv_b, cn_g, cn_b, rider):
    tm, hl, rb, cb = CONV_TM, CONV_HALO, CONV_RB, CONV_CB
    per = tm // hl

    def body(av_ref, ag_ref, hv_ref, hg_ref, w_ref, b_ref, g_ref, bb_ref, cat_ref, cv_ref, sh_ref):
        i = pl.program_id(0)
        for j in range(C // LANES):
            ln_ = _lanes(j)
            glu_h = hv_ref[:, ln_] * _sigmoid(hg_ref[:, ln_])
            sh_ref.at[j][_sp(0, hl)] = jnp.where(i > 0, glu_h, 0.0)
            for r0 in range(0, tm, cb):
                sh_ref.at[j][_sp(hl + r0, cb)] = av_ref[r0:r0 + cb, ln_] * _sigmoid(ag_ref[r0:r0 + cb, ln_])
            for r0 in range(0, tm, cb):
                acc = jnp.zeros((cb, LANES), F32) + b_ref[:, ln_]
                for k in range(CONV_K):
                    acc = acc + w_ref[k:k + 1, ln_] * sh_ref.at[j][_sp(r0 + hl - (CONV_K - 1) + k, cb)]
                cv_ref[r0:r0 + cb, ln_] = acc
        for r0 in range(0, tm, rb):
            acc = cv_ref[r0:r0 + rb, :]
            mu = jnp.mean(acc, axis=-1, keepdims=True)
            xc = acc - mu
            var = jnp.mean(xc * xc, axis=-1, keepdims=True)
            ln = xc * lax.rsqrt(var + EPS) * g_ref[...] + bb_ref[...]
            cat_ref[r0:r0 + rb, :] = (ln * _sigmoid(ln)).astype(BF16)

    halo = lambda col: pl.BlockSpec((hl, C), lambda i: (jnp.maximum(i * per - 1, 0), col))
    vec = pl.BlockSpec((1, C), lambda i: (0, 0))
    return _call(
        body, proj, proj, proj, proj, conv_w, conv_b, cn_g, cn_b, rider=rider, name="conv_fwd", grid=(T // tm,),
        in_specs=[pl.BlockSpec((tm, C), lambda i: (i, 0)), pl.BlockSpec((tm, C), lambda i: (i, 1)),
                  halo(0), halo(1), pl.BlockSpec((CONV_K, C), lambda i: (0, 0)), vec, vec, vec],
        out_specs=[pl.BlockSpec((tm, C), lambda i: (i, 0)), pl.BlockSpec((tm, C), lambda i: (i, 0))],
        out_shape=[jax.ShapeDtypeStruct((T, D), BF16), jax.ShapeDtypeStruct((T, C), F32)],
        scratch_shapes=[pltpu.VMEM((C // LANES, 2 * (tm + hl), LANES), F32)],
        compiler_params=_params(("arbitrary",), 40))


def _qkv_prep(proj, qg, kg, bd):
    tm = 512

    def body(q_ref, k_ref, qg_ref, kg_ref, bd_ref, qn_ref, kn_ref):
        for src, g, dst in ((q_ref, qg_ref, qn_ref), (k_ref, kg_ref, kn_ref)):
            xv = src[...]
            ms = _segsum(xv * xv, bd_ref[...]) * (1.0 / HEAD)
            dst[...] = xv * lax.rsqrt(ms + EPS) * g[...]

    col = lambda c: pl.BlockSpec((tm, C), lambda i: (i, c))
    vec = pl.BlockSpec((1, C), lambda i: (0, 0))
    out = pl.BlockSpec((tm, C), lambda i: (i, 0))
    return _call(
        body, proj, proj, qg, kg, bd, name="qkv_prep", grid=(T // tm,),
        in_specs=[col(2), col(3), vec, vec, pl.BlockSpec((C, C), lambda i: (0, 0))],
        out_specs=[out, out],
        out_shape=[jax.ShapeDtypeStruct((T, C), F32)] * 2,
        compiler_params=_params(("parallel",), 32))


def _stack_heads(a):
    lane = lax.broadcasted_iota(jnp.int32, a.shape, 1)
    zero = jnp.zeros_like(a)
    return jnp.concatenate([jnp.where(lane < HEAD, a, zero), jnp.where(lane >= HEAD, a, zero)], axis=0)


def _unstack_heads(a2):
    lane = lax.broadcasted_iota(jnp.int32, (QB, 2 * HEAD), 1)
    return jnp.where(lane < HEAD, a2[:QB], a2[QB:])


def _stack_cols(a):
    return jnp.concatenate([a[:, 0:1], a[:, HEAD:HEAD + 1]], axis=0)


ATT_WIN = 2048
V_COL = 4 * C // (2 * HEAD)
DO_COL = C // (2 * HEAD)


def _attn_geometry(d):
    sl = ATT_WIN // d
    return sl, sl // QB, QB * d


def _stream(ref, r, n, d):
    return ref[pl.ds(r, n, stride=d), :] if d > 1 else ref[pl.ds(r, n), :]


def _alibi_tables(d):
    qi = jnp.arange(QB)[:, None]
    kj = jnp.arange(2 * QB)[None, :]
    delta = qi + QB - kj
    band = (delta >= 0) & (delta <= QB)
    dist = (delta * d).astype(F32)
    heads = jnp.arange(8, dtype=F32)
    slopes = 2.0 ** (-(heads + 1.0))
    t = jnp.where(band[None], -slopes[:, None, None] * dist[None], NEG)
    return t.reshape(4, 2 * QB, 2 * QB)


def _attn_specs(d):
    _, _, hr = _attn_geometry(d)
    per = ATT_WIN // hr
    main = lambda off: pl.BlockSpec((ATT_WIN, 2 * HEAD), lambda cb, n: (n, off + cb))
    prev = lambda off: pl.BlockSpec((hr, 2 * HEAD), lambda cb, n: (jnp.maximum(n * per - 1, 0), off + cb))
    nxt = lambda off: pl.BlockSpec((hr, 2 * HEAD), lambda cb, n: (jnp.minimum((n + 1) * per, T // hr - 1), off + cb))
    bias = pl.BlockSpec((None, 2 * QB, 2 * QB), lambda cb, n: (cb, 0, 0))
    return main, prev, nxt, bias


def _attn_fwd(qn, kn, proj, bias, d, rider=None, merge=None):
    sl, nb, hr = _attn_geometry(d)
    slk = QB + sl
    mrows = 256

    def body(q_ref, k_ref, v_ref, kh_ref, vh_ref, bias_ref, *rest):
        if merge is None:
            o_ref, l_ref, qs, ks, vs, os_, ls = rest
        else:
            oa_ref, la_ref, ob_ref, lb_ref, _, cat_ref, of_ref, lg_ref, qs, ks, vs, os_, ls, o_ref, l_ref = rest
        n = pl.program_id(1)
        for r in range(d):
            qs[r * sl:(r + 1) * sl, :] = _stream(q_ref, r, sl, d).astype(BF16)
            for dst, halo, src in ((ks, kh_ref, k_ref), (vs, vh_ref, v_ref)):
                dst[r * slk:r * slk + QB, :] = _stream(halo, r, QB, d).astype(BF16)
                dst[r * slk + QB:(r + 1) * slk, :] = _stream(src, r, sl, d).astype(BF16)
        col = lax.broadcasted_iota(jnp.int32, (2 * QB, 2 * QB), 1)
        for r in range(d):
            for b in range(nb):
                rows = slice(r * sl + b * QB, r * sl + (b + 1) * QB)
                keys = slice(r * slk + b * QB, r * slk + (b + 2) * QB)
                s = _nt(_stack_heads(qs[rows, :]), ks[keys, :]) + bias_ref[...]
                if b == 0:
                    s = jnp.where((col < QB) & (n == 0), NEG, s)
                m = jnp.max(s, axis=-1, keepdims=True)
                p = jnp.exp(s - m)
                den = jnp.sum(p, axis=-1, keepdims=True)
                pv = jnp.dot(p.astype(BF16), vs[keys, :], preferred_element_type=F32)
                os_[rows, :] = _unstack_heads(pv / den)
                ls[rows, :] = _unstack_heads(jnp.broadcast_to(m + jnp.log(den), (2 * QB, 2 * HEAD)))
        for r in range(d):
            if d > 1:
                o_ref[pl.ds(r, sl, stride=d), :] = os_[r * sl:(r + 1) * sl, :]
                l_ref[pl.ds(r, sl, stride=d), :] = ls[r * sl:(r + 1) * sl, :]
            else:
                o_ref[...] = os_[...]
                l_ref[...] = ls[...]
        if merge is not None:
            for r0 in range(0, ATT_WIN, mrows):
                rows = slice(r0, r0 + mrows)
                a, b, c = la_ref[rows, :], lb_ref[rows, :], l_ref[rows, :]
                m = jnp.maximum(jnp.maximum(a, b), c)
                e0, e1, e2 = jnp.exp(a - m), jnp.exp(b - m), jnp.exp(c - m)
                den = e0 + e1 + e2
                o = (e0 * oa_ref[rows, :] + e1 * ob_ref[rows, :] + e2 * o_ref[rows, :]) / den
                of_ref[rows, :] = o
                cat_ref[rows, :] = o.astype(BF16)
                lg_ref[rows, :] = m + jnp.log(den)

    main, prev, _, bias_spec = _attn_specs(d)
    lanes = 2 * HEAD
    operands = [qn, kn, proj, kn, proj, bias]
    in_specs = [main(0), main(0), main(V_COL), prev(0), prev(V_COL), bias_spec]
    scratch = [pltpu.VMEM((ATT_WIN, lanes), BF16), pltpu.VMEM((ATT_WIN + hr, lanes), BF16),
               pltpu.VMEM((ATT_WIN + hr, lanes), BF16), pltpu.VMEM((ATT_WIN, lanes), F32),
               pltpu.VMEM((ATT_WIN, lanes), F32)]
    if merge is None:
        out_specs = [main(0), main(0)]
        out_shape = [jax.ShapeDtypeStruct((T, C), F32)] * 2
        aliases = None
    else:
        operands += list(merge)
        in_specs += [main(0)] * 4 + [ANY]
        out_specs = [main(C // lanes), main(0), main(0)]
        out_shape = [jax.ShapeDtypeStruct((T, D), BF16), jax.ShapeDtypeStruct((T, C), F32),
                     jax.ShapeDtypeStruct((T, C), F32)]
        scratch += [pltpu.VMEM((ATT_WIN, lanes), F32)] * 2
        aliases = {len(operands) - 1: 0}
    return _call(
        body, *operands, rider=rider, name=f"attn_fwd_d{d}", grid=(C // lanes, T // ATT_WIN),
        in_specs=in_specs, out_specs=out_specs, out_shape=out_shape, scratch_shapes=scratch,
        input_output_aliases=aliases, compiler_params=_params(("arbitrary", "arbitrary"), 48))


def _out_up(x, cat, w_out, g2, w_up, rider):
    tm, tn = 512, NUP // 4

    def body(x_ref, cat_ref, wo_ref, g_ref, wu_ref, x1_ref, h2_ref, up_ref):
        x1 = x_ref[...] + jnp.dot(cat_ref[...], wo_ref[...], preferred_element_type=F32)
        x1_ref[...] = x1
        r = lax.rsqrt(jnp.mean(x1 * x1, axis=-1, keepdims=True) + EPS)
        h2_ref[...] = (x1 * r * g_ref[...]).astype(BF16)
        for j in range(NUP // tn):
            cols = slice(j * tn, (j + 1) * tn)
            up_ref[:, cols] = jnp.dot(h2_ref[...], wu_ref[:, cols], preferred_element_type=F32)

    row = pl.BlockSpec((tm, D), lambda i: (i, 0))
    return _call(
        body, x, cat, w_out, g2, w_up, rider=rider, name="out_up", grid=(T // tm,),
        in_specs=[row, row, _resident((D, D)), pl.BlockSpec((1, D), lambda i: (0, 0)), _resident((D, NUP))],
        out_specs=[row, row, pl.BlockSpec((tm, NUP), lambda i: (i, 0))],
        out_shape=[jax.ShapeDtypeStruct((T, D), F32), jax.ShapeDtypeStruct((T, D), BF16),
                   jax.ShapeDtypeStruct((T, NUP), F32)],
        compiler_params=_params(("arbitrary",), 58))


FF_TM = 256
FF_HALO = 8
FF_RB = 64
FF_TILES = DFF // LANES


def _ff_conv(ext_ref, fw_ref, fb_ref, tile, r0):
    cols = _lanes(tile)
    base = FF_HALO + r0
    acc = fb_ref[:, cols] + fw_ref[0:1, cols] * ext_ref.at[tile][_sp(base - 2, FF_RB)]
    acc = acc + fw_ref[1:2, cols] * ext_ref.at[tile][_sp(base - 1, FF_RB)]
    return acc + fw_ref[2:3, cols] * ext_ref.at[tile][_sp(base, FF_RB)]


def _ffn(up, ffconv_w, ffconv_b, w_down, x1, target):
    tm, hl = FF_TM, FF_HALO
    per = tm // hl
    nt = T // tm
    tiles = 2 * FF_TILES

    def body(up_ref, uh_ref, fw_ref, fb_ref, wd_ref, x1_ref, tg_ref, act_ref, dy_ref, loss_ref, dup_ref, gff_ref,
             ext_ref, gv_ref, dact_ref, carry_ref):
        i = pl.program_id(0)

        @pl.when(i == 0)
        def _():
            gff_ref[...] = jnp.zeros_like(gff_ref)
            loss_ref[...] = jnp.zeros_like(loss_ref)
            carry_ref[...] = jnp.zeros_like(carry_ref)

        for j in range(tiles):
            ext_ref.at[j][_sp(0, hl)] = jnp.where(i < nt - 1, uh_ref[:, _lanes(j)], 0.0)
            for r0 in range(0, tm, FF_RB):
                ext_ref.at[j][_sp(hl + r0, FF_RB)] = up_ref[r0:r0 + FF_RB, _lanes(j)]
        for c in range(FF_TILES):
            for r0 in range(0, tm, FF_RB):
                rows = slice(r0, r0 + FF_RB)
                gate = _ff_conv(ext_ref, fw_ref, fb_ref, c, r0)
                val = _ff_conv(ext_ref, fw_ref, fb_ref, FF_TILES + c, r0)
                gv_ref[rows, _lanes(c)] = gate
                gv_ref[rows, _lanes(FF_TILES + c)] = val
                act_ref[rows, _lanes(c)] = (gate * _sigmoid(gate) * val).astype(BF16)
        err = x1_ref[...] + jnp.dot(act_ref[...], wd_ref[...], preferred_element_type=F32) - tg_ref[...]
        dy_ref[...] = err * (1.0 / D)
        loss_ref[...] += jnp.sum(err * err)
        dact_ref[...] = _nt(dy_ref[...].astype(BF16), wd_ref[...])

        for c in range(FF_TILES):
            for r0 in range(0, tm, FF_RB):
                rows = slice(r0, r0 + FF_RB)
                gate, val = gv_ref[rows, _lanes(c)], gv_ref[rows, _lanes(FF_TILES + c)]
                sg = _sigmoid(gate)
                da = dact_ref[rows, _lanes(c)]
                ext_ref.at[c][_sp(r0, FF_RB)] = da * val * (sg + gate * sg * (1.0 - sg))
                ext_ref.at[FF_TILES + c][_sp(r0, FF_RB)] = da * gate * sg
        fold = lambda a: jnp.sum(a.reshape(FF_RB // 8, 8, LANES), axis=0)
        for c in range(tiles):
            cols = _lanes(c)
            ext_ref.at[c][_sp(tm, hl)] = carry_ref[c]
            taps = [fw_ref[k:k + 1, cols] for k in range(FF_K)]
            acc = [jnp.zeros((8, LANES), F32) for _ in range(FF_K + 1)]
            for r0 in range(0, tm, FF_RB):
                shifted = [ext_ref.at[c][_sp(r0 + k, FF_RB)] for k in range(FF_K)]
                u = up_ref[r0:r0 + FF_RB, cols]
                dup = taps[2] * shifted[0] + taps[1] * shifted[1] + taps[0] * shifted[2]
                dup_ref[r0:r0 + FF_RB, cols] = dup.astype(BF16)
                for k in range(FF_K):
                    acc[2 - k] = acc[2 - k] + fold(shifted[k] * u)
                acc[FF_K] = acc[FF_K] + fold(shifted[0])
            for k in range(FF_K + 1):
                gff_ref[k:k + 1, cols] += jnp.sum(acc[k], axis=0, keepdims=True)
            carry_ref[c] = ext_ref.at[c][_sp(0, hl)]

    rev = lambda i: (nt - 1 - i, 0)
    row = pl.BlockSpec((tm, D), rev)
    wide = pl.BlockSpec((tm, NUP), rev)
    return _call(
        body, up, up, ffconv_w, ffconv_b, w_down, x1, target, name="ffn", grid=(nt,),
        in_specs=[wide, pl.BlockSpec((hl, NUP), lambda i: (jnp.maximum((nt - 1 - i) * per - 1, 0), 0)),
                  pl.BlockSpec((FF_K, NUP), lambda i: (0, 0)), pl.BlockSpec((1, NUP), lambda i: (0, 0)),
                  _resident((DFF, D)), row, row],
        out_specs=[pl.BlockSpec((tm, DFF), rev), row, pl.BlockSpec((8, 128), lambda i: (0, 0)), wide,
                   pl.BlockSpec((8, NUP), lambda i: (0, 0))],
        out_shape=[jax.ShapeDtypeStruct((T, DFF), BF16), jax.ShapeDtypeStruct((T, D), F32),
                   jax.ShapeDtypeStruct((8, 128), F32), jax.ShapeDtypeStruct((T, NUP), BF16),
                   jax.ShapeDtypeStruct((8, NUP), F32)],
        scratch_shapes=[pltpu.VMEM((tiles, 2 * (tm + hl), LANES), F32), pltpu.VMEM((tm, NUP), F32),
                        pltpu.VMEM((tm, DFF), F32), pltpu.VMEM((tiles, hl, LANES), F32)],
        compiler_params=_params(("arbitrary",), 58))


def _weight_grad(a, g, bm, bn, tk, name):
    m, n = a.shape[1], g.shape[1]
    nk = T // tk

    def body(a_ref, g_ref, of_ref, ob_ref):
        k = pl.program_id(2)

        @pl.when(k == 0)
        def _():
            of_ref[...] = jnp.zeros_like(of_ref)
        of_ref[...] += _tn_dot(a_ref[...].astype(BF16), g_ref[...].astype(BF16))

        @pl.when(k == nk - 1)
        def _():
            ob_ref[...] = of_ref[...].astype(BF16)

    out = pl.BlockSpec((bm, bn), lambda i, j, k: (i, j))
    return _call(
        body, a, g, name=name, grid=(m // bm, n // bn, nk),
        in_specs=[pl.BlockSpec((tk, bm), lambda i, j, k: (k, i)), pl.BlockSpec((tk, bn), lambda i, j, k: (k, j))],
        out_specs=[out, out],
        out_shape=[jax.ShapeDtypeStruct((m, n), F32), jax.ShapeDtypeStruct((m, n), BF16)],
        compiler_params=_params(("parallel", "parallel", "arbitrary"), 56))


def _norm_bwd_mm(dz, w, xin, base, gain, name, rider):
    kdim = dz.shape[1]
    tm = 512

    def body(dz_ref, w_ref, x_ref, b_ref, g_ref, dx_ref, gg_ref):
        @pl.when(pl.program_id(0) == 0)
        def _():
            gg_ref[...] = jnp.zeros_like(gg_ref)

        xv = x_ref[...]
        dh = _nt(dz_ref[...], w_ref[...])
        r = lax.rsqrt(jnp.mean(xv * xv, axis=-1, keepdims=True) + EPS)
        t = dh * g_ref[...]
        dx_ref[...] = b_ref[...] + r * t - xv * (r * r * r) * jnp.mean(t * xv, axis=-1, keepdims=True)
        gg_ref[...] += jnp.sum(dh * xv * r, axis=0, keepdims=True)

    row = pl.BlockSpec((tm, D), lambda i: (i, 0))
    vec = pl.BlockSpec((1, D), lambda i: (0, 0))
    return _call(
        body, dz, w, xin, base, gain, rider=rider, name=name, grid=(T // tm,),
        in_specs=[pl.BlockSpec((tm, kdim), lambda i: (i, 0)), _resident((D, kdim)), row, row, vec],
        out_specs=[row, vec],
        out_shape=[jax.ShapeDtypeStruct((T, D), F32), jax.ShapeDtypeStruct((1, D), F32)],
        compiler_params=_params(("arbitrary",), 48))


def _outproj_bwd(dx1, w_out):
    tm = 512

    def body(d_ref, w_ref, o_ref):
        o_ref[...] = _nt(d_ref[...].astype(BF16), w_ref[...])

    row = pl.BlockSpec((tm, D), lambda i: (i, 0))
    return _call(
        body, dx1, w_out, name="outproj_bwd", grid=(T // tm,),
        in_specs=[row, pl.BlockSpec((D, D), lambda i: (0, 0))], out_specs=[row],
        out_shape=[jax.ShapeDtypeStruct((T, D), F32)],
        compiler_params=_params(("parallel",), 32))[0]


def _conv_bwd(dcat, cv, proj, conv_w, cn_g, cn_b, rider):
    tm, hl, rb, cb = CONV_TM, CONV_HALO, CONV_RB, CONV_CB
    per = tm // hl
    nt = T // tm
    tiles = C // LANES

    def body(du_ref, dun_ref, cv_ref, cvn_ref, av_ref, ag_ref, hv_ref, hg_ref, w_ref, g_ref, bb_ref,
             dp_ref, gv_ref, gw_ref, dsh_ref, gsh_ref):
        i = pl.program_id(0)

        @pl.when(i == 0)
        def _():
            gv_ref[...] = jnp.zeros_like(gv_ref)
            gw_ref[...] = jnp.zeros_like(gw_ref)

        def ln_bwd(du, cvv):
            mu = jnp.mean(cvv, axis=-1, keepdims=True)
            xc = cvv - mu
            rs = lax.rsqrt(jnp.mean(xc * xc, axis=-1, keepdims=True) + EPS)
            xh = xc * rs
            ln = xh * g_ref[...] + bb_ref[...]
            sg = _sigmoid(ln)
            dln = du * (sg + ln * sg * (1.0 - sg))
            dxh = dln * g_ref[...]
            dcv = rs * (dxh - jnp.mean(dxh, axis=-1, keepdims=True)
                        - xh * jnp.mean(dxh * xh, axis=-1, keepdims=True))
            return dcv, dln, xh

        for r0 in range(0, tm, rb):
            dcv, dln, xh = ln_bwd(du_ref[r0:r0 + rb, :], cv_ref[r0:r0 + rb, :])
            for j in range(tiles):
                dsh_ref.at[j][_sp(r0, rb)] = dcv[:, _lanes(j)]
            gv_ref[0:1, :] += jnp.sum(dln * xh, axis=0, keepdims=True)
            gv_ref[1:2, :] += jnp.sum(dln, axis=0, keepdims=True)
            gv_ref[2:3, :] += jnp.sum(dcv, axis=0, keepdims=True)
        dcv_n, _, _ = ln_bwd(dun_ref[...], cvn_ref[...])
        dcv_n = jnp.where(i < nt - 1, dcv_n, 0.0)
        for j in range(tiles):
            ln_ = _lanes(j)
            dsh_ref.at[j][_sp(tm, hl)] = dcv_n[:, ln_]
            glu_h = hv_ref[:, ln_] * _sigmoid(hg_ref[:, ln_])
            gsh_ref.at[j][_sp(0, hl)] = jnp.where(i > 0, glu_h, 0.0)
            for r0 in range(0, tm, cb):
                gsh_ref.at[j][_sp(hl + r0, cb)] = av_ref[r0:r0 + cb, ln_] * _sigmoid(ag_ref[r0:r0 + cb, ln_])

        for j in range(tiles):
            ln_ = _lanes(j)
            for r0 in range(0, tm, cb):
                dglu = jnp.zeros((cb, LANES), F32)
                for k in range(CONV_K):
                    dglu = dglu + w_ref[k:k + 1, ln_] * dsh_ref.at[j][_sp(r0 + (CONV_K - 1) - k, cb)]
                av = av_ref[r0:r0 + cb, ln_]
                sg = _sigmoid(ag_ref[r0:r0 + cb, ln_])
                dp_ref[r0:r0 + cb, ln_] = (dglu * sg).astype(BF16)
                dp_ref[r0:r0 + cb, _lanes(tiles + j)] = (dglu * av * sg * (1.0 - sg)).astype(BF16)
            for k in range(CONV_K):
                part = jnp.zeros((8, LANES), F32)
                for r0 in range(0, tm, cb):
                    prod = dsh_ref.at[j][_sp(r0, cb)] * gsh_ref.at[j][_sp(r0 + hl - (CONV_K - 1) + k, cb)]
                    part = part + jnp.sum(prod.reshape(cb // 8, 8, LANES), axis=0)
                gw_ref[k:k + 1, ln_] += jnp.sum(part, axis=0, keepdims=True)

    main = lambda col: pl.BlockSpec((tm, C), lambda i: (i, col))
    prev = lambda col: pl.BlockSpec((hl, C), lambda i: (jnp.maximum(i * per - 1, 0), col))
    nxt = pl.BlockSpec((hl, C), lambda i: (jnp.minimum((i + 1) * per, T // hl - 1), 0))
    vec = pl.BlockSpec((1, C), lambda i: (0, 0))
    return _call(
        body, dcat, dcat, cv, cv, proj, proj, proj, proj, conv_w, cn_g, cn_b, rider=rider, name="conv_bwd",
        grid=(nt,),
        in_specs=[main(0), nxt, main(0), nxt, main(0), main(1), prev(0), prev(1),
                  pl.BlockSpec((CONV_K, C), lambda i: (0, 0)), vec, vec],
        out_specs=[pl.BlockSpec((tm, 2 * C), lambda i: (i, 0)), pl.BlockSpec((8, C), lambda i: (0, 0)),
                   pl.BlockSpec((32, C), lambda i: (0, 0))],
        out_shape=[jax.ShapeDtypeStruct((T, NPROJ), BF16), jax.ShapeDtypeStruct((8, C), F32),
                   jax.ShapeDtypeStruct((32, C), F32)],
        scratch_shapes=[pltpu.VMEM((C // LANES, 2 * (tm + hl), LANES), F32)] * 2,
        compiler_params=_params(("arbitrary",), 48))


def _attn_bwd_prep(dcat, o_f32, bd):
    tm = 512

    def body(do_ref, o_ref, bd_ref, dl_ref):
        dl_ref[...] = _segsum(do_ref[...] * o_ref[...], bd_ref[...])

    blk = pl.BlockSpec((tm, C), lambda i: (i, 0))
    return _call(
        body, dcat, o_f32, bd, name="attn_bwd_prep", grid=(T // tm,),
        in_specs=[pl.BlockSpec((tm, C), lambda i: (i, 1)), blk, pl.BlockSpec((C, C), lambda i: (0, 0))],
        out_specs=[blk],
        out_shape=[jax.ShapeDtypeStruct((T, C), F32)],
        compiler_params=_params(("parallel",), 32))[0]


def _attn_bwd_unit(qs, dos, lgs, dls, rows, kc, vc, biasv, invalid_prev):
    qst, dost = _stack_heads(qs[rows, :]), _stack_heads(dos[rows, :])
    s = _nt(qst, kc) + biasv
    if invalid_prev is not None:
        col = lax.broadcasted_iota(jnp.int32, s.shape, 1)
        s = jnp.where((col < QB) & invalid_prev, NEG, s)
    p = jnp.exp(s - _stack_cols(lgs[rows, :]))
    ds = p * (_nt(dost, vc) - _stack_cols(dls[rows, :]))
    dsb = ds.astype(BF16)
    dq = _unstack_heads(jnp.dot(dsb, kc, preferred_element_type=F32))
    return dq, _tn_dot(dsb, qst), _tn_dot(p.astype(BF16), dost)


def _attn_bwd_lagged(qn, kn, proj, dcat, lg, dl, bias, d):
    assert QB * d == ATT_WIN
    n_win = T // ATT_WIN
    lanes = 2 * HEAD

    def body(q_ref, k_ref, v_ref, do_ref, lg_ref, dl_ref, kh_ref, vh_ref, bias_ref, dq_ref, dk_ref, dv_ref,
             qs, dos, lgs, dls, ks, vs, dqs, ck, cv, ok, ov):
        n = pl.program_id(1)

        @pl.when(n == 0)
        def _():
            ck[...] = jnp.zeros_like(ck)
            cv[...] = jnp.zeros_like(cv)

        @pl.when(n < n_win)
        def _():
            for r in range(d):
                rows = slice(r * QB, (r + 1) * QB)
                for dst, src, dt in ((qs, q_ref, BF16), (dos, do_ref, BF16), (lgs, lg_ref, F32), (dls, dl_ref, F32)):
                    dst[rows, :] = _stream(src, r, QB, d).astype(dt)
                for dst, halo, src in ((ks, kh_ref, k_ref), (vs, vh_ref, v_ref)):
                    dst[2 * r * QB:(2 * r + 1) * QB, :] = _stream(halo, r, QB, d).astype(BF16)
                    dst[(2 * r + 1) * QB:(2 * r + 2) * QB, :] = _stream(src, r, QB, d).astype(BF16)
            for r in range(d):
                rows = slice(r * QB, (r + 1) * QB)
                keys = slice(2 * r * QB, (2 * r + 2) * QB)
                dq, dkc, dvc = _attn_bwd_unit(qs, dos, lgs, dls, rows, ks[keys, :], vs[keys, :], bias_ref[...], n == 0)
                dqs[rows, :] = dq
                ok[rows, :] = ck[rows, :] + dkc[:QB]
                ov[rows, :] = cv[rows, :] + dvc[:QB]
                ck[rows, :] = dkc[QB:]
                cv[rows, :] = dvc[QB:]
            for r in range(d):
                rows = slice(r * QB, (r + 1) * QB)
                dq_ref[pl.ds(r, QB, stride=d), :] = dqs[rows, :]
                dk_ref[pl.ds(r, QB, stride=d), :] = ok[rows, :]
                dv_ref[pl.ds(r, QB, stride=d), :] = ov[rows, :]

        @pl.when(n == n_win)
        def _():
            for r in range(d):
                rows = slice(r * QB, (r + 1) * QB)
                dk_ref[pl.ds(r, QB, stride=d), :] = ck[rows, :]
                dv_ref[pl.ds(r, QB, stride=d), :] = cv[rows, :]

    cur = lambda off: pl.BlockSpec((ATT_WIN, lanes), lambda cb, n: (jnp.minimum(n, n_win - 1), off + cb))
    prev = lambda off: pl.BlockSpec(
        (ATT_WIN, lanes), lambda cb, n: (jnp.maximum(jnp.minimum(n, n_win - 1) - 1, 0), off + cb))
    late = pl.BlockSpec((ATT_WIN, lanes), lambda cb, n: (jnp.maximum(n - 1, 0), cb))
    buf = lambda rows, dt: pltpu.VMEM((rows, lanes), dt)
    return _call(
        body, qn, kn, proj, dcat, lg, dl, kn, proj, bias, name=f"attn_bwd_d{d}", grid=(C // lanes, n_win + 1),
        in_specs=[cur(0), cur(0), cur(V_COL), cur(DO_COL), cur(0), cur(0), prev(0), prev(V_COL),
                  pl.BlockSpec((None, 2 * QB, 2 * QB), lambda cb, n: (cb, 0, 0))],
        out_specs=[cur(0), late, late],
        out_shape=[jax.ShapeDtypeStruct((T, C), F32)] * 3,
        scratch_shapes=[buf(ATT_WIN, BF16), buf(ATT_WIN, BF16), buf(ATT_WIN, F32), buf(ATT_WIN, F32),
                        buf(2 * ATT_WIN, BF16), buf(2 * ATT_WIN, BF16)] + [buf(ATT_WIN, F32)] * 5,
        compiler_params=_params(("arbitrary", "arbitrary"), 48))


def _attn_bwd(qn, kn, proj, dcat, lg, dl, bias, d, rider=None):
    sl, nb, hr = _attn_geometry(d)
    slk = QB + sl
    slq = sl + QB
    n_win = T // ATT_WIN

    def body(q_ref, k_ref, v_ref, do_ref, lg_ref, dl_ref, kh_ref, vh_ref, qx_ref, dox_ref, lgx_ref, dlx_ref,
             bias_ref, dq_ref, dk_ref, dv_ref, qs, dos, lgs, dls, ks, vs, dqs, dks, dvs):
        n = pl.program_id(1)
        for r in range(d):
            for dst, src, nx, dt in ((qs, q_ref, qx_ref, BF16), (dos, do_ref, dox_ref, BF16),
                                     (lgs, lg_ref, lgx_ref, F32), (dls, dl_ref, dlx_ref, F32)):
                dst[r * slq:r * slq + sl, :] = _stream(src, r, sl, d).astype(dt)
                dst[r * slq + sl:(r + 1) * slq, :] = _stream(nx, r, QB, d).astype(dt)
            for dst, halo, src in ((ks, kh_ref, k_ref), (vs, vh_ref, v_ref)):
                dst[r * slk:r * slk + QB, :] = _stream(halo, r, QB, d).astype(BF16)
                dst[r * slk + QB:(r + 1) * slk, :] = _stream(src, r, sl, d).astype(BF16)
        dks[...] = jnp.zeros_like(dks)
        dvs[...] = jnp.zeros_like(dvs)

        def unit(rows, kc, vc, biasv, invalid_prev):
            return _attn_bwd_unit(qs, dos, lgs, dls, rows, kc, vc, biasv, invalid_prev)

        for r in range(d):
            for b in range(nb):
                rows = slice(r * slq + b * QB, r * slq + (b + 1) * QB)
                keys = slice(r * slk + b * QB, r * slk + (b + 2) * QB)
                dq, dkc, dvc = unit(rows, ks[keys, :], vs[keys, :], bias_ref[...], (n == 0) if b == 0 else None)
                dqs[r * sl + b * QB:r * sl + (b + 1) * QB, :] = dq
                if b == 0:
                    dks[r * sl:r * sl + QB, :] += dkc[QB:]
                    dvs[r * sl:r * sl + QB, :] += dvc[QB:]
                else:
                    dks[r * sl + (b - 1) * QB:r * sl + (b + 1) * QB, :] += dkc
                    dvs[r * sl + (b - 1) * QB:r * sl + (b + 1) * QB, :] += dvc

        @pl.when(n < n_win - 1)
        def _():
            for r in range(d):
                rows = slice(r * slq + sl, (r + 1) * slq)
                keys = slice(r * slk + sl, (r + 1) * slk)
                _, dkc, dvc = unit(rows, ks[keys, :], vs[keys, :], bias_ref[:, 0:QB], None)
                dks[(r + 1) * sl - QB:(r + 1) * sl, :] += dkc
                dvs[(r + 1) * sl - QB:(r + 1) * sl, :] += dvc

        for dst, src in ((dq_ref, dqs), (dk_ref, dks), (dv_ref, dvs)):
            for r in range(d):
                if d > 1:
                    dst[pl.ds(r, sl, stride=d), :] = src[r * sl:(r + 1) * sl, :]
                else:
                    dst[...] = src[...]

    main, prev, nxt, bias_spec = _attn_specs(d)
    lanes = 2 * HEAD
    return _call(
        body, qn, kn, proj, dcat, lg, dl, kn, proj, qn, dcat, lg, dl, bias, rider=rider, name=f"attn_bwd_d{d}",
        grid=(C // lanes, n_win),
        in_specs=[main(0), main(0), main(V_COL), main(DO_COL), main(0), main(0), prev(0), prev(V_COL),
                  nxt(0), nxt(DO_COL), nxt(0), nxt(0), bias_spec],
        out_specs=[main(0)] * 3,
        out_shape=[jax.ShapeDtypeStruct((T, C), F32)] * 3,
        scratch_shapes=[pltpu.VMEM((ATT_WIN + hr, lanes), BF16), pltpu.VMEM((ATT_WIN + hr, lanes), BF16),
                        pltpu.VMEM((ATT_WIN + hr, lanes), F32), pltpu.VMEM((ATT_WIN + hr, lanes), F32),
                        pltpu.VMEM((ATT_WIN + hr, lanes), BF16), pltpu.VMEM((ATT_WIN + hr, lanes), BF16),
                        pltpu.VMEM((ATT_WIN, lanes), F32), pltpu.VMEM((ATT_WIN, lanes), F32),
                        pltpu.VMEM((ATT_WIN, lanes), F32)],
        compiler_params=_params(("arbitrary", "arbitrary"), 48))


def _qk_norm_bwd(d3, proj, col, gain, bd, dproj, name):
    tm = 512

    def body(d0, d1, d2, x_ref, g_ref, bd_ref, dp_in, dp_ref, gg_ref):
        del dp_in

        @pl.when(pl.program_id(0) == 0)
        def _():
            gg_ref[...] = jnp.zeros_like(gg_ref)
        dn = d0[...] + d1[...] + d2[...]
        xv = x_ref[...]
        r = lax.rsqrt(_segsum(xv * xv, bd_ref[...]) * (1.0 / HEAD) + EPS)
        t = dn * g_ref[...]
        mean_tx = _segsum(t * xv, bd_ref[...]) * (1.0 / HEAD)
        dp_ref[...] = (r * t - xv * (r * r * r) * mean_tx).astype(BF16)
        gg_ref[...] += jnp.sum(dn * xv * r, axis=0, keepdims=True)

    blk = pl.BlockSpec((tm, C), lambda i: (i, 0))
    vec = pl.BlockSpec((1, C), lambda i: (0, 0))
    return _call(
        body, *d3, proj, gain, bd, dproj, name=name, grid=(T // tm,),
        in_specs=[blk, blk, blk, pl.BlockSpec((tm, C), lambda i: (i, col)), vec,
                  pl.BlockSpec((C, C), lambda i: (0, 0)), ANY],
        out_specs=[pl.BlockSpec((tm, C), lambda i: (i, col)), vec],
        out_shape=[jax.ShapeDtypeStruct((T, NPROJ), BF16), jax.ShapeDtypeStruct((1, C), F32)],
        input_output_aliases={6: 0},
        compiler_params=_params(("arbitrary",), 32))


def _v_bwd(d3, dproj):
    tm = 512

    def body(d0, d1, d2, dp_in, dp_ref):
        del dp_in
        dp_ref[...] = (d0[...] + d1[...] + d2[...]).astype(BF16)

    blk = pl.BlockSpec((tm, C), lambda i: (i, 0))
    return _call(
        body, *d3, dproj, name="v_bwd", grid=(T // tm,),
        in_specs=[blk, blk, blk, ANY],
        out_specs=[pl.BlockSpec((tm, C), lambda i: (i, 4))],
        out_shape=[jax.ShapeDtypeStruct((T, NPROJ), BF16)],
        input_output_aliases={3: 0},
        compiler_params=_params(("parallel",), 32))[0]


def _adamw(w, g, m, v):
    m = ADAM_B1 * m + (1.0 - ADAM_B1) * g
    v = ADAM_B2 * v + (1.0 - ADAM_B2) * (g * g)
    m_hat = m / (1.0 - ADAM_B1 ** ADAM_STEP)
    v_hat = v / (1.0 - ADAM_B2 ** ADAM_STEP)
    delta = -ADAM_LR * (m_hat / (jnp.sqrt(v_hat) + ADAM_EPS) + ADAM_WD * w)
    return delta, m, v


def _row_block(shape):
    rows = shape[0]
    for cand in (256, 128, 64, 88, 32, 8):
        if rows % cand == 0 and cand * shape[1] * 4 <= (2 << 20):
            return cand
    return 8


def _partial_sum(own, recv, name):
    br = _row_block(own.shape)
    cols = own.shape[1]

    def body(o_ref, r_ref, p_ref):
        p_ref[...] = ((o_ref[...] + r_ref[0].astype(F32)) + r_ref[1].astype(F32)) + r_ref[2].astype(F32)

    blk = pl.BlockSpec((br, cols), lambda i: (i, 0))
    return _call(
        body, own, recv, name=name, grid=(own.shape[0] // br,),
        in_specs=[blk, pl.BlockSpec((3, br, cols), lambda i: (0, i, 0))], out_specs=[blk],
        out_shape=[jax.ShapeDtypeStruct(own.shape, F32)],
        compiler_params=_params(("parallel",), 32))[0]


def _adamw_mat(p_own, p_sib, w, m, v, name):
    br = _row_block(w.shape)
    cols = w.shape[1]

    def body(a_ref, b_ref, w_ref, m_ref, v_ref, g_ref, d_ref, nm_ref, nv_ref):
        g = a_ref[...] + b_ref[...]
        delta, nm, nv = _adamw(w_ref[...], g, m_ref[...], v_ref[...])
        g_ref[...] = g
        d_ref[...] = delta
        nm_ref[...] = nm
        nv_ref[...] = nv

    blk = pl.BlockSpec((br, cols), lambda i: (i, 0))
    return _call(
        body, p_own, p_sib, w, m, v, name=name, grid=(w.shape[0] // br,),
        in_specs=[blk] * 5, out_specs=[blk] * 4,
        out_shape=[jax.ShapeDtypeStruct(w.shape, F32)] * 4,
        compiler_params=_params(("parallel",), 40))


def _vec_reduce(vrecv):
    def body(v_ref, o_ref):
        acc = v_ref[0]
        for r in range(1, N_DEV):
            acc = acc + v_ref[r]
        o_ref[...] = acc

    return pl.pallas_call(
        body, name="vec_reduce",
        out_shape=jax.ShapeDtypeStruct((VPACK_ROWS, D), F32),
        compiler_params=_params((), 32),
    )(vrecv)


def _adamw_small(w, g, m, v):
    def body(w_ref, g_ref, m_ref, v_ref, d_ref, nm_ref, nv_ref):
        delta, nm, nv = _adamw(w_ref[...], g_ref[...], m_ref[...], v_ref[...])
        d_ref[...] = delta
        nm_ref[...] = nm
        nv_ref[...] = nv

    return pl.pallas_call(
        body, name="adamw_small",
        out_shape=[jax.ShapeDtypeStruct(w.shape, F32)] * 3,
        compiler_params=_params((), 32),
    )(w, g, m, v)


def _pack(parts, rows):
    flat = jnp.concatenate([p.reshape(-1) for p in parts])
    return jnp.pad(flat, (0, rows * D - flat.shape[0])).reshape(rows, D)


def _unpack(packed, shapes):
    flat = packed.reshape(-1)
    out, off = [], 0
    for shp in shapes:
        size = 1
        for s in shp:
            size *= s
        out.append(flat[off:off + size].reshape(shp))
        off += size
    return out


def _no_comm(shards, row_sharded, peers=(0, 1, 2), into=None):
    del row_sharded, peers, into
    return None, lambda res, n: (res, shards)


def _with_comm(shards, row_sharded, peers=(0, 1, 2), into=None):
    rider = _gather_rider(shards, row_sharded, peers, into)
    return rider, lambda res, n: (res[:n], res[n:])


def _local_step(x, target, norm1_g, conv_w, conv_b, cn_g, cn_b, q_norm_g, k_norm_g, norm2_g, ffconv_w, ffconv_b,
                w_in, late_weights, comm=True):
    row = lambda a: a.reshape(1, -1)
    head_of = jnp.arange(C) // HEAD
    bd = (head_of[:, None] == head_of[None, :]).astype(BF16)
    qg = row(jnp.tile(q_norm_g, C // HEAD) * (HEAD ** -0.5))
    kg = row(jnp.tile(k_norm_g, C // HEAD))
    biases = [_alibi_tables(d) for d in PATTERN_DILATIONS]
    gather = _with_comm if comm else _no_comm
    grad_rider = (lambda g, rs: _grad_rider(g[1], g[0], rs)) if comm else (lambda g, rs: None)

    rider, split = gather(late_weights[0:1], (True,))
    (h, proj), (w_out,) = split(_proj_fwd(x, row(norm1_g), w_in, rider), 2)
    rider, split = gather(late_weights[1:2], (False,), (0, 1))
    (cat, cv), w_up_part = split(_conv_fwd(proj, conv_w, row(conv_b), row(cn_g), row(cn_b), rider), 2)
    qn, kn = _qkv_prep(proj, qg, kg, bd)
    fwd = [_attn_fwd(qn, kn, proj, biases[i], d) for i, d in enumerate(PATTERN_DILATIONS[:-1])]
    rider, split = gather(late_weights[1:2], (False,), (2,), w_up_part)
    merge = (fwd[0][0], fwd[0][1], fwd[1][0], fwd[1][1], cat)
    (cat, o_f32, lg), (w_up,) = split(
        _attn_fwd(qn, kn, proj, biases[-1], PATTERN_DILATIONS[-1], rider, merge), 3)
    rider, split = gather(late_weights[2:3], (True,))
    (x1, h2, up), (w_down,) = split(_out_up(x, cat, w_out, row(norm2_g), w_up, rider), 3)
    act, dy, loss_acc, dup, gff = _ffn(up, ffconv_w, row(ffconv_b), w_down, x1, target)
    gw_down = _weight_grad(act, dy, DFF // 2, D, 1024, "grad_w_down")
    res = _norm_bwd_mm(dup, w_up, x1, dy, row(norm2_g), "up_bwd", grad_rider(gw_down, True))
    (dx1, g_norm2), ex_down = res[:2], res[2:]
    gw_up = _weight_grad(h2, dup, D, NUP // 4, 2048, "grad_w_up")
    dcat = _outproj_bwd(dx1, w_out)
    gw_out = _weight_grad(cat, dx1, D, D, 2048, "grad_w_out")
    res = _conv_bwd(dcat, cv, proj, conv_w, row(cn_g), row(cn_b), grad_rider(gw_up, False))
    (dproj, gconv_vec, gconv_w), ex_up = res[:3], res[3:]
    dl = _attn_bwd_prep(dcat, o_f32, bd)
    bwd, ex_out = [], []
    for i, d in enumerate(PATTERN_DILATIONS):
        if QB * d == ATT_WIN:
            res = _attn_bwd_lagged(qn, kn, proj, dcat, lg, dl, biases[i], d)
        else:
            res = _attn_bwd(qn, kn, proj, dcat, lg, dl, biases[i], d, grad_rider(gw_out, True) if i == 0 else None)
        bwd.append(res[:3])
        ex_out = res[3:] if i == 0 else ex_out
    dproj, gq_lane = _qk_norm_bwd([b[0] for b in bwd], proj, 2, qg, bd, dproj, "q_norm_bwd")
    dproj, gk_lane = _qk_norm_bwd([b[1] for b in bwd], proj, 3, kg, bd, dproj, "k_norm_bwd")
    dproj = _v_bwd([b[2] for b in bwd], dproj)
    gw_in = _weight_grad(h, dproj, D, NPROJ // 4, 2048, "grad_w_in")
    res = _norm_bwd_mm(dproj, w_in, x, dx1, row(norm1_g), "in_bwd", grad_rider(gw_in, False))
    (dx, g_norm1), ex_in = res[:2], res[2:]

    loss = loss_acc[0, 0] * (0.5 / D)
    g_qg = jnp.sum(gq_lane.reshape(C // HEAD, HEAD), axis=0) * (HEAD ** -0.5)
    g_kg = jnp.sum(gk_lane.reshape(C // HEAD, HEAD), axis=0)
    small = [g_norm1[0], gconv_vec[2], gconv_vec[0], gconv_vec[1], g_qg, g_kg, g_norm2[0], gff[3],
             gconv_w[:CONV_K], gff[:FF_K]]
    mats = [ex_in, ex_out, ex_up, ex_down] if comm else [gw_in, gw_out, gw_up, gw_down]
    return loss, dx, mats, small


def kernel(x, norm1_g, w_in, conv_w, conv_b, cn_g, cn_b, q_norm_g, k_norm_g, w_out, norm2_g, w_up, ffconv_w, ffconv_b, w_down, loss_target, m_norm1_g, m_w_in, m_conv_w, m_conv_b, m_cn_g, m_cn_b, m_q_norm_g, m_k_norm_g, m_w_out, m_norm2_g, m_w_up, m_ffconv_w, m_ffconv_b, m_w_down, v_norm1_g, v_w_in, v_conv_w, v_conv_b, v_cn_g, v_cn_b, v_q_norm_g, v_k_norm_g, v_w_out, v_norm2_g, v_w_up, v_ffconv_w, v_ffconv_b, v_w_down):
    chip = 2 * lax.axis_index("x") + lax.axis_index("y")

    w_in_full, conv_w_full, ffconv_w_full = _gather_now([w_in.astype(BF16), conv_w, ffconv_w], (False, False, False))
    loss, dx, mats, small = _local_step(
        x[0], loss_target[0], norm1_g, conv_w_full, conv_b, cn_g, cn_b, q_norm_g, k_norm_g, norm2_g,
        ffconv_w_full, ffconv_b, w_in_full, [w.astype(BF16) for w in (w_out, w_up, w_down)])

    names = ("w_in", "w_out", "w_up", "w_down")
    parts = [_partial_sum(own, recv, "partial_" + names[k]) for k, (recv, own) in enumerate(mats)]
    sib, vrecv = _final_exchange(parts, _pack(small + [loss.reshape(1)], VPACK_ROWS))
    ws = (w_in, w_out, w_up, w_down)
    ms = (m_w_in, m_w_out, m_w_up, m_w_down)
    vs = (v_w_in, v_w_out, v_w_up, v_w_down)
    mat = [_adamw_mat(parts[k], sib[k], ws[k], ms[k], vs[k], "adamw_" + names[k]) for k in range(4)]

    vsum = _vec_reduce(vrecv)
    vec_shapes = [(D,), (C,), (C,), (C,), (HEAD,), (HEAD,), (D,), (NUP,), (CONV_K, C), (FF_K, NUP), (1,)]
    gsmall = _unpack(vsum, vec_shapes)
    g_conv_w = lax.dynamic_slice_in_dim(gsmall[8], chip * (C // N_CHIPS), C // N_CHIPS, axis=1)
    g_ffconv_w = lax.dynamic_slice_in_dim(gsmall[9], chip * (NUP // N_CHIPS), NUP // N_CHIPS, axis=1)
    gs = gsmall[:8] + [g_conv_w, g_ffconv_w]
    w_s = [norm1_g, conv_b, cn_g, cn_b, q_norm_g, k_norm_g, norm2_g, ffconv_b, conv_w, ffconv_w]
    m_s = [m_norm1_g, m_conv_b, m_cn_g, m_cn_b, m_q_norm_g, m_k_norm_g, m_norm2_g, m_ffconv_b, m_conv_w, m_ffconv_w]
    v_s = [v_norm1_g, v_conv_b, v_cn_g, v_cn_b, v_q_norm_g, v_k_norm_g, v_norm2_g, v_ffconv_b, v_conv_w, v_ffconv_w]
    shapes_s = [a.shape for a in w_s]
    d_p, m_p, v_p = _adamw_small(_pack(w_s, SPACK_ROWS), _pack(gs, SPACK_ROWS), _pack(m_s, SPACK_ROWS),
                                 _pack(v_s, SPACK_ROWS))
    d_s, nm_s, nv_s = _unpack(d_p, shapes_s), _unpack(m_p, shapes_s), _unpack(v_p, shapes_s)

    def ordered(sm, mt):
        return [sm[0], mt[0], sm[8], sm[1], sm[2], sm[3], sm[4], sm[5], mt[1], sm[6], mt[2], sm[9], sm[7], mt[3]]

    loss_all = gsmall[10][0]
    grads = ordered(gs, [r[0] for r in mat])
    deltas = ordered(d_s, [r[1] for r in mat])
    new_m = ordered(nm_s, [r[2] for r in mat])
    new_v = ordered(nv_s, [r[3] for r in mat])
    return (loss_all, dx[None], *grads, *deltas, *new_m, *new_v)
```

```python
import types

import jax
import jax.numpy as jnp
from jax import lax
from jax.experimental import pallas as pl
from jax.experimental.pallas import tpu as pltpu

T = 8192
D = 1024
C = 512
NPROJ = 2560
DFF = 2816
NUP = 2 * DFF
CONV_K = 31
FF_K = 3
HEAD = 64
EPS = 1e-6
NEG = -1e30
N_CHIPS = 4
N_DEV = 8
PATTERN_DILATIONS = (1, 4, 16)
QB = 128

ADAM_LR = 0.001
ADAM_B1 = 0.9
ADAM_B2 = 0.999
ADAM_EPS = 1e-08
ADAM_WD = 0.01
ADAM_STEP = 10

F32 = jnp.float32
BF16 = jnp.bfloat16
MESH = pl.DeviceIdType.MESH
ANY = pl.BlockSpec(memory_space=pl.ANY)

VPACK_ROWS = 48
SPACK_ROWS = 24


def _params(sem, vmem_mb):
    return pltpu.CompilerParams(dimension_semantics=sem, vmem_limit_bytes=vmem_mb << 20)


def _resident(shape):
    return pl.BlockSpec(shape, lambda i: (0, 0), pipeline_mode=pl.Buffered(1))


def _nt(a, b):
    return lax.dot_general(a, b, (((1,), (1,)), ((), ())), preferred_element_type=F32)


def _tn_dot(a, b):
    return lax.dot_general(a, b, (((0,), (0,)), ((), ())), preferred_element_type=F32)


def _sigmoid(x):
    return 1.0 / (1.0 + jnp.exp(-x))


def _segsum(x, bd):
    hi = x.astype(BF16)
    lo = (x - hi.astype(F32)).astype(BF16)
    return (jnp.dot(hi, bd, preferred_element_type=F32)
            + jnp.dot(lo, bd, preferred_element_type=F32))


def _place():
    x, y, c = lax.axis_index("x"), lax.axis_index("y"), lax.axis_index("c")
    chips = [(1 - x, y), (x, 1 - y), (1 - x, 1 - y)]
    return x, y, c, chips


def _block_of(ref, shard_shape, row_sharded, s):
    r, cdim = shard_shape
    if row_sharded:
        return ref.at[pl.ds(s * r, r), :]
    return ref.at[:, pl.ds(s * cdim, cdim)]


def _full_shape(shard_shape, row_sharded):
    r, cdim = shard_shape
    return (r * N_CHIPS, cdim) if row_sharded else (r, cdim * N_CHIPS)


def _gather_rider(shards, row_sharded, peers=(0, 1, 2), into=None):
    n = len(shards)
    shapes = [a.shape for a in shards]

    def copies(ins, outs, sems):
        send_sems, recv_sems, local_sems = sems
        x, y, c, chips = _place()
        me = 2 * x + y
        place = lambda k, s: _block_of(outs[k], shapes[k], row_sharded[k], s)
        local = []
        if into is None:
            local = [pltpu.make_async_copy(ins[k], place(k, me), local_sems.at[k]) for k in range(n)]
        sends, recvs = [], []
        for k in range(n):
            for j in peers:
                px, py = chips[j]
                sem = dict(send_sem=send_sems.at[3 * k + j], recv_sem=recv_sems.at[3 * k + j],
                           device_id=(px, py, c), device_id_type=MESH)
                sends.append(pltpu.make_async_remote_copy(src_ref=ins[k], dst_ref=place(k, me), **sem))
                recvs.append(pltpu.make_async_remote_copy(src_ref=ins[k], dst_ref=place(k, 2 * px + py), **sem))
        return local, sends, recvs

    return types.SimpleNamespace(
        operands=list(shards) + list(into or []), copies=copies,
        aliases={n + k: k for k in range(n)} if into is not None else {},
        out_shape=[jax.ShapeDtypeStruct(_full_shape(s, rs), a.dtype) for s, rs, a in zip(shapes, row_sharded, shards)],
        sems=[pltpu.SemaphoreType.DMA((3 * n,)), pltpu.SemaphoreType.DMA((3 * n,)), pltpu.SemaphoreType.DMA((n,))])


def _grad_rider(g_bf16, g_f32, row_sharded):
    shard = tuple(d // N_CHIPS if (i == 0) == row_sharded else d for i, d in enumerate(g_f32.shape))

    def copies(ins, outs, sems):
        send_sems, recv_sems, local_sems = sems
        gb, gf = ins
        rec, own = outs
        x, y, c, chips = _place()
        me = 2 * x + y
        local = [pltpu.make_async_copy(_block_of(gf, shard, row_sharded, me), own, local_sems.at[0])]
        sends, recvs = [], []
        for j, (px, py) in enumerate(chips):
            sem = dict(send_sem=send_sems.at[j], recv_sem=recv_sems.at[j], device_id=(px, py, c), device_id_type=MESH)
            sends.append(pltpu.make_async_remote_copy(
                src_ref=_block_of(gb, shard, row_sharded, 2 * px + py), dst_ref=rec.at[j], **sem))
            recvs.append(pltpu.make_async_remote_copy(
                src_ref=_block_of(gb, shard, row_sharded, me), dst_ref=rec.at[j], **sem))
        return local, sends, recvs

    return types.SimpleNamespace(
        operands=[g_bf16, g_f32], copies=copies, aliases={},
        out_shape=[jax.ShapeDtypeStruct((3,) + shard, BF16), jax.ShapeDtypeStruct(shard, F32)],
        sems=[pltpu.SemaphoreType.DMA((3,)), pltpu.SemaphoreType.DMA((3,)), pltpu.SemaphoreType.DMA((1,))])


def _rider_start(rider, ins, outs, sems):
    local, sends, _ = rider.copies(ins, outs, sems)
    for cp in local + sends:
        cp.start()


def _rider_wait(rider, ins, outs, sems):
    local, sends, recvs = rider.copies(ins, outs, sems)
    for cp in recvs:
        cp.wait_recv()
    for cp in sends:
        cp.wait_send()
    for cp in local:
        cp.wait()


def _call(body, *operands, rider=None, name, grid, in_specs, out_specs, out_shape, scratch_shapes=(),
          compiler_params, input_output_aliases=None):
    operands = [pltpu.with_memory_space_constraint(a, pltpu.HBM) for a in operands]
    if rider is None:
        return pl.pallas_call(
            body, name=name, grid=grid, in_specs=list(in_specs), out_specs=list(out_specs), out_shape=list(out_shape),
            scratch_shapes=list(scratch_shapes), compiler_params=compiler_params,
            input_output_aliases=input_output_aliases or {})(*operands)
    n_in, n_out, n_scr = len(in_specs), len(out_specs), len(scratch_shapes)
    r_in, r_out = len(rider.operands), len(rider.out_shape)

    def riding(*refs):
        refs = list(refs)
        ins, refs = refs[:n_in], refs[n_in:]
        r_ins, refs = refs[:r_in], refs[r_in:]
        outs, refs = refs[:n_out], refs[n_out:]
        r_outs, refs = refs[:r_out], refs[r_out:]
        scr, sems = refs[:n_scr], refs[n_scr:]
        first = pl.program_id(0) == 0
        last = pl.program_id(0) == grid[0] - 1
        for axis in range(1, len(grid)):
            first = first & (pl.program_id(axis) == 0)
            last = last & (pl.program_id(axis) == grid[axis] - 1)

        @pl.when(first)
        def _():
            _rider_start(rider, r_ins, r_outs, sems)

        body(*ins, *outs, *scr)

        @pl.when(last)
        def _():
            _rider_wait(rider, r_ins, r_outs, sems)

    return pl.pallas_call(
        riding, name=name, grid=grid, in_specs=list(in_specs) + [ANY] * r_in,
        out_specs=list(out_specs) + [ANY] * r_out, out_shape=list(out_shape) + list(rider.out_shape),
        scratch_shapes=list(scratch_shapes) + list(rider.sems), compiler_params=compiler_params,
        input_output_aliases={**(input_output_aliases or {}),
                              **{n_in + i: n_out + o for i, o in rider.aliases.items()}})(
            *operands, *[pltpu.with_memory_space_constraint(a, pltpu.HBM) for a in rider.operands])


def _final_exchange(parts, vpack):
    def body(p0, p1, p2, p3, v_ref, o0, o1, o2, o3, vr_ref, send_sems, recv_sems, vsend_sems, vrecv_sems, local_sem):
        x, y, c, _ = _place()
        me = 4 * x + 2 * y + c
        mine = pltpu.make_async_copy(v_ref, vr_ref.at[me], local_sem)
        mine.start()
        copies = [pltpu.make_async_remote_copy(
            src_ref=p, dst_ref=o, send_sem=send_sems.at[k], recv_sem=recv_sems.at[k],
            device_id=(x, y, 1 - c), device_id_type=MESH)
            for k, (p, o) in enumerate(zip((p0, p1, p2, p3), (o0, o1, o2, o3)))]
        flips = [(fx, fy, fc) for fx in (0, 1) for fy in (0, 1) for fc in (0, 1)][1:]
        recvs = []
        for r, (fx, fy, fc) in enumerate(flips):
            peer = (x ^ fx, y ^ fy, c ^ fc)
            sem = dict(send_sem=vsend_sems.at[r], recv_sem=vrecv_sems.at[r], device_id=peer, device_id_type=MESH)
            copies.append(pltpu.make_async_remote_copy(src_ref=v_ref, dst_ref=vr_ref.at[me], **sem))
            recvs.append(pltpu.make_async_remote_copy(
                src_ref=v_ref, dst_ref=vr_ref.at[4 * peer[0] + 2 * peer[1] + peer[2]], **sem))
        for cp in copies:
            cp.start()
        for cp in copies[:4]:
            cp.wait_recv()
        for cp in recvs:
            cp.wait_recv()
        for cp in copies:
            cp.wait_send()
        mine.wait()

    res = pl.pallas_call(
        body, name="final_exchange",
        out_shape=[jax.ShapeDtypeStruct(p.shape, F32) for p in parts]
        + [jax.ShapeDtypeStruct((N_DEV, VPACK_ROWS, D), F32)],
        in_specs=[ANY] * 5, out_specs=[ANY] * 5,
        scratch_shapes=[pltpu.SemaphoreType.DMA((4,)), pltpu.SemaphoreType.DMA((4,)),
                        pltpu.SemaphoreType.DMA((7,)), pltpu.SemaphoreType.DMA((7,)), pltpu.SemaphoreType.DMA],
    )(*parts, vpack)
    return res[:4], res[4]


def _norm_fwd(x, g1, rider):
    tm = 512

    def body(x_ref, g_ref, h_ref):
        xv = x_ref[...]
        r = lax.rsqrt(jnp.mean(xv * xv, axis=-1, keepdims=True) + EPS)
        h_ref[...] = (xv * r * g_ref[...]).astype(BF16)

    row = pl.BlockSpec((tm, D), lambda i: (i, 0))
    return _call(
        body, x, g1, rider=rider, name="norm_fwd", grid=(T // tm,),
        in_specs=[row, pl.BlockSpec((1, D), lambda i: (0, 0))], out_specs=[row],
        out_shape=[jax.ShapeDtypeStruct((T, D), BF16)],
        compiler_params=_params(("arbitrary",), 32))


def _proj_fwd(h, w_in, qg, kg, bd, rider):
    tm, tn = 512, 640

    def body(h_ref, w_ref, qg_ref, kg_ref, bd_ref, p_ref, qn_ref, kn_ref):
        for j in range(NPROJ // tn):
            cols = slice(j * tn, (j + 1) * tn)
            p_ref[:, cols] = jnp.dot(h_ref[...], w_ref[:, cols], preferred_element_type=F32)
        for col, g, dst in ((2, qg_ref, qn_ref), (3, kg_ref, kn_ref)):
            xv = p_ref[:, col * C:(col + 1) * C]
            ms = _segsum(xv * xv, bd_ref[...]) * (1.0 / HEAD)
            dst[...] = xv * lax.rsqrt(ms + EPS) * g[...]

    vec = pl.BlockSpec((1, C), lambda i: (0, 0))
    blk = pl.BlockSpec((tm, C), lambda i: (i, 0))
    return _call(
        body, h, w_in, qg, kg, bd, rider=rider, name="proj_fwd", grid=(T // tm,),
        in_specs=[pl.BlockSpec((tm, D), lambda i: (i, 0)), _resident((D, NPROJ)), vec, vec, _resident((C, C))],
        out_specs=[pl.BlockSpec((tm, NPROJ), lambda i: (i, 0)), blk, blk],
        out_shape=[jax.ShapeDtypeStruct((T, NPROJ), F32), jax.ShapeDtypeStruct((T, C), F32),
                   jax.ShapeDtypeStruct((T, C), F32)],
        compiler_params=_params(("arbitrary",), 40))


CONV_TM = 512
CONV_HALO = 32
CONV_RB = 32
CONV_CB = 64


LANES = 128


def _sp(start, n):
    return (pl.ds(2 * start, n, stride=2), slice(None))


def _lanes(tile):
    return slice(tile * LANES, (tile + 1) * LANES)


def _conv_fwd(proj, conv_w, conv_b, cn_g, cn_b, rider):
    tm, hl, rb, cb = CONV_TM, CONV_HALO, CONV_RB, CONV_CB
    per = tm // hl

    def body(av_ref, ag_ref, hv_ref, hg_ref, w_ref, b_ref, g_ref, bb_ref, cat_ref, cv_ref, sh_ref):
        i = pl.program_id(0)
        for j in range(C // LANES):
            ln_ = _lanes(j)
            glu_h = hv_ref[:, ln_] * _sigmoid(hg_ref[:, ln_])
            sh_ref.at[j][_sp(0, hl)] = jnp.where(i > 0, glu_h, 0.0)
            for r0 in range(0, tm, cb):
                sh_ref.at[j][_sp(hl + r0, cb)] = av_ref[r0:r0 + cb, ln_] * _sigmoid(ag_ref[r0:r0 + cb, ln_])
            for r0 in range(0, tm, cb):
                acc = jnp.zeros((cb, LANES), F32) + b_ref[:, ln_]
                for k in range(CONV_K):
                    acc = acc + w_ref[k:k + 1, ln_] * sh_ref.at[j][_sp(r0 + hl - (CONV_K - 1) + k, cb)]
                cv_ref[r0:r0 + cb, ln_] = acc
        for r0 in range(0, tm, rb):
            acc = cv_ref[r0:r0 + rb, :]
            mu = jnp.mean(acc, axis=-1, keepdims=True)
            xc = acc - mu
            var = jnp.mean(xc * xc, axis=-1, keepdims=True)
            ln = xc * lax.rsqrt(var + EPS) * g_ref[...] + bb_ref[...]
            cat_ref[r0:r0 + rb, :] = (ln * _sigmoid(ln)).astype(BF16)

    halo = lambda col: pl.BlockSpec((hl, C), lambda i: (jnp.maximum(i * per - 1, 0), col))
    vec = pl.BlockSpec((1, C), lambda i: (0, 0))
    return _call(
        body, proj, proj, proj, proj, conv_w, conv_b, cn_g, cn_b, rider=rider, name="conv_fwd", grid=(T // tm,),
        in_specs=[pl.BlockSpec((tm, C), lambda i: (i, 0)), pl.BlockSpec((tm, C), lambda i: (i, 1)),
                  halo(0), halo(1), pl.BlockSpec((CONV_K, C), lambda i: (0, 0)), vec, vec, vec],
        out_specs=[pl.BlockSpec((tm, C), lambda i: (i, 0)), pl.BlockSpec((tm, C), lambda i: (i, 0))],
        out_shape=[jax.ShapeDtypeStruct((T, D), BF16), jax.ShapeDtypeStruct((T, C), F32)],
        scratch_shapes=[pltpu.VMEM((C // LANES, 2 * (tm + hl), LANES), F32)],
        compiler_params=_params(("arbitrary",), 40))


def _stack_heads(a):
    lane = lax.broadcasted_iota(jnp.int32, a.shape, 1)
    zero = jnp.zeros_like(a)
    return jnp.concatenate([jnp.where(lane < HEAD, a, zero), jnp.where(lane >= HEAD, a, zero)], axis=0)


def _unstack_heads(a2):
    lane = lax.broadcasted_iota(jnp.int32, (QB, 2 * HEAD), 1)
    return jnp.where(lane < HEAD, a2[:QB], a2[QB:])


def _stack_cols(a):
    return jnp.concatenate([a[:, 0:1], a[:, HEAD:HEAD + 1]], axis=0)


ATT_WIN = 2048
V_COL = 4 * C // (2 * HEAD)
DO_COL = C // (2 * HEAD)


def _attn_geometry(d):
    sl = ATT_WIN // d
    return sl, sl // QB, QB * d


def _stream(ref, r, n, d):
    return ref[pl.ds(r, n, stride=d), :] if d > 1 else ref[pl.ds(r, n), :]


def _alibi_tables(d):
    qi = jnp.arange(QB)[:, None]
    kj = jnp.arange(2 * QB)[None, :]
    delta = qi + QB - kj
    band = (delta >= 0) & (delta <= QB)
    dist = (delta * d).astype(F32)
    heads = jnp.arange(8, dtype=F32)
    slopes = 2.0 ** (-(heads + 1.0))
    t = jnp.where(band[None], -slopes[:, None, None] * dist[None], NEG)
    return t.reshape(4, 2 * QB, 2 * QB)


def _attn_specs(d):
    _, _, hr = _attn_geometry(d)
    per = ATT_WIN // hr
    main = lambda off: pl.BlockSpec((ATT_WIN, 2 * HEAD), lambda cb, n: (n, off + cb))
    prev = lambda off: pl.BlockSpec((hr, 2 * HEAD), lambda cb, n: (jnp.maximum(n * per - 1, 0), off + cb))
    nxt = lambda off: pl.BlockSpec((hr, 2 * HEAD), lambda cb, n: (jnp.minimum((n + 1) * per, T // hr - 1), off + cb))
    bias = pl.BlockSpec((None, 2 * QB, 2 * QB), lambda cb, n: (cb, 0, 0))
    return main, prev, nxt, bias


def _attn_fwd(qn, kn, proj, bias, d, rider=None, merge=None):
    sl, nb, hr = _attn_geometry(d)
    slk = QB + sl
    mrows = 256

    def body(q_ref, k_ref, v_ref, kh_ref, vh_ref, bias_ref, *rest):
        if merge is None:
            o_ref, l_ref, qs, ks, vs, os_, ls = rest
        else:
            oa_ref, la_ref, ob_ref, lb_ref, _, cat_ref, of_ref, lg_ref, qs, ks, vs, os_, ls, o_ref, l_ref = rest
        n = pl.program_id(1)
        for r in range(d):
            qs[r * sl:(r + 1) * sl, :] = _stream(q_ref, r, sl, d).astype(BF16)
            for dst, halo, src in ((ks, kh_ref, k_ref), (vs, vh_ref, v_ref)):
                dst[r * slk:r * slk + QB, :] = _stream(halo, r, QB, d).astype(BF16)
                dst[r * slk + QB:(r + 1) * slk, :] = _stream(src, r, sl, d).astype(BF16)
        col = lax.broadcasted_iota(jnp.int32, (2 * QB, 2 * QB), 1)
        for r in range(d):
            for b in range(nb):
                rows = slice(r * sl + b * QB, r * sl + (b + 1) * QB)
                keys = slice(r * slk + b * QB, r * slk + (b + 2) * QB)
                s = _nt(_stack_heads(qs[rows, :]), ks[keys, :]) + bias_ref[...]
                if b == 0:
                    s = jnp.where((col < QB) & (n == 0), NEG, s)
                m = jnp.max(s, axis=-1, keepdims=True)
                p = jnp.exp(s - m)
                den = jnp.sum(p, axis=-1, keepdims=True)
                pv = jnp.dot(p.astype(BF16), vs[keys, :], preferred_element_type=F32)
                os_[rows, :] = _unstack_heads(pv / den)
                ls[rows, :] = _unstack_heads(jnp.broadcast_to(m + jnp.log(den), (2 * QB, 2 * HEAD)))
        for r in range(d):
            if d > 1:
                o_ref[pl.ds(r, sl, stride=d), :] = os_[r * sl:(r + 1) * sl, :]
                l_ref[pl.ds(r, sl, stride=d), :] = ls[r * sl:(r + 1) * sl, :]
            else:
                o_ref[...] = os_[...]
                l_ref[...] = ls[...]
        if merge is not None:
            for r0 in range(0, ATT_WIN, mrows):
                rows = slice(r0, r0 + mrows)
                a, b, c = la_ref[rows, :], lb_ref[rows, :], l_ref[rows, :]
                m = jnp.maximum(jnp.maximum(a, b), c)
                e0, e1, e2 = jnp.exp(a - m), jnp.exp(b - m), jnp.exp(c - m)
                den = e0 + e1 + e2
                o = (e0 * oa_ref[rows, :] + e1 * ob_ref[rows, :] + e2 * o_ref[rows, :]) / den
                of_ref[rows, :] = o
                cat_ref[rows, :] = o.astype(BF16)
                lg_ref[rows, :] = m + jnp.log(den)

    main, prev, _, bias_spec = _attn_specs(d)
    lanes = 2 * HEAD
    operands = [qn, kn, proj, kn, proj, bias]
    in_specs = [main(0), main(0), main(V_COL), prev(0), prev(V_COL), bias_spec]
    scratch = [pltpu.VMEM((ATT_WIN, lanes), BF16), pltpu.VMEM((ATT_WIN + hr, lanes), BF16),
               pltpu.VMEM((ATT_WIN + hr, lanes), BF16), pltpu.VMEM((ATT_WIN, lanes), F32),
               pltpu.VMEM((ATT_WIN, lanes), F32)]
    if merge is None:
        out_specs = [main(0), main(0)]
        out_shape = [jax.ShapeDtypeStruct((T, C), F32)] * 2
        aliases = None
    else:
        operands += list(merge)
        in_specs += [main(0)] * 4 + [ANY]
        out_specs = [main(C // lanes), main(0), main(0)]
        out_shape = [jax.ShapeDtypeStruct((T, D), BF16), jax.ShapeDtypeStruct((T, C), F32),
                     jax.ShapeDtypeStruct((T, C), F32)]
        scratch += [pltpu.VMEM((ATT_WIN, lanes), F32)] * 2
        aliases = {len(operands) - 1: 0}
    return _call(
        body, *operands, rider=rider, name=f"attn_fwd_d{d}", grid=(C // lanes, T // ATT_WIN),
        in_specs=in_specs, out_specs=out_specs, out_shape=out_shape, scratch_shapes=scratch,
        input_output_aliases=aliases, compiler_params=_params(("arbitrary", "arbitrary"), 48))


def _out_up(x, cat, w_out, g2, w_up, rider):
    tm, tn = 512, NUP // 4

    def body(x_ref, cat_ref, wo_ref, g_ref, wu_ref, x1_ref, h2_ref, up_ref):
        x1 = x_ref[...] + jnp.dot(cat_ref[...], wo_ref[...], preferred_element_type=F32)
        x1_ref[...] = x1
        r = lax.rsqrt(jnp.mean(x1 * x1, axis=-1, keepdims=True) + EPS)
        h2_ref[...] = (x1 * r * g_ref[...]).astype(BF16)
        for j in range(NUP // tn):
            cols = slice(j * tn, (j + 1) * tn)
            up_ref[:, cols] = jnp.dot(h2_ref[...], wu_ref[:, cols], preferred_element_type=F32)

    row = pl.BlockSpec((tm, D), lambda i: (i, 0))
    return _call(
        body, x, cat, w_out, g2, w_up, rider=rider, name="out_up", grid=(T // tm,),
        in_specs=[row, row, _resident((D, D)), pl.BlockSpec((1, D), lambda i: (0, 0)), _resident((D, NUP))],
        out_specs=[row, row, pl.BlockSpec((tm, NUP), lambda i: (i, 0))],
        out_shape=[jax.ShapeDtypeStruct((T, D), F32), jax.ShapeDtypeStruct((T, D), BF16),
                   jax.ShapeDtypeStruct((T, NUP), F32)],
        compiler_params=_params(("arbitrary",), 58))


FF_TM = 256
FF_HALO = 8
FF_RB = 64
FF_TILES = DFF // LANES


def _ff_conv(ext_ref, fw_ref, fb_ref, tile, r0):
    cols = _lanes(tile)
    base = FF_HALO + r0
    acc = fb_ref[:, cols] + fw_ref[0:1, cols] * ext_ref.at[tile][_sp(base - 2, FF_RB)]
    acc = acc + fw_ref[1:2, cols] * ext_ref.at[tile][_sp(base - 1, FF_RB)]
    return acc + fw_ref[2:3, cols] * ext_ref.at[tile][_sp(base, FF_RB)]


def _ffn(up, ffconv_w, ffconv_b, w_down, x1, target):
    tm, hl = FF_TM, FF_HALO
    per = tm // hl
    nt = T // tm
    tiles = 2 * FF_TILES

    def body(up_ref, uh_ref, fw_ref, fb_ref, wd_ref, x1_ref, tg_ref, act_ref, dy_ref, loss_ref, dup_ref, gff_ref,
             ext_ref, gv_ref, dact_ref, carry_ref):
        i = pl.program_id(0)

        @pl.when(i == 0)
        def _():
            gff_ref[...] = jnp.zeros_like(gff_ref)
            loss_ref[...] = jnp.zeros_like(loss_ref)
            carry_ref[...] = jnp.zeros_like(carry_ref)

        for j in range(tiles):
            ext_ref.at[j][_sp(0, hl)] = jnp.where(i < nt - 1, uh_ref[:, _lanes(j)], 0.0)
            for r0 in range(0, tm, FF_RB):
                ext_ref.at[j][_sp(hl + r0, FF_RB)] = up_ref[r0:r0 + FF_RB, _lanes(j)]
        for c in range(FF_TILES):
            for r0 in range(0, tm, FF_RB):
                rows = slice(r0, r0 + FF_RB)
                gate = _ff_conv(ext_ref, fw_ref, fb_ref, c, r0)
                val = _ff_conv(ext_ref, fw_ref, fb_ref, FF_TILES + c, r0)
                gv_ref[rows, _lanes(c)] = gate
                gv_ref[rows, _lanes(FF_TILES + c)] = val
                act_ref[rows, _lanes(c)] = (gate * _sigmoid(gate) * val).astype(BF16)
        err = x1_ref[...] + jnp.dot(act_ref[...], wd_ref[...], preferred_element_type=F32) - tg_ref[...]
        dy_ref[...] = err * (1.0 / D)
        loss_ref[...] += jnp.sum(err * err)
        dact_ref[...] = _nt(dy_ref[...].astype(BF16), wd_ref[...])

        for c in range(FF_TILES):
            for r0 in range(0, tm, FF_RB):
                rows = slice(r0, r0 + FF_RB)
                gate, val = gv_ref[rows, _lanes(c)], gv_ref[rows, _lanes(FF_TILES + c)]
                sg = _sigmoid(gate)
                da = dact_ref[rows, _lanes(c)]
                ext_ref.at[c][_sp(r0, FF_RB)] = da * val * (sg + gate * sg * (1.0 - sg))
                ext_ref.at[FF_TILES + c][_sp(r0, FF_RB)] = da * gate * sg
        fold = lambda a: jnp.sum(a.reshape(FF_RB // 8, 8, LANES), axis=0)
        for c in range(tiles):
            cols = _lanes(c)
            ext_ref.at[c][_sp(tm, hl)] = carry_ref[c]
            taps = [fw_ref[k:k + 1, cols] for k in range(FF_K)]
            acc = [jnp.zeros((8, LANES), F32) for _ in range(FF_K + 1)]
            for r0 in range(0, tm, FF_RB):
                shifted = [ext_ref.at[c][_sp(r0 + k, FF_RB)] for k in range(FF_K)]
                u = up_ref[r0:r0 + FF_RB, cols]
                dup = taps[2] * shifted[0] + taps[1] * shifted[1] + taps[0] * shifted[2]
                dup_ref[r0:r0 + FF_RB, cols] = dup.astype(BF16)
                for k in range(FF_K):
                    acc[2 - k] = acc[2 - k] + fold(shifted[k] * u)
                acc[FF_K] = acc[FF_K] + fold(shifted[0])
            for k in range(FF_K + 1):
                gff_ref[k:k + 1, cols] += jnp.sum(acc[k], axis=0, keepdims=True)
            carry_ref[c] = ext_ref.at[c][_sp(0, hl)]

    rev = lambda i: (nt - 1 - i, 0)
    row = pl.BlockSpec((tm, D), rev)
    wide = pl.BlockSpec((tm, NUP), rev)
    return _call(
        body, up, up, ffconv_w, ffconv_b, w_down, x1, target, name="ffn", grid=(nt,),
        in_specs=[wide, pl.BlockSpec((hl, NUP), lambda i: (jnp.maximum((nt - 1 - i) * per - 1, 0), 0)),
                  pl.BlockSpec((FF_K, NUP), lambda i: (0, 0)), pl.BlockSpec((1, NUP), lambda i: (0, 0)),
                  _resident((DFF, D)), row, row],
        out_specs=[pl.BlockSpec((tm, DFF), rev), row, pl.BlockSpec((8, 128), lambda i: (0, 0)), wide,
                   pl.BlockSpec((8, NUP), lambda i: (0, 0))],
        out_shape=[jax.ShapeDtypeStruct((T, DFF), BF16), jax.ShapeDtypeStruct((T, D), F32),
                   jax.ShapeDtypeStruct((8, 128), F32), jax.ShapeDtypeStruct((T, NUP), BF16),
                   jax.ShapeDtypeStruct((8, NUP), F32)],
        scratch_shapes=[pltpu.VMEM((tiles, 2 * (tm + hl), LANES), F32), pltpu.VMEM((tm, NUP), F32),
                        pltpu.VMEM((tm, DFF), F32), pltpu.VMEM((tiles, hl, LANES), F32)],
        compiler_params=_params(("arbitrary",), 58))


def _weight_grad(a, g, bm, bn, tk, name):
    m, n = a.shape[1], g.shape[1]
    nk = T // tk

    def body(a_ref, g_ref, of_ref, ob_ref):
        k = pl.program_id(2)

        @pl.when(k == 0)
        def _():
            of_ref[...] = jnp.zeros_like(of_ref)
        of_ref[...] += _tn_dot(a_ref[...].astype(BF16), g_ref[...].astype(BF16))

        @pl.when(k == nk - 1)
        def _():
            ob_ref[...] = of_ref[...].astype(BF16)

    out = pl.BlockSpec((bm, bn), lambda i, j, k: (i, j))
    return _call(
        body, a, g, name=name, grid=(m // bm, n // bn, nk),
        in_specs=[pl.BlockSpec((tk, bm), lambda i, j, k: (k, i)), pl.BlockSpec((tk, bn), lambda i, j, k: (k, j))],
        out_specs=[out, out],
        out_shape=[jax.ShapeDtypeStruct((m, n), F32), jax.ShapeDtypeStruct((m, n), BF16)],
        compiler_params=_params(("parallel", "parallel", "arbitrary"), 56))


def _norm_bwd_mm(dz, w, xin, base, gain, name, rider):
    kdim = dz.shape[1]
    tm = 512

    def body(dz_ref, w_ref, x_ref, b_ref, g_ref, dx_ref, gg_ref):
        @pl.when(pl.program_id(0) == 0)
        def _():
            gg_ref[...] = jnp.zeros_like(gg_ref)

        xv = x_ref[...]
        dh = _nt(dz_ref[...], w_ref[...])
        r = lax.rsqrt(jnp.mean(xv * xv, axis=-1, keepdims=True) + EPS)
        t = dh * g_ref[...]
        dx_ref[...] = b_ref[...] + r * t - xv * (r * r * r) * jnp.mean(t * xv, axis=-1, keepdims=True)
        gg_ref[...] += jnp.sum(dh * xv * r, axis=0, keepdims=True)

    row = pl.BlockSpec((tm, D), lambda i: (i, 0))
    vec = pl.BlockSpec((1, D), lambda i: (0, 0))
    return _call(
        body, dz, w, xin, base, gain, rider=rider, name=name, grid=(T // tm,),
        in_specs=[pl.BlockSpec((tm, kdim), lambda i: (i, 0)), _resident((D, kdim)), row, row, vec],
        out_specs=[row, vec],
        out_shape=[jax.ShapeDtypeStruct((T, D), F32), jax.ShapeDtypeStruct((1, D), F32)],
        compiler_params=_params(("arbitrary",), 48))


def _outproj_bwd(dx1, w_out, o_f32, bd):
    tm = 512

    def body(d_ref, w_ref, o_ref, bd_ref, dc_ref, dl_ref):
        dc_ref[...] = _nt(d_ref[...].astype(BF16), w_ref[...])
        dl_ref[...] = _segsum(dc_ref[:, C:2 * C] * o_ref[...], bd_ref[...])

    row = pl.BlockSpec((tm, D), lambda i: (i, 0))
    blk = pl.BlockSpec((tm, C), lambda i: (i, 0))
    return _call(
        body, dx1, w_out, o_f32, bd, name="outproj_bwd", grid=(T // tm,),
        in_specs=[row, _resident((D, D)), blk, _resident((C, C))], out_specs=[row, blk],
        out_shape=[jax.ShapeDtypeStruct((T, D), F32), jax.ShapeDtypeStruct((T, C), F32)],
        compiler_params=_params(("arbitrary",), 32))


def _conv_bwd(dcat, cv, proj, conv_w, cn_g, cn_b, rider):
    tm, hl, rb, cb = CONV_TM, CONV_HALO, CONV_RB, CONV_CB
    per = tm // hl
    nt = T // tm
    tiles = C // LANES

    def body(du_ref, dun_ref, cv_ref, cvn_ref, av_ref, ag_ref, hv_ref, hg_ref, w_ref, g_ref, bb_ref,
             dp_ref, gv_ref, gw_ref, dsh_ref, gsh_ref):
        i = pl.program_id(0)

        @pl.when(i == 0)
        def _():
            gv_ref[...] = jnp.zeros_like(gv_ref)
            gw_ref[...] = jnp.zeros_like(gw_ref)

        def ln_bwd(du, cvv):
            mu = jnp.mean(cvv, axis=-1, keepdims=True)
            xc = cvv - mu
            rs = lax.rsqrt(jnp.mean(xc * xc, axis=-1, keepdims=True) + EPS)
            xh = xc * rs
            ln = xh * g_ref[...] + bb_ref[...]
            sg = _sigmoid(ln)
            dln = du * (sg + ln * sg * (1.0 - sg))
            dxh = dln * g_ref[...]
            dcv = rs * (dxh - jnp.mean(dxh, axis=-1, keepdims=True)
                        - xh * jnp.mean(dxh * xh, axis=-1, keepdims=True))
            return dcv, dln, xh

        for r0 in range(0, tm, rb):
            dcv, dln, xh = ln_bwd(du_ref[r0:r0 + rb, :], cv_ref[r0:r0 + rb, :])
            for j in range(tiles):
                dsh_ref.at[j][_sp(r0, rb)] = dcv[:, _lanes(j)]
            gv_ref[0:1, :] += jnp.sum(dln * xh, axis=0, keepdims=True)
            gv_ref[1:2, :] += jnp.sum(dln, axis=0, keepdims=True)
            gv_ref[2:3, :] += jnp.sum(dcv, axis=0, keepdims=True)
        dcv_n, _, _ = ln_bwd(dun_ref[...], cvn_ref[...])
        dcv_n = jnp.where(i < nt - 1, dcv_n, 0.0)
        for j in range(tiles):
            ln_ = _lanes(j)
            dsh_ref.at[j][_sp(tm, hl)] = dcv_n[:, ln_]
            glu_h = hv_ref[:, ln_] * _sigmoid(hg_ref[:, ln_])
            gsh_ref.at[j][_sp(0, hl)] = jnp.where(i > 0, glu_h, 0.0)
            for r0 in range(0, tm, cb):
                gsh_ref.at[j][_sp(hl + r0, cb)] = av_ref[r0:r0 + cb, ln_] * _sigmoid(ag_ref[r0:r0 + cb, ln_])

        for j in range(tiles):
            ln_ = _lanes(j)
            for r0 in range(0, tm, cb):
                dglu = jnp.zeros((cb, LANES), F32)
                for k in range(CONV_K):
                    dglu = dglu + w_ref[k:k + 1, ln_] * dsh_ref.at[j][_sp(r0 + (CONV_K - 1) - k, cb)]
                av = av_ref[r0:r0 + cb, ln_]
                sg = _sigmoid(ag_ref[r0:r0 + cb, ln_])
                dp_ref[r0:r0 + cb, ln_] = (dglu * sg).astype(BF16)
                dp_ref[r0:r0 + cb, _lanes(tiles + j)] = (dglu * av * sg * (1.0 - sg)).astype(BF16)
            for k in range(CONV_K):
                part = jnp.zeros((8, LANES), F32)
                for r0 in range(0, tm, cb):
                    prod = dsh_ref.at[j][_sp(r0, cb)] * gsh_ref.at[j][_sp(r0 + hl - (CONV_K - 1) + k, cb)]
                    part = part + jnp.sum(prod.reshape(cb // 8, 8, LANES), axis=0)
                gw_ref[k:k + 1, ln_] += jnp.sum(part, axis=0, keepdims=True)

    main = lambda col: pl.BlockSpec((tm, C), lambda i: (i, col))
    prev = lambda col: pl.BlockSpec((hl, C), lambda i: (jnp.maximum(i * per - 1, 0), col))
    nxt = pl.BlockSpec((hl, C), lambda i: (jnp.minimum((i + 1) * per, T // hl - 1), 0))
    vec = pl.BlockSpec((1, C), lambda i: (0, 0))
    return _call(
        body, dcat, dcat, cv, cv, proj, proj, proj, proj, conv_w, cn_g, cn_b, rider=rider, name="conv_bwd",
        grid=(nt,),
        in_specs=[main(0), nxt, main(0), nxt, main(0), main(1), prev(0), prev(1),
                  pl.BlockSpec((CONV_K, C), lambda i: (0, 0)), vec, vec],
        out_specs=[pl.BlockSpec((tm, 2 * C), lambda i: (i, 0)), pl.BlockSpec((8, C), lambda i: (0, 0)),
                   pl.BlockSpec((32, C), lambda i: (0, 0))],
        out_shape=[jax.ShapeDtypeStruct((T, NPROJ), BF16), jax.ShapeDtypeStruct((8, C), F32),
                   jax.ShapeDtypeStruct((32, C), F32)],
        scratch_shapes=[pltpu.VMEM((C // LANES, 2 * (tm + hl), LANES), F32)] * 2,
        compiler_params=_params(("arbitrary",), 48))


def _attn_bwd_unit(qs, dos, lgs, dls, rows, kc, vc, biasv, invalid_prev):
    qst, dost = _stack_heads(qs[rows, :]), _stack_heads(dos[rows, :])
    s = _nt(qst, kc) + biasv
    if invalid_prev is not None:
        col = lax.broadcasted_iota(jnp.int32, s.shape, 1)
        s = jnp.where((col < QB) & invalid_prev, NEG, s)
    p = jnp.exp(s - _stack_cols(lgs[rows, :]))
    ds = p * (_nt(dost, vc) - _stack_cols(dls[rows, :]))
    dsb = ds.astype(BF16)
    dq = _unstack_heads(jnp.dot(dsb, kc, preferred_element_type=F32))
    return dq, _tn_dot(dsb, qst), _tn_dot(p.astype(BF16), dost)


def _attn_bwd_lagged(qn, kn, proj, dcat, lg, dl, bias, d, earlier):
    assert QB * d == ATT_WIN
    n_win = T // ATT_WIN
    lanes = 2 * HEAD

    def body(q_ref, k_ref, v_ref, do_ref, lg_ref, dl_ref, kh_ref, vh_ref, bias_ref, eq_ref, ek_ref, ev_ref,
             dq_ref, dk_ref, dv_ref, qs, dos, lgs, dls, ks, vs, dqs, ck, cv, ok, ov):
        n = pl.program_id(1)

        @pl.when(n == 0)
        def _():
            ck[...] = jnp.zeros_like(ck)
            cv[...] = jnp.zeros_like(cv)

        @pl.when(n < n_win)
        def _():
            for r in range(d):
                rows = slice(r * QB, (r + 1) * QB)
                for dst, src, dt in ((qs, q_ref, BF16), (dos, do_ref, BF16), (lgs, lg_ref, F32), (dls, dl_ref, F32)):
                    dst[rows, :] = _stream(src, r, QB, d).astype(dt)
                for dst, halo, src in ((ks, kh_ref, k_ref), (vs, vh_ref, v_ref)):
                    dst[2 * r * QB:(2 * r + 1) * QB, :] = _stream(halo, r, QB, d).astype(BF16)
                    dst[(2 * r + 1) * QB:(2 * r + 2) * QB, :] = _stream(src, r, QB, d).astype(BF16)
            for r in range(d):
                rows = slice(r * QB, (r + 1) * QB)
                keys = slice(2 * r * QB, (2 * r + 2) * QB)
                dq, dkc, dvc = _attn_bwd_unit(qs, dos, lgs, dls, rows, ks[keys, :], vs[keys, :], bias_ref[...], n == 0)
                dqs[rows, :] = dq
                ok[rows, :] = ck[rows, :] + dkc[:QB]
                ov[rows, :] = cv[rows, :] + dvc[:QB]
                ck[rows, :] = dkc[QB:]
                cv[rows, :] = dvc[QB:]
            for r in range(d):
                rows = slice(r * QB, (r + 1) * QB)
                pos = pl.ds(r, QB, stride=d)
                dq_ref[pos, :] = dqs[rows, :] + eq_ref[pos, :]
                dk_ref[pos, :] = ok[rows, :] + ek_ref[pos, :]
                dv_ref[pos, :] = ov[rows, :] + ev_ref[pos, :]

        @pl.when(n == n_win)
        def _():
            for r in range(d):
                rows = slice(r * QB, (r + 1) * QB)
                pos = pl.ds(r, QB, stride=d)
                dk_ref[pos, :] = ck[rows, :] + ek_ref[pos, :]
                dv_ref[pos, :] = cv[rows, :] + ev_ref[pos, :]

    cur = lambda off: pl.BlockSpec((ATT_WIN, lanes), lambda cb, n: (jnp.minimum(n, n_win - 1), off + cb))
    prev = lambda off: pl.BlockSpec(
        (ATT_WIN, lanes), lambda cb, n: (jnp.maximum(jnp.minimum(n, n_win - 1) - 1, 0), off + cb))
    late = pl.BlockSpec((ATT_WIN, lanes), lambda cb, n: (jnp.maximum(n - 1, 0), cb))
    buf = lambda rows, dt: pltpu.VMEM((rows, lanes), dt)
    return _call(
        body, qn, kn, proj, dcat, lg, dl, kn, proj, bias, *earlier, name=f"attn_bwd_d{d}",
        grid=(C // lanes, n_win + 1),
        in_specs=[cur(0), cur(0), cur(V_COL), cur(DO_COL), cur(0), cur(0), prev(0), prev(V_COL),
                  pl.BlockSpec((None, 2 * QB, 2 * QB), lambda cb, n: (cb, 0, 0)), cur(0), late, late],
        out_specs=[cur(0), late, late],
        out_shape=[jax.ShapeDtypeStruct((T, C), F32)] * 3,
        scratch_shapes=[buf(ATT_WIN, BF16), buf(ATT_WIN, BF16), buf(ATT_WIN, F32), buf(ATT_WIN, F32),
                        buf(2 * ATT_WIN, BF16), buf(2 * ATT_WIN, BF16)] + [buf(ATT_WIN, F32)] * 5,
        compiler_params=_params(("arbitrary", "arbitrary"), 48))


def _attn_bwd(qn, kn, proj, dcat, lg, dl, bias, d, earlier=None, rider=None):
    sl, nb, hr = _attn_geometry(d)
    slk = QB + sl
    slq = sl + QB
    n_win = T // ATT_WIN

    def body(q_ref, k_ref, v_ref, do_ref, lg_ref, dl_ref, kh_ref, vh_ref, qx_ref, dox_ref, lgx_ref, dlx_ref,
             bias_ref, *rest):
        sums = rest[:3] if earlier is not None else (None, None, None)
        dq_ref, dk_ref, dv_ref, qs, dos, lgs, dls, ks, vs, dqs, dks, dvs = rest[-12:]
        n = pl.program_id(1)
        for r in range(d):
            for dst, src, nx, dt in ((qs, q_ref, qx_ref, BF16), (dos, do_ref, dox_ref, BF16),
                                     (lgs, lg_ref, lgx_ref, F32), (dls, dl_ref, dlx_ref, F32)):
                dst[r * slq:r * slq + sl, :] = _stream(src, r, sl, d).astype(dt)
                dst[r * slq + sl:(r + 1) * slq, :] = _stream(nx, r, QB, d).astype(dt)
            for dst, halo, src in ((ks, kh_ref, k_ref), (vs, vh_ref, v_ref)):
                dst[r * slk:r * slk + QB, :] = _stream(halo, r, QB, d).astype(BF16)
                dst[r * slk + QB:(r + 1) * slk, :] = _stream(src, r, sl, d).astype(BF16)
        dks[...] = jnp.zeros_like(dks)
        dvs[...] = jnp.zeros_like(dvs)

        def unit(rows, kc, vc, biasv, invalid_prev):
            return _attn_bwd_unit(qs, dos, lgs, dls, rows, kc, vc, biasv, invalid_prev)

        for r in range(d):
            for b in range(nb):
                rows = slice(r * slq + b * QB, r * slq + (b + 1) * QB)
                keys = slice(r * slk + b * QB, r * slk + (b + 2) * QB)
                dq, dkc, dvc = unit(rows, ks[keys, :], vs[keys, :], bias_ref[...], (n == 0) if b == 0 else None)
                dqs[r * sl + b * QB:r * sl + (b + 1) * QB, :] = dq
                if b == 0:
                    dks[r * sl:r * sl + QB, :] += dkc[QB:]
                    dvs[r * sl:r * sl + QB, :] += dvc[QB:]
                else:
                    dks[r * sl + (b - 1) * QB:r * sl + (b + 1) * QB, :] += dkc
                    dvs[r * sl + (b - 1) * QB:r * sl + (b + 1) * QB, :] += dvc

        @pl.when(n < n_win - 1)
        def _():
            for r in range(d):
                rows = slice(r * slq + sl, (r + 1) * slq)
                keys = slice(r * slk + sl, (r + 1) * slk)
                _, dkc, dvc = unit(rows, ks[keys, :], vs[keys, :], bias_ref[:, 0:QB], None)
                dks[(r + 1) * sl - QB:(r + 1) * sl, :] += dkc
                dvs[(r + 1) * sl - QB:(r + 1) * sl, :] += dvc

        for dst, src, before in zip((dq_ref, dk_ref, dv_ref), (dqs, dks, dvs), sums):
            for r in range(d):
                pos = (pl.ds(r, sl, stride=d) if d > 1 else slice(None), slice(None))
                val = src[r * sl:(r + 1) * sl, :]
                dst[pos] = val if before is None else val + before[pos]

    main, prev, nxt, bias_spec = _attn_specs(d)
    lanes = 2 * HEAD
    return _call(
        body, qn, kn, proj, dcat, lg, dl, kn, proj, qn, dcat, lg, dl, bias, *(earlier or ()), rider=rider,
        name=f"attn_bwd_d{d}", grid=(C // lanes, n_win),
        in_specs=[main(0), main(0), main(V_COL), main(DO_COL), main(0), main(0), prev(0), prev(V_COL),
                  nxt(0), nxt(DO_COL), nxt(0), nxt(0), bias_spec] + ([main(0)] * 3 if earlier is not None else []),
        out_specs=[main(0)] * 3,
        out_shape=[jax.ShapeDtypeStruct((T, C), F32)] * 3,
        scratch_shapes=[pltpu.VMEM((ATT_WIN + hr, lanes), BF16), pltpu.VMEM((ATT_WIN + hr, lanes), BF16),
                        pltpu.VMEM((ATT_WIN + hr, lanes), F32), pltpu.VMEM((ATT_WIN + hr, lanes), F32),
                        pltpu.VMEM((ATT_WIN + hr, lanes), BF16), pltpu.VMEM((ATT_WIN + hr, lanes), BF16),
                        pltpu.VMEM((ATT_WIN, lanes), F32), pltpu.VMEM((ATT_WIN, lanes), F32),
                        pltpu.VMEM((ATT_WIN, lanes), F32)],
        compiler_params=_params(("arbitrary", "arbitrary"), 48))


def _qk_norm_bwd(dn_sum, proj, col, gain, bd, dproj, name):
    tm = 512

    def body(d0, x_ref, g_ref, bd_ref, dp_in, dp_ref, gg_ref):
        del dp_in

        @pl.when(pl.program_id(0) == 0)
        def _():
            gg_ref[...] = jnp.zeros_like(gg_ref)
        dn = d0[...]
        xv = x_ref[...]
        r = lax.rsqrt(_segsum(xv * xv, bd_ref[...]) * (1.0 / HEAD) + EPS)
        t = dn * g_ref[...]
        mean_tx = _segsum(t * xv, bd_ref[...]) * (1.0 / HEAD)
        dp_ref[...] = (r * t - xv * (r * r * r) * mean_tx).astype(BF16)
        gg_ref[...] += jnp.sum(dn * xv * r, axis=0, keepdims=True)

    blk = pl.BlockSpec((tm, C), lambda i: (i, 0))
    vec = pl.BlockSpec((1, C), lambda i: (0, 0))
    return _call(
        body, dn_sum, proj, gain, bd, dproj, name=name, grid=(T // tm,),
        in_specs=[blk, pl.BlockSpec((tm, C), lambda i: (i, col)), vec, _resident((C, C)), ANY],
        out_specs=[pl.BlockSpec((tm, C), lambda i: (i, col)), vec],
        out_shape=[jax.ShapeDtypeStruct((T, NPROJ), BF16), jax.ShapeDtypeStruct((1, C), F32)],
        input_output_aliases={4: 0},
        compiler_params=_params(("arbitrary",), 32))


def _v_bwd(dv_sum, dproj):
    tm = 512

    def body(d0, dp_in, dp_ref):
        del dp_in
        dp_ref[...] = d0[...].astype(BF16)

    blk = pl.BlockSpec((tm, C), lambda i: (i, 0))
    return _call(
        body, dv_sum, dproj, name="v_bwd", grid=(T // tm,),
        in_specs=[blk, ANY],
        out_specs=[pl.BlockSpec((tm, C), lambda i: (i, 4))],
        out_shape=[jax.ShapeDtypeStruct((T, NPROJ), BF16)],
        input_output_aliases={1: 0},
        compiler_params=_params(("parallel",), 32))[0]


def _adamw(w, g, m, v):
    m = ADAM_B1 * m + (1.0 - ADAM_B1) * g
    v = ADAM_B2 * v + (1.0 - ADAM_B2) * (g * g)
    m_hat = m / (1.0 - ADAM_B1 ** ADAM_STEP)
    v_hat = v / (1.0 - ADAM_B2 ** ADAM_STEP)
    delta = -ADAM_LR * (m_hat / (jnp.sqrt(v_hat) + ADAM_EPS) + ADAM_WD * w)
    return delta, m, v


def _row_block(shape):
    rows = shape[0]
    for cand in (256, 128, 64, 88, 32, 8):
        if rows % cand == 0 and cand * shape[1] * 4 <= (2 << 20):
            return cand
    return 8


def _partial_sum(own, recv, name):
    br = _row_block(own.shape)
    cols = own.shape[1]

    def body(o_ref, r_ref, p_ref):
        p_ref[...] = ((o_ref[...] + r_ref[0].astype(F32)) + r_ref[1].astype(F32)) + r_ref[2].astype(F32)

    blk = pl.BlockSpec((br, cols), lambda i: (i, 0))
    return _call(
        body, own, recv, name=name, grid=(own.shape[0] // br,),
        in_specs=[blk, pl.BlockSpec((3, br, cols), lambda i: (0, i, 0))], out_specs=[blk],
        out_shape=[jax.ShapeDtypeStruct(own.shape, F32)],
        compiler_params=_params(("parallel",), 32))[0]


def _adamw_mat(p_own, p_sib, w, m, v, name):
    br = _row_block(w.shape)
    cols = w.shape[1]

    def body(a_ref, b_ref, w_ref, m_ref, v_ref, g_ref, d_ref, nm_ref, nv_ref):
        g = a_ref[...] + b_ref[...]
        delta, nm, nv = _adamw(w_ref[...], g, m_ref[...], v_ref[...])
        g_ref[...] = g
        d_ref[...] = delta
        nm_ref[...] = nm
        nv_ref[...] = nv

    blk = pl.BlockSpec((br, cols), lambda i: (i, 0))
    return _call(
        body, p_own, p_sib, w, m, v, name=name, grid=(w.shape[0] // br,),
        in_specs=[blk] * 5, out_specs=[blk] * 4,
        out_shape=[jax.ShapeDtypeStruct(w.shape, F32)] * 4,
        compiler_params=_params(("parallel",), 40))


def _vec_reduce(vrecv):
    def body(v_ref, o_ref):
        acc = v_ref[0]
        for r in range(1, N_DEV):
            acc = acc + v_ref[r]
        o_ref[...] = acc

    return pl.pallas_call(
        body, name="vec_reduce",
        out_shape=jax.ShapeDtypeStruct((VPACK_ROWS, D), F32),
        compiler_params=_params((), 32),
    )(vrecv)


def _adamw_small(w, g, m, v):
    def body(w_ref, g_ref, m_ref, v_ref, d_ref, nm_ref, nv_ref):
        delta, nm, nv = _adamw(w_ref[...], g_ref[...], m_ref[...], v_ref[...])
        d_ref[...] = delta
        nm_ref[...] = nm
        nv_ref[...] = nv

    return pl.pallas_call(
        body, name="adamw_small",
        out_shape=[jax.ShapeDtypeStruct(w.shape, F32)] * 3,
        compiler_params=_params((), 32),
    )(w, g, m, v)


def _pack(parts, rows):
    flat = jnp.concatenate([p.reshape(-1) for p in parts])
    return jnp.pad(flat, (0, rows * D - flat.shape[0])).reshape(rows, D)


def _unpack(packed, shapes):
    flat = packed.reshape(-1)
    out, off = [], 0
    for shp in shapes:
        size = 1
        for s in shp:
            size *= s
        out.append(flat[off:off + size].reshape(shp))
        off += size
    return out


def _no_comm(shards, row_sharded, peers=(0, 1, 2), into=None):
    del row_sharded, peers, into
    return None, lambda res, n: (res, shards)


def _with_comm(shards, row_sharded, peers=(0, 1, 2), into=None):
    rider = _gather_rider(shards, row_sharded, peers, into)
    return rider, lambda res, n: (res[:n], res[n:])


def _local_step(x, target, norm1_g, conv_b, cn_g, cn_b, q_norm_g, k_norm_g, norm2_g, ffconv_b,
                first_weights, late_weights, comm=True):
    row = lambda a: a.reshape(1, -1)
    head_of = jnp.arange(C) // HEAD
    bd = (head_of[:, None] == head_of[None, :]).astype(BF16)
    qg = row(jnp.tile(q_norm_g, C // HEAD) * (HEAD ** -0.5))
    kg = row(jnp.tile(k_norm_g, C // HEAD))
    biases = [_alibi_tables(d) for d in PATTERN_DILATIONS]
    gather = _with_comm if comm else _no_comm
    grad_rider = (lambda g, rs: _grad_rider(g[1], g[0], rs)) if comm else (lambda g, rs: None)

    rider, split = gather(first_weights, (False, False, False))
    (h,), (w_in, conv_w, ffconv_w) = split(_norm_fwd(x, row(norm1_g), rider), 1)
    rider, split = gather(late_weights[0:1], (True,))
    (proj, qn, kn), (w_out,) = split(_proj_fwd(h, w_in, qg, kg, bd, rider), 3)
    rider, split = gather(late_weights[1:2], (False,), (0, 1))
    (cat, cv), w_up_part = split(_conv_fwd(proj, conv_w, row(conv_b), row(cn_g), row(cn_b), rider), 2)
    fwd = [_attn_fwd(qn, kn, proj, biases[i], d) for i, d in enumerate(PATTERN_DILATIONS[:-1])]
    rider, split = gather(late_weights[1:2], (False,), (2,), w_up_part)
    merge = (fwd[0][0], fwd[0][1], fwd[1][0], fwd[1][1], cat)
    (cat, o_f32, lg), (w_up,) = split(
        _attn_fwd(qn, kn, proj, biases[-1], PATTERN_DILATIONS[-1], rider, merge), 3)
    rider, split = gather(late_weights[2:3], (True,))
    (x1, h2, up), (w_down,) = split(_out_up(x, cat, w_out, row(norm2_g), w_up, rider), 3)
    act, dy, loss_acc, dup, gff = _ffn(up, ffconv_w, row(ffconv_b), w_down, x1, target)
    gw_down = _weight_grad(act, dy, DFF // 2, D, 1024, "grad_w_down")
    res = _norm_bwd_mm(dup, w_up, x1, dy, row(norm2_g), "up_bwd", grad_rider(gw_down, True))
    (dx1, g_norm2), ex_down = res[:2], res[2:]
    gw_up = _weight_grad(h2, dup, D, NUP // 4, 2048, "grad_w_up")
    dcat, dl = _outproj_bwd(dx1, w_out, o_f32, bd)
    gw_out = _weight_grad(cat, dx1, D, D, 2048, "grad_w_out")
    res = _conv_bwd(dcat, cv, proj, conv_w, row(cn_g), row(cn_b), grad_rider(gw_up, False))
    (dproj, gconv_vec, gconv_w), ex_up = res[:3], res[3:]
    sums, ex_out = None, []
    for i, d in enumerate(PATTERN_DILATIONS):
        if QB * d == ATT_WIN:
            res = _attn_bwd_lagged(qn, kn, proj, dcat, lg, dl, biases[i], d, sums)
        else:
            res = _attn_bwd(qn, kn, proj, dcat, lg, dl, biases[i], d, sums,
                            grad_rider(gw_out, True) if i == 0 else None)
        sums = res[:3]
        ex_out = res[3:] if i == 0 else ex_out
    dproj, gq_lane = _qk_norm_bwd(sums[0], proj, 2, qg, bd, dproj, "q_norm_bwd")
    dproj, gk_lane = _qk_norm_bwd(sums[1], proj, 3, kg, bd, dproj, "k_norm_bwd")
    dproj = _v_bwd(sums[2], dproj)
    gw_in = _weight_grad(h, dproj, D, NPROJ // 4, 2048, "grad_w_in")
    res = _norm_bwd_mm(dproj, w_in, x, dx1, row(norm1_g), "in_bwd", grad_rider(gw_in, False))
    (dx, g_norm1), ex_in = res[:2], res[2:]

    loss = loss_acc[0, 0] * (0.5 / D)
    g_qg = jnp.sum(gq_lane.reshape(C // HEAD, HEAD), axis=0) * (HEAD ** -0.5)
    g_kg = jnp.sum(gk_lane.reshape(C // HEAD, HEAD), axis=0)
    small = [g_norm1[0], gconv_vec[2], gconv_vec[0], gconv_vec[1], g_qg, g_kg, g_norm2[0], gff[3],
             gconv_w[:CONV_K], gff[:FF_K]]
    mats = [ex_in, ex_out, ex_up, ex_down] if comm else [gw_in, gw_out, gw_up, gw_down]
    return loss, dx, mats, small


def kernel(x, norm1_g, w_in, conv_w, conv_b, cn_g, cn_b, q_norm_g, k_norm_g, w_out, norm2_g, w_up, ffconv_w, ffconv_b, w_down, loss_target, m_norm1_g, m_w_in, m_conv_w, m_conv_b, m_cn_g, m_cn_b, m_q_norm_g, m_k_norm_g, m_w_out, m_norm2_g, m_w_up, m_ffconv_w, m_ffconv_b, m_w_down, v_norm1_g, v_w_in, v_conv_w, v_conv_b, v_cn_g, v_cn_b, v_q_norm_g, v_k_norm_g, v_w_out, v_norm2_g, v_w_up, v_ffconv_w, v_ffconv_b, v_w_down):
    chip = 2 * lax.axis_index("x") + lax.axis_index("y")

    loss, dx, mats, small = _local_step(
        x[0], loss_target[0], norm1_g, conv_b, cn_g, cn_b, q_norm_g, k_norm_g, norm2_g, ffconv_b,
        [w_in.astype(BF16), conv_w, ffconv_w], [w.astype(BF16) for w in (w_out, w_up, w_down)])

    names = ("w_in", "w_out", "w_up", "w_down")
    parts = [_partial_sum(own, recv, "partial_" + names[k]) for k, (recv, own) in enumerate(mats)]
    sib, vrecv = _final_exchange(parts, _pack(small + [loss.reshape(1)], VPACK_ROWS))
    ws = (w_in, w_out, w_up, w_down)
    ms = (m_w_in, m_w_out, m_w_up, m_w_down)
    vs = (v_w_in, v_w_out, v_w_up, v_w_down)
    mat = [_adamw_mat(parts[k], sib[k], ws[k], ms[k], vs[k], "adamw_" + names[k]) for k in range(4)]

    vsum = _vec_reduce(vrecv)
    vec_shapes = [(D,), (C,), (C,), (C,), (HEAD,), (HEAD,), (D,), (NUP,), (CONV_K, C), (FF_K, NUP), (1,)]
    gsmall = _unpack(vsum, vec_shapes)
    g_conv_w = lax.dynamic_slice_in_dim(gsmall[8], chip * (C // N_CHIPS), C // N_CHIPS, axis=1)
    g_ffconv_w = lax.dynamic_slice_in_dim(gsmall[9], chip * (NUP // N_CHIPS), NUP // N_CHIPS, axis=1)
    gs = gsmall[:8] + [g_conv_w, g_ffconv_w]
    w_s = [norm1_g, conv_b, cn_g, cn_b, q_norm_g, k_norm_g, norm2_g, ffconv_b, conv_w, ffconv_w]
    m_s = [m_norm1_g, m_conv_b, m_cn_g, m_cn_b, m_q_norm_g, m_k_norm_g, m_norm2_g, m_ffconv_b, m_conv_w, m_ffconv_w]
    v_s = [v_norm1_g, v_conv_b, v_cn_g, v_cn_b, v_q_norm_g, v_k_norm_g, v_norm2_g, v_ffconv_b, v_conv_w, v_ffconv_w]
    shapes_s = [a.shape for a in w_s]
    d_p, m_p, v_p = _adamw_small(_pack(w_s, SPACK_ROWS), _pack(gs, SPACK_ROWS), _pack(m_s, SPACK_ROWS),
                                 _pack(v_s, SPACK_ROWS))
    d_s, nm_s, nv_s = _unpack(d_p, shapes_s), _unpack(m_p, shapes_s), _unpack(v_p, shapes_s)

    def ordered(sm, mt):
        return [sm[0], mt[0], sm[8], sm[1], sm[2], sm[3], sm[4], sm[5], mt[1], sm[6], mt[2], sm[9], sm[7], mt[3]]

    loss_all = gsmall[10][0]
    grads = ordered(gs, [r[0] for r in mat])
    deltas = ordered(d_s, [r[1] for r in mat])
    new_m = ordered(nm_s, [r[2] for r in mat])
    new_v = ordered(nv_s, [r[3] for r in mat])
    return (loss_all, dx[None], *grads, *deltas, *new_m, *new_v)
```

```python
import types

import jax
import jax.numpy as jnp
from jax import lax
from jax.experimental import pallas as pl
from jax.experimental.pallas import tpu as pltpu

T = 8192
D = 1024
C = 512
NPROJ = 2560
DFF = 2816
NUP = 2 * DFF
CONV_K = 31
FF_K = 3
HEAD = 64
EPS = 1e-6
NEG = -1e30
N_CHIPS = 4
N_DEV = 8
PATTERN_DILATIONS = (1, 4, 16)
QB = 128

ADAM_LR = 0.001
ADAM_B1 = 0.9
ADAM_B2 = 0.999
ADAM_EPS = 1e-08
ADAM_WD = 0.01
ADAM_STEP = 10

F32 = jnp.float32
BF16 = jnp.bfloat16
MESH = pl.DeviceIdType.MESH
ANY = pl.BlockSpec(memory_space=pl.ANY)

VPACK_ROWS = 48
SPACK_ROWS = 24


def _params(sem, vmem_mb):
    return pltpu.CompilerParams(dimension_semantics=sem, vmem_limit_bytes=vmem_mb << 20)


def _resident(shape):
    return pl.BlockSpec(shape, lambda i: (0, 0), pipeline_mode=pl.Buffered(1))


def _nt(a, b):
    return lax.dot_general(a, b, (((1,), (1,)), ((), ())), preferred_element_type=F32)


def _tn_dot(a, b):
    return lax.dot_general(a, b, (((0,), (0,)), ((), ())), preferred_element_type=F32)


def _sigmoid(x):
    return 1.0 / (1.0 + jnp.exp(-x))


def _segsum(x, bd):
    hi = x.astype(BF16)
    lo = (x - hi.astype(F32)).astype(BF16)
    return (jnp.dot(hi, bd, preferred_element_type=F32)
            + jnp.dot(lo, bd, preferred_element_type=F32))


def _place():
    x, y, c = lax.axis_index("x"), lax.axis_index("y"), lax.axis_index("c")
    chips = [(1 - x, y), (x, 1 - y), (1 - x, 1 - y)]
    return x, y, c, chips


def _block_of(ref, shard_shape, row_sharded, s):
    r, cdim = shard_shape
    if row_sharded:
        return ref.at[pl.ds(s * r, r), :]
    return ref.at[:, pl.ds(s * cdim, cdim)]


def _full_shape(shard_shape, row_sharded):
    r, cdim = shard_shape
    return (r * N_CHIPS, cdim) if row_sharded else (r, cdim * N_CHIPS)


def _gather_rider(shards, row_sharded, peers=(0, 1, 2), into=None):
    n = len(shards)
    shapes = [a.shape for a in shards]

    def copies(ins, outs, sems):
        send_sems, recv_sems, local_sems = sems
        x, y, c, chips = _place()
        me = 2 * x + y
        place = lambda k, s: _block_of(outs[k], shapes[k], row_sharded[k], s)
        local = []
        if into is None:
            local = [pltpu.make_async_copy(ins[k], place(k, me), local_sems.at[k]) for k in range(n)]
        sends, recvs = [], []
        for k in range(n):
            for j in peers:
                px, py = chips[j]
                sem = dict(send_sem=send_sems.at[3 * k + j], recv_sem=recv_sems.at[3 * k + j],
                           device_id=(px, py, c), device_id_type=MESH)
                sends.append(pltpu.make_async_remote_copy(src_ref=ins[k], dst_ref=place(k, me), **sem))
                recvs.append(pltpu.make_async_remote_copy(src_ref=ins[k], dst_ref=place(k, 2 * px + py), **sem))
        return local, sends, recvs

    return types.SimpleNamespace(
        operands=list(shards) + list(into or []), copies=copies,
        aliases={n + k: k for k in range(n)} if into is not None else {},
        out_shape=[jax.ShapeDtypeStruct(_full_shape(s, rs), a.dtype) for s, rs, a in zip(shapes, row_sharded, shards)],
        sems=[pltpu.SemaphoreType.DMA((3 * n,)), pltpu.SemaphoreType.DMA((3 * n,)), pltpu.SemaphoreType.DMA((n,))])


def _grad_rider(g_bf16, g_f32, row_sharded):
    shard = tuple(d // N_CHIPS if (i == 0) == row_sharded else d for i, d in enumerate(g_f32.shape))

    def copies(ins, outs, sems):
        send_sems, recv_sems, local_sems = sems
        gb, gf = ins
        rec, own = outs
        x, y, c, chips = _place()
        me = 2 * x + y
        local = [pltpu.make_async_copy(_block_of(gf, shard, row_sharded, me), own, local_sems.at[0])]
        sends, recvs = [], []
        for j, (px, py) in enumerate(chips):
            sem = dict(send_sem=send_sems.at[j], recv_sem=recv_sems.at[j], device_id=(px, py, c), device_id_type=MESH)
            sends.append(pltpu.make_async_remote_copy(
                src_ref=_block_of(gb, shard, row_sharded, 2 * px + py), dst_ref=rec.at[j], **sem))
            recvs.append(pltpu.make_async_remote_copy(
                src_ref=_block_of(gb, shard, row_sharded, me), dst_ref=rec.at[j], **sem))
        return local, sends, recvs

    return types.SimpleNamespace(
        operands=[g_bf16, g_f32], copies=copies, aliases={},
        out_shape=[jax.ShapeDtypeStruct((3,) + shard, BF16), jax.ShapeDtypeStruct(shard, F32)],
        sems=[pltpu.SemaphoreType.DMA((3,)), pltpu.SemaphoreType.DMA((3,)), pltpu.SemaphoreType.DMA((1,))])


def _rider_start(rider, ins, outs, sems):
    local, sends, _ = rider.copies(ins, outs, sems)
    for cp in local + sends:
        cp.start()


def _rider_wait(rider, ins, outs, sems):
    local, sends, recvs = rider.copies(ins, outs, sems)
    for cp in recvs:
        cp.wait_recv()
    for cp in sends:
        cp.wait_send()
    for cp in local:
        cp.wait()


def _call(body, *operands, rider=None, name, grid, in_specs, out_specs, out_shape, scratch_shapes=(),
          compiler_params, input_output_aliases=None):
    operands = [pltpu.with_memory_space_constraint(a, pltpu.HBM) for a in operands]
    if rider is None:
        return pl.pallas_call(
            body, name=name, grid=grid, in_specs=list(in_specs), out_specs=list(out_specs), out_shape=list(out_shape),
            scratch_shapes=list(scratch_shapes), compiler_params=compiler_params,
            input_output_aliases=input_output_aliases or {})(*operands)
    n_in, n_out, n_scr = len(in_specs), len(out_specs), len(scratch_shapes)
    r_in, r_out = len(rider.operands), len(rider.out_shape)

    def riding(*refs):
        refs = list(refs)
        ins, refs = refs[:n_in], refs[n_in:]
        r_ins, refs = refs[:r_in], refs[r_in:]
        outs, refs = refs[:n_out], refs[n_out:]
        r_outs, refs = refs[:r_out], refs[r_out:]
        scr, sems = refs[:n_scr], refs[n_scr:]
        first = pl.program_id(0) == 0
        last = pl.program_id(0) == grid[0] - 1
        for axis in range(1, len(grid)):
            first = first & (pl.program_id(axis) == 0)
            last = last & (pl.program_id(axis) == grid[axis] - 1)

        @pl.when(first)
        def _():
            _rider_start(rider, r_ins, r_outs, sems)

        body(*ins, *outs, *scr)

        @pl.when(last)
        def _():
            _rider_wait(rider, r_ins, r_outs, sems)

    return pl.pallas_call(
        riding, name=name, grid=grid, in_specs=list(in_specs) + [ANY] * r_in,
        out_specs=list(out_specs) + [ANY] * r_out, out_shape=list(out_shape) + list(rider.out_shape),
        scratch_shapes=list(scratch_shapes) + list(rider.sems), compiler_params=compiler_params,
        input_output_aliases={**(input_output_aliases or {}),
                              **{n_in + i: n_out + o for i, o in rider.aliases.items()}})(
            *operands, *[pltpu.with_memory_space_constraint(a, pltpu.HBM) for a in rider.operands])


def _final_exchange(parts, vpack):
    def body(p0, p1, p2, p3, v_ref, o0, o1, o2, o3, vr_ref, send_sems, recv_sems, vsend_sems, vrecv_sems, local_sem):
        x, y, c, _ = _place()
        me = 4 * x + 2 * y + c
        mine = pltpu.make_async_copy(v_ref, vr_ref.at[me], local_sem)
        mine.start()
        copies = [pltpu.make_async_remote_copy(
            src_ref=p, dst_ref=o, send_sem=send_sems.at[k], recv_sem=recv_sems.at[k],
            device_id=(x, y, 1 - c), device_id_type=MESH)
            for k, (p, o) in enumerate(zip((p0, p1, p2, p3), (o0, o1, o2, o3)))]
        flips = [(fx, fy, fc) for fx in (0, 1) for fy in (0, 1) for fc in (0, 1)][1:]
        recvs = []
        for r, (fx, fy, fc) in enumerate(flips):
            peer = (x ^ fx, y ^ fy, c ^ fc)
            sem = dict(send_sem=vsend_sems.at[r], recv_sem=vrecv_sems.at[r], device_id=peer, device_id_type=MESH)
            copies.append(pltpu.make_async_remote_copy(src_ref=v_ref, dst_ref=vr_ref.at[me], **sem))
            recvs.append(pltpu.make_async_remote_copy(
                src_ref=v_ref, dst_ref=vr_ref.at[4 * peer[0] + 2 * peer[1] + peer[2]], **sem))
        for cp in copies:
            cp.start()
        for cp in copies[:4]:
            cp.wait_recv()
        for cp in recvs:
            cp.wait_recv()
        for cp in copies:
            cp.wait_send()
        mine.wait()

    res = pl.pallas_call(
        body, name="final_exchange",
        out_shape=[jax.ShapeDtypeStruct(p.shape, F32) for p in parts]
        + [jax.ShapeDtypeStruct((N_DEV, VPACK_ROWS, D), F32)],
        in_specs=[ANY] * 5, out_specs=[ANY] * 5,
        scratch_shapes=[pltpu.SemaphoreType.DMA((4,)), pltpu.SemaphoreType.DMA((4,)),
                        pltpu.SemaphoreType.DMA((7,)), pltpu.SemaphoreType.DMA((7,)), pltpu.SemaphoreType.DMA],
    )(*parts, vpack)
    return res[:4], res[4]


def _norm_fwd(x, g1, rider):
    tm = 512

    def body(x_ref, g_ref, h_ref):
        xv = x_ref[...]
        r = lax.rsqrt(jnp.mean(xv * xv, axis=-1, keepdims=True) + EPS)
        h_ref[...] = (xv * r * g_ref[...]).astype(BF16)

    row = pl.BlockSpec((tm, D), lambda i: (i, 0))
    return _call(
        body, x, g1, rider=rider, name="norm_fwd", grid=(T // tm,),
        in_specs=[row, pl.BlockSpec((1, D), lambda i: (0, 0))], out_specs=[row],
        out_shape=[jax.ShapeDtypeStruct((T, D), BF16)],
        compiler_params=_params(("arbitrary",), 32))


def _proj_fwd(h, w_in, qg, kg, bd, rider):
    tm, tn = 512, 640

    def body(h_ref, w_ref, qg_ref, kg_ref, bd_ref, p_ref, qn_ref, kn_ref):
        for j in range(NPROJ // tn):
            cols = slice(j * tn, (j + 1) * tn)
            p_ref[:, cols] = jnp.dot(h_ref[...], w_ref[:, cols], preferred_element_type=F32)
        for col, g, dst in ((2, qg_ref, qn_ref), (3, kg_ref, kn_ref)):
            xv = p_ref[:, col * C:(col + 1) * C]
            ms = _segsum(xv * xv, bd_ref[...]) * (1.0 / HEAD)
            dst[...] = xv * lax.rsqrt(ms + EPS) * g[...]

    vec = pl.BlockSpec((1, C), lambda i: (0, 0))
    blk = pl.BlockSpec((tm, C), lambda i: (i, 0))
    return _call(
        body, h, w_in, qg, kg, bd, rider=rider, name="proj_fwd", grid=(T // tm,),
        in_specs=[pl.BlockSpec((tm, D), lambda i: (i, 0)), _resident((D, NPROJ)), vec, vec, _resident((C, C))],
        out_specs=[pl.BlockSpec((tm, NPROJ), lambda i: (i, 0)), blk, blk],
        out_shape=[jax.ShapeDtypeStruct((T, NPROJ), F32), jax.ShapeDtypeStruct((T, C), F32),
                   jax.ShapeDtypeStruct((T, C), F32)],
        compiler_params=_params(("arbitrary",), 40))


CONV_TM = 512
CONV_HALO = 32
CONV_RB = 32
CONV_CB = 64


LANES = 128


def _sp(start, n):
    return (pl.ds(2 * start, n, stride=2), slice(None))


def _lanes(tile):
    return slice(tile * LANES, (tile + 1) * LANES)


def _conv_fwd(proj, conv_w, conv_b, cn_g, cn_b, rider):
    tm, hl, rb, cb = CONV_TM, CONV_HALO, CONV_RB, CONV_CB
    per = tm // hl

    def body(av_ref, ag_ref, hv_ref, hg_ref, w_ref, b_ref, g_ref, bb_ref, cat_ref, cv_ref, sh_ref):
        i = pl.program_id(0)
        for j in range(C // LANES):
            ln_ = _lanes(j)
            glu_h = hv_ref[:, ln_] * _sigmoid(hg_ref[:, ln_])
            sh_ref.at[j][_sp(0, hl)] = jnp.where(i > 0, glu_h, 0.0)
            for r0 in range(0, tm, cb):
                sh_ref.at[j][_sp(hl + r0, cb)] = av_ref[r0:r0 + cb, ln_] * _sigmoid(ag_ref[r0:r0 + cb, ln_])
            for r0 in range(0, tm, cb):
                acc = jnp.zeros((cb, LANES), F32) + b_ref[:, ln_]
                for k in range(CONV_K):
                    acc = acc + w_ref[k:k + 1, ln_] * sh_ref.at[j][_sp(r0 + hl - (CONV_K - 1) + k, cb)]
                cv_ref[r0:r0 + cb, ln_] = acc
        for r0 in range(0, tm, rb):
            acc = cv_ref[r0:r0 + rb, :]
            mu = jnp.mean(acc, axis=-1, keepdims=True)
            xc = acc - mu
            var = jnp.mean(xc * xc, axis=-1, keepdims=True)
            ln = xc * lax.rsqrt(var + EPS) * g_ref[...] + bb_ref[...]
            cat_ref[r0:r0 + rb, :] = (ln * _sigmoid(ln)).astype(BF16)

    halo = lambda col: pl.BlockSpec((hl, C), lambda i: (jnp.maximum(i * per - 1, 0), col))
    vec = pl.BlockSpec((1, C), lambda i: (0, 0))
    return _call(
        body, proj, proj, proj, proj, conv_w, conv_b, cn_g, cn_b, rider=rider, name="conv_fwd", grid=(T // tm,),
        in_specs=[pl.BlockSpec((tm, C), lambda i: (i, 0)), pl.BlockSpec((tm, C), lambda i: (i, 1)),
                  halo(0), halo(1), pl.BlockSpec((CONV_K, C), lambda i: (0, 0)), vec, vec, vec],
        out_specs=[pl.BlockSpec((tm, C), lambda i: (i, 0)), pl.BlockSpec((tm, C), lambda i: (i, 0))],
        out_shape=[jax.ShapeDtypeStruct((T, D), BF16), jax.ShapeDtypeStruct((T, C), F32)],
        scratch_shapes=[pltpu.VMEM((C // LANES, 2 * (tm + hl), LANES), F32)],
        compiler_params=_params(("arbitrary",), 40))


def _stack_heads(a):
    lane = lax.broadcasted_iota(jnp.int32, a.shape, 1)
    zero = jnp.zeros_like(a)
    return jnp.concatenate([jnp.where(lane < HEAD, a, zero), jnp.where(lane >= HEAD, a, zero)], axis=0)


def _unstack_heads(a2):
    lane = lax.broadcasted_iota(jnp.int32, (QB, 2 * HEAD), 1)
    return jnp.where(lane < HEAD, a2[:QB], a2[QB:])


def _stack_cols(a):
    return jnp.concatenate([a[:, 0:1], a[:, HEAD:HEAD + 1]], axis=0)


ATT_WIN = 2048
V_COL = 4 * C // (2 * HEAD)
DO_COL = C // (2 * HEAD)


def _attn_geometry(d):
    sl = ATT_WIN // d
    return sl, sl // QB, QB * d


SPLIT = 4
PIECE = 128


def _gather_streams(src_ref, tmp_ref, d):
    if d <= SPLIT:
        return lambda r, n: _stream(src_ref, r, n, d)
    q = src_ref.shape[0] // SPLIT
    for a in range(SPLIT):
        for off in range(0, q, PIECE):
            tmp_ref[a * q + off:a * q + off + PIECE, :] = src_ref[pl.ds(a + SPLIT * off, PIECE, stride=SPLIT), :]
    return lambda r, n: tmp_ref[pl.ds((r % SPLIT) * q + r // SPLIT, n, stride=d // SPLIT), :]


def _scatter_streams(dst_ref, tmp_ref, d, value_of, n, before=None):
    if d <= SPLIT:
        for r in range(d):
            pos = (pl.ds(r, n, stride=d) if d > 1 else slice(None), slice(None))
            val = value_of(r)
            dst_ref[pos] = val if before is None else val + before[pos]
        return
    q = dst_ref.shape[0] // SPLIT
    for r in range(d):
        tmp_ref[pl.ds((r % SPLIT) * q + r // SPLIT, n, stride=d // SPLIT), :] = value_of(r)
    for a in range(SPLIT):
        for off in range(0, q, PIECE):
            pos = (pl.ds(a + SPLIT * off, PIECE, stride=SPLIT), slice(None))
            val = tmp_ref[a * q + off:a * q + off + PIECE, :]
            dst_ref[pos] = val if before is None else val + before[pos]


def _stream(ref, r, n, d):
    return ref[pl.ds(r, n, stride=d), :] if d > 1 else ref[pl.ds(r, n), :]


def _alibi_tables(d):
    qi = jnp.arange(QB)[:, None]
    kj = jnp.arange(2 * QB)[None, :]
    delta = qi + QB - kj
    band = (delta >= 0) & (delta <= QB)
    dist = (delta * d).astype(F32)
    heads = jnp.arange(8, dtype=F32)
    slopes = 2.0 ** (-(heads + 1.0))
    t = jnp.where(band[None], -slopes[:, None, None] * dist[None], NEG)
    return t.reshape(4, 2 * QB, 2 * QB)


def _attn_specs(d):
    _, _, hr = _attn_geometry(d)
    per = ATT_WIN // hr
    main = lambda off: pl.BlockSpec((ATT_WIN, 2 * HEAD), lambda cb, n: (n, off + cb))
    prev = lambda off: pl.BlockSpec((hr, 2 * HEAD), lambda cb, n: (jnp.maximum(n * per - 1, 0), off + cb))
    nxt = lambda off: pl.BlockSpec((hr, 2 * HEAD), lambda cb, n: (jnp.minimum((n + 1) * per, T // hr - 1), off + cb))
    bias = pl.BlockSpec((None, 2 * QB, 2 * QB), lambda cb, n: (cb, 0, 0))
    return main, prev, nxt, bias


def _attn_fwd(qn, kn, proj, bias, d, rider=None, merge=None):
    sl, nb, hr = _attn_geometry(d)
    slk = QB + sl
    mrows = 256

    def body(q_ref, k_ref, v_ref, kh_ref, vh_ref, bias_ref, *rest):
        if merge is None:
            o_ref, l_ref, qs, ks, vs, os_, ls, tmp = rest
        else:
            oa_ref, la_ref, ob_ref, lb_ref, _, cat_ref, of_ref, lg_ref, qs, ks, vs, os_, ls, tmp, o_ref, l_ref = rest
        n = pl.program_id(1)
        for dst, per, at, src, take in ((qs, sl, 0, q_ref, sl), (ks, slk, 0, kh_ref, QB), (ks, slk, QB, k_ref, sl),
                                        (vs, slk, 0, vh_ref, QB), (vs, slk, QB, v_ref, sl)):
            stream = _gather_streams(src, tmp, d)
            for r in range(d):
                dst[r * per + at:r * per + at + take, :] = stream(r, take).astype(BF16)
        col = lax.broadcasted_iota(jnp.int32, (2 * QB, 2 * QB), 1)
        for r in range(d):
            for b in range(nb):
                rows = slice(r * sl + b * QB, r * sl + (b + 1) * QB)
                keys = slice(r * slk + b * QB, r * slk + (b + 2) * QB)
                s = _nt(_stack_heads(qs[rows, :]), ks[keys, :]) + bias_ref[...]
                if b == 0:
                    s = jnp.where((col < QB) & (n == 0), NEG, s)
                m = jnp.max(s, axis=-1, keepdims=True)
                p = jnp.exp(s - m)
                den = jnp.sum(p, axis=-1, keepdims=True)
                pv = jnp.dot(p.astype(BF16), vs[keys, :], preferred_element_type=F32)
                os_[rows, :] = _unstack_heads(pv / den)
                ls[rows, :] = _unstack_heads(jnp.broadcast_to(m + jnp.log(den), (2 * QB, 2 * HEAD)))
        _scatter_streams(o_ref, tmp, d, lambda r: os_[r * sl:(r + 1) * sl, :], sl)
        _scatter_streams(l_ref, tmp, d, lambda r: ls[r * sl:(r + 1) * sl, :], sl)
        if merge is not None:
            for r0 in range(0, ATT_WIN, mrows):
                rows = slice(r0, r0 + mrows)
                a, b, c = la_ref[rows, :], lb_ref[rows, :], l_ref[rows, :]
                m = jnp.maximum(jnp.maximum(a, b), c)
                e0, e1, e2 = jnp.exp(a - m), jnp.exp(b - m), jnp.exp(c - m)
                den = e0 + e1 + e2
                o = (e0 * oa_ref[rows, :] + e1 * ob_ref[rows, :] + e2 * o_ref[rows, :]) / den
                of_ref[rows, :] = o
                cat_ref[rows, :] = o.astype(BF16)
                lg_ref[rows, :] = m + jnp.log(den)

    main, prev, _, bias_spec = _attn_specs(d)
    lanes = 2 * HEAD
    operands = [qn, kn, proj, kn, proj, bias]
    in_specs = [main(0), main(0), main(V_COL), prev(0), prev(V_COL), bias_spec]
    scratch = [pltpu.VMEM((ATT_WIN, lanes), BF16), pltpu.VMEM((ATT_WIN + hr, lanes), BF16),
               pltpu.VMEM((ATT_WIN + hr, lanes), BF16), pltpu.VMEM((ATT_WIN, lanes), F32),
               pltpu.VMEM((ATT_WIN, lanes), F32), pltpu.VMEM((ATT_WIN, lanes), F32)]
    if merge is None:
        out_specs = [main(0), main(0)]
        out_shape = [jax.ShapeDtypeStruct((T, C), F32)] * 2
        aliases = None
    else:
        operands += list(merge)
        in_specs += [main(0)] * 4 + [ANY]
        out_specs = [main(C // lanes), main(0), main(0)]
        out_shape = [jax.ShapeDtypeStruct((T, D), BF16), jax.ShapeDtypeStruct((T, C), F32),
                     jax.ShapeDtypeStruct((T, C), F32)]
        scratch += [pltpu.VMEM((ATT_WIN, lanes), F32)] * 2
        aliases = {len(operands) - 1: 0}
    return _call(
        body, *operands, rider=rider, name=f"attn_fwd_d{d}", grid=(C // lanes, T // ATT_WIN),
        in_specs=in_specs, out_specs=out_specs, out_shape=out_shape, scratch_shapes=scratch,
        input_output_aliases=aliases, compiler_params=_params(("arbitrary", "arbitrary"), 48))


def _out_up(x, cat, w_out, g2, w_up, rider):
    tm, tn = 512, NUP // 4

    def body(x_ref, cat_ref, wo_ref, g_ref, wu_ref, x1_ref, h2_ref, up_ref):
        x1 = x_ref[...] + jnp.dot(cat_ref[...], wo_ref[...], preferred_element_type=F32)
        x1_ref[...] = x1
        r = lax.rsqrt(jnp.mean(x1 * x1, axis=-1, keepdims=True) + EPS)
        h2_ref[...] = (x1 * r * g_ref[...]).astype(BF16)
        for j in range(NUP // tn):
            cols = slice(j * tn, (j + 1) * tn)
            up_ref[:, cols] = jnp.dot(h2_ref[...], wu_ref[:, cols], preferred_element_type=F32)

    row = pl.BlockSpec((tm, D), lambda i: (i, 0))
    return _call(
        body, x, cat, w_out, g2, w_up, rider=rider, name="out_up", grid=(T // tm,),
        in_specs=[row, row, _resident((D, D)), pl.BlockSpec((1, D), lambda i: (0, 0)), _resident((D, NUP))],
        out_specs=[row, row, pl.BlockSpec((tm, NUP), lambda i: (i, 0))],
        out_shape=[jax.ShapeDtypeStruct((T, D), F32), jax.ShapeDtypeStruct((T, D), BF16),
                   jax.ShapeDtypeStruct((T, NUP), F32)],
        compiler_params=_params(("arbitrary",), 58))


FF_TM = 256
FF_HALO = 8
FF_RB = 64
FF_TILES = DFF // LANES


def _ff_conv(ext_ref, fw_ref, fb_ref, tile, r0):
    cols = _lanes(tile)
    base = FF_HALO + r0
    acc = fb_ref[:, cols] + fw_ref[0:1, cols] * ext_ref.at[tile][_sp(base - 2, FF_RB)]
    acc = acc + fw_ref[1:2, cols] * ext_ref.at[tile][_sp(base - 1, FF_RB)]
    return acc + fw_ref[2:3, cols] * ext_ref.at[tile][_sp(base, FF_RB)]


def _ffn(up, ffconv_w, ffconv_b, w_down, x1, target):
    tm, hl = FF_TM, FF_HALO
    per = tm // hl
    nt = T // tm
    tiles = 2 * FF_TILES

    def body(up_ref, uh_ref, fw_ref, fb_ref, wd_ref, x1_ref, tg_ref, act_ref, dy_ref, loss_ref, dup_ref, gff_ref,
             ext_ref, gv_ref, dact_ref, carry_ref):
        i = pl.program_id(0)

        @pl.when(i == 0)
        def _():
            gff_ref[...] = jnp.zeros_like(gff_ref)
            loss_ref[...] = jnp.zeros_like(loss_ref)
            carry_ref[...] = jnp.zeros_like(carry_ref)

        for j in range(tiles):
            ext_ref.at[j][_sp(0, hl)] = jnp.where(i < nt - 1, uh_ref[:, _lanes(j)], 0.0)
            for r0 in range(0, tm, FF_RB):
                ext_ref.at[j][_sp(hl + r0, FF_RB)] = up_ref[r0:r0 + FF_RB, _lanes(j)]
        for c in range(FF_TILES):
            for r0 in range(0, tm, FF_RB):
                rows = slice(r0, r0 + FF_RB)
                gate = _ff_conv(ext_ref, fw_ref, fb_ref, c, r0)
                val = _ff_conv(ext_ref, fw_ref, fb_ref, FF_TILES + c, r0)
                gv_ref[rows, _lanes(c)] = gate
                gv_ref[rows, _lanes(FF_TILES + c)] = val
                act_ref[rows, _lanes(c)] = (gate * _sigmoid(gate) * val).astype(BF16)
        err = x1_ref[...] + jnp.dot(act_ref[...], wd_ref[...], preferred_element_type=F32) - tg_ref[...]
        dy_ref[...] = err * (1.0 / D)
        loss_ref[...] += jnp.sum(err * err)
        dact_ref[...] = _nt(dy_ref[...].astype(BF16), wd_ref[...])

        for c in range(FF_TILES):
            for r0 in range(0, tm, FF_RB):
                rows = slice(r0, r0 + FF_RB)
                gate, val = gv_ref[rows, _lanes(c)], gv_ref[rows, _lanes(FF_TILES + c)]
                sg = _sigmoid(gate)
                da = dact_ref[rows, _lanes(c)]
                ext_ref.at[c][_sp(r0, FF_RB)] = da * val * (sg + gate * sg * (1.0 - sg))
                ext_ref.at[FF_TILES + c][_sp(r0, FF_RB)] = da * gate * sg
        fold = lambda a: jnp.sum(a.reshape(FF_RB // 8, 8, LANES), axis=0)
        for c in range(tiles):
            cols = _lanes(c)
            ext_ref.at[c][_sp(tm, hl)] = carry_ref[c]
            taps = [fw_ref[k:k + 1, cols] for k in range(FF_K)]
            acc = [jnp.zeros((8, LANES), F32) for _ in range(FF_K + 1)]
            for r0 in range(0, tm, FF_RB):
                shifted = [ext_ref.at[c][_sp(r0 + k, FF_RB)] for k in range(FF_K)]
                u = up_ref[r0:r0 + FF_RB, cols]
                dup = taps[2] * shifted[0] + taps[1] * shifted[1] + taps[0] * shifted[2]
                dup_ref[r0:r0 + FF_RB, cols] = dup.astype(BF16)
                for k in range(FF_K):
                    acc[2 - k] = acc[2 - k] + fold(shifted[k] * u)
                acc[FF_K] = acc[FF_K] + fold(shifted[0])
            for k in range(FF_K + 1):
                gff_ref[k:k + 1, cols] += jnp.sum(acc[k], axis=0, keepdims=True)
            carry_ref[c] = ext_ref.at[c][_sp(0, hl)]

    rev = lambda i: (nt - 1 - i, 0)
    row = pl.BlockSpec((tm, D), rev)
    wide = pl.BlockSpec((tm, NUP), rev)
    return _call(
        body, up, up, ffconv_w, ffconv_b, w_down, x1, target, name="ffn", grid=(nt,),
        in_specs=[wide, pl.BlockSpec((hl, NUP), lambda i: (jnp.maximum((nt - 1 - i) * per - 1, 0), 0)),
                  pl.BlockSpec((FF_K, NUP), lambda i: (0, 0)), pl.BlockSpec((1, NUP), lambda i: (0, 0)),
                  _resident((DFF, D)), row, row],
        out_specs=[pl.BlockSpec((tm, DFF), rev), row, pl.BlockSpec((8, 128), lambda i: (0, 0)), wide,
                   pl.BlockSpec((8, NUP), lambda i: (0, 0))],
        out_shape=[jax.ShapeDtypeStruct((T, DFF), BF16), jax.ShapeDtypeStruct((T, D), F32),
                   jax.ShapeDtypeStruct((8, 128), F32), jax.ShapeDtypeStruct((T, NUP), BF16),
                   jax.ShapeDtypeStruct((8, NUP), F32)],
        scratch_shapes=[pltpu.VMEM((tiles, 2 * (tm + hl), LANES), F32), pltpu.VMEM((tm, NUP), F32),
                        pltpu.VMEM((tm, DFF), F32), pltpu.VMEM((tiles, hl, LANES), F32)],
        compiler_params=_params(("arbitrary",), 58))


def _weight_grad(a, g, bm, bn, tk, name):
    m, n = a.shape[1], g.shape[1]
    nk = T // tk

    def body(a_ref, g_ref, of_ref, ob_ref):
        k = pl.program_id(2)

        @pl.when(k == 0)
        def _():
            of_ref[...] = jnp.zeros_like(of_ref)
        of_ref[...] += _tn_dot(a_ref[...].astype(BF16), g_ref[...].astype(BF16))

        @pl.when(k == nk - 1)
        def _():
            ob_ref[...] = of_ref[...].astype(BF16)

    out = pl.BlockSpec((bm, bn), lambda i, j, k: (i, j))
    return _call(
        body, a, g, name=name, grid=(m // bm, n // bn, nk),
        in_specs=[pl.BlockSpec((tk, bm), lambda i, j, k: (k, i)), pl.BlockSpec((tk, bn), lambda i, j, k: (k, j))],
        out_specs=[out, out],
        out_shape=[jax.ShapeDtypeStruct((m, n), F32), jax.ShapeDtypeStruct((m, n), BF16)],
        compiler_params=_params(("parallel", "parallel", "arbitrary"), 56))


def _norm_bwd_mm(dz, w, xin, base, gain, name, rider):
    kdim = dz.shape[1]
    tm = 512

    def body(dz_ref, w_ref, x_ref, b_ref, g_ref, dx_ref, gg_ref):
        @pl.when(pl.program_id(0) == 0)
        def _():
            gg_ref[...] = jnp.zeros_like(gg_ref)

        xv = x_ref[...]
        dh = _nt(dz_ref[...], w_ref[...])
        r = lax.rsqrt(jnp.mean(xv * xv, axis=-1, keepdims=True) + EPS)
        t = dh * g_ref[...]
        dx_ref[...] = b_ref[...] + r * t - xv * (r * r * r) * jnp.mean(t * xv, axis=-1, keepdims=True)
        gg_ref[...] += jnp.sum(dh * xv * r, axis=0, keepdims=True)

    row = pl.BlockSpec((tm, D), lambda i: (i, 0))
    vec = pl.BlockSpec((1, D), lambda i: (0, 0))
    return _call(
        body, dz, w, xin, base, gain, rider=rider, name=name, grid=(T // tm,),
        in_specs=[pl.BlockSpec((tm, kdim), lambda i: (i, 0)), _resident((D, kdim)), row, row, vec],
        out_specs=[row, vec],
        out_shape=[jax.ShapeDtypeStruct((T, D), F32), jax.ShapeDtypeStruct((1, D), F32)],
        compiler_params=_params(("arbitrary",), 48))


def _outproj_bwd(dx1, w_out, o_f32, bd):
    tm = 512

    def body(d_ref, w_ref, o_ref, bd_ref, dc_ref, dl_ref):
        dc_ref[...] = _nt(d_ref[...].astype(BF16), w_ref[...])
        dl_ref[...] = _segsum(dc_ref[:, C:2 * C] * o_ref[...], bd_ref[...])

    row = pl.BlockSpec((tm, D), lambda i: (i, 0))
    blk = pl.BlockSpec((tm, C), lambda i: (i, 0))
    return _call(
        body, dx1, w_out, o_f32, bd, name="outproj_bwd", grid=(T // tm,),
        in_specs=[row, _resident((D, D)), blk, _resident((C, C))], out_specs=[row, blk],
        out_shape=[jax.ShapeDtypeStruct((T, D), F32), jax.ShapeDtypeStruct((T, C), F32)],
        compiler_params=_params(("arbitrary",), 32))


def _conv_bwd(dcat, cv, proj, conv_w, cn_g, cn_b, rider):
    tm, hl, rb, cb = CONV_TM, CONV_HALO, CONV_RB, CONV_CB
    per = tm // hl
    nt = T // tm
    tiles = C // LANES

    def body(du_ref, dun_ref, cv_ref, cvn_ref, av_ref, ag_ref, hv_ref, hg_ref, w_ref, g_ref, bb_ref,
             dp_ref, gv_ref, gw_ref, dsh_ref, gsh_ref):
        i = pl.program_id(0)

        @pl.when(i == 0)
        def _():
            gv_ref[...] = jnp.zeros_like(gv_ref)
            gw_ref[...] = jnp.zeros_like(gw_ref)

        def ln_bwd(du, cvv):
            mu = jnp.mean(cvv, axis=-1, keepdims=True)
            xc = cvv - mu
            rs = lax.rsqrt(jnp.mean(xc * xc, axis=-1, keepdims=True) + EPS)
            xh = xc * rs
            ln = xh * g_ref[...] + bb_ref[...]
            sg = _sigmoid(ln)
            dln = du * (sg + ln * sg * (1.0 - sg))
            dxh = dln * g_ref[...]
            dcv = rs * (dxh - jnp.mean(dxh, axis=-1, keepdims=True)
                        - xh * jnp.mean(dxh * xh, axis=-1, keepdims=True))
            return dcv, dln, xh

        for r0 in range(0, tm, rb):
            dcv, dln, xh = ln_bwd(du_ref[r0:r0 + rb, :], cv_ref[r0:r0 + rb, :])
            for j in range(tiles):
                dsh_ref.at[j][_sp(r0, rb)] = dcv[:, _lanes(j)]
            gv_ref[0:1, :] += jnp.sum(dln * xh, axis=0, keepdims=True)
            gv_ref[1:2, :] += jnp.sum(dln, axis=0, keepdims=True)
            gv_ref[2:3, :] += jnp.sum(dcv, axis=0, keepdims=True)
        dcv_n, _, _ = ln_bwd(dun_ref[...], cvn_ref[...])
        dcv_n = jnp.where(i < nt - 1, dcv_n, 0.0)
        for j in range(tiles):
            ln_ = _lanes(j)
            dsh_ref.at[j][_sp(tm, hl)] = dcv_n[:, ln_]
            glu_h = hv_ref[:, ln_] * _sigmoid(hg_ref[:, ln_])
            gsh_ref.at[j][_sp(0, hl)] = jnp.where(i > 0, glu_h, 0.0)
            for r0 in range(0, tm, cb):
                gsh_ref.at[j][_sp(hl + r0, cb)] = av_ref[r0:r0 + cb, ln_] * _sigmoid(ag_ref[r0:r0 + cb, ln_])

        for j in range(tiles):
            ln_ = _lanes(j)
            for r0 in range(0, tm, cb):
                dglu = jnp.zeros((cb, LANES), F32)
                for k in range(CONV_K):
                    dglu = dglu + w_ref[k:k + 1, ln_] * dsh_ref.at[j][_sp(r0 + (CONV_K - 1) - k, cb)]
                av = av_ref[r0:r0 + cb, ln_]
                sg = _sigmoid(ag_ref[r0:r0 + cb, ln_])
                dp_ref[r0:r0 + cb, ln_] = (dglu * sg).astype(BF16)
                dp_ref[r0:r0 + cb, _lanes(tiles + j)] = (dglu * av * sg * (1.0 - sg)).astype(BF16)
            for k in range(CONV_K):
                part = jnp.zeros((8, LANES), F32)
                for r0 in range(0, tm, cb):
                    prod = dsh_ref.at[j][_sp(r0, cb)] * gsh_ref.at[j][_sp(r0 + hl - (CONV_K - 1) + k, cb)]
                    part = part + jnp.sum(prod.reshape(cb // 8, 8, LANES), axis=0)
                gw_ref[k:k + 1, ln_] += jnp.sum(part, axis=0, keepdims=True)

    main = lambda col: pl.BlockSpec((tm, C), lambda i: (i, col))
    prev = lambda col: pl.BlockSpec((hl, C), lambda i: (jnp.maximum(i * per - 1, 0), col))
    nxt = pl.BlockSpec((hl, C), lambda i: (jnp.minimum((i + 1) * per, T // hl - 1), 0))
    vec = pl.BlockSpec((1, C), lambda i: (0, 0))
    return _call(
        body, dcat, dcat, cv, cv, proj, proj, proj, proj, conv_w, cn_g, cn_b, rider=rider, name="conv_bwd",
        grid=(nt,),
        in_specs=[main(0), nxt, main(0), nxt, main(0), main(1), prev(0), prev(1),
                  pl.BlockSpec((CONV_K, C), lambda i: (0, 0)), vec, vec],
        out_specs=[pl.BlockSpec((tm, 2 * C), lambda i: (i, 0)), pl.BlockSpec((8, C), lambda i: (0, 0)),
                   pl.BlockSpec((32, C), lambda i: (0, 0))],
        out_shape=[jax.ShapeDtypeStruct((T, NPROJ), BF16), jax.ShapeDtypeStruct((8, C), F32),
                   jax.ShapeDtypeStruct((32, C), F32)],
        scratch_shapes=[pltpu.VMEM((C // LANES, 2 * (tm + hl), LANES), F32)] * 2,
        compiler_params=_params(("arbitrary",), 48))


def _attn_bwd_unit(qs, dos, lgs, dls, rows, kc, vc, biasv, invalid_prev):
    qst, dost = _stack_heads(qs[rows, :]), _stack_heads(dos[rows, :])
    s = _nt(qst, kc) + biasv
    if invalid_prev is not None:
        col = lax.broadcasted_iota(jnp.int32, s.shape, 1)
        s = jnp.where((col < QB) & invalid_prev, NEG, s)
    p = jnp.exp(s - _stack_cols(lgs[rows, :]))
    ds = p * (_nt(dost, vc) - _stack_cols(dls[rows, :]))
    dsb = ds.astype(BF16)
    dq = _unstack_heads(jnp.dot(dsb, kc, preferred_element_type=F32))
    return dq, _tn_dot(dsb, qst), _tn_dot(p.astype(BF16), dost)


def _attn_bwd_lagged(qn, kn, proj, dcat, lg, dl, bias, d, earlier):
    assert QB * d == ATT_WIN
    n_win = T // ATT_WIN
    lanes = 2 * HEAD

    def body(q_ref, k_ref, v_ref, do_ref, lg_ref, dl_ref, kh_ref, vh_ref, bias_ref, eq_ref, ek_ref, ev_ref,
             dq_ref, dk_ref, dv_ref, qs, dos, lgs, dls, ks, vs, dqs, ck, cv, ok, ov, tmp):
        n = pl.program_id(1)
        block = lambda buf: (lambda r: buf[r * QB:(r + 1) * QB, :])

        @pl.when(n == 0)
        def _():
            ck[...] = jnp.zeros_like(ck)
            cv[...] = jnp.zeros_like(cv)

        @pl.when(n < n_win)
        def _():
            for dst, per, at, src, dt in ((qs, QB, 0, q_ref, BF16), (dos, QB, 0, do_ref, BF16),
                                          (lgs, QB, 0, lg_ref, F32), (dls, QB, 0, dl_ref, F32),
                                          (ks, 2 * QB, 0, kh_ref, BF16), (ks, 2 * QB, QB, k_ref, BF16),
                                          (vs, 2 * QB, 0, vh_ref, BF16), (vs, 2 * QB, QB, v_ref, BF16)):
                stream = _gather_streams(src, tmp, d)
                for r in range(d):
                    dst[r * per + at:r * per + at + QB, :] = stream(r, QB).astype(dt)
            for r in range(d):
                rows = slice(r * QB, (r + 1) * QB)
                keys = slice(2 * r * QB, (2 * r + 2) * QB)
                dq, dkc, dvc = _attn_bwd_unit(qs, dos, lgs, dls, rows, ks[keys, :], vs[keys, :], bias_ref[...], n == 0)
                dqs[rows, :] = dq
                ok[rows, :] = ck[rows, :] + dkc[:QB]
                ov[rows, :] = cv[rows, :] + dvc[:QB]
                ck[rows, :] = dkc[QB:]
                cv[rows, :] = dvc[QB:]
            _scatter_streams(dq_ref, tmp, d, block(dqs), QB, eq_ref)
            _scatter_streams(dk_ref, tmp, d, block(ok), QB, ek_ref)
            _scatter_streams(dv_ref, tmp, d, block(ov), QB, ev_ref)

        @pl.when(n == n_win)
        def _():
            _scatter_streams(dk_ref, tmp, d, block(ck), QB, ek_ref)
            _scatter_streams(dv_ref, tmp, d, block(cv), QB, ev_ref)

    cur = lambda off: pl.BlockSpec((ATT_WIN, lanes), lambda cb, n: (jnp.minimum(n, n_win - 1), off + cb))
    prev = lambda off: pl.BlockSpec(
        (ATT_WIN, lanes), lambda cb, n: (jnp.maximum(jnp.minimum(n, n_win - 1) - 1, 0), off + cb))
    late = pl.BlockSpec((ATT_WIN, lanes), lambda cb, n: (jnp.maximum(n - 1, 0), cb))
    buf = lambda rows, dt: pltpu.VMEM((rows, lanes), dt)
    return _call(
        body, qn, kn, proj, dcat, lg, dl, kn, proj, bias, *earlier, name=f"attn_bwd_d{d}",
        grid=(C // lanes, n_win + 1),
        in_specs=[cur(0), cur(0), cur(V_COL), cur(DO_COL), cur(0), cur(0), prev(0), prev(V_COL),
                  pl.BlockSpec((None, 2 * QB, 2 * QB), lambda cb, n: (cb, 0, 0)), cur(0), late, late],
        out_specs=[cur(0), late, late],
        out_shape=[jax.ShapeDtypeStruct((T, C), F32)] * 3,
        scratch_shapes=[buf(ATT_WIN, BF16), buf(ATT_WIN, BF16), buf(ATT_WIN, F32), buf(ATT_WIN, F32),
                        buf(2 * ATT_WIN, BF16), buf(2 * ATT_WIN, BF16)] + [buf(ATT_WIN, F32)] * 6,
        compiler_params=_params(("arbitrary", "arbitrary"), 48))


def _attn_bwd(qn, kn, proj, dcat, lg, dl, bias, d, earlier=None, rider=None):
    sl, nb, hr = _attn_geometry(d)
    slk = QB + sl
    slq = sl + QB
    n_win = T // ATT_WIN

    def body(q_ref, k_ref, v_ref, do_ref, lg_ref, dl_ref, kh_ref, vh_ref, qx_ref, dox_ref, lgx_ref, dlx_ref,
             bias_ref, *rest):
        sums = rest[:3] if earlier is not None else (None, None, None)
        dq_ref, dk_ref, dv_ref, qs, dos, lgs, dls, ks, vs, dqs, dks, dvs = rest[-12:]
        n = pl.program_id(1)
        for r in range(d):
            for dst, src, nx, dt in ((qs, q_ref, qx_ref, BF16), (dos, do_ref, dox_ref, BF16),
                                     (lgs, lg_ref, lgx_ref, F32), (dls, dl_ref, dlx_ref, F32)):
                dst[r * slq:r * slq + sl, :] = _stream(src, r, sl, d).astype(dt)
                dst[r * slq + sl:(r + 1) * slq, :] = _stream(nx, r, QB, d).astype(dt)
            for dst, halo, src in ((ks, kh_ref, k_ref), (vs, vh_ref, v_ref)):
                dst[r * slk:r * slk + QB, :] = _stream(halo, r, QB, d).astype(BF16)
                dst[r * slk + QB:(r + 1) * slk, :] = _stream(src, r, sl, d).astype(BF16)
        dks[...] = jnp.zeros_like(dks)
        dvs[...] = jnp.zeros_like(dvs)

        def unit(rows, kc, vc, biasv, invalid_prev):
            return _attn_bwd_unit(qs, dos, lgs, dls, rows, kc, vc, biasv, invalid_prev)

        for r in range(d):
            for b in range(nb):
                rows = slice(r * slq + b * QB, r * slq + (b + 1) * QB)
                keys = slice(r * slk + b * QB, r * slk + (b + 2) * QB)
                dq, dkc, dvc = unit(rows, ks[keys, :], vs[keys, :], bias_ref[...], (n == 0) if b == 0 else None)
                dqs[r * sl + b * QB:r * sl + (b + 1) * QB, :] = dq
                if b == 0:
                    dks[r * sl:r * sl + QB, :] += dkc[QB:]
                    dvs[r * sl:r * sl + QB, :] += dvc[QB:]
                else:
                    dks[r * sl + (b - 1) * QB:r * sl + (b + 1) * QB, :] += dkc
                    dvs[r * sl + (b - 1) * QB:r * sl + (b + 1) * QB, :] += dvc

        @pl.when(n < n_win - 1)
        def _():
            for r in range(d):
                rows = slice(r * slq + sl, (r + 1) * slq)
                keys = slice(r * slk + sl, (r + 1) * slk)
                _, dkc, dvc = unit(rows, ks[keys, :], vs[keys, :], bias_ref[:, 0:QB], None)
                dks[(r + 1) * sl - QB:(r + 1) * sl, :] += dkc
                dvs[(r + 1) * sl - QB:(r + 1) * sl, :] += dvc

        for dst, src, before in zip((dq_ref, dk_ref, dv_ref), (dqs, dks, dvs), sums):
            for r in range(d):
                pos = (pl.ds(r, sl, stride=d) if d > 1 else slice(None), slice(None))
                val = src[r * sl:(r + 1) * sl, :]
                dst[pos] = val if before is None else val + before[pos]

    main, prev, nxt, bias_spec = _attn_specs(d)
    lanes = 2 * HEAD
    return _call(
        body, qn, kn, proj, dcat, lg, dl, kn, proj, qn, dcat, lg, dl, bias, *(earlier or ()), rider=rider,
        name=f"attn_bwd_d{d}", grid=(C // lanes, n_win),
        in_specs=[main(0), main(0), main(V_COL), main(DO_COL), main(0), main(0), prev(0), prev(V_COL),
                  nxt(0), nxt(DO_COL), nxt(0), nxt(0), bias_spec] + ([main(0)] * 3 if earlier is not None else []),
        out_specs=[main(0)] * 3,
        out_shape=[jax.ShapeDtypeStruct((T, C), F32)] * 3,
        scratch_shapes=[pltpu.VMEM((ATT_WIN + hr, lanes), BF16), pltpu.VMEM((ATT_WIN + hr, lanes), BF16),
                        pltpu.VMEM((ATT_WIN + hr, lanes), F32), pltpu.VMEM((ATT_WIN + hr, lanes), F32),
                        pltpu.VMEM((ATT_WIN + hr, lanes), BF16), pltpu.VMEM((ATT_WIN + hr, lanes), BF16),
                        pltpu.VMEM((ATT_WIN, lanes), F32), pltpu.VMEM((ATT_WIN, lanes), F32),
                        pltpu.VMEM((ATT_WIN, lanes), F32)],
        compiler_params=_params(("arbitrary", "arbitrary"), 48))


def _qk_norm_bwd(dn_sum, proj, col, gain, bd, dproj, name):
    tm = 512

    def body(d0, x_ref, g_ref, bd_ref, dp_in, dp_ref, gg_ref):
        del dp_in

        @pl.when(pl.program_id(0) == 0)
        def _():
            gg_ref[...] = jnp.zeros_like(gg_ref)
        dn = d0[...]
        xv = x_ref[...]
        r = lax.rsqrt(_segsum(xv * xv, bd_ref[...]) * (1.0 / HEAD) + EPS)
        t = dn * g_ref[...]
        mean_tx = _segsum(t * xv, bd_ref[...]) * (1.0 / HEAD)
        dp_ref[...] = (r * t - xv * (r * r * r) * mean_tx).astype(BF16)
        gg_ref[...] += jnp.sum(dn * xv * r, axis=0, keepdims=True)

    blk = pl.BlockSpec((tm, C), lambda i: (i, 0))
    vec = pl.BlockSpec((1, C), lambda i: (0, 0))
    return _call(
        body, dn_sum, proj, gain, bd, dproj, name=name, grid=(T // tm,),
        in_specs=[blk, pl.BlockSpec((tm, C), lambda i: (i, col)), vec, _resident((C, C)), ANY],
        out_specs=[pl.BlockSpec((tm, C), lambda i: (i, col)), vec],
        out_shape=[jax.ShapeDtypeStruct((T, NPROJ), BF16), jax.ShapeDtypeStruct((1, C), F32)],
        input_output_aliases={4: 0},
        compiler_params=_params(("arbitrary",), 32))


def _v_bwd(dv_sum, dproj):
    tm = 512

    def body(d0, dp_in, dp_ref):
        del dp_in
        dp_ref[...] = d0[...].astype(BF16)

    blk = pl.BlockSpec((tm, C), lambda i: (i, 0))
    return _call(
        body, dv_sum, dproj, name="v_bwd", grid=(T // tm,),
        in_specs=[blk, ANY],
        out_specs=[pl.BlockSpec((tm, C), lambda i: (i, 4))],
        out_shape=[jax.ShapeDtypeStruct((T, NPROJ), BF16)],
        input_output_aliases={1: 0},
        compiler_params=_params(("parallel",), 32))[0]


def _adamw(w, g, m, v):
    m = ADAM_B1 * m + (1.0 - ADAM_B1) * g
    v = ADAM_B2 * v + (1.0 - ADAM_B2) * (g * g)
    m_hat = m / (1.0 - ADAM_B1 ** ADAM_STEP)
    v_hat = v / (1.0 - ADAM_B2 ** ADAM_STEP)
    delta = -ADAM_LR * (m_hat / (jnp.sqrt(v_hat) + ADAM_EPS) + ADAM_WD * w)
    return delta, m, v


def _row_block(shape):
    rows = shape[0]
    for cand in (256, 128, 64, 88, 32, 8):
        if rows % cand == 0 and cand * shape[1] * 4 <= (2 << 20):
            return cand
    return 8


def _partial_sum(own, recv, name):
    br = _row_block(own.shape)
    cols = own.shape[1]

    def body(o_ref, r_ref, p_ref):
        p_ref[...] = ((o_ref[...] + r_ref[0].astype(F32)) + r_ref[1].astype(F32)) + r_ref[2].astype(F32)

    blk = pl.BlockSpec((br, cols), lambda i: (i, 0))
    return _call(
        body, own, recv, name=name, grid=(own.shape[0] // br,),
        in_specs=[blk, pl.BlockSpec((3, br, cols), lambda i: (0, i, 0))], out_specs=[blk],
        out_shape=[jax.ShapeDtypeStruct(own.shape, F32)],
        compiler_params=_params(("parallel",), 32))[0]


def _adamw_mat(p_own, p_sib, w, m, v, name):
    br = _row_block(w.shape)
    cols = w.shape[1]

    def body(a_ref, b_ref, w_ref, m_ref, v_ref, g_ref, d_ref, nm_ref, nv_ref):
        g = a_ref[...] + b_ref[...]
        delta, nm, nv = _adamw(w_ref[...], g, m_ref[...], v_ref[...])
        g_ref[...] = g
        d_ref[...] = delta
        nm_ref[...] = nm
        nv_ref[...] = nv

    blk = pl.BlockSpec((br, cols), lambda i: (i, 0))
    return _call(
        body, p_own, p_sib, w, m, v, name=name, grid=(w.shape[0] // br,),
        in_specs=[blk] * 5, out_specs=[blk] * 4,
        out_shape=[jax.ShapeDtypeStruct(w.shape, F32)] * 4,
        compiler_params=_params(("parallel",), 40))


def _vec_reduce(vrecv):
    def body(v_ref, o_ref):
        acc = v_ref[0]
        for r in range(1, N_DEV):
            acc = acc + v_ref[r]
        o_ref[...] = acc

    return pl.pallas_call(
        body, name="vec_reduce",
        out_shape=jax.ShapeDtypeStruct((VPACK_ROWS, D), F32),
        compiler_params=_params((), 32),
    )(vrecv)


def _adamw_small(w, g, m, v):
    def body(w_ref, g_ref, m_ref, v_ref, d_ref, nm_ref, nv_ref):
        delta, nm, nv = _adamw(w_ref[...], g_ref[...], m_ref[...], v_ref[...])
        d_ref[...] = delta
        nm_ref[...] = nm
        nv_ref[...] = nv

    return pl.pallas_call(
        body, name="adamw_small",
        out_shape=[jax.ShapeDtypeStruct(w.shape, F32)] * 3,
        compiler_params=_params((), 32),
    )(w, g, m, v)


def _pack(parts, rows):
    flat = jnp.concatenate([p.reshape(-1) for p in parts])
    return jnp.pad(flat, (0, rows * D - flat.shape[0])).reshape(rows, D)


def _unpack(packed, shapes):
    flat = packed.reshape(-1)
    out, off = [], 0
    for shp in shapes:
        size = 1
        for s in shp:
            size *= s
        out.append(flat[off:off + size].reshape(shp))
        off += size
    return out


def _no_comm(shards, row_sharded, peers=(0, 1, 2), into=None):
    del row_sharded, peers, into
    return None, lambda res, n: (res, shards)


def _with_comm(shards, row_sharded, peers=(0, 1, 2), into=None):
    rider = _gather_rider(shards, row_sharded, peers, into)
    return rider, lambda res, n: (res[:n], res[n:])


def _local_step(x, target, norm1_g, conv_b, cn_g, cn_b, q_norm_g, k_norm_g, norm2_g, ffconv_b,
                first_weights, late_weights, comm=True):
    row = lambda a: a.reshape(1, -1)
    head_of = jnp.arange(C) // HEAD
    bd = (head_of[:, None] == head_of[None, :]).astype(BF16)
    qg = row(jnp.tile(q_norm_g, C // HEAD) * (HEAD ** -0.5))
    kg = row(jnp.tile(k_norm_g, C // HEAD))
    biases = [_alibi_tables(d) for d in PATTERN_DILATIONS]
    gather = _with_comm if comm else _no_comm
    grad_rider = (lambda g, rs: _grad_rider(g[1], g[0], rs)) if comm else (lambda g, rs: None)

    rider, split = gather(first_weights, (False, False, False))
    (h,), (w_in, conv_w, ffconv_w) = split(_norm_fwd(x, row(norm1_g), rider), 1)
    rider, split = gather(late_weights[0:1], (True,))
    (proj, qn, kn), (w_out,) = split(_proj_fwd(h, w_in, qg, kg, bd, rider), 3)
    rider, split = gather(late_weights[1:2], (False,), (0, 1))
    (cat, cv), w_up_part = split(_conv_fwd(proj, conv_w, row(conv_b), row(cn_g), row(cn_b), rider), 2)
    fwd = [_attn_fwd(qn, kn, proj, biases[i], d) for i, d in enumerate(PATTERN_DILATIONS[:-1])]
    rider, split = gather(late_weights[1:2], (False,), (2,), w_up_part)
    merge = (fwd[0][0], fwd[0][1], fwd[1][0], fwd[1][1], cat)
    (cat, o_f32, lg), (w_up,) = split(
        _attn_fwd(qn, kn, proj, biases[-1], PATTERN_DILATIONS[-1], rider, merge), 3)
    rider, split = gather(late_weights[2:3], (True,))
    (x1, h2, up), (w_down,) = split(_out_up(x, cat, w_out, row(norm2_g), w_up, rider), 3)
    act, dy, loss_acc, dup, gff = _ffn(up, ffconv_w, row(ffconv_b), w_down, x1, target)
    gw_down = _weight_grad(act, dy, DFF // 2, D, 1024, "grad_w_down")
    res = _norm_bwd_mm(dup, w_up, x1, dy, row(norm2_g), "up_bwd", grad_rider(gw_down, True))
    (dx1, g_norm2), ex_down = res[:2], res[2:]
    gw_up = _weight_grad(h2, dup, D, NUP // 4, 2048, "grad_w_up")
    dcat, dl = _outproj_bwd(dx1, w_out, o_f32, bd)
    gw_out = _weight_grad(cat, dx1, D, D, 2048, "grad_w_out")
    res = _conv_bwd(dcat, cv, proj, conv_w, row(cn_g), row(cn_b), grad_rider(gw_up, False))
    (dproj, gconv_vec, gconv_w), ex_up = res[:3], res[3:]
    sums, ex_out = None, []
    for i, d in enumerate(PATTERN_DILATIONS):
        if QB * d == ATT_WIN:
            res = _attn_bwd_lagged(qn, kn, proj, dcat, lg, dl, biases[i], d, sums)
        else:
            res = _attn_bwd(qn, kn, proj, dcat, lg, dl, biases[i], d, sums,
                            grad_rider(gw_out, True) if i == 0 else None)
        sums = res[:3]
        ex_out = res[3:] if i == 0 else ex_out
    dproj, gq_lane = _qk_norm_bwd(sums[0], proj, 2, qg, bd, dproj, "q_norm_bwd")
    dproj, gk_lane = _qk_norm_bwd(sums[1], proj, 3, kg, bd, dproj, "k_norm_bwd")
    dproj = _v_bwd(sums[2], dproj)
    gw_in = _weight_grad(h, dproj, D, NPROJ // 4, 2048, "grad_w_in")
    res = _norm_bwd_mm(dproj, w_in, x, dx1, row(norm1_g), "in_bwd", grad_rider(gw_in, False))
    (dx, g_norm1), ex_in = res[:2], res[2:]

    loss = loss_acc[0, 0] * (0.5 / D)
    g_qg = jnp.sum(gq_lane.reshape(C // HEAD, HEAD), axis=0) * (HEAD ** -0.5)
    g_kg = jnp.sum(gk_lane.reshape(C // HEAD, HEAD), axis=0)
    small = [g_norm1[0], gconv_vec[2], gconv_vec[0], gconv_vec[1], g_qg, g_kg, g_norm2[0], gff[3],
             gconv_w[:CONV_K], gff[:FF_K]]
    mats = [ex_in, ex_out, ex_up, ex_down] if comm else [gw_in, gw_out, gw_up, gw_down]
    return loss, dx, mats, small


def kernel(x, norm1_g, w_in, conv_w, conv_b, cn_g, cn_b, q_norm_g, k_norm_g, w_out, norm2_g, w_up, ffconv_w, ffconv_b, w_down, loss_target, m_norm1_g, m_w_in, m_conv_w, m_conv_b, m_cn_g, m_cn_b, m_q_norm_g, m_k_norm_g, m_w_out, m_norm2_g, m_w_up, m_ffconv_w, m_ffconv_b, m_w_down, v_norm1_g, v_w_in, v_conv_w, v_conv_b, v_cn_g, v_cn_b, v_q_norm_g, v_k_norm_g, v_w_out, v_norm2_g, v_w_up, v_ffconv_w, v_ffconv_b, v_w_down):
    chip = 2 * lax.axis_index("x") + lax.axis_index("y")

    loss, dx, mats, small = _local_step(
        x[0], loss_target[0], norm1_g, conv_b, cn_g, cn_b, q_norm_g, k_norm_g, norm2_g, ffconv_b,
        [w_in.astype(BF16), conv_w, ffconv_w], [w.astype(BF16) for w in (w_out, w_up, w_down)])

    names = ("w_in", "w_out", "w_up", "w_down")
    parts = [_partial_sum(own, recv, "partial_" + names[k]) for k, (recv, own) in enumerate(mats)]
    sib, vrecv = _final_exchange(parts, _pack(small + [loss.reshape(1)], VPACK_ROWS))
    ws = (w_in, w_out, w_up, w_down)
    ms = (m_w_in, m_w_out, m_w_up, m_w_down)
    vs = (v_w_in, v_w_out, v_w_up, v_w_down)
    mat = [_adamw_mat(parts[k], sib[k], ws[k], ms[k], vs[k], "adamw_" + names[k]) for k in range(4)]

    vsum = _vec_reduce(vrecv)
    vec_shapes = [(D,), (C,), (C,), (C,), (HEAD,), (HEAD,), (D,), (NUP,), (CONV_K, C), (FF_K, NUP), (1,)]
    gsmall = _unpack(vsum, vec_shapes)
    g_conv_w = lax.dynamic_slice_in_dim(gsmall[8], chip * (C // N_CHIPS), C // N_CHIPS, axis=1)
    g_ffconv_w = lax.dynamic_slice_in_dim(gsmall[9], chip * (NUP // N_CHIPS), NUP // N_CHIPS, axis=1)
    gs = gsmall[:8] + [g_conv_w, g_ffconv_w]
    w_s = [norm1_g, conv_b, cn_g, cn_b, q_norm_g, k_norm_g, norm2_g, ffconv_b, conv_w, ffconv_w]
    m_s = [m_norm1_g, m_conv_b, m_cn_g, m_cn_b, m_q_norm_g, m_k_norm_g, m_norm2_g, m_ffconv_b, m_conv_w, m_ffconv_w]
    v_s = [v_norm1_g, v_conv_b, v_cn_g, v_cn_b, v_q_norm_g, v_k_norm_g, v_norm2_g, v_ffconv_b, v_conv_w, v_ffconv_w]
    shapes_s = [a.shape for a in w_s]
    d_p, m_p, v_p = _adamw_small(_pack(w_s, SPACK_ROWS), _pack(gs, SPACK_ROWS), _pack(m_s, SPACK_ROWS),
                                 _pack(v_s, SPACK_ROWS))
    d_s, nm_s, nv_s = _unpack(d_p, shapes_s), _unpack(m_p, shapes_s), _unpack(v_p, shapes_s)

    def ordered(sm, mt):
        return [sm[0], mt[0], sm[8], sm[1], sm[2], sm[3], sm[4], sm[5], mt[1], sm[6], mt[2], sm[9], sm[7], mt[3]]

    loss_all = gsmall[10][0]
    grads = ordered(gs, [r[0] for r in mat])
    deltas = ordered(d_s, [r[1] for r in mat])
    new_m = ordered(nm_s, [r[2] for r in mat])
    new_v = ordered(nv_s, [r[3] for r in mat])
    return (loss_all, dx[None], *grads, *deltas, *new_m, *new_v)
```

```python
import types

import jax
import jax.numpy as jnp
from jax import lax
from jax.experimental import pallas as pl
from jax.experimental.pallas import tpu as pltpu

T = 8192
D = 1024
C = 512
NPROJ = 2560
DFF = 2816
NUP = 2 * DFF
CONV_K = 31
FF_K = 3
HEAD = 64
EPS = 1e-6
NEG = -1e30
N_CHIPS = 4
N_DEV = 8
PATTERN_DILATIONS = (1, 4, 16)
QB = 128

ADAM_LR = 0.001
ADAM_B1 = 0.9
ADAM_B2 = 0.999
ADAM_EPS = 1e-08
ADAM_WD = 0.01
ADAM_STEP = 10

F32 = jnp.float32
BF16 = jnp.bfloat16
MESH = pl.DeviceIdType.MESH
ANY = pl.BlockSpec(memory_space=pl.ANY)

VPACK_ROWS = 48
SPACK_ROWS = 24


def _params(sem, vmem_mb):
    return pltpu.CompilerParams(dimension_semantics=sem, vmem_limit_bytes=vmem_mb << 20)


def _resident(shape):
    return pl.BlockSpec(shape, lambda i: (0, 0), pipeline_mode=pl.Buffered(1))


def _nt(a, b):
    return lax.dot_general(a, b, (((1,), (1,)), ((), ())), preferred_element_type=F32)


def _tn_dot(a, b):
    return lax.dot_general(a, b, (((0,), (0,)), ((), ())), preferred_element_type=F32)


def _sigmoid(x):
    return 1.0 / (1.0 + jnp.exp(-x))


def _segsum(x, bd):
    hi = x.astype(BF16)
    lo = (x - hi.astype(F32)).astype(BF16)
    return (jnp.dot(hi, bd, preferred_element_type=F32)
            + jnp.dot(lo, bd, preferred_element_type=F32))


def _place():
    x, y, c = lax.axis_index("x"), lax.axis_index("y"), lax.axis_index("c")
    chips = [(1 - x, y), (x, 1 - y), (1 - x, 1 - y)]
    return x, y, c, chips


def _block_of(ref, shard_shape, row_sharded, s):
    r, cdim = shard_shape
    if row_sharded:
        return ref.at[pl.ds(s * r, r), :]
    return ref.at[:, pl.ds(s * cdim, cdim)]


def _full_shape(shard_shape, row_sharded):
    r, cdim = shard_shape
    return (r * N_CHIPS, cdim) if row_sharded else (r, cdim * N_CHIPS)


def _gather_rider(shards, row_sharded, peers=(0, 1, 2), into=None):
    n = len(shards)
    shapes = [a.shape for a in shards]

    def copies(ins, outs, sems):
        send_sems, recv_sems, local_sems = sems
        x, y, c, chips = _place()
        me = 2 * x + y
        place = lambda k, s: _block_of(outs[k], shapes[k], row_sharded[k], s)
        local = []
        if into is None:
            local = [pltpu.make_async_copy(ins[k], place(k, me), local_sems.at[k]) for k in range(n)]
        sends, recvs = [], []
        for k in range(n):
            for j in peers:
                px, py = chips[j]
                sem = dict(send_sem=send_sems.at[3 * k + j], recv_sem=recv_sems.at[3 * k + j],
                           device_id=(px, py, c), device_id_type=MESH)
                sends.append(pltpu.make_async_remote_copy(src_ref=ins[k], dst_ref=place(k, me), **sem))
                recvs.append(pltpu.make_async_remote_copy(src_ref=ins[k], dst_ref=place(k, 2 * px + py), **sem))
        return local, sends, recvs

    return types.SimpleNamespace(
        operands=list(shards) + list(into or []), copies=copies,
        aliases={n + k: k for k in range(n)} if into is not None else {},
        out_shape=[jax.ShapeDtypeStruct(_full_shape(s, rs), a.dtype) for s, rs, a in zip(shapes, row_sharded, shards)],
        sems=[pltpu.SemaphoreType.DMA((3 * n,)), pltpu.SemaphoreType.DMA((3 * n,)), pltpu.SemaphoreType.DMA((n,))])


def _grad_rider(g_bf16, g_f32, row_sharded):
    shard = tuple(d // N_CHIPS if (i == 0) == row_sharded else d for i, d in enumerate(g_f32.shape))

    def copies(ins, outs, sems):
        send_sems, recv_sems, local_sems = sems
        gb, gf = ins
        rec, own = outs
        x, y, c, chips = _place()
        me = 2 * x + y
        local = [pltpu.make_async_copy(_block_of(gf, shard, row_sharded, me), own, local_sems.at[0])]
        sends, recvs = [], []
        for j, (px, py) in enumerate(chips):
            sem = dict(send_sem=send_sems.at[j], recv_sem=recv_sems.at[j], device_id=(px, py, c), device_id_type=MESH)
            sends.append(pltpu.make_async_remote_copy(
                src_ref=_block_of(gb, shard, row_sharded, 2 * px + py), dst_ref=rec.at[j], **sem))
            recvs.append(pltpu.make_async_remote_copy(
                src_ref=_block_of(gb, shard, row_sharded, me), dst_ref=rec.at[j], **sem))
        return local, sends, recvs

    return types.SimpleNamespace(
        operands=[g_bf16, g_f32], copies=copies, aliases={},
        out_shape=[jax.ShapeDtypeStruct((3,) + shard, BF16), jax.ShapeDtypeStruct(shard, F32)],
        sems=[pltpu.SemaphoreType.DMA((3,)), pltpu.SemaphoreType.DMA((3,)), pltpu.SemaphoreType.DMA((1,))])


def _rider_start(rider, ins, outs, sems):
    local, sends, _ = rider.copies(ins, outs, sems)
    for cp in local + sends:
        cp.start()


def _rider_wait(rider, ins, outs, sems):
    local, sends, recvs = rider.copies(ins, outs, sems)
    for cp in recvs:
        cp.wait_recv()
    for cp in sends:
        cp.wait_send()
    for cp in local:
        cp.wait()


PIN_BYTES = 1 << 20


def _in_hbm(a):
    if a.size * a.dtype.itemsize < PIN_BYTES:
        return a
    return pltpu.with_memory_space_constraint(a, pltpu.HBM)


def _call(body, *operands, rider=None, name, grid, in_specs, out_specs, out_shape, scratch_shapes=(),
          compiler_params, input_output_aliases=None):
    operands = [_in_hbm(a) for a in operands]
    if rider is None:
        return pl.pallas_call(
            body, name=name, grid=grid, in_specs=list(in_specs), out_specs=list(out_specs), out_shape=list(out_shape),
            scratch_shapes=list(scratch_shapes), compiler_params=compiler_params,
            input_output_aliases=input_output_aliases or {})(*operands)
    n_in, n_out, n_scr = len(in_specs), len(out_specs), len(scratch_shapes)
    r_in, r_out = len(rider.operands), len(rider.out_shape)

    def riding(*refs):
        refs = list(refs)
        ins, refs = refs[:n_in], refs[n_in:]
        r_ins, refs = refs[:r_in], refs[r_in:]
        outs, refs = refs[:n_out], refs[n_out:]
        r_outs, refs = refs[:r_out], refs[r_out:]
        scr, sems = refs[:n_scr], refs[n_scr:]
        first = pl.program_id(0) == 0
        last = pl.program_id(0) == grid[0] - 1
        for axis in range(1, len(grid)):
            first = first & (pl.program_id(axis) == 0)
            last = last & (pl.program_id(axis) == grid[axis] - 1)

        @pl.when(first)
        def _():
            _rider_start(rider, r_ins, r_outs, sems)

        body(*ins, *outs, *scr)

        @pl.when(last)
        def _():
            _rider_wait(rider, r_ins, r_outs, sems)

    return pl.pallas_call(
        riding, name=name, grid=grid, in_specs=list(in_specs) + [ANY] * r_in,
        out_specs=list(out_specs) + [ANY] * r_out, out_shape=list(out_shape) + list(rider.out_shape),
        scratch_shapes=list(scratch_shapes) + list(rider.sems), compiler_params=compiler_params,
        input_output_aliases={**(input_output_aliases or {}),
                              **{n_in + i: n_out + o for i, o in rider.aliases.items()}})(
            *operands, *[_in_hbm(a) for a in rider.operands])


def _final_exchange(parts, vpack):
    def body(p0, p1, p2, p3, v_ref, o0, o1, o2, o3, vr_ref, send_sems, recv_sems, vsend_sems, vrecv_sems, local_sem):
        x, y, c, _ = _place()
        me = 4 * x + 2 * y + c
        mine = pltpu.make_async_copy(v_ref, vr_ref.at[me], local_sem)
        mine.start()
        copies = [pltpu.make_async_remote_copy(
            src_ref=p, dst_ref=o, send_sem=send_sems.at[k], recv_sem=recv_sems.at[k],
            device_id=(x, y, 1 - c), device_id_type=MESH)
            for k, (p, o) in enumerate(zip((p0, p1, p2, p3), (o0, o1, o2, o3)))]
        flips = [(fx, fy, fc) for fx in (0, 1) for fy in (0, 1) for fc in (0, 1)][1:]
        recvs = []
        for r, (fx, fy, fc) in enumerate(flips):
            peer = (x ^ fx, y ^ fy, c ^ fc)
            sem = dict(send_sem=vsend_sems.at[r], recv_sem=vrecv_sems.at[r], device_id=peer, device_id_type=MESH)
            copies.append(pltpu.make_async_remote_copy(src_ref=v_ref, dst_ref=vr_ref.at[me], **sem))
            recvs.append(pltpu.make_async_remote_copy(
                src_ref=v_ref, dst_ref=vr_ref.at[4 * peer[0] + 2 * peer[1] + peer[2]], **sem))
        for cp in copies:
            cp.start()
        for cp in copies[:4]:
            cp.wait_recv()
        for cp in recvs:
            cp.wait_recv()
        for cp in copies:
            cp.wait_send()
        mine.wait()

    res = pl.pallas_call(
        body, name="final_exchange",
        out_shape=[jax.ShapeDtypeStruct(p.shape, F32) for p in parts]
        + [jax.ShapeDtypeStruct((N_DEV, VPACK_ROWS, D), F32)],
        in_specs=[ANY] * 5, out_specs=[ANY] * 5,
        scratch_shapes=[pltpu.SemaphoreType.DMA((4,)), pltpu.SemaphoreType.DMA((4,)),
                        pltpu.SemaphoreType.DMA((7,)), pltpu.SemaphoreType.DMA((7,)), pltpu.SemaphoreType.DMA],
    )(*parts, vpack)
    return res[:4], res[4]


def _norm_fwd(x, g1, rider):
    tm = 512

    def body(x_ref, g_ref, h_ref):
        xv = x_ref[...]
        r = lax.rsqrt(jnp.mean(xv * xv, axis=-1, keepdims=True) + EPS)
        h_ref[...] = (xv * r * g_ref[...]).astype(BF16)

    row = pl.BlockSpec((tm, D), lambda i: (i, 0))
    return _call(
        body, x, g1, rider=rider, name="norm_fwd", grid=(T // tm,),
        in_specs=[row, pl.BlockSpec((1, D), lambda i: (0, 0))], out_specs=[row],
        out_shape=[jax.ShapeDtypeStruct((T, D), BF16)],
        compiler_params=_params(("arbitrary",), 32))


def _proj_fwd(h, w_in, qg, kg, bd, rider):
    tm, tn = 512, 640

    def body(h_ref, w_ref, qg_ref, kg_ref, bd_ref, p_ref, qn_ref, kn_ref):
        for j in range(NPROJ // tn):
            cols = slice(j * tn, (j + 1) * tn)
            p_ref[:, cols] = jnp.dot(h_ref[...], w_ref[:, cols], preferred_element_type=F32)
        for col, g, dst in ((2, qg_ref, qn_ref), (3, kg_ref, kn_ref)):
            xv = p_ref[:, col * C:(col + 1) * C]
            ms = _segsum(xv * xv, bd_ref[...]) * (1.0 / HEAD)
            dst[...] = xv * lax.rsqrt(ms + EPS) * g[...]

    vec = pl.BlockSpec((1, C), lambda i: (0, 0))
    blk = pl.BlockSpec((tm, C), lambda i: (i, 0))
    return _call(
        body, h, w_in, qg, kg, bd, rider=rider, name="proj_fwd", grid=(T // tm,),
        in_specs=[pl.BlockSpec((tm, D), lambda i: (i, 0)), _resident((D, NPROJ)), vec, vec, _resident((C, C))],
        out_specs=[pl.BlockSpec((tm, NPROJ), lambda i: (i, 0)), blk, blk],
        out_shape=[jax.ShapeDtypeStruct((T, NPROJ), F32), jax.ShapeDtypeStruct((T, C), F32),
                   jax.ShapeDtypeStruct((T, C), F32)],
        compiler_params=_params(("arbitrary",), 40))


CONV_TM = 512
CONV_HALO = 32
CONV_RB = 32
CONV_CB = 64


LANES = 128


def _sp(start, n):
    return (pl.ds(2 * start, n, stride=2), slice(None))


def _lanes(tile):
    return slice(tile * LANES, (tile + 1) * LANES)


def _conv_fwd(proj, conv_w, conv_b, cn_g, cn_b, rider):
    tm, hl, rb, cb = CONV_TM, CONV_HALO, CONV_RB, CONV_CB
    per = tm // hl

    def body(av_ref, ag_ref, hv_ref, hg_ref, w_ref, b_ref, g_ref, bb_ref, cat_ref, cv_ref, sh_ref):
        i = pl.program_id(0)
        for j in range(C // LANES):
            ln_ = _lanes(j)
            glu_h = hv_ref[:, ln_] * _sigmoid(hg_ref[:, ln_])
            sh_ref.at[j][_sp(0, hl)] = jnp.where(i > 0, glu_h, 0.0)
            for r0 in range(0, tm, cb):
                sh_ref.at[j][_sp(hl + r0, cb)] = av_ref[r0:r0 + cb, ln_] * _sigmoid(ag_ref[r0:r0 + cb, ln_])
            for r0 in range(0, tm, cb):
                acc = jnp.zeros((cb, LANES), F32) + b_ref[:, ln_]
                for k in range(CONV_K):
                    acc = acc + w_ref[k:k + 1, ln_] * sh_ref.at[j][_sp(r0 + hl - (CONV_K - 1) + k, cb)]
                cv_ref[r0:r0 + cb, ln_] = acc
        for r0 in range(0, tm, rb):
            acc = cv_ref[r0:r0 + rb, :]
            mu = jnp.mean(acc, axis=-1, keepdims=True)
            xc = acc - mu
            var = jnp.mean(xc * xc, axis=-1, keepdims=True)
            ln = xc * lax.rsqrt(var + EPS) * g_ref[...] + bb_ref[...]
            cat_ref[r0:r0 + rb, :] = (ln * _sigmoid(ln)).astype(BF16)

    halo = lambda col: pl.BlockSpec((hl, C), lambda i: (jnp.maximum(i * per - 1, 0), col))
    vec = pl.BlockSpec((1, C), lambda i: (0, 0))
    return _call(
        body, proj, proj, proj, proj, conv_w, conv_b, cn_g, cn_b, rider=rider, name="conv_fwd", grid=(T // tm,),
        in_specs=[pl.BlockSpec((tm, C), lambda i: (i, 0)), pl.BlockSpec((tm, C), lambda i: (i, 1)),
                  halo(0), halo(1), pl.BlockSpec((CONV_K, C), lambda i: (0, 0)), vec, vec, vec],
        out_specs=[pl.BlockSpec((tm, C), lambda i: (i, 0)), pl.BlockSpec((tm, C), lambda i: (i, 0))],
        out_shape=[jax.ShapeDtypeStruct((T, D), BF16), jax.ShapeDtypeStruct((T, C), F32)],
        scratch_shapes=[pltpu.VMEM((C // LANES, 2 * (tm + hl), LANES), F32)],
        compiler_params=_params(("arbitrary",), 40))


def _stack_heads(a):
    lane = lax.broadcasted_iota(jnp.int32, a.shape, 1)
    zero = jnp.zeros_like(a)
    return jnp.concatenate([jnp.where(lane < HEAD, a, zero), jnp.where(lane >= HEAD, a, zero)], axis=0)


def _unstack_heads(a2):
    lane = lax.broadcasted_iota(jnp.int32, (QB, 2 * HEAD), 1)
    return jnp.where(lane < HEAD, a2[:QB], a2[QB:])


def _stack_cols(a):
    return jnp.concatenate([a[:, 0:1], a[:, HEAD:HEAD + 1]], axis=0)


ATT_WIN = 2048
V_COL = 4 * C // (2 * HEAD)
DO_COL = C // (2 * HEAD)


def _attn_geometry(d):
    sl = ATT_WIN // d
    return sl, sl // QB, QB * d


SPLIT = 4
PIECE = 128


def _gather_streams(src_ref, tmp_ref, d):
    if d <= SPLIT:
        return lambda r, n: _stream(src_ref, r, n, d)
    q = src_ref.shape[0] // SPLIT
    for a in range(SPLIT):
        for off in range(0, q, PIECE):
            tmp_ref[a * q + off:a * q + off + PIECE, :] = src_ref[pl.ds(a + SPLIT * off, PIECE, stride=SPLIT), :]
    return lambda r, n: tmp_ref[pl.ds((r % SPLIT) * q + r // SPLIT, n, stride=d // SPLIT), :]


def _scatter_streams(dst_ref, tmp_ref, d, value_of, n, before=None):
    if d <= SPLIT:
        for r in range(d):
            pos = (pl.ds(r, n, stride=d) if d > 1 else slice(None), slice(None))
            val = value_of(r)
            dst_ref[pos] = val if before is None else val + before[pos]
        return
    q = dst_ref.shape[0] // SPLIT
    for r in range(d):
        tmp_ref[pl.ds((r % SPLIT) * q + r // SPLIT, n, stride=d // SPLIT), :] = value_of(r)
    for a in range(SPLIT):
        for off in range(0, q, PIECE):
            pos = (pl.ds(a + SPLIT * off, PIECE, stride=SPLIT), slice(None))
            val = tmp_ref[a * q + off:a * q + off + PIECE, :]
            dst_ref[pos] = val if before is None else val + before[pos]


def _stream(ref, r, n, d):
    return ref[pl.ds(r, n, stride=d), :] if d > 1 else ref[pl.ds(r, n), :]


def _alibi_tables(d):
    qi = jnp.arange(QB)[:, None]
    kj = jnp.arange(2 * QB)[None, :]
    delta = qi + QB - kj
    band = (delta >= 0) & (delta <= QB)
    dist = (delta * d).astype(F32)
    heads = jnp.arange(8, dtype=F32)
    slopes = 2.0 ** (-(heads + 1.0))
    t = jnp.where(band[None], -slopes[:, None, None] * dist[None], NEG)
    return t.reshape(4, 2 * QB, 2 * QB)


def _attn_specs(d):
    _, _, hr = _attn_geometry(d)
    per = ATT_WIN // hr
    main = lambda off: pl.BlockSpec((ATT_WIN, 2 * HEAD), lambda cb, n: (n, off + cb))
    prev = lambda off: pl.BlockSpec((hr, 2 * HEAD), lambda cb, n: (jnp.maximum(n * per - 1, 0), off + cb))
    nxt = lambda off: pl.BlockSpec((hr, 2 * HEAD), lambda cb, n: (jnp.minimum((n + 1) * per, T // hr - 1), off + cb))
    bias = pl.BlockSpec((None, 2 * QB, 2 * QB), lambda cb, n: (cb, 0, 0))
    return main, prev, nxt, bias


def _attn_fwd(qn, kn, proj, bias, d, rider=None, merge=None):
    sl, nb, hr = _attn_geometry(d)
    slk = QB + sl
    mrows = 256

    def body(q_ref, k_ref, v_ref, kh_ref, vh_ref, bias_ref, *rest):
        if merge is None:
            o_ref, l_ref, qs, ks, vs, os_, ls, tmp = rest
        else:
            oa_ref, la_ref, ob_ref, lb_ref, _, cat_ref, of_ref, lg_ref, qs, ks, vs, os_, ls, tmp, o_ref, l_ref = rest
        n = pl.program_id(1)
        for dst, per, at, src, take in ((qs, sl, 0, q_ref, sl), (ks, slk, 0, kh_ref, QB), (ks, slk, QB, k_ref, sl),
                                        (vs, slk, 0, vh_ref, QB), (vs, slk, QB, v_ref, sl)):
            stream = _gather_streams(src, tmp, d)
            for r in range(d):
                dst[r * per + at:r * per + at + take, :] = stream(r, take).astype(BF16)
        col = lax.broadcasted_iota(jnp.int32, (2 * QB, 2 * QB), 1)
        for r in range(d):
            for b in range(nb):
                rows = slice(r * sl + b * QB, r * sl + (b + 1) * QB)
                keys = slice(r * slk + b * QB, r * slk + (b + 2) * QB)
                s = _nt(_stack_heads(qs[rows, :]), ks[keys, :]) + bias_ref[...]
                if b == 0:
                    s = jnp.where((col < QB) & (n == 0), NEG, s)
                m = jnp.max(s, axis=-1, keepdims=True)
                p = jnp.exp(s - m)
                den = jnp.sum(p, axis=-1, keepdims=True)
                pv = jnp.dot(p.astype(BF16), vs[keys, :], preferred_element_type=F32)
                os_[rows, :] = _unstack_heads(pv / den)
                ls[rows, :] = _unstack_heads(jnp.broadcast_to(m + jnp.log(den), (2 * QB, 2 * HEAD)))
        _scatter_streams(o_ref, tmp, d, lambda r: os_[r * sl:(r + 1) * sl, :], sl)
        _scatter_streams(l_ref, tmp, d, lambda r: ls[r * sl:(r + 1) * sl, :], sl)
        if merge is not None:
            for r0 in range(0, ATT_WIN, mrows):
                rows = slice(r0, r0 + mrows)
                a, b, c = la_ref[rows, :], lb_ref[rows, :], l_ref[rows, :]
                m = jnp.maximum(jnp.maximum(a, b), c)
                e0, e1, e2 = jnp.exp(a - m), jnp.exp(b - m), jnp.exp(c - m)
                den = e0 + e1 + e2
                o = (e0 * oa_ref[rows, :] + e1 * ob_ref[rows, :] + e2 * o_ref[rows, :]) / den
                of_ref[rows, :] = o
                cat_ref[rows, :] = o.astype(BF16)
                lg_ref[rows, :] = m + jnp.log(den)

    main, prev, _, bias_spec = _attn_specs(d)
    lanes = 2 * HEAD
    operands = [qn, kn, proj, kn, proj, bias]
    in_specs = [main(0), main(0), main(V_COL), prev(0), prev(V_COL), bias_spec]
    scratch = [pltpu.VMEM((ATT_WIN, lanes), BF16), pltpu.VMEM((ATT_WIN + hr, lanes), BF16),
               pltpu.VMEM((ATT_WIN + hr, lanes), BF16), pltpu.VMEM((ATT_WIN, lanes), F32),
               pltpu.VMEM((ATT_WIN, lanes), F32), pltpu.VMEM((ATT_WIN, lanes), F32)]
    if merge is None:
        out_specs = [main(0), main(0)]
        out_shape = [jax.ShapeDtypeStruct((T, C), F32)] * 2
        aliases = None
    else:
        operands += list(merge)
        in_specs += [main(0)] * 4 + [ANY]
        out_specs = [main(C // lanes), main(0), main(0)]
        out_shape = [jax.ShapeDtypeStruct((T, D), BF16), jax.ShapeDtypeStruct((T, C), F32),
                     jax.ShapeDtypeStruct((T, C), F32)]
        scratch += [pltpu.VMEM((ATT_WIN, lanes), F32)] * 2
        aliases = {len(operands) - 1: 0}
    return _call(
        body, *operands, rider=rider, name=f"attn_fwd_d{d}", grid=(C // lanes, T // ATT_WIN),
        in_specs=in_specs, out_specs=out_specs, out_shape=out_shape, scratch_shapes=scratch,
        input_output_aliases=aliases, compiler_params=_params(("arbitrary", "arbitrary"), 48))


def _out_up(x, cat, w_out, g2, w_up, rider):
    tm, tn = 512, NUP // 4

    def body(x_ref, cat_ref, wo_ref, g_ref, wu_ref, x1_ref, h2_ref, up_ref):
        x1 = x_ref[...] + jnp.dot(cat_ref[...], wo_ref[...], preferred_element_type=F32)
        x1_ref[...] = x1
        r = lax.rsqrt(jnp.mean(x1 * x1, axis=-1, keepdims=True) + EPS)
        h2_ref[...] = (x1 * r * g_ref[...]).astype(BF16)
        for j in range(NUP // tn):
            cols = slice(j * tn, (j + 1) * tn)
            up_ref[:, cols] = jnp.dot(h2_ref[...], wu_ref[:, cols], preferred_element_type=F32)

    row = pl.BlockSpec((tm, D), lambda i: (i, 0))
    return _call(
        body, x, cat, w_out, g2, w_up, rider=rider, name="out_up", grid=(T // tm,),
        in_specs=[row, row, _resident((D, D)), pl.BlockSpec((1, D), lambda i: (0, 0)), _resident((D, NUP))],
        out_specs=[row, row, pl.BlockSpec((tm, NUP), lambda i: (i, 0))],
        out_shape=[jax.ShapeDtypeStruct((T, D), F32), jax.ShapeDtypeStruct((T, D), BF16),
                   jax.ShapeDtypeStruct((T, NUP), F32)],
        compiler_params=_params(("arbitrary",), 58))


FF_TM = 256
FF_HALO = 8
FF_RB = 64
FF_TILES = DFF // LANES


def _ff_taps(fw_ref, fb_ref, tile):
    cols = _lanes(tile)
    return [fw_ref[k:k + 1, cols] for k in range(FF_K)] + [fb_ref[:, cols]]


def _ff_conv(ext_ref, taps, tile, r0):
    base = FF_HALO + r0
    acc = taps[3] + taps[0] * ext_ref.at[tile][_sp(base - 2, FF_RB)]
    acc = acc + taps[1] * ext_ref.at[tile][_sp(base - 1, FF_RB)]
    return acc + taps[2] * ext_ref.at[tile][_sp(base, FF_RB)]


def _ffn(up, ffconv_w, ffconv_b, w_down, x1, target):
    tm, hl = FF_TM, FF_HALO
    per = tm // hl
    nt = T // tm
    tiles = 2 * FF_TILES

    def body(up_ref, uh_ref, fw_ref, fb_ref, wd_ref, x1_ref, tg_ref, act_ref, dy_ref, loss_ref, dup_ref, gff_ref,
             ext_ref, gv_ref, dact_ref, carry_ref):
        i = pl.program_id(0)

        @pl.when(i == 0)
        def _():
            gff_ref[...] = jnp.zeros_like(gff_ref)
            loss_ref[...] = jnp.zeros_like(loss_ref)
            carry_ref[...] = jnp.zeros_like(carry_ref)

        for j in range(tiles):
            ext_ref.at[j][_sp(0, hl)] = jnp.where(i < nt - 1, uh_ref[:, _lanes(j)], 0.0)
            for r0 in range(0, tm, FF_RB):
                ext_ref.at[j][_sp(hl + r0, FF_RB)] = up_ref[r0:r0 + FF_RB, _lanes(j)]
        for c in range(FF_TILES):
            gate_taps, val_taps = _ff_taps(fw_ref, fb_ref, c), _ff_taps(fw_ref, fb_ref, FF_TILES + c)
            for r0 in range(0, tm, FF_RB):
                rows = slice(r0, r0 + FF_RB)
                gate = _ff_conv(ext_ref, gate_taps, c, r0)
                val = _ff_conv(ext_ref, val_taps, FF_TILES + c, r0)
                gv_ref[rows, _lanes(c)] = gate
                gv_ref[rows, _lanes(FF_TILES + c)] = val
                act_ref[rows, _lanes(c)] = (gate * _sigmoid(gate) * val).astype(BF16)
        err = x1_ref[...] + jnp.dot(act_ref[...], wd_ref[...], preferred_element_type=F32) - tg_ref[...]
        dy_ref[...] = err * (1.0 / D)
        loss_ref[...] += jnp.sum(err * err)
        dact_ref[...] = _nt(dy_ref[...].astype(BF16), wd_ref[...])

        for c in range(FF_TILES):
            for r0 in range(0, tm, FF_RB):
                rows = slice(r0, r0 + FF_RB)
                gate, val = gv_ref[rows, _lanes(c)], gv_ref[rows, _lanes(FF_TILES + c)]
                sg = _sigmoid(gate)
                da = dact_ref[rows, _lanes(c)]
                ext_ref.at[c][_sp(r0, FF_RB)] = da * val * (sg + gate * sg * (1.0 - sg))
                ext_ref.at[FF_TILES + c][_sp(r0, FF_RB)] = da * gate * sg
        fold = lambda a: jnp.sum(a.reshape(FF_RB // 8, 8, LANES), axis=0)
        for c in range(tiles):
            cols = _lanes(c)
            ext_ref.at[c][_sp(tm, hl)] = carry_ref[c]
            taps = [fw_ref[k:k + 1, cols] for k in range(FF_K)]
            acc = [jnp.zeros((8, LANES), F32) for _ in range(FF_K + 1)]
            for r0 in range(0, tm, FF_RB):
                shifted = [ext_ref.at[c][_sp(r0 + k, FF_RB)] for k in range(FF_K)]
                u = up_ref[r0:r0 + FF_RB, cols]
                dup = taps[2] * shifted[0] + taps[1] * shifted[1] + taps[0] * shifted[2]
                dup_ref[r0:r0 + FF_RB, cols] = dup.astype(BF16)
                for k in range(FF_K):
                    acc[2 - k] = acc[2 - k] + fold(shifted[k] * u)
                acc[FF_K] = acc[FF_K] + fold(shifted[0])
            for k in range(FF_K + 1):
                gff_ref[k:k + 1, cols] += jnp.sum(acc[k], axis=0, keepdims=True)
            carry_ref[c] = ext_ref.at[c][_sp(0, hl)]

    rev = lambda i: (nt - 1 - i, 0)
    row = pl.BlockSpec((tm, D), rev)
    wide = pl.BlockSpec((tm, NUP), rev)
    return _call(
        body, up, up, ffconv_w, ffconv_b, w_down, x1, target, name="ffn", grid=(nt,),
        in_specs=[wide, pl.BlockSpec((hl, NUP), lambda i: (jnp.maximum((nt - 1 - i) * per - 1, 0), 0)),
                  pl.BlockSpec((FF_K, NUP), lambda i: (0, 0)), pl.BlockSpec((1, NUP), lambda i: (0, 0)),
                  _resident((DFF, D)), row, row],
        out_specs=[pl.BlockSpec((tm, DFF), rev), row, pl.BlockSpec((8, 128), lambda i: (0, 0)), wide,
                   pl.BlockSpec((8, NUP), lambda i: (0, 0))],
        out_shape=[jax.ShapeDtypeStruct((T, DFF), BF16), jax.ShapeDtypeStruct((T, D), F32),
                   jax.ShapeDtypeStruct((8, 128), F32), jax.ShapeDtypeStruct((T, NUP), BF16),
                   jax.ShapeDtypeStruct((8, NUP), F32)],
        scratch_shapes=[pltpu.VMEM((tiles, 2 * (tm + hl), LANES), F32), pltpu.VMEM((tm, NUP), F32),
                        pltpu.VMEM((tm, DFF), F32), pltpu.VMEM((tiles, hl, LANES), F32)],
        compiler_params=_params(("arbitrary",), 58))


def _weight_grad(a, g, bm, bn, tk, name):
    m, n = a.shape[1], g.shape[1]
    nk = T // tk

    def body(a_ref, g_ref, of_ref, ob_ref):
        k = pl.program_id(2)

        @pl.when(k == 0)
        def _():
            of_ref[...] = jnp.zeros_like(of_ref)
        of_ref[...] += _tn_dot(a_ref[...].astype(BF16), g_ref[...].astype(BF16))

        @pl.when(k == nk - 1)
        def _():
            ob_ref[...] = of_ref[...].astype(BF16)

    out = pl.BlockSpec((bm, bn), lambda i, j, k: (i, j))
    return _call(
        body, a, g, name=name, grid=(m // bm, n // bn, nk),
        in_specs=[pl.BlockSpec((tk, bm), lambda i, j, k: (k, i)), pl.BlockSpec((tk, bn), lambda i, j, k: (k, j))],
        out_specs=[out, out],
        out_shape=[jax.ShapeDtypeStruct((m, n), F32), jax.ShapeDtypeStruct((m, n), BF16)],
        compiler_params=_params(("parallel", "parallel", "arbitrary"), 56))


def _norm_bwd_mm(dz, w, xin, base, gain, name, rider):
    kdim = dz.shape[1]
    tm = 512

    def body(dz_ref, w_ref, x_ref, b_ref, g_ref, dx_ref, gg_ref):
        @pl.when(pl.program_id(0) == 0)
        def _():
            gg_ref[...] = jnp.zeros_like(gg_ref)

        xv = x_ref[...]
        dh = _nt(dz_ref[...], w_ref[...])
        r = lax.rsqrt(jnp.mean(xv * xv, axis=-1, keepdims=True) + EPS)
        t = dh * g_ref[...]
        dx_ref[...] = b_ref[...] + r * t - xv * (r * r * r) * jnp.mean(t * xv, axis=-1, keepdims=True)
        gg_ref[...] += jnp.sum(dh * xv * r, axis=0, keepdims=True)

    row = pl.BlockSpec((tm, D), lambda i: (i, 0))
    vec = pl.BlockSpec((1, D), lambda i: (0, 0))
    return _call(
        body, dz, w, xin, base, gain, rider=rider, name=name, grid=(T // tm,),
        in_specs=[pl.BlockSpec((tm, kdim), lambda i: (i, 0)), _resident((D, kdim)), row, row, vec],
        out_specs=[row, vec],
        out_shape=[jax.ShapeDtypeStruct((T, D), F32), jax.ShapeDtypeStruct((1, D), F32)],
        compiler_params=_params(("arbitrary",), 48))


def _outproj_bwd(dx1, w_out, o_f32, bd):
    tm = 512

    def body(d_ref, w_ref, o_ref, bd_ref, dc_ref, dl_ref):
        dc_ref[...] = _nt(d_ref[...].astype(BF16), w_ref[...])
        dl_ref[...] = _segsum(dc_ref[:, C:2 * C] * o_ref[...], bd_ref[...])

    row = pl.BlockSpec((tm, D), lambda i: (i, 0))
    blk = pl.BlockSpec((tm, C), lambda i: (i, 0))
    return _call(
        body, dx1, w_out, o_f32, bd, name="outproj_bwd", grid=(T // tm,),
        in_specs=[row, _resident((D, D)), blk, _resident((C, C))], out_specs=[row, blk],
        out_shape=[jax.ShapeDtypeStruct((T, D), F32), jax.ShapeDtypeStruct((T, C), F32)],
        compiler_params=_params(("arbitrary",), 32))


def _conv_bwd(dcat, cv, proj, conv_w, cn_g, cn_b, rider):
    tm, hl, rb, cb = CONV_TM, CONV_HALO, CONV_RB, CONV_CB
    per = tm // hl
    nt = T // tm
    tiles = C // LANES

    def body(du_ref, dun_ref, cv_ref, cvn_ref, av_ref, ag_ref, hv_ref, hg_ref, w_ref, g_ref, bb_ref,
             dp_ref, gv_ref, gw_ref, dsh_ref, gsh_ref):
        i = pl.program_id(0)

        @pl.when(i == 0)
        def _():
            gv_ref[...] = jnp.zeros_like(gv_ref)
            gw_ref[...] = jnp.zeros_like(gw_ref)

        def ln_bwd(du, cvv):
            mu = jnp.mean(cvv, axis=-1, keepdims=True)
            xc = cvv - mu
            rs = lax.rsqrt(jnp.mean(xc * xc, axis=-1, keepdims=True) + EPS)
            xh = xc * rs
            ln = xh * g_ref[...] + bb_ref[...]
            sg = _sigmoid(ln)
            dln = du * (sg + ln * sg * (1.0 - sg))
            dxh = dln * g_ref[...]
            dcv = rs * (dxh - jnp.mean(dxh, axis=-1, keepdims=True)
                        - xh * jnp.mean(dxh * xh, axis=-1, keepdims=True))
            return dcv, dln, xh

        for r0 in range(0, tm, rb):
            dcv, dln, xh = ln_bwd(du_ref[r0:r0 + rb, :], cv_ref[r0:r0 + rb, :])
            for j in range(tiles):
                dsh_ref.at[j][_sp(r0, rb)] = dcv[:, _lanes(j)]
            gv_ref[0:1, :] += jnp.sum(dln * xh, axis=0, keepdims=True)
            gv_ref[1:2, :] += jnp.sum(dln, axis=0, keepdims=True)
            gv_ref[2:3, :] += jnp.sum(dcv, axis=0, keepdims=True)
        dcv_n, _, _ = ln_bwd(dun_ref[...], cvn_ref[...])
        dcv_n = jnp.where(i < nt - 1, dcv_n, 0.0)
        for j in range(tiles):
            ln_ = _lanes(j)
            dsh_ref.at[j][_sp(tm, hl)] = dcv_n[:, ln_]
            glu_h = hv_ref[:, ln_] * _sigmoid(hg_ref[:, ln_])
            gsh_ref.at[j][_sp(0, hl)] = jnp.where(i > 0, glu_h, 0.0)
            for r0 in range(0, tm, cb):
                gsh_ref.at[j][_sp(hl + r0, cb)] = av_ref[r0:r0 + cb, ln_] * _sigmoid(ag_ref[r0:r0 + cb, ln_])

        for j in range(tiles):
            ln_ = _lanes(j)
            for r0 in range(0, tm, cb):
                dglu = jnp.zeros((cb, LANES), F32)
                for k in range(CONV_K):
                    dglu = dglu + w_ref[k:k + 1, ln_] * dsh_ref.at[j][_sp(r0 + (CONV_K - 1) - k, cb)]
                av = av_ref[r0:r0 + cb, ln_]
                sg = _sigmoid(ag_ref[r0:r0 + cb, ln_])
                dp_ref[r0:r0 + cb, ln_] = (dglu * sg).astype(BF16)
                dp_ref[r0:r0 + cb, _lanes(tiles + j)] = (dglu * av * sg * (1.0 - sg)).astype(BF16)
            for k in range(CONV_K):
                part = jnp.zeros((8, LANES), F32)
                for r0 in range(0, tm, cb):
                    prod = dsh_ref.at[j][_sp(r0, cb)] * gsh_ref.at[j][_sp(r0 + hl - (CONV_K - 1) + k, cb)]
                    part = part + jnp.sum(prod.reshape(cb // 8, 8, LANES), axis=0)
                gw_ref[k:k + 1, ln_] += jnp.sum(part, axis=0, keepdims=True)

    main = lambda col: pl.BlockSpec((tm, C), lambda i: (i, col))
    prev = lambda col: pl.BlockSpec((hl, C), lambda i: (jnp.maximum(i * per - 1, 0), col))
    nxt = pl.BlockSpec((hl, C), lambda i: (jnp.minimum((i + 1) * per, T // hl - 1), 0))
    vec = pl.BlockSpec((1, C), lambda i: (0, 0))
    return _call(
        body, dcat, dcat, cv, cv, proj, proj, proj, proj, conv_w, cn_g, cn_b, rider=rider, name="conv_bwd",
        grid=(nt,),
        in_specs=[main(0), nxt, main(0), nxt, main(0), main(1), prev(0), prev(1),
                  pl.BlockSpec((CONV_K, C), lambda i: (0, 0)), vec, vec],
        out_specs=[pl.BlockSpec((tm, 2 * C), lambda i: (i, 0)), pl.BlockSpec((8, C), lambda i: (0, 0)),
                   pl.BlockSpec((32, C), lambda i: (0, 0))],
        out_shape=[jax.ShapeDtypeStruct((T, NPROJ), BF16), jax.ShapeDtypeStruct((8, C), F32),
                   jax.ShapeDtypeStruct((32, C), F32)],
        scratch_shapes=[pltpu.VMEM((C // LANES, 2 * (tm + hl), LANES), F32)] * 2,
        compiler_params=_params(("arbitrary",), 48))


def _attn_bwd_unit(qs, dos, lgs, dls, rows, kc, vc, biasv, invalid_prev):
    qst, dost = _stack_heads(qs[rows, :]), _stack_heads(dos[rows, :])
    s = _nt(qst, kc) + biasv
    if invalid_prev is not None:
        col = lax.broadcasted_iota(jnp.int32, s.shape, 1)
        s = jnp.where((col < QB) & invalid_prev, NEG, s)
    p = jnp.exp(s - _stack_cols(lgs[rows, :]))
    ds = p * (_nt(dost, vc) - _stack_cols(dls[rows, :]))
    dsb = ds.astype(BF16)
    dq = _unstack_heads(jnp.dot(dsb, kc, preferred_element_type=F32))
    return dq, _tn_dot(dsb, qst), _tn_dot(p.astype(BF16), dost)


def _attn_bwd_lagged(qn, kn, proj, dcat, lg, dl, bias, d, earlier):
    assert QB * d == ATT_WIN
    n_win = T // ATT_WIN
    lanes = 2 * HEAD

    def body(q_ref, k_ref, v_ref, do_ref, lg_ref, dl_ref, kh_ref, vh_ref, bias_ref, eq_ref, ek_ref, ev_ref,
             dq_ref, dk_ref, dv_ref, qs, dos, lgs, dls, ks, vs, dqs, ck, cv, ok, ov, tmp):
        n = pl.program_id(1)
        block = lambda buf: (lambda r: buf[r * QB:(r + 1) * QB, :])

        @pl.when(n == 0)
        def _():
            ck[...] = jnp.zeros_like(ck)
            cv[...] = jnp.zeros_like(cv)

        @pl.when(n < n_win)
        def _():
            for dst, per, at, src, dt in ((qs, QB, 0, q_ref, BF16), (dos, QB, 0, do_ref, BF16),
                                          (lgs, QB, 0, lg_ref, F32), (dls, QB, 0, dl_ref, F32),
                                          (ks, 2 * QB, 0, kh_ref, BF16), (ks, 2 * QB, QB, k_ref, BF16),
                                          (vs, 2 * QB, 0, vh_ref, BF16), (vs, 2 * QB, QB, v_ref, BF16)):
                stream = _gather_streams(src, tmp, d)
                for r in range(d):
                    dst[r * per + at:r * per + at + QB, :] = stream(r, QB).astype(dt)
            for r in range(d):
                rows = slice(r * QB, (r + 1) * QB)
                keys = slice(2 * r * QB, (2 * r + 2) * QB)
                dq, dkc, dvc = _attn_bwd_unit(qs, dos, lgs, dls, rows, ks[keys, :], vs[keys, :], bias_ref[...], n == 0)
                dqs[rows, :] = dq
                ok[rows, :] = ck[rows, :] + dkc[:QB]
                ov[rows, :] = cv[rows, :] + dvc[:QB]
                ck[rows, :] = dkc[QB:]
                cv[rows, :] = dvc[QB:]
            _scatter_streams(dq_ref, tmp, d, block(dqs), QB, eq_ref)
            _scatter_streams(dk_ref, tmp, d, block(ok), QB, ek_ref)
            _scatter_streams(dv_ref, tmp, d, block(ov), QB, ev_ref)

        @pl.when(n == n_win)
        def _():
            _scatter_streams(dk_ref, tmp, d, block(ck), QB, ek_ref)
            _scatter_streams(dv_ref, tmp, d, block(cv), QB, ev_ref)

    cur = lambda off: pl.BlockSpec((ATT_WIN, lanes), lambda cb, n: (jnp.minimum(n, n_win - 1), off + cb))
    prev = lambda off: pl.BlockSpec(
        (ATT_WIN, lanes), lambda cb, n: (jnp.maximum(jnp.minimum(n, n_win - 1) - 1, 0), off + cb))
    late = pl.BlockSpec((ATT_WIN, lanes), lambda cb, n: (jnp.maximum(n - 1, 0), cb))
    buf = lambda rows, dt: pltpu.VMEM((rows, lanes), dt)
    return _call(
        body, qn, kn, proj, dcat, lg, dl, kn, proj, bias, *earlier, name=f"attn_bwd_d{d}",
        grid=(C // lanes, n_win + 1),
        in_specs=[cur(0), cur(0), cur(V_COL), cur(DO_COL), cur(0), cur(0), prev(0), prev(V_COL),
                  pl.BlockSpec((None, 2 * QB, 2 * QB), lambda cb, n: (cb, 0, 0)), cur(0), late, late],
        out_specs=[cur(0), late, late],
        out_shape=[jax.ShapeDtypeStruct((T, C), F32)] * 3,
        scratch_shapes=[buf(ATT_WIN, BF16), buf(ATT_WIN, BF16), buf(ATT_WIN, F32), buf(ATT_WIN, F32),
                        buf(2 * ATT_WIN, BF16), buf(2 * ATT_WIN, BF16)] + [buf(ATT_WIN, F32)] * 6,
        compiler_params=_params(("arbitrary", "arbitrary"), 48))


def _attn_bwd(qn, kn, proj, dcat, lg, dl, bias, d, earlier=None, rider=None):
    sl, nb, hr = _attn_geometry(d)
    slk = QB + sl
    slq = sl + QB
    n_win = T // ATT_WIN

    def body(q_ref, k_ref, v_ref, do_ref, lg_ref, dl_ref, kh_ref, vh_ref, qx_ref, dox_ref, lgx_ref, dlx_ref,
             bias_ref, *rest):
        sums = rest[:3] if earlier is not None else (None, None, None)
        dq_ref, dk_ref, dv_ref, qs, dos, lgs, dls, ks, vs, dqs, dks, dvs = rest[-12:]
        n = pl.program_id(1)
        for r in range(d):
            for dst, src, nx, dt in ((qs, q_ref, qx_ref, BF16), (dos, do_ref, dox_ref, BF16),
                                     (lgs, lg_ref, lgx_ref, F32), (dls, dl_ref, dlx_ref, F32)):
                dst[r * slq:r * slq + sl, :] = _stream(src, r, sl, d).astype(dt)
                dst[r * slq + sl:(r + 1) * slq, :] = _stream(nx, r, QB, d).astype(dt)
            for dst, halo, src in ((ks, kh_ref, k_ref), (vs, vh_ref, v_ref)):
                dst[r * slk:r * slk + QB, :] = _stream(halo, r, QB, d).astype(BF16)
                dst[r * slk + QB:(r + 1) * slk, :] = _stream(src, r, sl, d).astype(BF16)
        dks[...] = jnp.zeros_like(dks)
        dvs[...] = jnp.zeros_like(dvs)

        def unit(rows, kc, vc, biasv, invalid_prev):
            return _attn_bwd_unit(qs, dos, lgs, dls, rows, kc, vc, biasv, invalid_prev)

        for r in range(d):
            for b in range(nb):
                rows = slice(r * slq + b * QB, r * slq + (b + 1) * QB)
                keys = slice(r * slk + b * QB, r * slk + (b + 2) * QB)
                dq, dkc, dvc = unit(rows, ks[keys, :], vs[keys, :], bias_ref[...], (n == 0) if b == 0 else None)
                dqs[r * sl + b * QB:r * sl + (b + 1) * QB, :] = dq
                if b == 0:
                    dks[r * sl:r * sl + QB, :] += dkc[QB:]
                    dvs[r * sl:r * sl + QB, :] += dvc[QB:]
                else:
                    dks[r * sl + (b - 1) * QB:r * sl + (b + 1) * QB, :] += dkc
                    dvs[r * sl + (b - 1) * QB:r * sl + (b + 1) * QB, :] += dvc

        @pl.when(n < n_win - 1)
        def _():
            for r in range(d):
                rows = slice(r * slq + sl, (r + 1) * slq)
                keys = slice(r * slk + sl, (r + 1) * slk)
                _, dkc, dvc = unit(rows, ks[keys, :], vs[keys, :], bias_ref[:, 0:QB], None)
                dks[(r + 1) * sl - QB:(r + 1) * sl, :] += dkc
                dvs[(r + 1) * sl - QB:(r + 1) * sl, :] += dvc

        for dst, src, before in zip((dq_ref, dk_ref, dv_ref), (dqs, dks, dvs), sums):
            for r in range(d):
                pos = (pl.ds(r, sl, stride=d) if d > 1 else slice(None), slice(None))
                val = src[r * sl:(r + 1) * sl, :]
                dst[pos] = val if before is None else val + before[pos]

    main, prev, nxt, bias_spec = _attn_specs(d)
    lanes = 2 * HEAD
    return _call(
        body, qn, kn, proj, dcat, lg, dl, kn, proj, qn, dcat, lg, dl, bias, *(earlier or ()), rider=rider,
        name=f"attn_bwd_d{d}", grid=(C // lanes, n_win),
        in_specs=[main(0), main(0), main(V_COL), main(DO_COL), main(0), main(0), prev(0), prev(V_COL),
                  nxt(0), nxt(DO_COL), nxt(0), nxt(0), bias_spec] + ([main(0)] * 3 if earlier is not None else []),
        out_specs=[main(0)] * 3,
        out_shape=[jax.ShapeDtypeStruct((T, C), F32)] * 3,
        scratch_shapes=[pltpu.VMEM((ATT_WIN + hr, lanes), BF16), pltpu.VMEM((ATT_WIN + hr, lanes), BF16),
                        pltpu.VMEM((ATT_WIN + hr, lanes), F32), pltpu.VMEM((ATT_WIN + hr, lanes), F32),
                        pltpu.VMEM((ATT_WIN + hr, lanes), BF16), pltpu.VMEM((ATT_WIN + hr, lanes), BF16),
                        pltpu.VMEM((ATT_WIN, lanes), F32), pltpu.VMEM((ATT_WIN, lanes), F32),
                        pltpu.VMEM((ATT_WIN, lanes), F32)],
        compiler_params=_params(("arbitrary", "arbitrary"), 48))


def _qk_norm_bwd(dn_sum, proj, col, gain, bd, dproj, name):
    tm = 512

    def body(d0, x_ref, g_ref, bd_ref, dp_in, dp_ref, gg_ref):
        del dp_in

        @pl.when(pl.program_id(0) == 0)
        def _():
            gg_ref[...] = jnp.zeros_like(gg_ref)
        dn = d0[...]
        xv = x_ref[...]
        r = lax.rsqrt(_segsum(xv * xv, bd_ref[...]) * (1.0 / HEAD) + EPS)
        t = dn * g_ref[...]
        mean_tx = _segsum(t * xv, bd_ref[...]) * (1.0 / HEAD)
        dp_ref[...] = (r * t - xv * (r * r * r) * mean_tx).astype(BF16)
        gg_ref[...] += jnp.sum(dn * xv * r, axis=0, keepdims=True)

    blk = pl.BlockSpec((tm, C), lambda i: (i, 0))
    vec = pl.BlockSpec((1, C), lambda i: (0, 0))
    return _call(
        body, dn_sum, proj, gain, bd, dproj, name=name, grid=(T // tm,),
        in_specs=[blk, pl.BlockSpec((tm, C), lambda i: (i, col)), vec, _resident((C, C)), ANY],
        out_specs=[pl.BlockSpec((tm, C), lambda i: (i, col)), vec],
        out_shape=[jax.ShapeDtypeStruct((T, NPROJ), BF16), jax.ShapeDtypeStruct((1, C), F32)],
        input_output_aliases={4: 0},
        compiler_params=_params(("arbitrary",), 32))


def _v_bwd(dv_sum, dproj):
    tm = 512

    def body(d0, dp_in, dp_ref):
        del dp_in
        dp_ref[...] = d0[...].astype(BF16)

    blk = pl.BlockSpec((tm, C), lambda i: (i, 0))
    return _call(
        body, dv_sum, dproj, name="v_bwd", grid=(T // tm,),
        in_specs=[blk, ANY],
        out_specs=[pl.BlockSpec((tm, C), lambda i: (i, 4))],
        out_shape=[jax.ShapeDtypeStruct((T, NPROJ), BF16)],
        input_output_aliases={1: 0},
        compiler_params=_params(("parallel",), 32))[0]


def _adamw(w, g, m, v):
    m = ADAM_B1 * m + (1.0 - ADAM_B1) * g
    v = ADAM_B2 * v + (1.0 - ADAM_B2) * (g * g)
    m_hat = m / (1.0 - ADAM_B1 ** ADAM_STEP)
    v_hat = v / (1.0 - ADAM_B2 ** ADAM_STEP)
    delta = -ADAM_LR * (m_hat / (jnp.sqrt(v_hat) + ADAM_EPS) + ADAM_WD * w)
    return delta, m, v


def _row_block(shape):
    rows = shape[0]
    for cand in (256, 128, 64, 88, 32, 8):
        if rows % cand == 0 and cand * shape[1] * 4 <= (2 << 20):
            return cand
    return 8


def _partial_sum(own, recv, name):
    br = _row_block(own.shape)
    cols = own.shape[1]

    def body(o_ref, r_ref, p_ref):
        p_ref[...] = ((o_ref[...] + r_ref[0].astype(F32)) + r_ref[1].astype(F32)) + r_ref[2].astype(F32)

    blk = pl.BlockSpec((br, cols), lambda i: (i, 0))
    return _call(
        body, own, recv, name=name, grid=(own.shape[0] // br,),
        in_specs=[blk, pl.BlockSpec((3, br, cols), lambda i: (0, i, 0))], out_specs=[blk],
        out_shape=[jax.ShapeDtypeStruct(own.shape, F32)],
        compiler_params=_params(("parallel",), 32))[0]


def _adamw_mat(p_own, p_sib, w, m, v, name):
    br = _row_block(w.shape)
    cols = w.shape[1]

    def body(a_ref, b_ref, w_ref, m_ref, v_ref, g_ref, d_ref, nm_ref, nv_ref):
        g = a_ref[...] + b_ref[...]
        delta, nm, nv = _adamw(w_ref[...], g, m_ref[...], v_ref[...])
        g_ref[...] = g
        d_ref[...] = delta
        nm_ref[...] = nm
        nv_ref[...] = nv

    blk = pl.BlockSpec((br, cols), lambda i: (i, 0))
    return _call(
        body, p_own, p_sib, w, m, v, name=name, grid=(w.shape[0] // br,),
        in_specs=[blk] * 5, out_specs=[blk] * 4,
        out_shape=[jax.ShapeDtypeStruct(w.shape, F32)] * 4,
        compiler_params=_params(("parallel",), 40))


def _vec_reduce(vrecv):
    def body(v_ref, o_ref):
        acc = v_ref[0]
        for r in range(1, N_DEV):
            acc = acc + v_ref[r]
        o_ref[...] = acc

    return pl.pallas_call(
        body, name="vec_reduce",
        out_shape=jax.ShapeDtypeStruct((VPACK_ROWS, D), F32),
        compiler_params=_params((), 32),
    )(vrecv)


def _adamw_small(w, g, m, v):
    def body(w_ref, g_ref, m_ref, v_ref, d_ref, nm_ref, nv_ref):
        delta, nm, nv = _adamw(w_ref[...], g_ref[...], m_ref[...], v_ref[...])
        d_ref[...] = delta
        nm_ref[...] = nm
        nv_ref[...] = nv

    return pl.pallas_call(
        body, name="adamw_small",
        out_shape=[jax.ShapeDtypeStruct(w.shape, F32)] * 3,
        compiler_params=_params((), 32),
    )(w, g, m, v)


def _pack(parts, rows):
    flat = jnp.concatenate([p.reshape(-1) for p in parts])
    return jnp.pad(flat, (0, rows * D - flat.shape[0])).reshape(rows, D)


def _unpack(packed, shapes):
    flat = packed.reshape(-1)
    out, off = [], 0
    for shp in shapes:
        size = 1
        for s in shp:
            size *= s
        out.append(flat[off:off + size].reshape(shp))
        off += size
    return out


def _no_comm(shards, row_sharded, peers=(0, 1, 2), into=None):
    del row_sharded, peers, into
    return None, lambda res, n: (res, shards)


def _with_comm(shards, row_sharded, peers=(0, 1, 2), into=None):
    rider = _gather_rider(shards, row_sharded, peers, into)
    return rider, lambda res, n: (res[:n], res[n:])


def _local_step(x, target, norm1_g, conv_b, cn_g, cn_b, q_norm_g, k_norm_g, norm2_g, ffconv_b,
                first_weights, late_weights, comm=True):
    row = lambda a: a.reshape(1, -1)
    head_of = jnp.arange(C) // HEAD
    bd = (head_of[:, None] == head_of[None, :]).astype(BF16)
    qg = row(jnp.tile(q_norm_g, C // HEAD) * (HEAD ** -0.5))
    kg = row(jnp.tile(k_norm_g, C // HEAD))
    biases = [_alibi_tables(d) for d in PATTERN_DILATIONS]
    gather = _with_comm if comm else _no_comm
    grad_rider = (lambda g, rs: _grad_rider(g[1], g[0], rs)) if comm else (lambda g, rs: None)

    rider, split = gather(first_weights, (False, False, False))
    (h,), (w_in, conv_w, ffconv_w) = split(_norm_fwd(x, row(norm1_g), rider), 1)
    rider, split = gather(late_weights[0:1], (True,))
    (proj, qn, kn), (w_out,) = split(_proj_fwd(h, w_in, qg, kg, bd, rider), 3)
    rider, split = gather(late_weights[1:2], (False,), (0, 1))
    (cat, cv), w_up_part = split(_conv_fwd(proj, conv_w, row(conv_b), row(cn_g), row(cn_b), rider), 2)
    fwd = [_attn_fwd(qn, kn, proj, biases[i], d) for i, d in enumerate(PATTERN_DILATIONS[:-1])]
    rider, split = gather(late_weights[1:2], (False,), (2,), w_up_part)
    merge = (fwd[0][0], fwd[0][1], fwd[1][0], fwd[1][1], cat)
    (cat, o_f32, lg), (w_up,) = split(
        _attn_fwd(qn, kn, proj, biases[-1], PATTERN_DILATIONS[-1], rider, merge), 3)
    rider, split = gather(late_weights[2:3], (True,))
    (x1, h2, up), (w_down,) = split(_out_up(x, cat, w_out, row(norm2_g), w_up, rider), 3)
    act, dy, loss_acc, dup, gff = _ffn(up, ffconv_w, row(ffconv_b), w_down, x1, target)
    gw_down = _weight_grad(act, dy, DFF // 2, D, 1024, "grad_w_down")
    res = _norm_bwd_mm(dup, w_up, x1, dy, row(norm2_g), "up_bwd", grad_rider(gw_down, True))
    (dx1, g_norm2), ex_down = res[:2], res[2:]
    gw_up = _weight_grad(h2, dup, D, NUP // 4, 2048, "grad_w_up")
    dcat, dl = _outproj_bwd(dx1, w_out, o_f32, bd)
    gw_out = _weight_grad(cat, dx1, D, D, 2048, "grad_w_out")
    res = _conv_bwd(dcat, cv, proj, conv_w, row(cn_g), row(cn_b), grad_rider(gw_up, False))
    (dproj, gconv_vec, gconv_w), ex_up = res[:3], res[3:]
    sums, ex_out = None, []
    for i, d in enumerate(PATTERN_DILATIONS):
        if QB * d == ATT_WIN:
            res = _attn_bwd_lagged(qn, kn, proj, dcat, lg, dl, biases[i], d, sums)
        else:
            res = _attn_bwd(qn, kn, proj, dcat, lg, dl, biases[i], d, sums,
                            grad_rider(gw_out, True) if i == 0 else None)
        sums = res[:3]
        ex_out = res[3:] if i == 0 else ex_out
    dproj, gq_lane = _qk_norm_bwd(sums[0], proj, 2, qg, bd, dproj, "q_norm_bwd")
    dproj, gk_lane = _qk_norm_bwd(sums[1], proj, 3, kg, bd, dproj, "k_norm_bwd")
    dproj = _v_bwd(sums[2], dproj)
    gw_in = _weight_grad(h, dproj, D, NPROJ // 4, 2048, "grad_w_in")
    res = _norm_bwd_mm(dproj, w_in, x, dx1, row(norm1_g), "in_bwd", grad_rider(gw_in, False))
    (dx, g_norm1), ex_in = res[:2], res[2:]

    loss = loss_acc[0, 0] * (0.5 / D)
    g_qg = jnp.sum(gq_lane.reshape(C // HEAD, HEAD), axis=0) * (HEAD ** -0.5)
    g_kg = jnp.sum(gk_lane.reshape(C // HEAD, HEAD), axis=0)
    small = [g_norm1[0], gconv_vec[2], gconv_vec[0], gconv_vec[1], g_qg, g_kg, g_norm2[0], gff[3],
             gconv_w[:CONV_K], gff[:FF_K]]
    mats = [ex_in, ex_out, ex_up, ex_down] if comm else [gw_in, gw_out, gw_up, gw_down]
    return loss, dx, mats, small


def kernel(x, norm1_g, w_in, conv_w, conv_b, cn_g, cn_b, q_norm_g, k_norm_g, w_out, norm2_g, w_up, ffconv_w, ffconv_b, w_down, loss_target, m_norm1_g, m_w_in, m_conv_w, m_conv_b, m_cn_g, m_cn_b, m_q_norm_g, m_k_norm_g, m_w_out, m_norm2_g, m_w_up, m_ffconv_w, m_ffconv_b, m_w_down, v_norm1_g, v_w_in, v_conv_w, v_conv_b, v_cn_g, v_cn_b, v_q_norm_g, v_k_norm_g, v_w_out, v_norm2_g, v_w_up, v_ffconv_w, v_ffconv_b, v_w_down):
    chip = 2 * lax.axis_index("x") + lax.axis_index("y")

    loss, dx, mats, small = _local_step(
        x[0], loss_target[0], norm1_g, conv_b, cn_g, cn_b, q_norm_g, k_norm_g, norm2_g, ffconv_b,
        [w_in.astype(BF16), conv_w, ffconv_w], [w.astype(BF16) for w in (w_out, w_up, w_down)])

    names = ("w_in", "w_out", "w_up", "w_down")
    parts = [_partial_sum(own, recv, "partial_" + names[k]) for k, (recv, own) in enumerate(mats)]
    sib, vrecv = _final_exchange(parts, _pack(small + [loss.reshape(1)], VPACK_ROWS))
    ws = (w_in, w_out, w_up, w_down)
    ms = (m_w_in, m_w_out, m_w_up, m_w_down)
    vs = (v_w_in, v_w_out, v_w_up, v_w_down)
    mat = [_adamw_mat(parts[k], sib[k], ws[k], ms[k], vs[k], "adamw_" + names[k]) for k in range(4)]

    vsum = _vec_reduce(vrecv)
    vec_shapes = [(D,), (C,), (C,), (C,), (HEAD,), (HEAD,), (D,), (NUP,), (CONV_K, C), (FF_K, NUP), (1,)]
    gsmall = _unpack(vsum, vec_shapes)
    g_conv_w = lax.dynamic_slice_in_dim(gsmall[8], chip * (C // N_CHIPS), C // N_CHIPS, axis=1)
    g_ffconv_w = lax.dynamic_slice_in_dim(gsmall[9], chip * (NUP // N_CHIPS), NUP // N_CHIPS, axis=1)
    gs = gsmall[:8] + [g_conv_w, g_ffconv_w]
    w_s = [norm1_g, conv_b, cn_g, cn_b, q_norm_g, k_norm_g, norm2_g, ffconv_b, conv_w, ffconv_w]
    m_s = [m_norm1_g, m_conv_b, m_cn_g, m_cn_b, m_q_norm_g, m_k_norm_g, m_norm2_g, m_ffconv_b, m_conv_w, m_ffconv_w]
    v_s = [v_norm1_g, v_conv_b, v_cn_g, v_cn_b, v_q_norm_g, v_k_norm_g, v_norm2_g, v_ffconv_b, v_conv_w, v_ffconv_w]
    shapes_s = [a.shape for a in w_s]
    d_p, m_p, v_p = _adamw_small(_pack(w_s, SPACK_ROWS), _pack(gs, SPACK_ROWS), _pack(m_s, SPACK_ROWS),
                                 _pack(v_s, SPACK_ROWS))
    d_s, nm_s, nv_s = _unpack(d_p, shapes_s), _unpack(m_p, shapes_s), _unpack(v_p, shapes_s)

    def ordered(sm, mt):
        return [sm[0], mt[0], sm[8], sm[1], sm[2], sm[3], sm[4], sm[5], mt[1], sm[6], mt[2], sm[9], sm[7], mt[3]]

    loss_all = gsmall[10][0]
    grads = ordered(gs, [r[0] for r in mat])
    deltas = ordered(d_s, [r[1] for r in mat])
    new_m = ordered(nm_s, [r[2] for r in mat])
    new_v = ordered(nv_s, [r[3] for r in mat])
    return (loss_all, dx[None], *grads, *deltas, *new_m, *new_v)
```

```python
import types

import jax
import jax.numpy as jnp
from jax import lax
from jax.experimental import pallas as pl
from jax.experimental.pallas import tpu as pltpu

T = 8192
D = 1024
C = 512
NPROJ = 2560
DFF = 2816
NUP = 2 * DFF
CONV_K = 31
FF_K = 3
HEAD = 64
EPS = 1e-6
NEG = -1e30
N_CHIPS = 4
N_DEV = 8
PATTERN_DILATIONS = (1, 4, 16)
QB = 128

ADAM_LR = 0.001
ADAM_B1 = 0.9
ADAM_B2 = 0.999
ADAM_EPS = 1e-08
ADAM_WD = 0.01
ADAM_STEP = 10

F32 = jnp.float32
BF16 = jnp.bfloat16
MESH = pl.DeviceIdType.MESH
ANY = pl.BlockSpec(memory_space=pl.ANY)

VPACK_ROWS = 48
SPACK_ROWS = 24


def _params(sem, vmem_mb):
    return pltpu.CompilerParams(dimension_semantics=sem, vmem_limit_bytes=vmem_mb << 20)


def _resident(shape):
    return pl.BlockSpec(shape, lambda i: (0, 0), pipeline_mode=pl.Buffered(1))


def _nt(a, b):
    return lax.dot_general(a, b, (((1,), (1,)), ((), ())), preferred_element_type=F32)


def _tn_dot(a, b):
    return lax.dot_general(a, b, (((0,), (0,)), ((), ())), preferred_element_type=F32)


def _sigmoid(x):
    return 1.0 / (1.0 + jnp.exp(-x))


def _segsum(x, bd):
    hi = x.astype(BF16)
    lo = (x - hi.astype(F32)).astype(BF16)
    return (jnp.dot(hi, bd, preferred_element_type=F32)
            + jnp.dot(lo, bd, preferred_element_type=F32))


def _place():
    x, y, c = lax.axis_index("x"), lax.axis_index("y"), lax.axis_index("c")
    chips = [(1 - x, y), (x, 1 - y), (1 - x, 1 - y)]
    return x, y, c, chips


def _block_of(ref, shard_shape, row_sharded, s):
    r, cdim = shard_shape
    if row_sharded:
        return ref.at[pl.ds(s * r, r), :]
    return ref.at[:, pl.ds(s * cdim, cdim)]


def _full_shape(shard_shape, row_sharded):
    r, cdim = shard_shape
    return (r * N_CHIPS, cdim) if row_sharded else (r, cdim * N_CHIPS)


def _gather_rider(shards, row_sharded, peers=(0, 1, 2), into=None):
    n = len(shards)
    shapes = [a.shape for a in shards]

    def copies(ins, outs, sems):
        send_sems, recv_sems, local_sems = sems
        x, y, c, chips = _place()
        me = 2 * x + y
        place = lambda k, s: _block_of(outs[k], shapes[k], row_sharded[k], s)
        local = []
        if into is None:
            local = [pltpu.make_async_copy(ins[k], place(k, me), local_sems.at[k]) for k in range(n)]
        sends, recvs = [], []
        for k in range(n):
            for j in peers:
                px, py = chips[j]
                sem = dict(send_sem=send_sems.at[3 * k + j], recv_sem=recv_sems.at[3 * k + j],
                           device_id=(px, py, c), device_id_type=MESH)
                sends.append(pltpu.make_async_remote_copy(src_ref=ins[k], dst_ref=place(k, me), **sem))
                recvs.append(pltpu.make_async_remote_copy(src_ref=ins[k], dst_ref=place(k, 2 * px + py), **sem))
        return local, sends, recvs

    return types.SimpleNamespace(
        operands=list(shards) + list(into or []), copies=copies,
        aliases={n + k: k for k in range(n)} if into is not None else {},
        out_shape=[jax.ShapeDtypeStruct(_full_shape(s, rs), a.dtype) for s, rs, a in zip(shapes, row_sharded, shards)],
        sems=[pltpu.SemaphoreType.DMA((3 * n,)), pltpu.SemaphoreType.DMA((3 * n,)), pltpu.SemaphoreType.DMA((n,))])


def _halved_gather_rider(shards, halve):
    n = len(shards)
    shapes = [a.shape for a in shards]

    def copies(ins, outs, sems):
        send_sems, recv_sems, local_sems, pass_send_sems, pass_recv_sems = sems
        x, y, c, chips = _place()
        me = 2 * x + y
        place = lambda k, s: _block_of(outs[k], shapes[k], False, s)

        def half(ref, k, which):
            rows = shapes[k][0] // 2
            return ref.at[pl.ds(which * rows, rows), :]

        local = [pltpu.make_async_copy(ins[k], place(k, me), local_sems.at[k]) for k in range(n)]
        sends, recvs, passes, pass_recvs = [], [], [], []
        for k in range(n):
            for j, (px, py) in enumerate(chips):
                theirs = 2 * px + py
                sem = dict(send_sem=send_sems.at[3 * k + j], recv_sem=recv_sems.at[3 * k + j],
                           device_id=(px, py, c), device_id_type=MESH)
                if not halve[k]:
                    sends.append(pltpu.make_async_remote_copy(src_ref=ins[k], dst_ref=place(k, me), **sem))
                    recvs.append(pltpu.make_async_remote_copy(src_ref=ins[k], dst_ref=place(k, theirs), **sem))
                    continue
                sends.append(pltpu.make_async_remote_copy(
                    src_ref=half(ins[k], k, c), dst_ref=half(place(k, me), k, c), **sem))
                recvs.append(pltpu.make_async_remote_copy(
                    src_ref=half(ins[k], k, c), dst_ref=half(place(k, theirs), k, c), **sem))
                sem = dict(send_sem=pass_send_sems.at[3 * k + j], recv_sem=pass_recv_sems.at[3 * k + j],
                           device_id=(x, y, 1 - c), device_id_type=MESH)
                mine, other = half(place(k, theirs), k, c), half(place(k, theirs), k, 1 - c)
                passes.append(pltpu.make_async_remote_copy(src_ref=mine, dst_ref=mine, **sem))
                pass_recvs.append(pltpu.make_async_remote_copy(src_ref=other, dst_ref=other, **sem))
        return local, sends, recvs, passes, pass_recvs

    return types.SimpleNamespace(
        operands=list(shards), copies=copies, aliases={},
        out_shape=[jax.ShapeDtypeStruct(_full_shape(s, False), a.dtype) for s, a in zip(shapes, shards)],
        sems=[pltpu.SemaphoreType.DMA((3 * n,)), pltpu.SemaphoreType.DMA((3 * n,)), pltpu.SemaphoreType.DMA((n,)),
              pltpu.SemaphoreType.DMA((3 * n,)), pltpu.SemaphoreType.DMA((3 * n,))])


def _grad_rider(g_bf16, g_f32, row_sharded):
    shard = tuple(d // N_CHIPS if (i == 0) == row_sharded else d for i, d in enumerate(g_f32.shape))

    def copies(ins, outs, sems):
        send_sems, recv_sems, local_sems = sems
        gb, gf = ins
        rec, own = outs
        x, y, c, chips = _place()
        me = 2 * x + y
        local = [pltpu.make_async_copy(_block_of(gf, shard, row_sharded, me), own, local_sems.at[0])]
        sends, recvs = [], []
        for j, (px, py) in enumerate(chips):
            sem = dict(send_sem=send_sems.at[j], recv_sem=recv_sems.at[j], device_id=(px, py, c), device_id_type=MESH)
            sends.append(pltpu.make_async_remote_copy(
                src_ref=_block_of(gb, shard, row_sharded, 2 * px + py), dst_ref=rec.at[j], **sem))
            recvs.append(pltpu.make_async_remote_copy(
                src_ref=_block_of(gb, shard, row_sharded, me), dst_ref=rec.at[j], **sem))
        return local, sends, recvs

    return types.SimpleNamespace(
        operands=[g_bf16, g_f32], copies=copies, aliases={},
        out_shape=[jax.ShapeDtypeStruct((3,) + shard, BF16), jax.ShapeDtypeStruct(shard, F32)],
        sems=[pltpu.SemaphoreType.DMA((3,)), pltpu.SemaphoreType.DMA((3,)), pltpu.SemaphoreType.DMA((1,))])


def _rider_start(rider, ins, outs, sems):
    local, sends = rider.copies(ins, outs, sems)[:2]
    for cp in local + sends:
        cp.start()


def _rider_wait(rider, ins, outs, sems):
    local, sends, recvs, *second = rider.copies(ins, outs, sems)
    passes, pass_recvs = second if second else ([], [])
    for cp in recvs:
        cp.wait_recv()
    for cp in passes:
        cp.start()
    for cp in pass_recvs:
        cp.wait_recv()
    for cp in sends + passes:
        cp.wait_send()
    for cp in local:
        cp.wait()


PIN_BYTES = 1 << 20


def _in_hbm(a):
    if a.size * a.dtype.itemsize < PIN_BYTES:
        return a
    return pltpu.with_memory_space_constraint(a, pltpu.HBM)


def _call(body, *operands, rider=None, name, grid, in_specs, out_specs, out_shape, scratch_shapes=(),
          compiler_params, input_output_aliases=None):
    operands = [_in_hbm(a) for a in operands]
    if rider is None:
        return pl.pallas_call(
            body, name=name, grid=grid, in_specs=list(in_specs), out_specs=list(out_specs), out_shape=list(out_shape),
            scratch_shapes=list(scratch_shapes), compiler_params=compiler_params,
            input_output_aliases=input_output_aliases or {})(*operands)
    n_in, n_out, n_scr = len(in_specs), len(out_specs), len(scratch_shapes)
    r_in, r_out = len(rider.operands), len(rider.out_shape)

    def riding(*refs):
        refs = list(refs)
        ins, refs = refs[:n_in], refs[n_in:]
        r_ins, refs = refs[:r_in], refs[r_in:]
        outs, refs = refs[:n_out], refs[n_out:]
        r_outs, refs = refs[:r_out], refs[r_out:]
        scr, sems = refs[:n_scr], refs[n_scr:]
        first = pl.program_id(0) == 0
        last = pl.program_id(0) == grid[0] - 1
        for axis in range(1, len(grid)):
            first = first & (pl.program_id(axis) == 0)
            last = last & (pl.program_id(axis) == grid[axis] - 1)

        @pl.when(first)
        def _():
            _rider_start(rider, r_ins, r_outs, sems)

        body(*ins, *outs, *scr)

        @pl.when(last)
        def _():
            _rider_wait(rider, r_ins, r_outs, sems)

    return pl.pallas_call(
        riding, name=name, grid=grid, in_specs=list(in_specs) + [ANY] * r_in,
        out_specs=list(out_specs) + [ANY] * r_out, out_shape=list(out_shape) + list(rider.out_shape),
        scratch_shapes=list(scratch_shapes) + list(rider.sems), compiler_params=compiler_params,
        input_output_aliases={**(input_output_aliases or {}),
                              **{n_in + i: n_out + o for i, o in rider.aliases.items()}})(
            *operands, *[_in_hbm(a) for a in rider.operands])


def _final_exchange(parts, vpack):
    def body(p0, p1, p2, p3, v_ref, o0, o1, o2, o3, vr_ref, send_sems, recv_sems, vsend_sems, vrecv_sems, local_sem):
        x, y, c, _ = _place()
        me = 4 * x + 2 * y + c
        mine = pltpu.make_async_copy(v_ref, vr_ref.at[me], local_sem)
        mine.start()
        copies = [pltpu.make_async_remote_copy(
            src_ref=p, dst_ref=o, send_sem=send_sems.at[k], recv_sem=recv_sems.at[k],
            device_id=(x, y, 1 - c), device_id_type=MESH)
            for k, (p, o) in enumerate(zip((p0, p1, p2, p3), (o0, o1, o2, o3)))]
        flips = [(fx, fy, fc) for fx in (0, 1) for fy in (0, 1) for fc in (0, 1)][1:]
        recvs = []
        for r, (fx, fy, fc) in enumerate(flips):
            peer = (x ^ fx, y ^ fy, c ^ fc)
            sem = dict(send_sem=vsend_sems.at[r], recv_sem=vrecv_sems.at[r], device_id=peer, device_id_type=MESH)
            copies.append(pltpu.make_async_remote_copy(src_ref=v_ref, dst_ref=vr_ref.at[me], **sem))
            recvs.append(pltpu.make_async_remote_copy(
                src_ref=v_ref, dst_ref=vr_ref.at[4 * peer[0] + 2 * peer[1] + peer[2]], **sem))
        for cp in copies:
            cp.start()
        for cp in copies[:4]:
            cp.wait_recv()
        for cp in recvs:
            cp.wait_recv()
        for cp in copies:
            cp.wait_send()
        mine.wait()

    res = pl.pallas_call(
        body, name="final_exchange",
        out_shape=[jax.ShapeDtypeStruct(p.shape, F32) for p in parts]
        + [jax.ShapeDtypeStruct((N_DEV, VPACK_ROWS, D), F32)],
        in_specs=[ANY] * 5, out_specs=[ANY] * 5,
        scratch_shapes=[pltpu.SemaphoreType.DMA((4,)), pltpu.SemaphoreType.DMA((4,)),
                        pltpu.SemaphoreType.DMA((7,)), pltpu.SemaphoreType.DMA((7,)), pltpu.SemaphoreType.DMA],
    )(*parts, vpack)
    return res[:4], res[4]


def _norm_fwd(x, g1, rider):
    tm = 512

    def body(x_ref, g_ref, h_ref):
        xv = x_ref[...]
        r = lax.rsqrt(jnp.mean(xv * xv, axis=-1, keepdims=True) + EPS)
        h_ref[...] = (xv * r * g_ref[...]).astype(BF16)

    row = pl.BlockSpec((tm, D), lambda i: (i, 0))
    return _call(
        body, x, g1, rider=rider, name="norm_fwd", grid=(T // tm,),
        in_specs=[row, pl.BlockSpec((1, D), lambda i: (0, 0))], out_specs=[row],
        out_shape=[jax.ShapeDtypeStruct((T, D), BF16)],
        compiler_params=_params(("arbitrary",), 32))


def _proj_fwd(h, w_in, qg, kg, bd, rider):
    tm, tn = 512, 640

    def body(h_ref, w_ref, qg_ref, kg_ref, bd_ref, p_ref, qn_ref, kn_ref):
        for j in range(NPROJ // tn):
            cols = slice(j * tn, (j + 1) * tn)
            p_ref[:, cols] = jnp.dot(h_ref[...], w_ref[:, cols], preferred_element_type=F32)
        for col, g, dst in ((2, qg_ref, qn_ref), (3, kg_ref, kn_ref)):
            xv = p_ref[:, col * C:(col + 1) * C]
            ms = _segsum(xv * xv, bd_ref[...]) * (1.0 / HEAD)
            dst[...] = xv * lax.rsqrt(ms + EPS) * g[...]

    vec = pl.BlockSpec((1, C), lambda i: (0, 0))
    blk = pl.BlockSpec((tm, C), lambda i: (i, 0))
    return _call(
        body, h, w_in, qg, kg, bd, rider=rider, name="proj_fwd", grid=(T // tm,),
        in_specs=[pl.BlockSpec((tm, D), lambda i: (i, 0)), _resident((D, NPROJ)), vec, vec, _resident((C, C))],
        out_specs=[pl.BlockSpec((tm, NPROJ), lambda i: (i, 0)), blk, blk],
        out_shape=[jax.ShapeDtypeStruct((T, NPROJ), F32), jax.ShapeDtypeStruct((T, C), F32),
                   jax.ShapeDtypeStruct((T, C), F32)],
        compiler_params=_params(("arbitrary",), 40))


CONV_TM = 512
CONV_HALO = 32
CONV_RB = 32
CONV_CB = 64


LANES = 128


def _sp(start, n):
    return (pl.ds(2 * start, n, stride=2), slice(None))


def _lanes(tile):
    return slice(tile * LANES, (tile + 1) * LANES)


def _conv_fwd(proj, conv_w, conv_b, cn_g, cn_b, rider):
    tm, hl, rb, cb = CONV_TM, CONV_HALO, CONV_RB, CONV_CB
    per = tm // hl

    def body(av_ref, ag_ref, hv_ref, hg_ref, w_ref, b_ref, g_ref, bb_ref, cat_ref, cv_ref, sh_ref):
        i = pl.program_id(0)
        for j in range(C // LANES):
            ln_ = _lanes(j)
            glu_h = hv_ref[:, ln_] * _sigmoid(hg_ref[:, ln_])
            sh_ref.at[j][_sp(0, hl)] = jnp.where(i > 0, glu_h, 0.0)
            for r0 in range(0, tm, cb):
                sh_ref.at[j][_sp(hl + r0, cb)] = av_ref[r0:r0 + cb, ln_] * _sigmoid(ag_ref[r0:r0 + cb, ln_])
            for r0 in range(0, tm, cb):
                acc = jnp.zeros((cb, LANES), F32) + b_ref[:, ln_]
                for k in range(CONV_K):
                    acc = acc + w_ref[k:k + 1, ln_] * sh_ref.at[j][_sp(r0 + hl - (CONV_K - 1) + k, cb)]
                cv_ref[r0:r0 + cb, ln_] = acc
        for r0 in range(0, tm, rb):
            acc = cv_ref[r0:r0 + rb, :]
            mu = jnp.mean(acc, axis=-1, keepdims=True)
            xc = acc - mu
            var = jnp.mean(xc * xc, axis=-1, keepdims=True)
            ln = xc * lax.rsqrt(var + EPS) * g_ref[...] + bb_ref[...]
            cat_ref[r0:r0 + rb, :] = (ln * _sigmoid(ln)).astype(BF16)

    halo = lambda col: pl.BlockSpec((hl, C), lambda i: (jnp.maximum(i * per - 1, 0), col))
    vec = pl.BlockSpec((1, C), lambda i: (0, 0))
    return _call(
        body, proj, proj, proj, proj, conv_w, conv_b, cn_g, cn_b, rider=rider, name="conv_fwd", grid=(T // tm,),
        in_specs=[pl.BlockSpec((tm, C), lambda i: (i, 0)), pl.BlockSpec((tm, C), lambda i: (i, 1)),
                  halo(0), halo(1), pl.BlockSpec((CONV_K, C), lambda i: (0, 0)), vec, vec, vec],
        out_specs=[pl.BlockSpec((tm, C), lambda i: (i, 0)), pl.BlockSpec((tm, C), lambda i: (i, 0))],
        out_shape=[jax.ShapeDtypeStruct((T, D), BF16), jax.ShapeDtypeStruct((T, C), F32)],
        scratch_shapes=[pltpu.VMEM((C // LANES, 2 * (tm + hl), LANES), F32)],
        compiler_params=_params(("arbitrary",), 40))


def _stack_heads(a):
    lane = lax.broadcasted_iota(jnp.int32, a.shape, 1)
    zero = jnp.zeros_like(a)
    return jnp.concatenate([jnp.where(lane < HEAD, a, zero), jnp.where(lane >= HEAD, a, zero)], axis=0)


def _unstack_heads(a2):
    lane = lax.broadcasted_iota(jnp.int32, (QB, 2 * HEAD), 1)
    return jnp.where(lane < HEAD, a2[:QB], a2[QB:])


def _stack_cols(a):
    return jnp.concatenate([a[:, 0:1], a[:, HEAD:HEAD + 1]], axis=0)


ATT_WIN = 2048
V_COL = 4 * C // (2 * HEAD)
DO_COL = C // (2 * HEAD)


def _attn_geometry(d):
    sl = ATT_WIN // d
    return sl, sl // QB, QB * d


SPLIT = 4
PIECE = 128


def _gather_streams(src_ref, tmp_ref, d):
    if d <= SPLIT:
        return lambda r, n: _stream(src_ref, r, n, d)
    q = src_ref.shape[0] // SPLIT
    for a in range(SPLIT):
        for off in range(0, q, PIECE):
            tmp_ref[a * q + off:a * q + off + PIECE, :] = src_ref[pl.ds(a + SPLIT * off, PIECE, stride=SPLIT), :]
    return lambda r, n: tmp_ref[pl.ds((r % SPLIT) * q + r // SPLIT, n, stride=d // SPLIT), :]


def _scatter_streams(dst_ref, tmp_ref, d, value_of, n, before=None):
    if d <= SPLIT:
        for r in range(d):
            pos = (pl.ds(r, n, stride=d) if d > 1 else slice(None), slice(None))
            val = value_of(r)
            dst_ref[pos] = val if before is None else val + before[pos]
        return
    q = dst_ref.shape[0] // SPLIT
    for r in range(d):
        tmp_ref[pl.ds((r % SPLIT) * q + r // SPLIT, n, stride=d // SPLIT), :] = value_of(r)
    for a in range(SPLIT):
        for off in range(0, q, PIECE):
            pos = (pl.ds(a + SPLIT * off, PIECE, stride=SPLIT), slice(None))
            val = tmp_ref[a * q + off:a * q + off + PIECE, :]
            dst_ref[pos] = val if before is None else val + before[pos]


def _stream(ref, r, n, d):
    return ref[pl.ds(r, n, stride=d), :] if d > 1 else ref[pl.ds(r, n), :]


def _alibi_tables(d):
    qi = jnp.arange(QB)[:, None]
    kj = jnp.arange(2 * QB)[None, :]
    delta = qi + QB - kj
    band = (delta >= 0) & (delta <= QB)
    dist = (delta * d).astype(F32)
    heads = jnp.arange(8, dtype=F32)
    slopes = 2.0 ** (-(heads + 1.0))
    t = jnp.where(band[None], -slopes[:, None, None] * dist[None], NEG)
    return t.reshape(4, 2 * QB, 2 * QB)


def _attn_specs(d):
    _, _, hr = _attn_geometry(d)
    per = ATT_WIN // hr
    main = lambda off: pl.BlockSpec((ATT_WIN, 2 * HEAD), lambda cb, n: (n, off + cb))
    prev = lambda off: pl.BlockSpec((hr, 2 * HEAD), lambda cb, n: (jnp.maximum(n * per - 1, 0), off + cb))
    nxt = lambda off: pl.BlockSpec((hr, 2 * HEAD), lambda cb, n: (jnp.minimum((n + 1) * per, T // hr - 1), off + cb))
    bias = pl.BlockSpec((None, 2 * QB, 2 * QB), lambda cb, n: (cb, 0, 0))
    return main, prev, nxt, bias


def _attn_fwd(qn, kn, proj, bias, d, rider=None, merge=None):
    sl, nb, hr = _attn_geometry(d)
    slk = QB + sl
    mrows = 256

    def body(q_ref, k_ref, v_ref, kh_ref, vh_ref, bias_ref, *rest):
        if merge is None:
            o_ref, l_ref, qs, ks, vs, os_, ls, tmp = rest
        else:
            oa_ref, la_ref, ob_ref, lb_ref, _, cat_ref, of_ref, lg_ref, qs, ks, vs, os_, ls, tmp, o_ref, l_ref = rest
        n = pl.program_id(1)
        for dst, per, at, src, take in ((qs, sl, 0, q_ref, sl), (ks, slk, 0, kh_ref, QB), (ks, slk, QB, k_ref, sl),
                                        (vs, slk, 0, vh_ref, QB), (vs, slk, QB, v_ref, sl)):
            stream = _gather_streams(src, tmp, d)
            for r in range(d):
                dst[r * per + at:r * per + at + take, :] = stream(r, take).astype(BF16)
        col = lax.broadcasted_iota(jnp.int32, (2 * QB, 2 * QB), 1)
        for r in range(d):
            for b in range(nb):
                rows = slice(r * sl + b * QB, r * sl + (b + 1) * QB)
                keys = slice(r * slk + b * QB, r * slk + (b + 2) * QB)
                s = _nt(_stack_heads(qs[rows, :]), ks[keys, :]) + bias_ref[...]
                if b == 0:
                    s = jnp.where((col < QB) & (n == 0), NEG, s)
                m = jnp.max(s, axis=-1, keepdims=True)
                p = jnp.exp(s - m)
                den = jnp.sum(p, axis=-1, keepdims=True)
                pv = jnp.dot(p.astype(BF16), vs[keys, :], preferred_element_type=F32)
                os_[rows, :] = _unstack_heads(pv / den)
                ls[rows, :] = _unstack_heads(jnp.broadcast_to(m + jnp.log(den), (2 * QB, 2 * HEAD)))
        _scatter_streams(o_ref, tmp, d, lambda r: os_[r * sl:(r + 1) * sl, :], sl)
        _scatter_streams(l_ref, tmp, d, lambda r: ls[r * sl:(r + 1) * sl, :], sl)
        if merge is not None:
            for r0 in range(0, ATT_WIN, mrows):
                rows = slice(r0, r0 + mrows)
                a, b, c = la_ref[rows, :], lb_ref[rows, :], l_ref[rows, :]
                m = jnp.maximum(jnp.maximum(a, b), c)
                e0, e1, e2 = jnp.exp(a - m), jnp.exp(b - m), jnp.exp(c - m)
                den = e0 + e1 + e2
                o = (e0 * oa_ref[rows, :] + e1 * ob_ref[rows, :] + e2 * o_ref[rows, :]) / den
                of_ref[rows, :] = o
                cat_ref[rows, :] = o.astype(BF16)
                lg_ref[rows, :] = m + jnp.log(den)

    main, prev, _, bias_spec = _attn_specs(d)
    lanes = 2 * HEAD
    operands = [qn, kn, proj, kn, proj, bias]
    in_specs = [main(0), main(0), main(V_COL), prev(0), prev(V_COL), bias_spec]
    scratch = [pltpu.VMEM((ATT_WIN, lanes), BF16), pltpu.VMEM((ATT_WIN + hr, lanes), BF16),
               pltpu.VMEM((ATT_WIN + hr, lanes), BF16), pltpu.VMEM((ATT_WIN, lanes), F32),
               pltpu.VMEM((ATT_WIN, lanes), F32), pltpu.VMEM((ATT_WIN, lanes), F32)]
    if merge is None:
        out_specs = [main(0), main(0)]
        out_shape = [jax.ShapeDtypeStruct((T, C), F32)] * 2
        aliases = None
    else:
        operands += list(merge)
        in_specs += [main(0)] * 4 + [ANY]
        out_specs = [main(C // lanes), main(0), main(0)]
        out_shape = [jax.ShapeDtypeStruct((T, D), BF16), jax.ShapeDtypeStruct((T, C), F32),
                     jax.ShapeDtypeStruct((T, C), F32)]
        scratch += [pltpu.VMEM((ATT_WIN, lanes), F32)] * 2
        aliases = {len(operands) - 1: 0}
    return _call(
        body, *operands, rider=rider, name=f"attn_fwd_d{d}", grid=(C // lanes, T // ATT_WIN),
        in_specs=in_specs, out_specs=out_specs, out_shape=out_shape, scratch_shapes=scratch,
        input_output_aliases=aliases, compiler_params=_params(("arbitrary", "arbitrary"), 48))


def _out_up(x, cat, w_out, g2, w_up, rider):
    tm, tn = 512, NUP // 4

    def body(x_ref, cat_ref, wo_ref, g_ref, wu_ref, x1_ref, h2_ref, up_ref):
        x1 = x_ref[...] + jnp.dot(cat_ref[...], wo_ref[...], preferred_element_type=F32)
        x1_ref[...] = x1
        r = lax.rsqrt(jnp.mean(x1 * x1, axis=-1, keepdims=True) + EPS)
        h2_ref[...] = (x1 * r * g_ref[...]).astype(BF16)
        for j in range(NUP // tn):
            cols = slice(j * tn, (j + 1) * tn)
            up_ref[:, cols] = jnp.dot(h2_ref[...], wu_ref[:, cols], preferred_element_type=F32)

    row = pl.BlockSpec((tm, D), lambda i: (i, 0))
    return _call(
        body, x, cat, w_out, g2, w_up, rider=rider, name="out_up", grid=(T // tm,),
        in_specs=[row, row, _resident((D, D)), pl.BlockSpec((1, D), lambda i: (0, 0)), _resident((D, NUP))],
        out_specs=[row, row, pl.BlockSpec((tm, NUP), lambda i: (i, 0))],
        out_shape=[jax.ShapeDtypeStruct((T, D), F32), jax.ShapeDtypeStruct((T, D), BF16),
                   jax.ShapeDtypeStruct((T, NUP), F32)],
        compiler_params=_params(("arbitrary",), 58))


FF_TM = 256
FF_HALO = 8
FF_RB = 64
FF_TILES = DFF // LANES


def _ff_taps(fw_ref, fb_ref, tile):
    cols = _lanes(tile)
    return [fw_ref[k:k + 1, cols] for k in range(FF_K)] + [fb_ref[:, cols]]


def _ff_conv(ext_ref, taps, tile, r0):
    base = FF_HALO + r0
    acc = taps[3] + taps[0] * ext_ref.at[tile][_sp(base - 2, FF_RB)]
    acc = acc + taps[1] * ext_ref.at[tile][_sp(base - 1, FF_RB)]
    return acc + taps[2] * ext_ref.at[tile][_sp(base, FF_RB)]


def _ffn(up, ffconv_w, ffconv_b, w_down, x1, target):
    tm, hl = FF_TM, FF_HALO
    per = tm // hl
    nt = T // tm
    tiles = 2 * FF_TILES

    def body(up_ref, uh_ref, fw_ref, fb_ref, wd_ref, x1_ref, tg_ref, act_ref, dy_ref, loss_ref, dup_ref, gff_ref,
             ext_ref, gv_ref, dact_ref, carry_ref):
        i = pl.program_id(0)

        @pl.when(i == 0)
        def _():
            gff_ref[...] = jnp.zeros_like(gff_ref)
            loss_ref[...] = jnp.zeros_like(loss_ref)
            carry_ref[...] = jnp.zeros_like(carry_ref)

        for j in range(tiles):
            ext_ref.at[j][_sp(0, hl)] = jnp.where(i < nt - 1, uh_ref[:, _lanes(j)], 0.0)
            for r0 in range(0, tm, FF_RB):
                ext_ref.at[j][_sp(hl + r0, FF_RB)] = up_ref[r0:r0 + FF_RB, _lanes(j)]
        for c in range(FF_TILES):
            gate_taps, val_taps = _ff_taps(fw_ref, fb_ref, c), _ff_taps(fw_ref, fb_ref, FF_TILES + c)
            for r0 in range(0, tm, FF_RB):
                rows = slice(r0, r0 + FF_RB)
                gate = _ff_conv(ext_ref, gate_taps, c, r0)
                val = _ff_conv(ext_ref, val_taps, FF_TILES + c, r0)
                gv_ref[rows, _lanes(c)] = gate
                gv_ref[rows, _lanes(FF_TILES + c)] = val
                act_ref[rows, _lanes(c)] = (gate * _sigmoid(gate) * val).astype(BF16)
        err = x1_ref[...] + jnp.dot(act_ref[...], wd_ref[...], preferred_element_type=F32) - tg_ref[...]
        dy_ref[...] = err * (1.0 / D)
        loss_ref[...] += jnp.sum(err * err)
        dact_ref[...] = _nt(dy_ref[...].astype(BF16), wd_ref[...])

        for c in range(FF_TILES):
            for r0 in range(0, tm, FF_RB):
                rows = slice(r0, r0 + FF_RB)
                gate, val = gv_ref[rows, _lanes(c)], gv_ref[rows, _lanes(FF_TILES + c)]
                sg = _sigmoid(gate)
                da = dact_ref[rows, _lanes(c)]
                ext_ref.at[c][_sp(r0, FF_RB)] = da * val * (sg + gate * sg * (1.0 - sg))
                ext_ref.at[FF_TILES + c][_sp(r0, FF_RB)] = da * gate * sg
        fold = lambda a: jnp.sum(a.reshape(FF_RB // 8, 8, LANES), axis=0)
        for c in range(tiles):
            cols = _lanes(c)
            ext_ref.at[c][_sp(tm, hl)] = carry_ref[c]
            taps = [fw_ref[k:k + 1, cols] for k in range(FF_K)]
            acc = [jnp.zeros((8, LANES), F32) for _ in range(FF_K + 1)]
            for r0 in range(0, tm, FF_RB):
                shifted = [ext_ref.at[c][_sp(r0 + k, FF_RB)] for k in range(FF_K)]
                u = up_ref[r0:r0 + FF_RB, cols]
                dup = taps[2] * shifted[0] + taps[1] * shifted[1] + taps[0] * shifted[2]
                dup_ref[r0:r0 + FF_RB, cols] = dup.astype(BF16)
                for k in range(FF_K):
                    acc[2 - k] = acc[2 - k] + fold(shifted[k] * u)
                acc[FF_K] = acc[FF_K] + fold(shifted[0])
            for k in range(FF_K + 1):
                gff_ref[k:k + 1, cols] += jnp.sum(acc[k], axis=0, keepdims=True)
            carry_ref[c] = ext_ref.at[c][_sp(0, hl)]

    rev = lambda i: (nt - 1 - i, 0)
    row = pl.BlockSpec((tm, D), rev)
    wide = pl.BlockSpec((tm, NUP), rev)
    return _call(
        body, up, up, ffconv_w, ffconv_b, w_down, x1, target, name="ffn", grid=(nt,),
        in_specs=[wide, pl.BlockSpec((hl, NUP), lambda i: (jnp.maximum((nt - 1 - i) * per - 1, 0), 0)),
                  pl.BlockSpec((FF_K, NUP), lambda i: (0, 0)), pl.BlockSpec((1, NUP), lambda i: (0, 0)),
                  _resident((DFF, D)), row, row],
        out_specs=[pl.BlockSpec((tm, DFF), rev), row, pl.BlockSpec((8, 128), lambda i: (0, 0)), wide,
                   pl.BlockSpec((8, NUP), lambda i: (0, 0))],
        out_shape=[jax.ShapeDtypeStruct((T, DFF), BF16), jax.ShapeDtypeStruct((T, D), F32),
                   jax.ShapeDtypeStruct((8, 128), F32), jax.ShapeDtypeStruct((T, NUP), BF16),
                   jax.ShapeDtypeStruct((8, NUP), F32)],
        scratch_shapes=[pltpu.VMEM((tiles, 2 * (tm + hl), LANES), F32), pltpu.VMEM((tm, NUP), F32),
                        pltpu.VMEM((tm, DFF), F32), pltpu.VMEM((tiles, hl, LANES), F32)],
        compiler_params=_params(("arbitrary",), 58))


def _weight_grad(a, g, bm, bn, tk, name):
    m, n = a.shape[1], g.shape[1]
    nk = T // tk

    def body(a_ref, g_ref, of_ref, ob_ref):
        k = pl.program_id(2)

        @pl.when(k == 0)
        def _():
            of_ref[...] = jnp.zeros_like(of_ref)
        of_ref[...] += _tn_dot(a_ref[...].astype(BF16), g_ref[...].astype(BF16))

        @pl.when(k == nk - 1)
        def _():
            ob_ref[...] = of_ref[...].astype(BF16)

    out = pl.BlockSpec((bm, bn), lambda i, j, k: (i, j))
    return _call(
        body, a, g, name=name, grid=(m // bm, n // bn, nk),
        in_specs=[pl.BlockSpec((tk, bm), lambda i, j, k: (k, i)), pl.BlockSpec((tk, bn), lambda i, j, k: (k, j))],
        out_specs=[out, out],
        out_shape=[jax.ShapeDtypeStruct((m, n), F32), jax.ShapeDtypeStruct((m, n), BF16)],
        compiler_params=_params(("parallel", "parallel", "arbitrary"), 56))


def _norm_bwd_mm(dz, w, xin, base, gain, name, rider):
    kdim = dz.shape[1]
    tm = 512

    def body(dz_ref, w_ref, x_ref, b_ref, g_ref, dx_ref, gg_ref):
        @pl.when(pl.program_id(0) == 0)
        def _():
            gg_ref[...] = jnp.zeros_like(gg_ref)

        xv = x_ref[...]
        dh = _nt(dz_ref[...], w_ref[...])
        r = lax.rsqrt(jnp.mean(xv * xv, axis=-1, keepdims=True) + EPS)
        t = dh * g_ref[...]
        dx_ref[...] = b_ref[...] + r * t - xv * (r * r * r) * jnp.mean(t * xv, axis=-1, keepdims=True)
        gg_ref[...] += jnp.sum(dh * xv * r, axis=0, keepdims=True)

    row = pl.BlockSpec((tm, D), lambda i: (i, 0))
    vec = pl.BlockSpec((1, D), lambda i: (0, 0))
    return _call(
        body, dz, w, xin, base, gain, rider=rider, name=name, grid=(T // tm,),
        in_specs=[pl.BlockSpec((tm, kdim), lambda i: (i, 0)), _resident((D, kdim)), row, row, vec],
        out_specs=[row, vec],
        out_shape=[jax.ShapeDtypeStruct((T, D), F32), jax.ShapeDtypeStruct((1, D), F32)],
        compiler_params=_params(("arbitrary",), 48))


def _outproj_bwd(dx1, w_out, o_f32, bd):
    tm = 512

    def body(d_ref, w_ref, o_ref, bd_ref, dc_ref, dl_ref):
        dc_ref[...] = _nt(d_ref[...].astype(BF16), w_ref[...])
        dl_ref[...] = _segsum(dc_ref[:, C:2 * C] * o_ref[...], bd_ref[...])

    row = pl.BlockSpec((tm, D), lambda i: (i, 0))
    blk = pl.BlockSpec((tm, C), lambda i: (i, 0))
    return _call(
        body, dx1, w_out, o_f32, bd, name="outproj_bwd", grid=(T // tm,),
        in_specs=[row, _resident((D, D)), blk, _resident((C, C))], out_specs=[row, blk],
        out_shape=[jax.ShapeDtypeStruct((T, D), F32), jax.ShapeDtypeStruct((T, C), F32)],
        compiler_params=_params(("arbitrary",), 32))


def _conv_bwd(dcat, cv, proj, conv_w, cn_g, cn_b, rider):
    tm, hl, rb, cb = CONV_TM, CONV_HALO, CONV_RB, CONV_CB
    per = tm // hl
    nt = T // tm
    tiles = C // LANES

    def body(du_ref, dun_ref, cv_ref, cvn_ref, av_ref, ag_ref, hv_ref, hg_ref, w_ref, g_ref, bb_ref,
             dp_ref, gv_ref, gw_ref, dsh_ref, gsh_ref):
        i = pl.program_id(0)

        @pl.when(i == 0)
        def _():
            gv_ref[...] = jnp.zeros_like(gv_ref)
            gw_ref[...] = jnp.zeros_like(gw_ref)

        def ln_bwd(du, cvv):
            mu = jnp.mean(cvv, axis=-1, keepdims=True)
            xc = cvv - mu
            rs = lax.rsqrt(jnp.mean(xc * xc, axis=-1, keepdims=True) + EPS)
            xh = xc * rs
            ln = xh * g_ref[...] + bb_ref[...]
            sg = _sigmoid(ln)
            dln = du * (sg + ln * sg * (1.0 - sg))
            dxh = dln * g_ref[...]
            dcv = rs * (dxh - jnp.mean(dxh, axis=-1, keepdims=True)
                        - xh * jnp.mean(dxh * xh, axis=-1, keepdims=True))
            return dcv, dln, xh

        for r0 in range(0, tm, rb):
            dcv, dln, xh = ln_bwd(du_ref[r0:r0 + rb, :], cv_ref[r0:r0 + rb, :])
            for j in range(tiles):
                dsh_ref.at[j][_sp(r0, rb)] = dcv[:, _lanes(j)]
            gv_ref[0:1, :] += jnp.sum(dln * xh, axis=0, keepdims=True)
            gv_ref[1:2, :] += jnp.sum(dln, axis=0, keepdims=True)
            gv_ref[2:3, :] += jnp.sum(dcv, axis=0, keepdims=True)
        dcv_n, _, _ = ln_bwd(dun_ref[...], cvn_ref[...])
        dcv_n = jnp.where(i < nt - 1, dcv_n, 0.0)
        for j in range(tiles):
            ln_ = _lanes(j)
            dsh_ref.at[j][_sp(tm, hl)] = dcv_n[:, ln_]
            glu_h = hv_ref[:, ln_] * _sigmoid(hg_ref[:, ln_])
            gsh_ref.at[j][_sp(0, hl)] = jnp.where(i > 0, glu_h, 0.0)
            for r0 in range(0, tm, cb):
                gsh_ref.at[j][_sp(hl + r0, cb)] = av_ref[r0:r0 + cb, ln_] * _sigmoid(ag_ref[r0:r0 + cb, ln_])

        for j in range(tiles):
            ln_ = _lanes(j)
            for r0 in range(0, tm, cb):
                dglu = jnp.zeros((cb, LANES), F32)
                for k in range(CONV_K):
                    dglu = dglu + w_ref[k:k + 1, ln_] * dsh_ref.at[j][_sp(r0 + (CONV_K - 1) - k, cb)]
                av = av_ref[r0:r0 + cb, ln_]
                sg = _sigmoid(ag_ref[r0:r0 + cb, ln_])
                dp_ref[r0:r0 + cb, ln_] = (dglu * sg).astype(BF16)
                dp_ref[r0:r0 + cb, _lanes(tiles + j)] = (dglu * av * sg * (1.0 - sg)).astype(BF16)
            for k in range(CONV_K):
                part = jnp.zeros((8, LANES), F32)
                for r0 in range(0, tm, cb):
                    prod = dsh_ref.at[j][_sp(r0, cb)] * gsh_ref.at[j][_sp(r0 + hl - (CONV_K - 1) + k, cb)]
                    part = part + jnp.sum(prod.reshape(cb // 8, 8, LANES), axis=0)
                gw_ref[k:k + 1, ln_] += jnp.sum(part, axis=0, keepdims=True)

    main = lambda col: pl.BlockSpec((tm, C), lambda i: (i, col))
    prev = lambda col: pl.BlockSpec((hl, C), lambda i: (jnp.maximum(i * per - 1, 0), col))
    nxt = pl.BlockSpec((hl, C), lambda i: (jnp.minimum((i + 1) * per, T // hl - 1), 0))
    vec = pl.BlockSpec((1, C), lambda i: (0, 0))
    return _call(
        body, dcat, dcat, cv, cv, proj, proj, proj, proj, conv_w, cn_g, cn_b, rider=rider, name="conv_bwd",
        grid=(nt,),
        in_specs=[main(0), nxt, main(0), nxt, main(0), main(1), prev(0), prev(1),
                  pl.BlockSpec((CONV_K, C), lambda i: (0, 0)), vec, vec],
        out_specs=[pl.BlockSpec((tm, 2 * C), lambda i: (i, 0)), pl.BlockSpec((8, C), lambda i: (0, 0)),
                   pl.BlockSpec((32, C), lambda i: (0, 0))],
        out_shape=[jax.ShapeDtypeStruct((T, NPROJ), BF16), jax.ShapeDtypeStruct((8, C), F32),
                   jax.ShapeDtypeStruct((32, C), F32)],
        scratch_shapes=[pltpu.VMEM((C // LANES, 2 * (tm + hl), LANES), F32)] * 2,
        compiler_params=_params(("arbitrary",), 48))


def _attn_bwd_unit(qs, dos, lgs, dls, rows, kc, vc, biasv, invalid_prev):
    qst, dost = _stack_heads(qs[rows, :]), _stack_heads(dos[rows, :])
    s = _nt(qst, kc) + biasv
    if invalid_prev is not None:
        col = lax.broadcasted_iota(jnp.int32, s.shape, 1)
        s = jnp.where((col < QB) & invalid_prev, NEG, s)
    p = jnp.exp(s - _stack_cols(lgs[rows, :]))
    ds = p * (_nt(dost, vc) - _stack_cols(dls[rows, :]))
    dsb = ds.astype(BF16)
    dq = _unstack_heads(jnp.dot(dsb, kc, preferred_element_type=F32))
    return dq, _tn_dot(dsb, qst), _tn_dot(p.astype(BF16), dost)


def _attn_bwd_lagged(qn, kn, proj, dcat, lg, dl, bias, d, earlier):
    assert QB * d == ATT_WIN
    n_win = T // ATT_WIN
    lanes = 2 * HEAD

    def body(q_ref, k_ref, v_ref, do_ref, lg_ref, dl_ref, kh_ref, vh_ref, bias_ref, eq_ref, ek_ref, ev_ref,
             dq_ref, dk_ref, dv_ref, qs, dos, lgs, dls, ks, vs, dqs, ck, cv, ok, ov, tmp):
        n = pl.program_id(1)
        block = lambda buf: (lambda r: buf[r * QB:(r + 1) * QB, :])

        @pl.when(n == 0)
        def _():
            ck[...] = jnp.zeros_like(ck)
            cv[...] = jnp.zeros_like(cv)

        @pl.when(n < n_win)
        def _():
            for dst, per, at, src, dt in ((qs, QB, 0, q_ref, BF16), (dos, QB, 0, do_ref, BF16),
                                          (lgs, QB, 0, lg_ref, F32), (dls, QB, 0, dl_ref, F32),
                                          (ks, 2 * QB, 0, kh_ref, BF16), (ks, 2 * QB, QB, k_ref, BF16),
                                          (vs, 2 * QB, 0, vh_ref, BF16), (vs, 2 * QB, QB, v_ref, BF16)):
                stream = _gather_streams(src, tmp, d)
                for r in range(d):
                    dst[r * per + at:r * per + at + QB, :] = stream(r, QB).astype(dt)
            for r in range(d):
                rows = slice(r * QB, (r + 1) * QB)
                keys = slice(2 * r * QB, (2 * r + 2) * QB)
                dq, dkc, dvc = _attn_bwd_unit(qs, dos, lgs, dls, rows, ks[keys, :], vs[keys, :], bias_ref[...], n == 0)
                dqs[rows, :] = dq
                ok[rows, :] = ck[rows, :] + dkc[:QB]
                ov[rows, :] = cv[rows, :] + dvc[:QB]
                ck[rows, :] = dkc[QB:]
                cv[rows, :] = dvc[QB:]
            _scatter_streams(dq_ref, tmp, d, block(dqs), QB, eq_ref)
            _scatter_streams(dk_ref, tmp, d, block(ok), QB, ek_ref)
            _scatter_streams(dv_ref, tmp, d, block(ov), QB, ev_ref)

        @pl.when(n == n_win)
        def _():
            _scatter_streams(dk_ref, tmp, d, block(ck), QB, ek_ref)
            _scatter_streams(dv_ref, tmp, d, block(cv), QB, ev_ref)

    cur = lambda off: pl.BlockSpec((ATT_WIN, lanes), lambda cb, n: (jnp.minimum(n, n_win - 1), off + cb))
    prev = lambda off: pl.BlockSpec(
        (ATT_WIN, lanes), lambda cb, n: (jnp.maximum(jnp.minimum(n, n_win - 1) - 1, 0), off + cb))
    late = pl.BlockSpec((ATT_WIN, lanes), lambda cb, n: (jnp.maximum(n - 1, 0), cb))
    buf = lambda rows, dt: pltpu.VMEM((rows, lanes), dt)
    return _call(
        body, qn, kn, proj, dcat, lg, dl, kn, proj, bias, *earlier, name=f"attn_bwd_d{d}",
        grid=(C // lanes, n_win + 1),
        in_specs=[cur(0), cur(0), cur(V_COL), cur(DO_COL), cur(0), cur(0), prev(0), prev(V_COL),
                  pl.BlockSpec((None, 2 * QB, 2 * QB), lambda cb, n: (cb, 0, 0)), cur(0), late, late],
        out_specs=[cur(0), late, late],
        out_shape=[jax.ShapeDtypeStruct((T, C), F32)] * 3,
        scratch_shapes=[buf(ATT_WIN, BF16), buf(ATT_WIN, BF16), buf(ATT_WIN, F32), buf(ATT_WIN, F32),
                        buf(2 * ATT_WIN, BF16), buf(2 * ATT_WIN, BF16)] + [buf(ATT_WIN, F32)] * 6,
        compiler_params=_params(("arbitrary", "arbitrary"), 48))


def _attn_bwd(qn, kn, proj, dcat, lg, dl, bias, d, earlier=None, rider=None):
    sl, nb, hr = _attn_geometry(d)
    slk = QB + sl
    slq = sl + QB
    n_win = T // ATT_WIN

    def body(q_ref, k_ref, v_ref, do_ref, lg_ref, dl_ref, kh_ref, vh_ref, qx_ref, dox_ref, lgx_ref, dlx_ref,
             bias_ref, *rest):
        sums = rest[:3] if earlier is not None else (None, None, None)
        dq_ref, dk_ref, dv_ref, qs, dos, lgs, dls, ks, vs, dqs, dks, dvs = rest[-12:]
        n = pl.program_id(1)
        for r in range(d):
            for dst, src, nx, dt in ((qs, q_ref, qx_ref, BF16), (dos, do_ref, dox_ref, BF16),
                                     (lgs, lg_ref, lgx_ref, F32), (dls, dl_ref, dlx_ref, F32)):
                dst[r * slq:r * slq + sl, :] = _stream(src, r, sl, d).astype(dt)
                dst[r * slq + sl:(r + 1) * slq, :] = _stream(nx, r, QB, d).astype(dt)
            for dst, halo, src in ((ks, kh_ref, k_ref), (vs, vh_ref, v_ref)):
                dst[r * slk:r * slk + QB, :] = _stream(halo, r, QB, d).astype(BF16)
                dst[r * slk + QB:(r + 1) * slk, :] = _stream(src, r, sl, d).astype(BF16)
        dks[...] = jnp.zeros_like(dks)
        dvs[...] = jnp.zeros_like(dvs)

        def unit(rows, kc, vc, biasv, invalid_prev):
            return _attn_bwd_unit(qs, dos, lgs, dls, rows, kc, vc, biasv, invalid_prev)

        for r in range(d):
            for b in range(nb):
                rows = slice(r * slq + b * QB, r * slq + (b + 1) * QB)
                keys = slice(r * slk + b * QB, r * slk + (b + 2) * QB)
                dq, dkc, dvc = unit(rows, ks[keys, :], vs[keys, :], bias_ref[...], (n == 0) if b == 0 else None)
                dqs[r * sl + b * QB:r * sl + (b + 1) * QB, :] = dq
                if b == 0:
                    dks[r * sl:r * sl + QB, :] += dkc[QB:]
                    dvs[r * sl:r * sl + QB, :] += dvc[QB:]
                else:
                    dks[r * sl + (b - 1) * QB:r * sl + (b + 1) * QB, :] += dkc
                    dvs[r * sl + (b - 1) * QB:r * sl + (b + 1) * QB, :] += dvc

        @pl.when(n < n_win - 1)
        def _():
            for r in range(d):
                rows = slice(r * slq + sl, (r + 1) * slq)
                keys = slice(r * slk + sl, (r + 1) * slk)
                _, dkc, dvc = unit(rows, ks[keys, :], vs[keys, :], bias_ref[:, 0:QB], None)
                dks[(r + 1) * sl - QB:(r + 1) * sl, :] += dkc
                dvs[(r + 1) * sl - QB:(r + 1) * sl, :] += dvc

        for dst, src, before in zip((dq_ref, dk_ref, dv_ref), (dqs, dks, dvs), sums):
            for r in range(d):
                pos = (pl.ds(r, sl, stride=d) if d > 1 else slice(None), slice(None))
                val = src[r * sl:(r + 1) * sl, :]
                dst[pos] = val if before is None else val + before[pos]

    main, prev, nxt, bias_spec = _attn_specs(d)
    lanes = 2 * HEAD
    return _call(
        body, qn, kn, proj, dcat, lg, dl, kn, proj, qn, dcat, lg, dl, bias, *(earlier or ()), rider=rider,
        name=f"attn_bwd_d{d}", grid=(C // lanes, n_win),
        in_specs=[main(0), main(0), main(V_COL), main(DO_COL), main(0), main(0), prev(0), prev(V_COL),
                  nxt(0), nxt(DO_COL), nxt(0), nxt(0), bias_spec] + ([main(0)] * 3 if earlier is not None else []),
        out_specs=[main(0)] * 3,
        out_shape=[jax.ShapeDtypeStruct((T, C), F32)] * 3,
        scratch_shapes=[pltpu.VMEM((ATT_WIN + hr, lanes), BF16), pltpu.VMEM((ATT_WIN + hr, lanes), BF16),
                        pltpu.VMEM((ATT_WIN + hr, lanes), F32), pltpu.VMEM((ATT_WIN + hr, lanes), F32),
                        pltpu.VMEM((ATT_WIN + hr, lanes), BF16), pltpu.VMEM((ATT_WIN + hr, lanes), BF16),
                        pltpu.VMEM((ATT_WIN, lanes), F32), pltpu.VMEM((ATT_WIN, lanes), F32),
                        pltpu.VMEM((ATT_WIN, lanes), F32)],
        compiler_params=_params(("arbitrary", "arbitrary"), 48))


def _qk_norm_bwd(dn_sum, proj, col, gain, bd, dproj, name):
    tm = 512

    def body(d0, x_ref, g_ref, bd_ref, dp_in, dp_ref, gg_ref):
        del dp_in

        @pl.when(pl.program_id(0) == 0)
        def _():
            gg_ref[...] = jnp.zeros_like(gg_ref)
        dn = d0[...]
        xv = x_ref[...]
        r = lax.rsqrt(_segsum(xv * xv, bd_ref[...]) * (1.0 / HEAD) + EPS)
        t = dn * g_ref[...]
        mean_tx = _segsum(t * xv, bd_ref[...]) * (1.0 / HEAD)
        dp_ref[...] = (r * t - xv * (r * r * r) * mean_tx).astype(BF16)
        gg_ref[...] += jnp.sum(dn * xv * r, axis=0, keepdims=True)

    blk = pl.BlockSpec((tm, C), lambda i: (i, 0))
    vec = pl.BlockSpec((1, C), lambda i: (0, 0))
    return _call(
        body, dn_sum, proj, gain, bd, dproj, name=name, grid=(T // tm,),
        in_specs=[blk, pl.BlockSpec((tm, C), lambda i: (i, col)), vec, _resident((C, C)), ANY],
        out_specs=[pl.BlockSpec((tm, C), lambda i: (i, col)), vec],
        out_shape=[jax.ShapeDtypeStruct((T, NPROJ), BF16), jax.ShapeDtypeStruct((1, C), F32)],
        input_output_aliases={4: 0},
        compiler_params=_params(("arbitrary",), 32))


def _v_bwd(dv_sum, dproj):
    tm = 512

    def body(d0, dp_in, dp_ref):
        del dp_in
        dp_ref[...] = d0[...].astype(BF16)

    blk = pl.BlockSpec((tm, C), lambda i: (i, 0))
    return _call(
        body, dv_sum, dproj, name="v_bwd", grid=(T // tm,),
        in_specs=[blk, ANY],
        out_specs=[pl.BlockSpec((tm, C), lambda i: (i, 4))],
        out_shape=[jax.ShapeDtypeStruct((T, NPROJ), BF16)],
        input_output_aliases={1: 0},
        compiler_params=_params(("parallel",), 32))[0]


def _adamw(w, g, m, v):
    m = ADAM_B1 * m + (1.0 - ADAM_B1) * g
    v = ADAM_B2 * v + (1.0 - ADAM_B2) * (g * g)
    m_hat = m / (1.0 - ADAM_B1 ** ADAM_STEP)
    v_hat = v / (1.0 - ADAM_B2 ** ADAM_STEP)
    delta = -ADAM_LR * (m_hat / (jnp.sqrt(v_hat) + ADAM_EPS) + ADAM_WD * w)
    return delta, m, v


def _row_block(shape):
    rows = shape[0]
    for cand in (256, 128, 64, 88, 32, 8):
        if rows % cand == 0 and cand * shape[1] * 4 <= (2 << 20):
            return cand
    return 8


def _partial_sum(own, recv, name):
    br = _row_block(own.shape)
    cols = own.shape[1]

    def body(o_ref, r_ref, p_ref):
        p_ref[...] = ((o_ref[...] + r_ref[0].astype(F32)) + r_ref[1].astype(F32)) + r_ref[2].astype(F32)

    blk = pl.BlockSpec((br, cols), lambda i: (i, 0))
    return _call(
        body, own, recv, name=name, grid=(own.shape[0] // br,),
        in_specs=[blk, pl.BlockSpec((3, br, cols), lambda i: (0, i, 0))], out_specs=[blk],
        out_shape=[jax.ShapeDtypeStruct(own.shape, F32)],
        compiler_params=_params(("parallel",), 32))[0]


def _adamw_mat(p_own, p_sib, w, m, v, name):
    br = _row_block(w.shape)
    cols = w.shape[1]

    def body(a_ref, b_ref, w_ref, m_ref, v_ref, g_ref, d_ref, nm_ref, nv_ref):
        g = a_ref[...] + b_ref[...]
        delta, nm, nv = _adamw(w_ref[...], g, m_ref[...], v_ref[...])
        g_ref[...] = g
        d_ref[...] = delta
        nm_ref[...] = nm
        nv_ref[...] = nv

    blk = pl.BlockSpec((br, cols), lambda i: (i, 0))
    return _call(
        body, p_own, p_sib, w, m, v, name=name, grid=(w.shape[0] // br,),
        in_specs=[blk] * 5, out_specs=[blk] * 4,
        out_shape=[jax.ShapeDtypeStruct(w.shape, F32)] * 4,
        compiler_params=_params(("parallel",), 40))


def _vec_reduce(vrecv):
    def body(v_ref, o_ref):
        acc = v_ref[0]
        for r in range(1, N_DEV):
            acc = acc + v_ref[r]
        o_ref[...] = acc

    return pl.pallas_call(
        body, name="vec_reduce",
        out_shape=jax.ShapeDtypeStruct((VPACK_ROWS, D), F32),
        compiler_params=_params((), 32),
    )(vrecv)


def _adamw_small(w, g, m, v):
    def body(w_ref, g_ref, m_ref, v_ref, d_ref, nm_ref, nv_ref):
        delta, nm, nv = _adamw(w_ref[...], g_ref[...], m_ref[...], v_ref[...])
        d_ref[...] = delta
        nm_ref[...] = nm
        nv_ref[...] = nv

    return pl.pallas_call(
        body, name="adamw_small",
        out_shape=[jax.ShapeDtypeStruct(w.shape, F32)] * 3,
        compiler_params=_params((), 32),
    )(w, g, m, v)


def _pack(parts, rows):
    flat = jnp.concatenate([p.reshape(-1) for p in parts])
    return jnp.pad(flat, (0, rows * D - flat.shape[0])).reshape(rows, D)


def _unpack(packed, shapes):
    flat = packed.reshape(-1)
    out, off = [], 0
    for shp in shapes:
        size = 1
        for s in shp:
            size *= s
        out.append(flat[off:off + size].reshape(shp))
        off += size
    return out


def _no_comm(shards, row_sharded, peers=(0, 1, 2), into=None):
    del row_sharded, peers, into
    return None, lambda res, n: (res, shards)


def _with_comm(shards, row_sharded, peers=(0, 1, 2), into=None):
    rider = _gather_rider(shards, row_sharded, peers, into)
    return rider, lambda res, n: (res[:n], res[n:])


def _local_step(x, target, norm1_g, conv_b, cn_g, cn_b, q_norm_g, k_norm_g, norm2_g, ffconv_b,
                first_weights, late_weights, comm=True):
    row = lambda a: a.reshape(1, -1)
    head_of = jnp.arange(C) // HEAD
    bd = (head_of[:, None] == head_of[None, :]).astype(BF16)
    qg = row(jnp.tile(q_norm_g, C // HEAD) * (HEAD ** -0.5))
    kg = row(jnp.tile(k_norm_g, C // HEAD))
    biases = [_alibi_tables(d) for d in PATTERN_DILATIONS]
    gather = _with_comm if comm else _no_comm
    grad_rider = (lambda g, rs: _grad_rider(g[1], g[0], rs)) if comm else (lambda g, rs: None)

    rider, split = gather(first_weights, (False, False, False))
    if comm:
        rider = _halved_gather_rider(first_weights, (True, False, False))
    (h,), (w_in, conv_w, ffconv_w) = split(_norm_fwd(x, row(norm1_g), rider), 1)
    rider, split = gather(late_weights[0:1], (True,))
    (proj, qn, kn), (w_out,) = split(_proj_fwd(h, w_in, qg, kg, bd, rider), 3)
    rider, split = gather(late_weights[1:2], (False,), (0, 1))
    (cat, cv), w_up_part = split(_conv_fwd(proj, conv_w, row(conv_b), row(cn_g), row(cn_b), rider), 2)
    fwd = [_attn_fwd(qn, kn, proj, biases[i], d) for i, d in enumerate(PATTERN_DILATIONS[:-1])]
    rider, split = gather(late_weights[1:2], (False,), (2,), w_up_part)
    merge = (fwd[0][0], fwd[0][1], fwd[1][0], fwd[1][1], cat)
    (cat, o_f32, lg), (w_up,) = split(
        _attn_fwd(qn, kn, proj, biases[-1], PATTERN_DILATIONS[-1], rider, merge), 3)
    rider, split = gather(late_weights[2:3], (True,))
    (x1, h2, up), (w_down,) = split(_out_up(x, cat, w_out, row(norm2_g), w_up, rider), 3)
    act, dy, loss_acc, dup, gff = _ffn(up, ffconv_w, row(ffconv_b), w_down, x1, target)
    gw_down = _weight_grad(act, dy, DFF // 2, D, 1024, "grad_w_down")
    res = _norm_bwd_mm(dup, w_up, x1, dy, row(norm2_g), "up_bwd", grad_rider(gw_down, True))
    (dx1, g_norm2), ex_down = res[:2], res[2:]
    gw_up = _weight_grad(h2, dup, D, NUP // 4, 2048, "grad_w_up")
    dcat, dl = _outproj_bwd(dx1, w_out, o_f32, bd)
    gw_out = _weight_grad(cat, dx1, D, D, 2048, "grad_w_out")
    res = _conv_bwd(dcat, cv, proj, conv_w, row(cn_g), row(cn_b), grad_rider(gw_up, False))
    (dproj, gconv_vec, gconv_w), ex_up = res[:3], res[3:]
    sums, ex_out = None, []
    for i, d in enumerate(PATTERN_DILATIONS):
        if QB * d == ATT_WIN:
            res = _attn_bwd_lagged(qn, kn, proj, dcat, lg, dl, biases[i], d, sums)
        else:
            res = _attn_bwd(qn, kn, proj, dcat, lg, dl, biases[i], d, sums,
                            grad_rider(gw_out, True) if i == 0 else None)
        sums = res[:3]
        ex_out = res[3:] if i == 0 else ex_out
    dproj, gq_lane = _qk_norm_bwd(sums[0], proj, 2, qg, bd, dproj, "q_norm_bwd")
    dproj, gk_lane = _qk_norm_bwd(sums[1], proj, 3, kg, bd, dproj, "k_norm_bwd")
    dproj = _v_bwd(sums[2], dproj)
    gw_in = _weight_grad(h, dproj, D, NPROJ // 4, 2048, "grad_w_in")
    res = _norm_bwd_mm(dproj, w_in, x, dx1, row(norm1_g), "in_bwd", grad_rider(gw_in, False))
    (dx, g_norm1), ex_in = res[:2], res[2:]

    loss = loss_acc[0, 0] * (0.5 / D)
    g_qg = jnp.sum(gq_lane.reshape(C // HEAD, HEAD), axis=0) * (HEAD ** -0.5)
    g_kg = jnp.sum(gk_lane.reshape(C // HEAD, HEAD), axis=0)
    small = [g_norm1[0], gconv_vec[2], gconv_vec[0], gconv_vec[1], g_qg, g_kg, g_norm2[0], gff[3],
             gconv_w[:CONV_K], gff[:FF_K]]
    mats = [ex_in, ex_out, ex_up, ex_down] if comm else [gw_in, gw_out, gw_up, gw_down]
    return loss, dx, mats, small


def kernel(x, norm1_g, w_in, conv_w, conv_b, cn_g, cn_b, q_norm_g, k_norm_g, w_out, norm2_g, w_up, ffconv_w, ffconv_b, w_down, loss_target, m_norm1_g, m_w_in, m_conv_w, m_conv_b, m_cn_g, m_cn_b, m_q_norm_g, m_k_norm_g, m_w_out, m_norm2_g, m_w_up, m_ffconv_w, m_ffconv_b, m_w_down, v_norm1_g, v_w_in, v_conv_w, v_conv_b, v_cn_g, v_cn_b, v_q_norm_g, v_k_norm_g, v_w_out, v_norm2_g, v_w_up, v_ffconv_w, v_ffconv_b, v_w_down):
    chip = 2 * lax.axis_index("x") + lax.axis_index("y")

    loss, dx, mats, small = _local_step(
        x[0], loss_target[0], norm1_g, conv_b, cn_g, cn_b, q_norm_g, k_norm_g, norm2_g, ffconv_b,
        [w_in.astype(BF16), conv_w, ffconv_w], [w.astype(BF16) for w in (w_out, w_up, w_down)])

    names = ("w_in", "w_out", "w_up", "w_down")
    parts = [_partial_sum(own, recv, "partial_" + names[k]) for k, (recv, own) in enumerate(mats)]
    sib, vrecv = _final_exchange(parts, _pack(small + [loss.reshape(1)], VPACK_ROWS))
    ws = (w_in, w_out, w_up, w_down)
    ms = (m_w_in, m_w_out, m_w_up, m_w_down)
    vs = (v_w_in, v_w_out, v_w_up, v_w_down)
    mat = [_adamw_mat(parts[k], sib[k], ws[k], ms[k], vs[k], "adamw_" + names[k]) for k in range(4)]

    vsum = _vec_reduce(vrecv)
    vec_shapes = [(D,), (C,), (C,), (C,), (HEAD,), (HEAD,), (D,), (NUP,), (CONV_K, C), (FF_K, NUP), (1,)]
    gsmall = _unpack(vsum, vec_shapes)
    g_conv_w = lax.dynamic_slice_in_dim(gsmall[8], chip * (C // N_CHIPS), C // N_CHIPS, axis=1)
    g_ffconv_w = lax.dynamic_slice_in_dim(gsmall[9], chip * (NUP // N_CHIPS), NUP // N_CHIPS, axis=1)
    gs = gsmall[:8] + [g_conv_w, g_ffconv_w]
    w_s = [norm1_g, conv_b, cn_g, cn_b, q_norm_g, k_norm_g, norm2_g, ffconv_b, conv_w, ffconv_w]
    m_s = [m_norm1_g, m_conv_b, m_cn_g, m_cn_b, m_q_norm_g, m_k_norm_g, m_norm2_g, m_ffconv_b, m_conv_w, m_ffconv_w]
    v_s = [v_norm1_g, v_conv_b, v_cn_g, v_cn_b, v_q_norm_g, v_k_norm_g, v_norm2_g, v_ffconv_b, v_conv_w, v_ffconv_w]
    shapes_s = [a.shape for a in w_s]
    d_p, m_p, v_p = _adamw_small(_pack(w_s, SPACK_ROWS), _pack(gs, SPACK_ROWS), _pack(m_s, SPACK_ROWS),
                                 _pack(v_s, SPACK_ROWS))
    d_s, nm_s, nv_s = _unpack(d_p, shapes_s), _unpack(m_p, shapes_s), _unpack(v_p, shapes_s)

    def ordered(sm, mt):
        return [sm[0], mt[0], sm[8], sm[1], sm[2], sm[3], sm[4], sm[5], mt[1], sm[6], mt[2], sm[9], sm[7], mt[3]]

    loss_all = gsmall[10][0]
    grads = ordered(gs, [r[0] for r in mat])
    deltas = ordered(d_s, [r[1] for r in mat])
    new_m = ordered(nm_s, [r[2] for r in mat])
    new_v = ordered(nv_s, [r[3] for r in mat])
    return (loss_all, dx[None], *grads, *deltas, *new_m, *new_v)
```

```python
import types

import jax
import jax.numpy as jnp
from jax import lax
from jax.experimental import pallas as pl
from jax.experimental.pallas import tpu as pltpu

T = 8192
D = 1024
C = 512
NPROJ = 2560
DFF = 2816
NUP = 2 * DFF
CONV_K = 31
FF_K = 3
HEAD = 64
EPS = 1e-6
NEG = -1e30
N_CHIPS = 4
N_DEV = 8
PATTERN_DILATIONS = (1, 4, 16)
QB = 128

ADAM_LR = 0.001
ADAM_B1 = 0.9
ADAM_B2 = 0.999
ADAM_EPS = 1e-08
ADAM_WD = 0.01
ADAM_STEP = 10

F32 = jnp.float32
BF16 = jnp.bfloat16
MESH = pl.DeviceIdType.MESH
ANY = pl.BlockSpec(memory_space=pl.ANY)

VPACK_ROWS = 48
SPACK_ROWS = 24


def _params(sem, vmem_mb):
    return pltpu.CompilerParams(dimension_semantics=sem, vmem_limit_bytes=vmem_mb << 20)


def _resident(shape):
    return pl.BlockSpec(shape, lambda i: (0, 0), pipeline_mode=pl.Buffered(1))


def _nt(a, b):
    return lax.dot_general(a, b, (((1,), (1,)), ((), ())), preferred_element_type=F32)


def _tn_dot(a, b):
    return lax.dot_general(a, b, (((0,), (0,)), ((), ())), preferred_element_type=F32)


def _sigmoid(x):
    return 1.0 / (1.0 + jnp.exp(-x))


def _segsum(x, bd):
    hi = x.astype(BF16)
    lo = (x - hi.astype(F32)).astype(BF16)
    return (jnp.dot(hi, bd, preferred_element_type=F32)
            + jnp.dot(lo, bd, preferred_element_type=F32))


def _place():
    x, y, c = lax.axis_index("x"), lax.axis_index("y"), lax.axis_index("c")
    chips = [(1 - x, y), (x, 1 - y), (1 - x, 1 - y)]
    return x, y, c, chips


def _block_of(ref, shard_shape, row_sharded, s):
    r, cdim = shard_shape
    if row_sharded:
        return ref.at[pl.ds(s * r, r), :]
    return ref.at[:, pl.ds(s * cdim, cdim)]


def _full_shape(shard_shape, row_sharded):
    r, cdim = shard_shape
    return (r * N_CHIPS, cdim) if row_sharded else (r, cdim * N_CHIPS)


def _gather_rider(shards, row_sharded):
    n = len(shards)
    shapes = [a.shape for a in shards]

    def copies(ins, outs, sems):
        send_sems, recv_sems, local_sems = sems
        x, y, c, chips = _place()
        me = 2 * x + y
        place = lambda k, s: _block_of(outs[k], shapes[k], row_sharded[k], s)
        local = [pltpu.make_async_copy(ins[k], place(k, me), local_sems.at[k]) for k in range(n)]
        sends, recvs = [], []
        for k in range(n):
            for j, (px, py) in enumerate(chips):
                sem = dict(send_sem=send_sems.at[3 * k + j], recv_sem=recv_sems.at[3 * k + j],
                           device_id=(px, py, c), device_id_type=MESH)
                sends.append(pltpu.make_async_remote_copy(src_ref=ins[k], dst_ref=place(k, me), **sem))
                recvs.append(pltpu.make_async_remote_copy(src_ref=ins[k], dst_ref=place(k, 2 * px + py), **sem))
        return local, sends, recvs

    return types.SimpleNamespace(
        operands=list(shards), copies=copies, aliases={},
        out_shape=[jax.ShapeDtypeStruct(_full_shape(s, rs), a.dtype) for s, rs, a in zip(shapes, row_sharded, shards)],
        sems=[pltpu.SemaphoreType.DMA((3 * n,)), pltpu.SemaphoreType.DMA((3 * n,)), pltpu.SemaphoreType.DMA((n,))])


def _halved_gather_rider(shards, halve):
    n = len(shards)
    shapes = [a.shape for a in shards]

    def copies(ins, outs, sems):
        send_sems, recv_sems, local_sems, pass_send_sems, pass_recv_sems = sems
        x, y, c, chips = _place()
        me = 2 * x + y
        place = lambda k, s: _block_of(outs[k], shapes[k], False, s)

        def half(ref, k, which):
            rows = shapes[k][0] // 2
            return ref.at[pl.ds(which * rows, rows), :]

        local = [pltpu.make_async_copy(ins[k], place(k, me), local_sems.at[k]) for k in range(n)]
        sends, recvs, passes, pass_recvs = [], [], [], []
        for k in range(n):
            for j, (px, py) in enumerate(chips):
                theirs = 2 * px + py
                sem = dict(send_sem=send_sems.at[3 * k + j], recv_sem=recv_sems.at[3 * k + j],
                           device_id=(px, py, c), device_id_type=MESH)
                if not halve[k]:
                    sends.append(pltpu.make_async_remote_copy(src_ref=ins[k], dst_ref=place(k, me), **sem))
                    recvs.append(pltpu.make_async_remote_copy(src_ref=ins[k], dst_ref=place(k, theirs), **sem))
                    continue
                sends.append(pltpu.make_async_remote_copy(
                    src_ref=half(ins[k], k, c), dst_ref=half(place(k, me), k, c), **sem))
                recvs.append(pltpu.make_async_remote_copy(
                    src_ref=half(ins[k], k, c), dst_ref=half(place(k, theirs), k, c), **sem))
                sem = dict(send_sem=pass_send_sems.at[3 * k + j], recv_sem=pass_recv_sems.at[3 * k + j],
                           device_id=(x, y, 1 - c), device_id_type=MESH)
                mine, other = half(place(k, theirs), k, c), half(place(k, theirs), k, 1 - c)
                passes.append(pltpu.make_async_remote_copy(src_ref=mine, dst_ref=mine, **sem))
                pass_recvs.append(pltpu.make_async_remote_copy(src_ref=other, dst_ref=other, **sem))
        return local, sends, recvs, passes, pass_recvs

    return types.SimpleNamespace(
        operands=list(shards), copies=copies, aliases={},
        out_shape=[jax.ShapeDtypeStruct(_full_shape(s, False), a.dtype) for s, a in zip(shapes, shards)],
        sems=[pltpu.SemaphoreType.DMA((3 * n,)), pltpu.SemaphoreType.DMA((3 * n,)), pltpu.SemaphoreType.DMA((n,)),
              pltpu.SemaphoreType.DMA((3 * n,)), pltpu.SemaphoreType.DMA((3 * n,))])


def _grad_rider(g_bf16, g_f32, row_sharded):
    shard = tuple(d // N_CHIPS if (i == 0) == row_sharded else d for i, d in enumerate(g_f32.shape))

    def copies(ins, outs, sems):
        send_sems, recv_sems, local_sems = sems
        gb, gf = ins
        rec, own = outs
        x, y, c, chips = _place()
        me = 2 * x + y
        local = [pltpu.make_async_copy(_block_of(gf, shard, row_sharded, me), own, local_sems.at[0])]
        sends, recvs = [], []
        for j, (px, py) in enumerate(chips):
            sem = dict(send_sem=send_sems.at[j], recv_sem=recv_sems.at[j], device_id=(px, py, c), device_id_type=MESH)
            sends.append(pltpu.make_async_remote_copy(
                src_ref=_block_of(gb, shard, row_sharded, 2 * px + py), dst_ref=rec.at[j], **sem))
            recvs.append(pltpu.make_async_remote_copy(
                src_ref=_block_of(gb, shard, row_sharded, me), dst_ref=rec.at[j], **sem))
        return local, sends, recvs

    return types.SimpleNamespace(
        operands=[g_bf16, g_f32], copies=copies, aliases={},
        out_shape=[jax.ShapeDtypeStruct((3,) + shard, BF16), jax.ShapeDtypeStruct(shard, F32)],
        sems=[pltpu.SemaphoreType.DMA((3,)), pltpu.SemaphoreType.DMA((3,)), pltpu.SemaphoreType.DMA((1,))])


def _rider_start(rider, ins, outs, sems):
    local, sends = rider.copies(ins, outs, sems)[:2]
    for cp in local + sends:
        cp.start()


def _rider_wait(rider, ins, outs, sems):
    local, sends, recvs, *second = rider.copies(ins, outs, sems)
    passes, pass_recvs = second if second else ([], [])
    for cp in recvs:
        cp.wait_recv()
    for cp in passes:
        cp.start()
    for cp in pass_recvs:
        cp.wait_recv()
    for cp in sends + passes:
        cp.wait_send()
    for cp in local:
        cp.wait()


PIN_BYTES = 1 << 20


def _in_hbm(a):
    if a.size * a.dtype.itemsize < PIN_BYTES:
        return a
    return pltpu.with_memory_space_constraint(a, pltpu.HBM)


def _call(body, *operands, rider=None, name, grid, in_specs, out_specs, out_shape, scratch_shapes=(),
          compiler_params, input_output_aliases=None):
    operands = [_in_hbm(a) for a in operands]
    if rider is None:
        return pl.pallas_call(
            body, name=name, grid=grid, in_specs=list(in_specs), out_specs=list(out_specs), out_shape=list(out_shape),
            scratch_shapes=list(scratch_shapes), compiler_params=compiler_params,
            input_output_aliases=input_output_aliases or {})(*operands)
    n_in, n_out, n_scr = len(in_specs), len(out_specs), len(scratch_shapes)
    r_in, r_out = len(rider.operands), len(rider.out_shape)

    def riding(*refs):
        refs = list(refs)
        ins, refs = refs[:n_in], refs[n_in:]
        r_ins, refs = refs[:r_in], refs[r_in:]
        outs, refs = refs[:n_out], refs[n_out:]
        r_outs, refs = refs[:r_out], refs[r_out:]
        scr, sems = refs[:n_scr], refs[n_scr:]
        first = pl.program_id(0) == 0
        last = pl.program_id(0) == grid[0] - 1
        for axis in range(1, len(grid)):
            first = first & (pl.program_id(axis) == 0)
            last = last & (pl.program_id(axis) == grid[axis] - 1)

        @pl.when(first)
        def _():
            _rider_start(rider, r_ins, r_outs, sems)

        body(*ins, *outs, *scr)

        @pl.when(last)
        def _():
            _rider_wait(rider, r_ins, r_outs, sems)

    return pl.pallas_call(
        riding, name=name, grid=grid, in_specs=list(in_specs) + [ANY] * r_in,
        out_specs=list(out_specs) + [ANY] * r_out, out_shape=list(out_shape) + list(rider.out_shape),
        scratch_shapes=list(scratch_shapes) + list(rider.sems), compiler_params=compiler_params,
        input_output_aliases={**(input_output_aliases or {}),
                              **{n_in + i: n_out + o for i, o in rider.aliases.items()}})(
            *operands, *[_in_hbm(a) for a in rider.operands])


def _final_exchange(parts, vpack):
    def body(p0, p1, p2, p3, v_ref, o0, o1, o2, o3, vr_ref, send_sems, recv_sems, vsend_sems, vrecv_sems, local_sem):
        x, y, c, _ = _place()
        me = 4 * x + 2 * y + c
        mine = pltpu.make_async_copy(v_ref, vr_ref.at[me], local_sem)
        mine.start()
        copies = [pltpu.make_async_remote_copy(
            src_ref=p, dst_ref=o, send_sem=send_sems.at[k], recv_sem=recv_sems.at[k],
            device_id=(x, y, 1 - c), device_id_type=MESH)
            for k, (p, o) in enumerate(zip((p0, p1, p2, p3), (o0, o1, o2, o3)))]
        flips = [(fx, fy, fc) for fx in (0, 1) for fy in (0, 1) for fc in (0, 1)][1:]
        recvs = []
        for r, (fx, fy, fc) in enumerate(flips):
            peer = (x ^ fx, y ^ fy, c ^ fc)
            sem = dict(send_sem=vsend_sems.at[r], recv_sem=vrecv_sems.at[r], device_id=peer, device_id_type=MESH)
            copies.append(pltpu.make_async_remote_copy(src_ref=v_ref, dst_ref=vr_ref.at[me], **sem))
            recvs.append(pltpu.make_async_remote_copy(
                src_ref=v_ref, dst_ref=vr_ref.at[4 * peer[0] + 2 * peer[1] + peer[2]], **sem))
        for cp in copies:
            cp.start()
        for cp in copies[:4]:
            cp.wait_recv()
        for cp in recvs:
            cp.wait_recv()
        for cp in copies:
            cp.wait_send()
        mine.wait()

    res = pl.pallas_call(
        body, name="final_exchange",
        out_shape=[jax.ShapeDtypeStruct(p.shape, F32) for p in parts]
        + [jax.ShapeDtypeStruct((N_DEV, VPACK_ROWS, D), F32)],
        in_specs=[ANY] * 5, out_specs=[ANY] * 5,
        scratch_shapes=[pltpu.SemaphoreType.DMA((4,)), pltpu.SemaphoreType.DMA((4,)),
                        pltpu.SemaphoreType.DMA((7,)), pltpu.SemaphoreType.DMA((7,)), pltpu.SemaphoreType.DMA],
    )(*parts, vpack)
    return res[:4], res[4]


def _norm_fwd(x, g1, rider):
    tm = 512

    def body(x_ref, g_ref, h_ref):
        xv = x_ref[...]
        r = lax.rsqrt(jnp.mean(xv * xv, axis=-1, keepdims=True) + EPS)
        h_ref[...] = (xv * r * g_ref[...]).astype(BF16)

    row = pl.BlockSpec((tm, D), lambda i: (i, 0))
    return _call(
        body, x, g1, rider=rider, name="norm_fwd", grid=(T // tm,),
        in_specs=[row, pl.BlockSpec((1, D), lambda i: (0, 0))], out_specs=[row],
        out_shape=[jax.ShapeDtypeStruct((T, D), BF16)],
        compiler_params=_params(("arbitrary",), 32))


def _proj_fwd(h, w_in, qg, kg, bd, rider):
    tm, tn = 512, 640

    def body(h_ref, w_ref, qg_ref, kg_ref, bd_ref, p_ref, qn_ref, kn_ref):
        for j in range(NPROJ // tn):
            cols = slice(j * tn, (j + 1) * tn)
            p_ref[:, cols] = jnp.dot(h_ref[...], w_ref[:, cols], preferred_element_type=F32)
        for col, g, dst in ((2, qg_ref, qn_ref), (3, kg_ref, kn_ref)):
            xv = p_ref[:, col * C:(col + 1) * C]
            ms = _segsum(xv * xv, bd_ref[...]) * (1.0 / HEAD)
            dst[...] = xv * lax.rsqrt(ms + EPS) * g[...]

    vec = pl.BlockSpec((1, C), lambda i: (0, 0))
    blk = pl.BlockSpec((tm, C), lambda i: (i, 0))
    return _call(
        body, h, w_in, qg, kg, bd, rider=rider, name="proj_fwd", grid=(T // tm,),
        in_specs=[pl.BlockSpec((tm, D), lambda i: (i, 0)), _resident((D, NPROJ)), vec, vec, _resident((C, C))],
        out_specs=[pl.BlockSpec((tm, NPROJ), lambda i: (i, 0)), blk, blk],
        out_shape=[jax.ShapeDtypeStruct((T, NPROJ), F32), jax.ShapeDtypeStruct((T, C), F32),
                   jax.ShapeDtypeStruct((T, C), F32)],
        compiler_params=_params(("arbitrary",), 40))


CONV_TM = 512
CONV_HALO = 32
CONV_RB = 32
CONV_CB = 64


LANES = 128


def _sp(start, n):
    return (pl.ds(2 * start, n, stride=2), slice(None))


def _lanes(tile):
    return slice(tile * LANES, (tile + 1) * LANES)


def _conv_fwd(proj, conv_w, conv_b, cn_g, cn_b, rider):
    tm, hl, rb, cb = CONV_TM, CONV_HALO, CONV_RB, CONV_CB
    per = tm // hl

    def body(av_ref, ag_ref, hv_ref, hg_ref, w_ref, b_ref, g_ref, bb_ref, cat_ref, cv_ref, sh_ref):
        i = pl.program_id(0)
        for j in range(C // LANES):
            ln_ = _lanes(j)
            glu_h = hv_ref[:, ln_] * _sigmoid(hg_ref[:, ln_])
            sh_ref.at[j][_sp(0, hl)] = jnp.where(i > 0, glu_h, 0.0)
            for r0 in range(0, tm, cb):
                sh_ref.at[j][_sp(hl + r0, cb)] = av_ref[r0:r0 + cb, ln_] * _sigmoid(ag_ref[r0:r0 + cb, ln_])
            for r0 in range(0, tm, cb):
                acc = jnp.zeros((cb, LANES), F32) + b_ref[:, ln_]
                for k in range(CONV_K):
                    acc = acc + w_ref[k:k + 1, ln_] * sh_ref.at[j][_sp(r0 + hl - (CONV_K - 1) + k, cb)]
                cv_ref[r0:r0 + cb, ln_] = acc
        for r0 in range(0, tm, rb):
            acc = cv_ref[r0:r0 + rb, :]
            mu = jnp.mean(acc, axis=-1, keepdims=True)
            xc = acc - mu
            var = jnp.mean(xc * xc, axis=-1, keepdims=True)
            ln = xc * lax.rsqrt(var + EPS) * g_ref[...] + bb_ref[...]
            cat_ref[r0:r0 + rb, :] = (ln * _sigmoid(ln)).astype(BF16)

    halo = lambda col: pl.BlockSpec((hl, C), lambda i: (jnp.maximum(i * per - 1, 0), col))
    vec = pl.BlockSpec((1, C), lambda i: (0, 0))
    return _call(
        body, proj, proj, proj, proj, conv_w, conv_b, cn_g, cn_b, rider=rider, name="conv_fwd", grid=(T // tm,),
        in_specs=[pl.BlockSpec((tm, C), lambda i: (i, 0)), pl.BlockSpec((tm, C), lambda i: (i, 1)),
                  halo(0), halo(1), pl.BlockSpec((CONV_K, C), lambda i: (0, 0)), vec, vec, vec],
        out_specs=[pl.BlockSpec((tm, C), lambda i: (i, 0)), pl.BlockSpec((tm, C), lambda i: (i, 0))],
        out_shape=[jax.ShapeDtypeStruct((T, D), BF16), jax.ShapeDtypeStruct((T, C), F32)],
        scratch_shapes=[pltpu.VMEM((C // LANES, 2 * (tm + hl), LANES), F32)],
        compiler_params=_params(("arbitrary",), 40))


def _stack_heads(a):
    lane = lax.broadcasted_iota(jnp.int32, a.shape, 1)
    zero = jnp.zeros_like(a)
    return jnp.concatenate([jnp.where(lane < HEAD, a, zero), jnp.where(lane >= HEAD, a, zero)], axis=0)


def _unstack_heads(a2):
    lane = lax.broadcasted_iota(jnp.int32, (QB, 2 * HEAD), 1)
    return jnp.where(lane < HEAD, a2[:QB], a2[QB:])


def _stack_cols(a):
    return jnp.concatenate([a[:, 0:1], a[:, HEAD:HEAD + 1]], axis=0)


ATT_WIN = 2048
V_COL = 4 * C // (2 * HEAD)
DO_COL = C // (2 * HEAD)


def _attn_geometry(d):
    sl = ATT_WIN // d
    return sl, sl // QB, QB * d


SPLIT = 4
PIECE = 128


def _gather_streams(src_ref, tmp_ref, d):
    if d <= SPLIT:
        return lambda r, n: _stream(src_ref, r, n, d)
    q = src_ref.shape[0] // SPLIT
    for a in range(SPLIT):
        for off in range(0, q, PIECE):
            tmp_ref[a * q + off:a * q + off + PIECE, :] = src_ref[pl.ds(a + SPLIT * off, PIECE, stride=SPLIT), :]
    return lambda r, n: tmp_ref[pl.ds((r % SPLIT) * q + r // SPLIT, n, stride=d // SPLIT), :]


def _scatter_streams(dst_ref, tmp_ref, d, value_of, n, before=None):
    if d <= SPLIT:
        for r in range(d):
            pos = (pl.ds(r, n, stride=d) if d > 1 else slice(None), slice(None))
            val = value_of(r)
            dst_ref[pos] = val if before is None else val + before[pos]
        return
    q = dst_ref.shape[0] // SPLIT
    for r in range(d):
        tmp_ref[pl.ds((r % SPLIT) * q + r // SPLIT, n, stride=d // SPLIT), :] = value_of(r)
    for a in range(SPLIT):
        for off in range(0, q, PIECE):
            pos = (pl.ds(a + SPLIT * off, PIECE, stride=SPLIT), slice(None))
            val = tmp_ref[a * q + off:a * q + off + PIECE, :]
            dst_ref[pos] = val if before is None else val + before[pos]


def _stream(ref, r, n, d):
    return ref[pl.ds(r, n, stride=d), :] if d > 1 else ref[pl.ds(r, n), :]


def _alibi_tables(d):
    qi = jnp.arange(QB)[:, None]
    kj = jnp.arange(2 * QB)[None, :]
    delta = qi + QB - kj
    band = (delta >= 0) & (delta <= QB)
    dist = (delta * d).astype(F32)
    heads = jnp.arange(8, dtype=F32)
    slopes = 2.0 ** (-(heads + 1.0))
    t = jnp.where(band[None], -slopes[:, None, None] * dist[None], NEG)
    return t.reshape(4, 2 * QB, 2 * QB)


def _attn_specs(d):
    _, _, hr = _attn_geometry(d)
    per = ATT_WIN // hr
    main = lambda off: pl.BlockSpec((ATT_WIN, 2 * HEAD), lambda cb, n: (n, off + cb))
    prev = lambda off: pl.BlockSpec((hr, 2 * HEAD), lambda cb, n: (jnp.maximum(n * per - 1, 0), off + cb))
    nxt = lambda off: pl.BlockSpec((hr, 2 * HEAD), lambda cb, n: (jnp.minimum((n + 1) * per, T // hr - 1), off + cb))
    bias = pl.BlockSpec((None, 2 * QB, 2 * QB), lambda cb, n: (cb, 0, 0))
    return main, prev, nxt, bias


def _attn_fwd(qn, kn, proj, bias, d, rider=None, merge=None):
    sl, nb, hr = _attn_geometry(d)
    slk = QB + sl
    mrows = 256

    def body(q_ref, k_ref, v_ref, kh_ref, vh_ref, bias_ref, *rest):
        if merge is None:
            o_ref, l_ref, qs, ks, vs, os_, ls, tmp = rest
        else:
            oa_ref, la_ref, ob_ref, lb_ref, _, cat_ref, of_ref, lg_ref, qs, ks, vs, os_, ls, tmp, o_ref, l_ref = rest
        n = pl.program_id(1)
        for dst, per, at, src, take in ((qs, sl, 0, q_ref, sl), (ks, slk, 0, kh_ref, QB), (ks, slk, QB, k_ref, sl),
                                        (vs, slk, 0, vh_ref, QB), (vs, slk, QB, v_ref, sl)):
            stream = _gather_streams(src, tmp, d)
            for r in range(d):
                dst[r * per + at:r * per + at + take, :] = stream(r, take).astype(BF16)
        col = lax.broadcasted_iota(jnp.int32, (2 * QB, 2 * QB), 1)
        for r in range(d):
            for b in range(nb):
                rows = slice(r * sl + b * QB, r * sl + (b + 1) * QB)
                keys = slice(r * slk + b * QB, r * slk + (b + 2) * QB)
                s = _nt(_stack_heads(qs[rows, :]), ks[keys, :]) + bias_ref[...]
                if b == 0:
                    s = jnp.where((col < QB) & (n == 0), NEG, s)
                m = jnp.max(s, axis=-1, keepdims=True)
                p = jnp.exp(s - m)
                den = jnp.sum(p, axis=-1, keepdims=True)
                pv = jnp.dot(p.astype(BF16), vs[keys, :], preferred_element_type=F32)
                os_[rows, :] = _unstack_heads(pv / den)
                ls[rows, :] = _unstack_heads(jnp.broadcast_to(m + jnp.log(den), (2 * QB, 2 * HEAD)))
        _scatter_streams(o_ref, tmp, d, lambda r: os_[r * sl:(r + 1) * sl, :], sl)
        _scatter_streams(l_ref, tmp, d, lambda r: ls[r * sl:(r + 1) * sl, :], sl)
        if merge is not None:
            for r0 in range(0, ATT_WIN, mrows):
                rows = slice(r0, r0 + mrows)
                a, b, c = la_ref[rows, :], lb_ref[rows, :], l_ref[rows, :]
                m = jnp.maximum(jnp.maximum(a, b), c)
                e0, e1, e2 = jnp.exp(a - m), jnp.exp(b - m), jnp.exp(c - m)
                den = e0 + e1 + e2
                o = (e0 * oa_ref[rows, :] + e1 * ob_ref[rows, :] + e2 * o_ref[rows, :]) / den
                of_ref[rows, :] = o
                cat_ref[rows, :] = o.astype(BF16)
                lg_ref[rows, :] = m + jnp.log(den)

    main, prev, _, bias_spec = _attn_specs(d)
    lanes = 2 * HEAD
    operands = [qn, kn, proj, kn, proj, bias]
    in_specs = [main(0), main(0), main(V_COL), prev(0), prev(V_COL), bias_spec]
    scratch = [pltpu.VMEM((ATT_WIN, lanes), BF16), pltpu.VMEM((ATT_WIN + hr, lanes), BF16),
               pltpu.VMEM((ATT_WIN + hr, lanes), BF16), pltpu.VMEM((ATT_WIN, lanes), F32),
               pltpu.VMEM((ATT_WIN, lanes), F32), pltpu.VMEM((ATT_WIN, lanes), F32)]
    if merge is None:
        out_specs = [main(0), main(0)]
        out_shape = [jax.ShapeDtypeStruct((T, C), F32)] * 2
        aliases = None
    else:
        operands += list(merge)
        in_specs += [main(0)] * 4 + [ANY]
        out_specs = [main(C // lanes), main(0), main(0)]
        out_shape = [jax.ShapeDtypeStruct((T, D), BF16), jax.ShapeDtypeStruct((T, C), F32),
                     jax.ShapeDtypeStruct((T, C), F32)]
        scratch += [pltpu.VMEM((ATT_WIN, lanes), F32)] * 2
        aliases = {len(operands) - 1: 0}
    return _call(
        body, *operands, rider=rider, name=f"attn_fwd_d{d}", grid=(C // lanes, T // ATT_WIN),
        in_specs=in_specs, out_specs=out_specs, out_shape=out_shape, scratch_shapes=scratch,
        input_output_aliases=aliases, compiler_params=_params(("arbitrary", "arbitrary"), 48))


def _out_up(x, cat, w_out, g2, w_up, rider):
    tm, tn = 512, NUP // 4

    def body(x_ref, cat_ref, wo_ref, g_ref, wu_ref, x1_ref, h2_ref, up_ref):
        x1 = x_ref[...] + jnp.dot(cat_ref[...], wo_ref[...], preferred_element_type=F32)
        x1_ref[...] = x1
        r = lax.rsqrt(jnp.mean(x1 * x1, axis=-1, keepdims=True) + EPS)
        h2_ref[...] = (x1 * r * g_ref[...]).astype(BF16)
        for j in range(NUP // tn):
            cols = slice(j * tn, (j + 1) * tn)
            up_ref[:, cols] = jnp.dot(h2_ref[...], wu_ref[:, cols], preferred_element_type=F32)

    row = pl.BlockSpec((tm, D), lambda i: (i, 0))
    return _call(
        body, x, cat, w_out, g2, w_up, rider=rider, name="out_up", grid=(T // tm,),
        in_specs=[row, row, _resident((D, D)), pl.BlockSpec((1, D), lambda i: (0, 0)), _resident((D, NUP))],
        out_specs=[row, row, pl.BlockSpec((tm, NUP), lambda i: (i, 0))],
        out_shape=[jax.ShapeDtypeStruct((T, D), F32), jax.ShapeDtypeStruct((T, D), BF16),
                   jax.ShapeDtypeStruct((T, NUP), F32)],
        compiler_params=_params(("arbitrary",), 58))


FF_TM = 256
FF_HALO = 8
FF_RB = 64
FF_TILES = DFF // LANES


def _ff_taps(fw_ref, fb_ref, tile):
    cols = _lanes(tile)
    return [fw_ref[k:k + 1, cols] for k in range(FF_K)] + [fb_ref[:, cols]]


def _ff_conv(ext_ref, taps, tile, r0):
    base = FF_HALO + r0
    acc = taps[3] + taps[0] * ext_ref.at[tile][_sp(base - 2, FF_RB)]
    acc = acc + taps[1] * ext_ref.at[tile][_sp(base - 1, FF_RB)]
    return acc + taps[2] * ext_ref.at[tile][_sp(base, FF_RB)]


def _ffn(up, ffconv_w, ffconv_b, w_down, x1, target):
    tm, hl = FF_TM, FF_HALO
    per = tm // hl
    nt = T // tm
    tiles = 2 * FF_TILES

    def body(up_ref, uh_ref, fw_ref, fb_ref, wd_ref, x1_ref, tg_ref, act_ref, dy_ref, loss_ref, dup_ref, gff_ref,
             ext_ref, gv_ref, dact_ref, carry_ref):
        i = pl.program_id(0)

        @pl.when(i == 0)
        def _():
            gff_ref[...] = jnp.zeros_like(gff_ref)
            loss_ref[...] = jnp.zeros_like(loss_ref)
            carry_ref[...] = jnp.zeros_like(carry_ref)

        for j in range(tiles):
            ext_ref.at[j][_sp(0, hl)] = jnp.where(i < nt - 1, uh_ref[:, _lanes(j)], 0.0)
            for r0 in range(0, tm, FF_RB):
                ext_ref.at[j][_sp(hl + r0, FF_RB)] = up_ref[r0:r0 + FF_RB, _lanes(j)]
        for c in range(FF_TILES):
            gate_taps, val_taps = _ff_taps(fw_ref, fb_ref, c), _ff_taps(fw_ref, fb_ref, FF_TILES + c)
            for r0 in range(0, tm, FF_RB):
                rows = slice(r0, r0 + FF_RB)
                gate = _ff_conv(ext_ref, gate_taps, c, r0)
                val = _ff_conv(ext_ref, val_taps, FF_TILES + c, r0)
                gv_ref[rows, _lanes(c)] = gate
                gv_ref[rows, _lanes(FF_TILES + c)] = val
                act_ref[rows, _lanes(c)] = (gate * _sigmoid(gate) * val).astype(BF16)
        err = x1_ref[...] + jnp.dot(act_ref[...], wd_ref[...], preferred_element_type=F32) - tg_ref[...]
        dy_ref[...] = err * (1.0 / D)
        loss_ref[...] += jnp.sum(err * err)
        dact_ref[...] = _nt(dy_ref[...].astype(BF16), wd_ref[...])

        for c in range(FF_TILES):
            for r0 in range(0, tm, FF_RB):
                rows = slice(r0, r0 + FF_RB)
                gate, val = gv_ref[rows, _lanes(c)], gv_ref[rows, _lanes(FF_TILES + c)]
                sg = _sigmoid(gate)
                da = dact_ref[rows, _lanes(c)]
                ext_ref.at[c][_sp(r0, FF_RB)] = da * val * (sg + gate * sg * (1.0 - sg))
                ext_ref.at[FF_TILES + c][_sp(r0, FF_RB)] = da * gate * sg
        fold = lambda a: jnp.sum(a.reshape(FF_RB // 8, 8, LANES), axis=0)
        for c in range(tiles):
            cols = _lanes(c)
            ext_ref.at[c][_sp(tm, hl)] = carry_ref[c]
            taps = [fw_ref[k:k + 1, cols] for k in range(FF_K)]
            acc = [jnp.zeros((8, LANES), F32) for _ in range(FF_K + 1)]
            for r0 in range(0, tm, FF_RB):
                shifted = [ext_ref.at[c][_sp(r0 + k, FF_RB)] for k in range(FF_K)]
                u = up_ref[r0:r0 + FF_RB, cols]
                dup = taps[2] * shifted[0] + taps[1] * shifted[1] + taps[0] * shifted[2]
                dup_ref[r0:r0 + FF_RB, cols] = dup.astype(BF16)
                for k in range(FF_K):
                    acc[2 - k] = acc[2 - k] + fold(shifted[k] * u)
                acc[FF_K] = acc[FF_K] + fold(shifted[0])
            for k in range(FF_K + 1):
                gff_ref[k:k + 1, cols] += jnp.sum(acc[k], axis=0, keepdims=True)
            carry_ref[c] = ext_ref.at[c][_sp(0, hl)]

    rev = lambda i: (nt - 1 - i, 0)
    row = pl.BlockSpec((tm, D), rev)
    wide = pl.BlockSpec((tm, NUP), rev)
    return _call(
        body, up, up, ffconv_w, ffconv_b, w_down, x1, target, name="ffn", grid=(nt,),
        in_specs=[wide, pl.BlockSpec((hl, NUP), lambda i: (jnp.maximum((nt - 1 - i) * per - 1, 0), 0)),
                  pl.BlockSpec((FF_K, NUP), lambda i: (0, 0)), pl.BlockSpec((1, NUP), lambda i: (0, 0)),
                  _resident((DFF, D)), row, row],
        out_specs=[pl.BlockSpec((tm, DFF), rev), row, pl.BlockSpec((8, 128), lambda i: (0, 0)), wide,
                   pl.BlockSpec((8, NUP), lambda i: (0, 0))],
        out_shape=[jax.ShapeDtypeStruct((T, DFF), BF16), jax.ShapeDtypeStruct((T, D), F32),
                   jax.ShapeDtypeStruct((8, 128), F32), jax.ShapeDtypeStruct((T, NUP), BF16),
                   jax.ShapeDtypeStruct((8, NUP), F32)],
        scratch_shapes=[pltpu.VMEM((tiles, 2 * (tm + hl), LANES), F32), pltpu.VMEM((tm, NUP), F32),
                        pltpu.VMEM((tm, DFF), F32), pltpu.VMEM((tiles, hl, LANES), F32)],
        compiler_params=_params(("arbitrary",), 58))


def _weight_grad(a, g, bm, bn, tk, name):
    m, n = a.shape[1], g.shape[1]
    nk = T // tk

    def body(a_ref, g_ref, of_ref, ob_ref):
        k = pl.program_id(2)

        @pl.when(k == 0)
        def _():
            of_ref[...] = jnp.zeros_like(of_ref)
        of_ref[...] += _tn_dot(a_ref[...].astype(BF16), g_ref[...].astype(BF16))

        @pl.when(k == nk - 1)
        def _():
            ob_ref[...] = of_ref[...].astype(BF16)

    out = pl.BlockSpec((bm, bn), lambda i, j, k: (i, j))
    return _call(
        body, a, g, name=name, grid=(m // bm, n // bn, nk),
        in_specs=[pl.BlockSpec((tk, bm), lambda i, j, k: (k, i)), pl.BlockSpec((tk, bn), lambda i, j, k: (k, j))],
        out_specs=[out, out],
        out_shape=[jax.ShapeDtypeStruct((m, n), F32), jax.ShapeDtypeStruct((m, n), BF16)],
        compiler_params=_params(("parallel", "parallel", "arbitrary"), 56))


def _norm_bwd_mm(dz, w, xin, base, gain, name, rider):
    kdim = dz.shape[1]
    tm = 512

    def body(dz_ref, w_ref, x_ref, b_ref, g_ref, dx_ref, gg_ref):
        @pl.when(pl.program_id(0) == 0)
        def _():
            gg_ref[...] = jnp.zeros_like(gg_ref)

        xv = x_ref[...]
        dh = _nt(dz_ref[...], w_ref[...])
        r = lax.rsqrt(jnp.mean(xv * xv, axis=-1, keepdims=True) + EPS)
        t = dh * g_ref[...]
        dx_ref[...] = b_ref[...] + r * t - xv * (r * r * r) * jnp.mean(t * xv, axis=-1, keepdims=True)
        gg_ref[...] += jnp.sum(dh * xv * r, axis=0, keepdims=True)

    row = pl.BlockSpec((tm, D), lambda i: (i, 0))
    vec = pl.BlockSpec((1, D), lambda i: (0, 0))
    return _call(
        body, dz, w, xin, base, gain, rider=rider, name=name, grid=(T // tm,),
        in_specs=[pl.BlockSpec((tm, kdim), lambda i: (i, 0)), _resident((D, kdim)), row, row, vec],
        out_specs=[row, vec],
        out_shape=[jax.ShapeDtypeStruct((T, D), F32), jax.ShapeDtypeStruct((1, D), F32)],
        compiler_params=_params(("arbitrary",), 48))


def _outproj_bwd(dx1, w_out, o_f32, bd):
    tm = 512

    def body(d_ref, w_ref, o_ref, bd_ref, dc_ref, dl_ref):
        dc_ref[...] = _nt(d_ref[...].astype(BF16), w_ref[...])
        dl_ref[...] = _segsum(dc_ref[:, C:2 * C] * o_ref[...], bd_ref[...])

    row = pl.BlockSpec((tm, D), lambda i: (i, 0))
    blk = pl.BlockSpec((tm, C), lambda i: (i, 0))
    return _call(
        body, dx1, w_out, o_f32, bd, name="outproj_bwd", grid=(T // tm,),
        in_specs=[row, _resident((D, D)), blk, _resident((C, C))], out_specs=[row, blk],
        out_shape=[jax.ShapeDtypeStruct((T, D), F32), jax.ShapeDtypeStruct((T, C), F32)],
        compiler_params=_params(("arbitrary",), 32))


def _conv_bwd(dcat, cv, proj, conv_w, cn_g, cn_b, rider):
    tm, hl, rb, cb = CONV_TM, CONV_HALO, CONV_RB, CONV_CB
    per = tm // hl
    nt = T // tm
    tiles = C // LANES

    def body(du_ref, dun_ref, cv_ref, cvn_ref, av_ref, ag_ref, hv_ref, hg_ref, w_ref, g_ref, bb_ref,
             dp_ref, gv_ref, gw_ref, dsh_ref, gsh_ref):
        i = pl.program_id(0)

        @pl.when(i == 0)
        def _():
            gv_ref[...] = jnp.zeros_like(gv_ref)
            gw_ref[...] = jnp.zeros_like(gw_ref)

        def ln_bwd(du, cvv):
            mu = jnp.mean(cvv, axis=-1, keepdims=True)
            xc = cvv - mu
            rs = lax.rsqrt(jnp.mean(xc * xc, axis=-1, keepdims=True) + EPS)
            xh = xc * rs
            ln = xh * g_ref[...] + bb_ref[...]
            sg = _sigmoid(ln)
            dln = du * (sg + ln * sg * (1.0 - sg))
            dxh = dln * g_ref[...]
            dcv = rs * (dxh - jnp.mean(dxh, axis=-1, keepdims=True)
                        - xh * jnp.mean(dxh * xh, axis=-1, keepdims=True))
            return dcv, dln, xh

        for r0 in range(0, tm, rb):
            dcv, dln, xh = ln_bwd(du_ref[r0:r0 + rb, :], cv_ref[r0:r0 + rb, :])
            for j in range(tiles):
                dsh_ref.at[j][_sp(r0, rb)] = dcv[:, _lanes(j)]
            gv_ref[0:1, :] += jnp.sum(dln * xh, axis=0, keepdims=True)
            gv_ref[1:2, :] += jnp.sum(dln, axis=0, keepdims=True)
            gv_ref[2:3, :] += jnp.sum(dcv, axis=0, keepdims=True)
        dcv_n, _, _ = ln_bwd(dun_ref[...], cvn_ref[...])
        dcv_n = jnp.where(i < nt - 1, dcv_n, 0.0)
        for j in range(tiles):
            ln_ = _lanes(j)
            dsh_ref.at[j][_sp(tm, hl)] = dcv_n[:, ln_]
            glu_h = hv_ref[:, ln_] * _sigmoid(hg_ref[:, ln_])
            gsh_ref.at[j][_sp(0, hl)] = jnp.where(i > 0, glu_h, 0.0)
            for r0 in range(0, tm, cb):
                gsh_ref.at[j][_sp(hl + r0, cb)] = av_ref[r0:r0 + cb, ln_] * _sigmoid(ag_ref[r0:r0 + cb, ln_])

        for j in range(tiles):
            ln_ = _lanes(j)
            for r0 in range(0, tm, cb):
                dglu = jnp.zeros((cb, LANES), F32)
                for k in range(CONV_K):
                    dglu = dglu + w_ref[k:k + 1, ln_] * dsh_ref.at[j][_sp(r0 + (CONV_K - 1) - k, cb)]
                av = av_ref[r0:r0 + cb, ln_]
                sg = _sigmoid(ag_ref[r0:r0 + cb, ln_])
                dp_ref[r0:r0 + cb, ln_] = (dglu * sg).astype(BF16)
                dp_ref[r0:r0 + cb, _lanes(tiles + j)] = (dglu * av * sg * (1.0 - sg)).astype(BF16)
            for k in range(CONV_K):
                part = jnp.zeros((8, LANES), F32)
                for r0 in range(0, tm, cb):
                    prod = dsh_ref.at[j][_sp(r0, cb)] * gsh_ref.at[j][_sp(r0 + hl - (CONV_K - 1) + k, cb)]
                    part = part + jnp.sum(prod.reshape(cb // 8, 8, LANES), axis=0)
                gw_ref[k:k + 1, ln_] += jnp.sum(part, axis=0, keepdims=True)

    main = lambda col: pl.BlockSpec((tm, C), lambda i: (i, col))
    prev = lambda col: pl.BlockSpec((hl, C), lambda i: (jnp.maximum(i * per - 1, 0), col))
    nxt = pl.BlockSpec((hl, C), lambda i: (jnp.minimum((i + 1) * per, T // hl - 1), 0))
    vec = pl.BlockSpec((1, C), lambda i: (0, 0))
    return _call(
        body, dcat, dcat, cv, cv, proj, proj, proj, proj, conv_w, cn_g, cn_b, rider=rider, name="conv_bwd",
        grid=(nt,),
        in_specs=[main(0), nxt, main(0), nxt, main(0), main(1), prev(0), prev(1),
                  pl.BlockSpec((CONV_K, C), lambda i: (0, 0)), vec, vec],
        out_specs=[pl.BlockSpec((tm, 2 * C), lambda i: (i, 0)), pl.BlockSpec((8, C), lambda i: (0, 0)),
                   pl.BlockSpec((32, C), lambda i: (0, 0))],
        out_shape=[jax.ShapeDtypeStruct((T, NPROJ), BF16), jax.ShapeDtypeStruct((8, C), F32),
                   jax.ShapeDtypeStruct((32, C), F32)],
        scratch_shapes=[pltpu.VMEM((C // LANES, 2 * (tm + hl), LANES), F32)] * 2,
        compiler_params=_params(("arbitrary",), 48))


def _attn_bwd_unit(qs, dos, lgs, dls, rows, kc, vc, biasv, invalid_prev):
    qst, dost = _stack_heads(qs[rows, :]), _stack_heads(dos[rows, :])
    s = _nt(qst, kc) + biasv
    if invalid_prev is not None:
        col = lax.broadcasted_iota(jnp.int32, s.shape, 1)
        s = jnp.where((col < QB) & invalid_prev, NEG, s)
    p = jnp.exp(s - _stack_cols(lgs[rows, :]))
    ds = p * (_nt(dost, vc) - _stack_cols(dls[rows, :]))
    dsb = ds.astype(BF16)
    dq = _unstack_heads(jnp.dot(dsb, kc, preferred_element_type=F32))
    return dq, _tn_dot(dsb, qst), _tn_dot(p.astype(BF16), dost)


def _attn_bwd_lagged(qn, kn, proj, dcat, lg, dl, bias, d, earlier):
    assert QB * d == ATT_WIN
    n_win = T // ATT_WIN
    lanes = 2 * HEAD

    def body(q_ref, k_ref, v_ref, do_ref, lg_ref, dl_ref, kh_ref, vh_ref, bias_ref, eq_ref, ek_ref, ev_ref,
             dq_ref, dk_ref, dv_ref, qs, dos, lgs, dls, ks, vs, dqs, ck, cv, ok, ov, tmp):
        n = pl.program_id(1)
        block = lambda buf: (lambda r: buf[r * QB:(r + 1) * QB, :])

        @pl.when(n == 0)
        def _():
            ck[...] = jnp.zeros_like(ck)
            cv[...] = jnp.zeros_like(cv)

        @pl.when(n < n_win)
        def _():
            for dst, per, at, src, dt in ((qs, QB, 0, q_ref, BF16), (dos, QB, 0, do_ref, BF16),
                                          (lgs, QB, 0, lg_ref, F32), (dls, QB, 0, dl_ref, F32),
                                          (ks, 2 * QB, 0, kh_ref, BF16), (ks, 2 * QB, QB, k_ref, BF16),
                                          (vs, 2 * QB, 0, vh_ref, BF16), (vs, 2 * QB, QB, v_ref, BF16)):
                stream = _gather_streams(src, tmp, d)
                for r in range(d):
                    dst[r * per + at:r * per + at + QB, :] = stream(r, QB).astype(dt)
            for r in range(d):
                rows = slice(r * QB, (r + 1) * QB)
                keys = slice(2 * r * QB, (2 * r + 2) * QB)
                dq, dkc, dvc = _attn_bwd_unit(qs, dos, lgs, dls, rows, ks[keys, :], vs[keys, :], bias_ref[...], n == 0)
                dqs[rows, :] = dq
                ok[rows, :] = ck[rows, :] + dkc[:QB]
                ov[rows, :] = cv[rows, :] + dvc[:QB]
                ck[rows, :] = dkc[QB:]
                cv[rows, :] = dvc[QB:]
            _scatter_streams(dq_ref, tmp, d, block(dqs), QB, eq_ref)
            _scatter_streams(dk_ref, tmp, d, block(ok), QB, ek_ref)
            _scatter_streams(dv_ref, tmp, d, block(ov), QB, ev_ref)

        @pl.when(n == n_win)
        def _():
            _scatter_streams(dk_ref, tmp, d, block(ck), QB, ek_ref)
            _scatter_streams(dv_ref, tmp, d, block(cv), QB, ev_ref)

    cur = lambda off: pl.BlockSpec((ATT_WIN, lanes), lambda cb, n: (jnp.minimum(n, n_win - 1), off + cb))
    prev = lambda off: pl.BlockSpec(
        (ATT_WIN, lanes), lambda cb, n: (jnp.maximum(jnp.minimum(n, n_win - 1) - 1, 0), off + cb))
    late = pl.BlockSpec((ATT_WIN, lanes), lambda cb, n: (jnp.maximum(n - 1, 0), cb))
    buf = lambda rows, dt: pltpu.VMEM((rows, lanes), dt)
    return _call(
        body, qn, kn, proj, dcat, lg, dl, kn, proj, bias, *earlier, name=f"attn_bwd_d{d}",
        grid=(C // lanes, n_win + 1),
        in_specs=[cur(0), cur(0), cur(V_COL), cur(DO_COL), cur(0), cur(0), prev(0), prev(V_COL),
                  pl.BlockSpec((None, 2 * QB, 2 * QB), lambda cb, n: (cb, 0, 0)), cur(0), late, late],
        out_specs=[cur(0), late, late],
        out_shape=[jax.ShapeDtypeStruct((T, C), F32)] * 3,
        scratch_shapes=[buf(ATT_WIN, BF16), buf(ATT_WIN, BF16), buf(ATT_WIN, F32), buf(ATT_WIN, F32),
                        buf(2 * ATT_WIN, BF16), buf(2 * ATT_WIN, BF16)] + [buf(ATT_WIN, F32)] * 6,
        compiler_params=_params(("arbitrary", "arbitrary"), 48))


def _attn_bwd(qn, kn, proj, dcat, lg, dl, bias, d, earlier=None, rider=None):
    sl, nb, hr = _attn_geometry(d)
    slk = QB + sl
    slq = sl + QB
    n_win = T // ATT_WIN

    def body(q_ref, k_ref, v_ref, do_ref, lg_ref, dl_ref, kh_ref, vh_ref, qx_ref, dox_ref, lgx_ref, dlx_ref,
             bias_ref, *rest):
        sums = rest[:3] if earlier is not None else (None, None, None)
        dq_ref, dk_ref, dv_ref, qs, dos, lgs, dls, ks, vs, dqs, dks, dvs = rest[-12:]
        n = pl.program_id(1)
        for r in range(d):
            for dst, src, nx, dt in ((qs, q_ref, qx_ref, BF16), (dos, do_ref, dox_ref, BF16),
                                     (lgs, lg_ref, lgx_ref, F32), (dls, dl_ref, dlx_ref, F32)):
                dst[r * slq:r * slq + sl, :] = _stream(src, r, sl, d).astype(dt)
                dst[r * slq + sl:(r + 1) * slq, :] = _stream(nx, r, QB, d).astype(dt)
            for dst, halo, src in ((ks, kh_ref, k_ref), (vs, vh_ref, v_ref)):
                dst[r * slk:r * slk + QB, :] = _stream(halo, r, QB, d).astype(BF16)
                dst[r * slk + QB:(r + 1) * slk, :] = _stream(src, r, sl, d).astype(BF16)
        dks[...] = jnp.zeros_like(dks)
        dvs[...] = jnp.zeros_like(dvs)

        def unit(rows, kc, vc, biasv, invalid_prev):
            return _attn_bwd_unit(qs, dos, lgs, dls, rows, kc, vc, biasv, invalid_prev)

        for r in range(d):
            for b in range(nb):
                rows = slice(r * slq + b * QB, r * slq + (b + 1) * QB)
                keys = slice(r * slk + b * QB, r * slk + (b + 2) * QB)
                dq, dkc, dvc = unit(rows, ks[keys, :], vs[keys, :], bias_ref[...], (n == 0) if b == 0 else None)
                dqs[r * sl + b * QB:r * sl + (b + 1) * QB, :] = dq
                if b == 0:
                    dks[r * sl:r * sl + QB, :] += dkc[QB:]
                    dvs[r * sl:r * sl + QB, :] += dvc[QB:]
                else:
                    dks[r * sl + (b - 1) * QB:r * sl + (b + 1) * QB, :] += dkc
                    dvs[r * sl + (b - 1) * QB:r * sl + (b + 1) * QB, :] += dvc

        @pl.when(n < n_win - 1)
        def _():
            for r in range(d):
                rows = slice(r * slq + sl, (r + 1) * slq)
                keys = slice(r * slk + sl, (r + 1) * slk)
                _, dkc, dvc = unit(rows, ks[keys, :], vs[keys, :], bias_ref[:, 0:QB], None)
                dks[(r + 1) * sl - QB:(r + 1) * sl, :] += dkc
                dvs[(r + 1) * sl - QB:(r + 1) * sl, :] += dvc

        for dst, src, before in zip((dq_ref, dk_ref, dv_ref), (dqs, dks, dvs), sums):
            for r in range(d):
                pos = (pl.ds(r, sl, stride=d) if d > 1 else slice(None), slice(None))
                val = src[r * sl:(r + 1) * sl, :]
                dst[pos] = val if before is None else val + before[pos]

    main, prev, nxt, bias_spec = _attn_specs(d)
    lanes = 2 * HEAD
    return _call(
        body, qn, kn, proj, dcat, lg, dl, kn, proj, qn, dcat, lg, dl, bias, *(earlier or ()), rider=rider,
        name=f"attn_bwd_d{d}", grid=(C // lanes, n_win),
        in_specs=[main(0), main(0), main(V_COL), main(DO_COL), main(0), main(0), prev(0), prev(V_COL),
                  nxt(0), nxt(DO_COL), nxt(0), nxt(0), bias_spec] + ([main(0)] * 3 if earlier is not None else []),
        out_specs=[main(0)] * 3,
        out_shape=[jax.ShapeDtypeStruct((T, C), F32)] * 3,
        scratch_shapes=[pltpu.VMEM((ATT_WIN + hr, lanes), BF16), pltpu.VMEM((ATT_WIN + hr, lanes), BF16),
                        pltpu.VMEM((ATT_WIN + hr, lanes), F32), pltpu.VMEM((ATT_WIN + hr, lanes), F32),
                        pltpu.VMEM((ATT_WIN + hr, lanes), BF16), pltpu.VMEM((ATT_WIN + hr, lanes), BF16),
                        pltpu.VMEM((ATT_WIN, lanes), F32), pltpu.VMEM((ATT_WIN, lanes), F32),
                        pltpu.VMEM((ATT_WIN, lanes), F32)],
        compiler_params=_params(("arbitrary", "arbitrary"), 48))


def _qk_norm_bwd(dn_sum, proj, col, gain, bd, dproj, name):
    tm = 512

    def body(d0, x_ref, g_ref, bd_ref, dp_in, dp_ref, gg_ref):
        del dp_in

        @pl.when(pl.program_id(0) == 0)
        def _():
            gg_ref[...] = jnp.zeros_like(gg_ref)
        dn = d0[...]
        xv = x_ref[...]
        r = lax.rsqrt(_segsum(xv * xv, bd_ref[...]) * (1.0 / HEAD) + EPS)
        t = dn * g_ref[...]
        mean_tx = _segsum(t * xv, bd_ref[...]) * (1.0 / HEAD)
        dp_ref[...] = (r * t - xv * (r * r * r) * mean_tx).astype(BF16)
        gg_ref[...] += jnp.sum(dn * xv * r, axis=0, keepdims=True)

    blk = pl.BlockSpec((tm, C), lambda i: (i, 0))
    vec = pl.BlockSpec((1, C), lambda i: (0, 0))
    return _call(
        body, dn_sum, proj, gain, bd, dproj, name=name, grid=(T // tm,),
        in_specs=[blk, pl.BlockSpec((tm, C), lambda i: (i, col)), vec, _resident((C, C)), ANY],
        out_specs=[pl.BlockSpec((tm, C), lambda i: (i, col)), vec],
        out_shape=[jax.ShapeDtypeStruct((T, NPROJ), BF16), jax.ShapeDtypeStruct((1, C), F32)],
        input_output_aliases={4: 0},
        compiler_params=_params(("arbitrary",), 32))


def _v_bwd(dv_sum, dproj):
    tm = 512

    def body(d0, dp_in, dp_ref):
        del dp_in
        dp_ref[...] = d0[...].astype(BF16)

    blk = pl.BlockSpec((tm, C), lambda i: (i, 0))
    return _call(
        body, dv_sum, dproj, name="v_bwd", grid=(T // tm,),
        in_specs=[blk, ANY],
        out_specs=[pl.BlockSpec((tm, C), lambda i: (i, 4))],
        out_shape=[jax.ShapeDtypeStruct((T, NPROJ), BF16)],
        input_output_aliases={1: 0},
        compiler_params=_params(("parallel",), 32))[0]


def _adamw(w, g, m, v):
    m = ADAM_B1 * m + (1.0 - ADAM_B1) * g
    v = ADAM_B2 * v + (1.0 - ADAM_B2) * (g * g)
    m_hat = m / (1.0 - ADAM_B1 ** ADAM_STEP)
    v_hat = v / (1.0 - ADAM_B2 ** ADAM_STEP)
    delta = -ADAM_LR * (m_hat / (jnp.sqrt(v_hat) + ADAM_EPS) + ADAM_WD * w)
    return delta, m, v


def _row_block(shape):
    rows = shape[0]
    for cand in (256, 128, 64, 88, 32, 8):
        if rows % cand == 0 and cand * shape[1] * 4 <= (2 << 20):
            return cand
    return 8


def _partial_sum(own, recv, name):
    br = _row_block(own.shape)
    cols = own.shape[1]

    def body(o_ref, r_ref, p_ref):
        p_ref[...] = ((o_ref[...] + r_ref[0].astype(F32)) + r_ref[1].astype(F32)) + r_ref[2].astype(F32)

    blk = pl.BlockSpec((br, cols), lambda i: (i, 0))
    return _call(
        body, own, recv, name=name, grid=(own.shape[0] // br,),
        in_specs=[blk, pl.BlockSpec((3, br, cols), lambda i: (0, i, 0))], out_specs=[blk],
        out_shape=[jax.ShapeDtypeStruct(own.shape, F32)],
        compiler_params=_params(("parallel",), 32))[0]


def _adamw_mat(p_own, p_sib, w, m, v, name):
    br = _row_block(w.shape)
    cols = w.shape[1]

    def body(a_ref, b_ref, w_ref, m_ref, v_ref, g_ref, d_ref, nm_ref, nv_ref):
        g = a_ref[...] + b_ref[...]
        delta, nm, nv = _adamw(w_ref[...], g, m_ref[...], v_ref[...])
        g_ref[...] = g
        d_ref[...] = delta
        nm_ref[...] = nm
        nv_ref[...] = nv

    blk = pl.BlockSpec((br, cols), lambda i: (i, 0))
    return _call(
        body, p_own, p_sib, w, m, v, name=name, grid=(w.shape[0] // br,),
        in_specs=[blk] * 5, out_specs=[blk] * 4,
        out_shape=[jax.ShapeDtypeStruct(w.shape, F32)] * 4,
        compiler_params=_params(("parallel",), 40))


def _vec_reduce(vrecv):
    def body(v_ref, o_ref):
        acc = v_ref[0]
        for r in range(1, N_DEV):
            acc = acc + v_ref[r]
        o_ref[...] = acc

    return pl.pallas_call(
        body, name="vec_reduce",
        out_shape=jax.ShapeDtypeStruct((VPACK_ROWS, D), F32),
        compiler_params=_params((), 32),
    )(vrecv)


def _adamw_small(w, g, m, v):
    def body(w_ref, g_ref, m_ref, v_ref, d_ref, nm_ref, nv_ref):
        delta, nm, nv = _adamw(w_ref[...], g_ref[...], m_ref[...], v_ref[...])
        d_ref[...] = delta
        nm_ref[...] = nm
        nv_ref[...] = nv

    return pl.pallas_call(
        body, name="adamw_small",
        out_shape=[jax.ShapeDtypeStruct(w.shape, F32)] * 3,
        compiler_params=_params((), 32),
    )(w, g, m, v)


def _pack(parts, rows):
    flat = jnp.concatenate([p.reshape(-1) for p in parts])
    return jnp.pad(flat, (0, rows * D - flat.shape[0])).reshape(rows, D)


def _unpack(packed, shapes):
    flat = packed.reshape(-1)
    out, off = [], 0
    for shp in shapes:
        size = 1
        for s in shp:
            size *= s
        out.append(flat[off:off + size].reshape(shp))
        off += size
    return out


def _no_comm(shards, row_sharded, halve=None):
    del row_sharded, halve
    return None, lambda res, n: (res, shards)


def _with_comm(shards, row_sharded, halve=None):
    rider = _gather_rider(shards, row_sharded) if halve is None else _halved_gather_rider(shards, halve)
    return rider, lambda res, n: (res[:n], res[n:])


def _local_step(x, target, norm1_g, conv_b, cn_g, cn_b, q_norm_g, k_norm_g, norm2_g, ffconv_b,
                first_weights, late_weights, comm=True):
    row = lambda a: a.reshape(1, -1)
    head_of = jnp.arange(C) // HEAD
    bd = (head_of[:, None] == head_of[None, :]).astype(BF16)
    qg = row(jnp.tile(q_norm_g, C // HEAD) * (HEAD ** -0.5))
    kg = row(jnp.tile(k_norm_g, C // HEAD))
    biases = [_alibi_tables(d) for d in PATTERN_DILATIONS]
    gather = _with_comm if comm else _no_comm
    grad_rider = (lambda g, rs: _grad_rider(g[1], g[0], rs)) if comm else (lambda g, rs: None)

    rider, split = gather(first_weights, (False, False, False), (True, False, False))
    (h,), (w_in, conv_w, ffconv_w) = split(_norm_fwd(x, row(norm1_g), rider), 1)
    rider, split = gather(late_weights[0:1], (True,))
    (proj, qn, kn), (w_out,) = split(_proj_fwd(h, w_in, qg, kg, bd, rider), 3)
    rider, split = gather(late_weights[1:2], (False,), (True,))
    (cat, cv), (w_up,) = split(_conv_fwd(proj, conv_w, row(conv_b), row(cn_g), row(cn_b), rider), 2)
    fwd = [_attn_fwd(qn, kn, proj, biases[i], d) for i, d in enumerate(PATTERN_DILATIONS[:-1])]
    merge = (fwd[0][0], fwd[0][1], fwd[1][0], fwd[1][1], cat)
    cat, o_f32, lg = _attn_fwd(qn, kn, proj, biases[-1], PATTERN_DILATIONS[-1], None, merge)
    rider, split = gather(late_weights[2:3], (True,))
    (x1, h2, up), (w_down,) = split(_out_up(x, cat, w_out, row(norm2_g), w_up, rider), 3)
    act, dy, loss_acc, dup, gff = _ffn(up, ffconv_w, row(ffconv_b), w_down, x1, target)
    gw_down = _weight_grad(act, dy, DFF // 2, D, 1024, "grad_w_down")
    res = _norm_bwd_mm(dup, w_up, x1, dy, row(norm2_g), "up_bwd", grad_rider(gw_down, True))
    (dx1, g_norm2), ex_down = res[:2], res[2:]
    gw_up = _weight_grad(h2, dup, D, NUP // 4, 2048, "grad_w_up")
    dcat, dl = _outproj_bwd(dx1, w_out, o_f32, bd)
    gw_out = _weight_grad(cat, dx1, D, D, 2048, "grad_w_out")
    res = _conv_bwd(dcat, cv, proj, conv_w, row(cn_g), row(cn_b), grad_rider(gw_up, False))
    (dproj, gconv_vec, gconv_w), ex_up = res[:3], res[3:]
    sums, ex_out = None, []
    for i, d in enumerate(PATTERN_DILATIONS):
        if QB * d == ATT_WIN:
            res = _attn_bwd_lagged(qn, kn, proj, dcat, lg, dl, biases[i], d, sums)
        else:
            res = _attn_bwd(qn, kn, proj, dcat, lg, dl, biases[i], d, sums,
                            grad_rider(gw_out, True) if i == 0 else None)
        sums = res[:3]
        ex_out = res[3:] if i == 0 else ex_out
    dproj, gq_lane = _qk_norm_bwd(sums[0], proj, 2, qg, bd, dproj, "q_norm_bwd")
    dproj, gk_lane = _qk_norm_bwd(sums[1], proj, 3, kg, bd, dproj, "k_norm_bwd")
    dproj = _v_bwd(sums[2], dproj)
    gw_in = _weight_grad(h, dproj, D, NPROJ // 4, 2048, "grad_w_in")
    res = _norm_bwd_mm(dproj, w_in, x, dx1, row(norm1_g), "in_bwd", grad_rider(gw_in, False))
    (dx, g_norm1), ex_in = res[:2], res[2:]

    loss = loss_acc[0, 0] * (0.5 / D)
    g_qg = jnp.sum(gq_lane.reshape(C // HEAD, HEAD), axis=0) * (HEAD ** -0.5)
    g_kg = jnp.sum(gk_lane.reshape(C // HEAD, HEAD), axis=0)
    small = [g_norm1[0], gconv_vec[2], gconv_vec[0], gconv_vec[1], g_qg, g_kg, g_norm2[0], gff[3],
             gconv_w[:CONV_K], gff[:FF_K]]
    mats = [ex_in, ex_out, ex_up, ex_down] if comm else [gw_in, gw_out, gw_up, gw_down]
    return loss, dx, mats, small


def kernel(x, norm1_g, w_in, conv_w, conv_b, cn_g, cn_b, q_norm_g, k_norm_g, w_out, norm2_g, w_up, ffconv_w, ffconv_b, w_down, loss_target, m_norm1_g, m_w_in, m_conv_w, m_conv_b, m_cn_g, m_cn_b, m_q_norm_g, m_k_norm_g, m_w_out, m_norm2_g, m_w_up, m_ffconv_w, m_ffconv_b, m_w_down, v_norm1_g, v_w_in, v_conv_w, v_conv_b, v_cn_g, v_cn_b, v_q_norm_g, v_k_norm_g, v_w_out, v_norm2_g, v_w_up, v_ffconv_w, v_ffconv_b, v_w_down):
    chip = 2 * lax.axis_index("x") + lax.axis_index("y")

    loss, dx, mats, small = _local_step(
        x[0], loss_target[0], norm1_g, conv_b, cn_g, cn_b, q_norm_g, k_norm_g, norm2_g, ffconv_b,
        [w_in.astype(BF16), conv_w, ffconv_w], [w.astype(BF16) for w in (w_out, w_up, w_down)])

    names = ("w_in", "w_out", "w_up", "w_down")
    parts = [_partial_sum(own, recv, "partial_" + names[k]) for k, (recv, own) in enumerate(mats)]
    sib, vrecv = _final_exchange(parts, _pack(small + [loss.reshape(1)], VPACK_ROWS))
    ws = (w_in, w_out, w_up, w_down)
    ms = (m_w_in, m_w_out, m_w_up, m_w_down)
    vs = (v_w_in, v_w_out, v_w_up, v_w_down)
    mat = [_adamw_mat(parts[k], sib[k], ws[k], ms[k], vs[k], "adamw_" + names[k]) for k in range(4)]

    vsum = _vec_reduce(vrecv)
    vec_shapes = [(D,), (C,), (C,), (C,), (HEAD,), (HEAD,), (D,), (NUP,), (CONV_K, C), (FF_K, NUP), (1,)]
    gsmall = _unpack(vsum, vec_shapes)
    g_conv_w = lax.dynamic_slice_in_dim(gsmall[8], chip * (C // N_CHIPS), C // N_CHIPS, axis=1)
    g_ffconv_w = lax.dynamic_slice_in_dim(gsmall[9], chip * (NUP // N_CHIPS), NUP // N_CHIPS, axis=1)
    gs = gsmall[:8] + [g_conv_w, g_ffconv_w]
    w_s = [norm1_g, conv_b, cn_g, cn_b, q_norm_g, k_norm_g, norm2_g, ffconv_b, conv_w, ffconv_w]
    m_s = [m_norm1_g, m_conv_b, m_cn_g, m_cn_b, m_q_norm_g, m_k_norm_g, m_norm2_g, m_ffconv_b, m_conv_w, m_ffconv_w]
    v_s = [v_norm1_g, v_conv_b, v_cn_g, v_cn_b, v_q_norm_g, v_k_norm_g, v_norm2_g, v_ffconv_b, v_conv_w, v_ffconv_w]
    shapes_s = [a.shape for a in w_s]
    d_p, m_p, v_p = _adamw_small(_pack(w_s, SPACK_ROWS), _pack(gs, SPACK_ROWS), _pack(m_s, SPACK_ROWS),
                                 _pack(v_s, SPACK_ROWS))
    d_s, nm_s, nv_s = _unpack(d_p, shapes_s), _unpack(m_p, shapes_s), _unpack(v_p, shapes_s)

    def ordered(sm, mt):
        return [sm[0], mt[0], sm[8], sm[1], sm[2], sm[3], sm[4], sm[5], mt[1], sm[6], mt[2], sm[9], sm[7], mt[3]]

    loss_all = gsmall[10][0]
    grads = ordered(gs, [r[0] for r in mat])
    deltas = ordered(d_s, [r[1] for r in mat])
    new_m = ordered(nm_s, [r[2] for r in mat])
    new_v = ordered(nv_s, [r[3] for r in mat])
    return (loss_all, dx[None], *grads, *deltas, *new_m, *new_v)
```

```python
import types

import jax
import jax.numpy as jnp
from jax import lax
from jax.experimental import pallas as pl
from jax.experimental.pallas import tpu as pltpu

T = 8192
D = 1024
C = 512
NPROJ = 2560
DFF = 2816
NUP = 2 * DFF
CONV_K = 31
FF_K = 3
HEAD = 64
EPS = 1e-6
NEG = -1e30
N_CHIPS = 4
N_DEV = 8
PATTERN_DILATIONS = (1, 4, 16)
QB = 128

ADAM_LR = 0.001
ADAM_B1 = 0.9
ADAM_B2 = 0.999
ADAM_EPS = 1e-08
ADAM_WD = 0.01
ADAM_STEP = 10

F32 = jnp.float32
BF16 = jnp.bfloat16
MESH = pl.DeviceIdType.MESH
ANY = pl.BlockSpec(memory_space=pl.ANY)

VPACK_ROWS = 48
SPACK_ROWS = 24


def _params(sem, vmem_mb):
    return pltpu.CompilerParams(dimension_semantics=sem, vmem_limit_bytes=vmem_mb << 20)


def _resident(shape):
    return pl.BlockSpec(shape, lambda i: (0, 0), pipeline_mode=pl.Buffered(1))


def _nt(a, b):
    return lax.dot_general(a, b, (((1,), (1,)), ((), ())), preferred_element_type=F32)


def _tn_dot(a, b):
    return lax.dot_general(a, b, (((0,), (0,)), ((), ())), preferred_element_type=F32)


def _sigmoid(x):
    return 1.0 / (1.0 + jnp.exp(-x))


def _segsum(x, bd):
    hi = x.astype(BF16)
    lo = (x - hi.astype(F32)).astype(BF16)
    return (jnp.dot(hi, bd, preferred_element_type=F32)
            + jnp.dot(lo, bd, preferred_element_type=F32))


def _place():
    x, y, c = lax.axis_index("x"), lax.axis_index("y"), lax.axis_index("c")
    chips = [(1 - x, y), (x, 1 - y), (1 - x, 1 - y)]
    return x, y, c, chips


def _block_of(ref, shard_shape, row_sharded, s):
    r, cdim = shard_shape
    if row_sharded:
        return ref.at[pl.ds(s * r, r), :]
    return ref.at[:, pl.ds(s * cdim, cdim)]


def _full_shape(shard_shape, row_sharded):
    r, cdim = shard_shape
    return (r * N_CHIPS, cdim) if row_sharded else (r, cdim * N_CHIPS)


def _gather_rider(shards, row_sharded, peers=(0, 1, 2), into=None):
    n = len(shards)
    shapes = [a.shape for a in shards]

    def copies(ins, outs, sems):
        send_sems, recv_sems, local_sems = sems
        x, y, c, chips = _place()
        me = 2 * x + y
        place = lambda k, s: _block_of(outs[k], shapes[k], row_sharded[k], s)
        local = []
        if into is None:
            local = [pltpu.make_async_copy(ins[k], place(k, me), local_sems.at[k]) for k in range(n)]
        sends, recvs = [], []
        for k in range(n):
            for j in peers:
                px, py = chips[j]
                sem = dict(send_sem=send_sems.at[3 * k + j], recv_sem=recv_sems.at[3 * k + j],
                           device_id=(px, py, c), device_id_type=MESH)
                sends.append(pltpu.make_async_remote_copy(src_ref=ins[k], dst_ref=place(k, me), **sem))
                recvs.append(pltpu.make_async_remote_copy(src_ref=ins[k], dst_ref=place(k, 2 * px + py), **sem))
        return local, sends, recvs

    return types.SimpleNamespace(
        operands=list(shards) + list(into or []), copies=copies,
        aliases={n + k: k for k in range(n)} if into is not None else {},
        out_shape=[jax.ShapeDtypeStruct(_full_shape(s, rs), a.dtype) for s, rs, a in zip(shapes, row_sharded, shards)],
        sems=[pltpu.SemaphoreType.DMA((3 * n,)), pltpu.SemaphoreType.DMA((3 * n,)), pltpu.SemaphoreType.DMA((n,))])


def _halved_gather_rider(shards, halve):
    n = len(shards)
    shapes = [a.shape for a in shards]

    def copies(ins, outs, sems):
        send_sems, recv_sems, local_sems, pass_send_sems, pass_recv_sems = sems
        x, y, c, chips = _place()
        me = 2 * x + y
        place = lambda k, s: _block_of(outs[k], shapes[k], False, s)

        def half(ref, k, which):
            rows = shapes[k][0] // 2
            return ref.at[pl.ds(which * rows, rows), :]

        local = [pltpu.make_async_copy(ins[k], place(k, me), local_sems.at[k]) for k in range(n)]
        sends, recvs, passes, pass_recvs = [], [], [], []
        for k in range(n):
            for j, (px, py) in enumerate(chips):
                theirs = 2 * px + py
                sem = dict(send_sem=send_sems.at[3 * k + j], recv_sem=recv_sems.at[3 * k + j],
                           device_id=(px, py, c), device_id_type=MESH)
                if not halve[k]:
                    sends.append(pltpu.make_async_remote_copy(src_ref=ins[k], dst_ref=place(k, me), **sem))
                    recvs.append(pltpu.make_async_remote_copy(src_ref=ins[k], dst_ref=place(k, theirs), **sem))
                    continue
                sends.append(pltpu.make_async_remote_copy(
                    src_ref=half(ins[k], k, c), dst_ref=half(place(k, me), k, c), **sem))
                recvs.append(pltpu.make_async_remote_copy(
                    src_ref=half(ins[k], k, c), dst_ref=half(place(k, theirs), k, c), **sem))
                sem = dict(send_sem=pass_send_sems.at[3 * k + j], recv_sem=pass_recv_sems.at[3 * k + j],
                           device_id=(x, y, 1 - c), device_id_type=MESH)
                mine, other = half(place(k, theirs), k, c), half(place(k, theirs), k, 1 - c)
                passes.append(pltpu.make_async_remote_copy(src_ref=mine, dst_ref=mine, **sem))
                pass_recvs.append(pltpu.make_async_remote_copy(src_ref=other, dst_ref=other, **sem))
        return local, sends, recvs, passes, pass_recvs

    return types.SimpleNamespace(
        operands=list(shards), copies=copies, aliases={},
        out_shape=[jax.ShapeDtypeStruct(_full_shape(s, False), a.dtype) for s, a in zip(shapes, shards)],
        sems=[pltpu.SemaphoreType.DMA((3 * n,)), pltpu.SemaphoreType.DMA((3 * n,)), pltpu.SemaphoreType.DMA((n,)),
              pltpu.SemaphoreType.DMA((3 * n,)), pltpu.SemaphoreType.DMA((3 * n,))])


def _grad_rider(g_bf16, g_f32, row_sharded):
    shard = tuple(d // N_CHIPS if (i == 0) == row_sharded else d for i, d in enumerate(g_f32.shape))

    def copies(ins, outs, sems):
        send_sems, recv_sems, local_sems = sems
        gb, gf = ins
        rec, own = outs
        x, y, c, chips = _place()
        me = 2 * x + y
        local = [pltpu.make_async_copy(_block_of(gf, shard, row_sharded, me), own, local_sems.at[0])]
        sends, recvs = [], []
        for j, (px, py) in enumerate(chips):
            sem = dict(send_sem=send_sems.at[j], recv_sem=recv_sems.at[j], device_id=(px, py, c), device_id_type=MESH)
            sends.append(pltpu.make_async_remote_copy(
                src_ref=_block_of(gb, shard, row_sharded, 2 * px + py), dst_ref=rec.at[j], **sem))
            recvs.append(pltpu.make_async_remote_copy(
                src_ref=_block_of(gb, shard, row_sharded, me), dst_ref=rec.at[j], **sem))
        return local, sends, recvs

    return types.SimpleNamespace(
        operands=[g_bf16, g_f32], copies=copies, aliases={},
        out_shape=[jax.ShapeDtypeStruct((3,) + shard, BF16), jax.ShapeDtypeStruct(shard, F32)],
        sems=[pltpu.SemaphoreType.DMA((3,)), pltpu.SemaphoreType.DMA((3,)), pltpu.SemaphoreType.DMA((1,))])


def _rider_start(rider, ins, outs, sems):
    local, sends = rider.copies(ins, outs, sems)[:2]
    for cp in local + sends:
        cp.start()


def _rider_wait(rider, ins, outs, sems):
    local, sends, recvs, *second = rider.copies(ins, outs, sems)
    passes, pass_recvs = second if second else ([], [])
    for cp in recvs:
        cp.wait_recv()
    for cp in passes:
        cp.start()
    for cp in pass_recvs:
        cp.wait_recv()
    for cp in sends + passes:
        cp.wait_send()
    for cp in local:
        cp.wait()


PIN_BYTES = 1 << 20


def _in_hbm(a):
    if a.size * a.dtype.itemsize < PIN_BYTES:
        return a
    return pltpu.with_memory_space_constraint(a, pltpu.HBM)


def _call(body, *operands, rider=None, name, grid, in_specs, out_specs, out_shape, scratch_shapes=(),
          compiler_params, input_output_aliases=None):
    operands = [_in_hbm(a) for a in operands]
    if rider is None:
        return pl.pallas_call(
            body, name=name, grid=grid, in_specs=list(in_specs), out_specs=list(out_specs), out_shape=list(out_shape),
            scratch_shapes=list(scratch_shapes), compiler_params=compiler_params,
            input_output_aliases=input_output_aliases or {})(*operands)
    n_in, n_out, n_scr = len(in_specs), len(out_specs), len(scratch_shapes)
    r_in, r_out = len(rider.operands), len(rider.out_shape)

    def riding(*refs):
        refs = list(refs)
        ins, refs = refs[:n_in], refs[n_in:]
        r_ins, refs = refs[:r_in], refs[r_in:]
        outs, refs = refs[:n_out], refs[n_out:]
        r_outs, refs = refs[:r_out], refs[r_out:]
        scr, sems = refs[:n_scr], refs[n_scr:]
        first = pl.program_id(0) == 0
        last = pl.program_id(0) == grid[0] - 1
        for axis in range(1, len(grid)):
            first = first & (pl.program_id(axis) == 0)
            last = last & (pl.program_id(axis) == grid[axis] - 1)

        @pl.when(first)
        def _():
            _rider_start(rider, r_ins, r_outs, sems)

        body(*ins, *outs, *scr)

        @pl.when(last)
        def _():
            _rider_wait(rider, r_ins, r_outs, sems)

    return pl.pallas_call(
        riding, name=name, grid=grid, in_specs=list(in_specs) + [ANY] * r_in,
        out_specs=list(out_specs) + [ANY] * r_out, out_shape=list(out_shape) + list(rider.out_shape),
        scratch_shapes=list(scratch_shapes) + list(rider.sems), compiler_params=compiler_params,
        input_output_aliases={**(input_output_aliases or {}),
                              **{n_in + i: n_out + o for i, o in rider.aliases.items()}})(
            *operands, *[_in_hbm(a) for a in rider.operands])


def _final_exchange(parts, vpack):
    def body(p0, p1, p2, p3, v_ref, o0, o1, o2, o3, vr_ref, send_sems, recv_sems, vsend_sems, vrecv_sems, local_sem):
        x, y, c, _ = _place()
        me = 4 * x + 2 * y + c
        mine = pltpu.make_async_copy(v_ref, vr_ref.at[me], local_sem)
        mine.start()
        copies = [pltpu.make_async_remote_copy(
            src_ref=p, dst_ref=o, send_sem=send_sems.at[k], recv_sem=recv_sems.at[k],
            device_id=(x, y, 1 - c), device_id_type=MESH)
            for k, (p, o) in enumerate(zip((p0, p1, p2, p3), (o0, o1, o2, o3)))]
        flips = [(fx, fy, fc) for fx in (0, 1) for fy in (0, 1) for fc in (0, 1)][1:]
        recvs = []
        for r, (fx, fy, fc) in enumerate(flips):
            peer = (x ^ fx, y ^ fy, c ^ fc)
            sem = dict(send_sem=vsend_sems.at[r], recv_sem=vrecv_sems.at[r], device_id=peer, device_id_type=MESH)
            copies.append(pltpu.make_async_remote_copy(src_ref=v_ref, dst_ref=vr_ref.at[me], **sem))
            recvs.append(pltpu.make_async_remote_copy(
                src_ref=v_ref, dst_ref=vr_ref.at[4 * peer[0] + 2 * peer[1] + peer[2]], **sem))
        for cp in copies:
            cp.start()
        for cp in copies[:4]:
            cp.wait_recv()
        for cp in recvs:
            cp.wait_recv()
        for cp in copies:
            cp.wait_send()
        mine.wait()

    res = pl.pallas_call(
        body, name="final_exchange",
        out_shape=[jax.ShapeDtypeStruct(p.shape, F32) for p in parts]
        + [jax.ShapeDtypeStruct((N_DEV, VPACK_ROWS, D), F32)],
        in_specs=[ANY] * 5, out_specs=[ANY] * 5,
        scratch_shapes=[pltpu.SemaphoreType.DMA((4,)), pltpu.SemaphoreType.DMA((4,)),
                        pltpu.SemaphoreType.DMA((7,)), pltpu.SemaphoreType.DMA((7,)), pltpu.SemaphoreType.DMA],
    )(*parts, vpack)
    return res[:4], res[4]


def _norm_fwd(x, g1, rider):
    tm = 512

    def body(x_ref, g_ref, h_ref):
        xv = x_ref[...]
        r = lax.rsqrt(jnp.mean(xv * xv, axis=-1, keepdims=True) + EPS)
        h_ref[...] = (xv * r * g_ref[...]).astype(BF16)

    row = pl.BlockSpec((tm, D), lambda i: (i, 0))
    return _call(
        body, x, g1, rider=rider, name="norm_fwd", grid=(T // tm,),
        in_specs=[row, pl.BlockSpec((1, D), lambda i: (0, 0))], out_specs=[row],
        out_shape=[jax.ShapeDtypeStruct((T, D), BF16)],
        compiler_params=_params(("arbitrary",), 32))


def _proj_fwd(h, w_in, qg, kg, bd, rider):
    tm, tn = 512, 640

    def body(h_ref, w_ref, qg_ref, kg_ref, bd_ref, p_ref, qn_ref, kn_ref):
        for j in range(NPROJ // tn):
            cols = slice(j * tn, (j + 1) * tn)
            p_ref[:, cols] = jnp.dot(h_ref[...], w_ref[:, cols], preferred_element_type=F32)
        for col, g, dst in ((2, qg_ref, qn_ref), (3, kg_ref, kn_ref)):
            xv = p_ref[:, col * C:(col + 1) * C]
            ms = _segsum(xv * xv, bd_ref[...]) * (1.0 / HEAD)
            dst[...] = xv * lax.rsqrt(ms + EPS) * g[...]

    vec = pl.BlockSpec((1, C), lambda i: (0, 0))
    blk = pl.BlockSpec((tm, C), lambda i: (i, 0))
    return _call(
        body, h, w_in, qg, kg, bd, rider=rider, name="proj_fwd", grid=(T // tm,),
        in_specs=[pl.BlockSpec((tm, D), lambda i: (i, 0)), _resident((D, NPROJ)), vec, vec, _resident((C, C))],
        out_specs=[pl.BlockSpec((tm, NPROJ), lambda i: (i, 0)), blk, blk],
        out_shape=[jax.ShapeDtypeStruct((T, NPROJ), F32), jax.ShapeDtypeStruct((T, C), F32),
                   jax.ShapeDtypeStruct((T, C), F32)],
        compiler_params=_params(("arbitrary",), 40))


CONV_TM = 512
CONV_HALO = 32
CONV_RB = 32
CONV_CB = 64


LANES = 128


def _sp(start, n):
    return (pl.ds(2 * start, n, stride=2), slice(None))


def _lanes(tile):
    return slice(tile * LANES, (tile + 1) * LANES)


def _conv_fwd(proj, conv_w, conv_b, cn_g, cn_b, rider):
    tm, hl, rb, cb = CONV_TM, CONV_HALO, CONV_RB, CONV_CB
    per = tm // hl

    def body(av_ref, ag_ref, hv_ref, hg_ref, w_ref, b_ref, g_ref, bb_ref, cat_ref, cv_ref, sh_ref):
        i = pl.program_id(0)
        for j in range(C // LANES):
            ln_ = _lanes(j)
            glu_h = hv_ref[:, ln_] * _sigmoid(hg_ref[:, ln_])
            sh_ref.at[j][_sp(0, hl)] = jnp.where(i > 0, glu_h, 0.0)
            for r0 in range(0, tm, cb):
                sh_ref.at[j][_sp(hl + r0, cb)] = av_ref[r0:r0 + cb, ln_] * _sigmoid(ag_ref[r0:r0 + cb, ln_])
            for r0 in range(0, tm, cb):
                acc = jnp.zeros((cb, LANES), F32) + b_ref[:, ln_]
                for k in range(CONV_K):
                    acc = acc + w_ref[k:k + 1, ln_] * sh_ref.at[j][_sp(r0 + hl - (CONV_K - 1) + k, cb)]
                cv_ref[r0:r0 + cb, ln_] = acc
        for r0 in range(0, tm, rb):
            acc = cv_ref[r0:r0 + rb, :]
            mu = jnp.mean(acc, axis=-1, keepdims=True)
            xc = acc - mu
            var = jnp.mean(xc * xc, axis=-1, keepdims=True)
            ln = xc * lax.rsqrt(var + EPS) * g_ref[...] + bb_ref[...]
            cat_ref[r0:r0 + rb, :] = (ln * _sigmoid(ln)).astype(BF16)

    halo = lambda col: pl.BlockSpec((hl, C), lambda i: (jnp.maximum(i * per - 1, 0), col))
    vec = pl.BlockSpec((1, C), lambda i: (0, 0))
    return _call(
        body, proj, proj, proj, proj, conv_w, conv_b, cn_g, cn_b, rider=rider, name="conv_fwd", grid=(T // tm,),
        in_specs=[pl.BlockSpec((tm, C), lambda i: (i, 0)), pl.BlockSpec((tm, C), lambda i: (i, 1)),
                  halo(0), halo(1), pl.BlockSpec((CONV_K, C), lambda i: (0, 0)), vec, vec, vec],
        out_specs=[pl.BlockSpec((tm, C), lambda i: (i, 0)), pl.BlockSpec((tm, C), lambda i: (i, 0))],
        out_shape=[jax.ShapeDtypeStruct((T, D), BF16), jax.ShapeDtypeStruct((T, C), F32)],
        scratch_shapes=[pltpu.VMEM((C // LANES, 2 * (tm + hl), LANES), F32)],
        compiler_params=_params(("arbitrary",), 40))


def _stack_heads(a):
    lane = lax.broadcasted_iota(jnp.int32, a.shape, 1)
    zero = jnp.zeros_like(a)
    return jnp.concatenate([jnp.where(lane < HEAD, a, zero), jnp.where(lane >= HEAD, a, zero)], axis=0)


def _unstack_heads(a2):
    lane = lax.broadcasted_iota(jnp.int32, (QB, 2 * HEAD), 1)
    return jnp.where(lane < HEAD, a2[:QB], a2[QB:])


def _stack_cols(a):
    return jnp.concatenate([a[:, 0:1], a[:, HEAD:HEAD + 1]], axis=0)


ATT_WIN = 2048
V_COL = 4 * C // (2 * HEAD)
DO_COL = C // (2 * HEAD)


def _attn_geometry(d):
    sl = ATT_WIN // d
    return sl, sl // QB, QB * d


SPLIT = 4
PIECE = 128


def _gather_streams(src_ref, tmp_ref, d):
    if d <= SPLIT:
        return lambda r, n: _stream(src_ref, r, n, d)
    q = src_ref.shape[0] // SPLIT
    for a in range(SPLIT):
        for off in range(0, q, PIECE):
            tmp_ref[a * q + off:a * q + off + PIECE, :] = src_ref[pl.ds(a + SPLIT * off, PIECE, stride=SPLIT), :]
    return lambda r, n: tmp_ref[pl.ds((r % SPLIT) * q + r // SPLIT, n, stride=d // SPLIT), :]


def _scatter_streams(dst_ref, tmp_ref, d, value_of, n, before=None):
    if d <= SPLIT:
        for r in range(d):
            pos = (pl.ds(r, n, stride=d) if d > 1 else slice(None), slice(None))
            val = value_of(r)
            dst_ref[pos] = val if before is None else val + before[pos]
        return
    q = dst_ref.shape[0] // SPLIT
    for r in range(d):
        tmp_ref[pl.ds((r % SPLIT) * q + r // SPLIT, n, stride=d // SPLIT), :] = value_of(r)
    for a in range(SPLIT):
        for off in range(0, q, PIECE):
            pos = (pl.ds(a + SPLIT * off, PIECE, stride=SPLIT), slice(None))
            val = tmp_ref[a * q + off:a * q + off + PIECE, :]
            dst_ref[pos] = val if before is None else val + before[pos]


def _stream(ref, r, n, d):
    return ref[pl.ds(r, n, stride=d), :] if d > 1 else ref[pl.ds(r, n), :]


def _alibi_tables(d):
    qi = jnp.arange(QB)[:, None]
    kj = jnp.arange(2 * QB)[None, :]
    delta = qi + QB - kj
    band = (delta >= 0) & (delta <= QB)
    dist = (delta * d).astype(F32)
    heads = jnp.arange(8, dtype=F32)
    slopes = 2.0 ** (-(heads + 1.0))
    t = jnp.where(band[None], -slopes[:, None, None] * dist[None], NEG)
    return t.reshape(4, 2 * QB, 2 * QB)


def _attn_specs(d):
    _, _, hr = _attn_geometry(d)
    per = ATT_WIN // hr
    main = lambda off: pl.BlockSpec((ATT_WIN, 2 * HEAD), lambda cb, n: (n, off + cb))
    prev = lambda off: pl.BlockSpec((hr, 2 * HEAD), lambda cb, n: (jnp.maximum(n * per - 1, 0), off + cb))
    nxt = lambda off: pl.BlockSpec((hr, 2 * HEAD), lambda cb, n: (jnp.minimum((n + 1) * per, T // hr - 1), off + cb))
    bias = pl.BlockSpec((None, 2 * QB, 2 * QB), lambda cb, n: (cb, 0, 0))
    return main, prev, nxt, bias


def _attn_fwd(qn, kn, proj, bias, d, rider=None, merge=None):
    sl, nb, hr = _attn_geometry(d)
    slk = QB + sl
    mrows = 256

    def body(q_ref, k_ref, v_ref, kh_ref, vh_ref, bias_ref, *rest):
        if merge is None:
            o_ref, l_ref, qs, ks, vs, os_, ls, tmp = rest
        else:
            oa_ref, la_ref, ob_ref, lb_ref, _, cat_ref, of_ref, lg_ref, qs, ks, vs, os_, ls, tmp, o_ref, l_ref = rest
        n = pl.program_id(1)
        for dst, per, at, src, take in ((qs, sl, 0, q_ref, sl), (ks, slk, 0, kh_ref, QB), (ks, slk, QB, k_ref, sl),
                                        (vs, slk, 0, vh_ref, QB), (vs, slk, QB, v_ref, sl)):
            stream = _gather_streams(src, tmp, d)
            for r in range(d):
                dst[r * per + at:r * per + at + take, :] = stream(r, take).astype(BF16)
        col = lax.broadcasted_iota(jnp.int32, (2 * QB, 2 * QB), 1)
        for r in range(d):
            for b in range(nb):
                rows = slice(r * sl + b * QB, r * sl + (b + 1) * QB)
                keys = slice(r * slk + b * QB, r * slk + (b + 2) * QB)
                s = _nt(_stack_heads(qs[rows, :]), ks[keys, :]) + bias_ref[...]
                if b == 0:
                    s = jnp.where((col < QB) & (n == 0), NEG, s)
                m = jnp.max(s, axis=-1, keepdims=True)
                p = jnp.exp(s - m)
                den = jnp.sum(p, axis=-1, keepdims=True)
                pv = jnp.dot(p.astype(BF16), vs[keys, :], preferred_element_type=F32)
                os_[rows, :] = _unstack_heads(pv / den)
                ls[rows, :] = _unstack_heads(jnp.broadcast_to(m + jnp.log(den), (2 * QB, 2 * HEAD)))
        _scatter_streams(o_ref, tmp, d, lambda r: os_[r * sl:(r + 1) * sl, :], sl)
        _scatter_streams(l_ref, tmp, d, lambda r: ls[r * sl:(r + 1) * sl, :], sl)
        if merge is not None:
            for r0 in range(0, ATT_WIN, mrows):
                rows = slice(r0, r0 + mrows)
                a, b, c = la_ref[rows, :], lb_ref[rows, :], l_ref[rows, :]
                m = jnp.maximum(jnp.maximum(a, b), c)
                e0, e1, e2 = jnp.exp(a - m), jnp.exp(b - m), jnp.exp(c - m)
                den = e0 + e1 + e2
                o = (e0 * oa_ref[rows, :] + e1 * ob_ref[rows, :] + e2 * o_ref[rows, :]) / den
                of_ref[rows, :] = o
                cat_ref[rows, :] = o.astype(BF16)
                lg_ref[rows, :] = m + jnp.log(den)

    main, prev, _, bias_spec = _attn_specs(d)
    lanes = 2 * HEAD
    operands = [qn, kn, proj, kn, proj, bias]
    in_specs = [main(0), main(0), main(V_COL), prev(0), prev(V_COL), bias_spec]
    scratch = [pltpu.VMEM((ATT_WIN, lanes), BF16), pltpu.VMEM((ATT_WIN + hr, lanes), BF16),
               pltpu.VMEM((ATT_WIN + hr, lanes), BF16), pltpu.VMEM((ATT_WIN, lanes), F32),
               pltpu.VMEM((ATT_WIN, lanes), F32), pltpu.VMEM((ATT_WIN, lanes), F32)]
    if merge is None:
        out_specs = [main(0), main(0)]
        out_shape = [jax.ShapeDtypeStruct((T, C), F32)] * 2
        aliases = None
    else:
        operands += list(merge)
        in_specs += [main(0)] * 4 + [ANY]
        out_specs = [main(C // lanes), main(0), main(0)]
        out_shape = [jax.ShapeDtypeStruct((T, D), BF16), jax.ShapeDtypeStruct((T, C), F32),
                     jax.ShapeDtypeStruct((T, C), F32)]
        scratch += [pltpu.VMEM((ATT_WIN, lanes), F32)] * 2
        aliases = {len(operands) - 1: 0}
    return _call(
        body, *operands, rider=rider, name=f"attn_fwd_d{d}", grid=(C // lanes, T // ATT_WIN),
        in_specs=in_specs, out_specs=out_specs, out_shape=out_shape, scratch_shapes=scratch,
        input_output_aliases=aliases, compiler_params=_params(("arbitrary", "arbitrary"), 48))


def _out_up(x, cat, w_out, g2, w_up, rider):
    tm, tn = 512, NUP // 4

    def body(x_ref, cat_ref, wo_ref, g_ref, wu_ref, x1_ref, h2_ref, up_ref):
        x1 = x_ref[...] + jnp.dot(cat_ref[...], wo_ref[...], preferred_element_type=F32)
        x1_ref[...] = x1
        r = lax.rsqrt(jnp.mean(x1 * x1, axis=-1, keepdims=True) + EPS)
        h2_ref[...] = (x1 * r * g_ref[...]).astype(BF16)
        for j in range(NUP // tn):
            cols = slice(j * tn, (j + 1) * tn)
            up_ref[:, cols] = jnp.dot(h2_ref[...], wu_ref[:, cols], preferred_element_type=F32)

    row = pl.BlockSpec((tm, D), lambda i: (i, 0))
    return _call(
        body, x, cat, w_out, g2, w_up, rider=rider, name="out_up", grid=(T // tm,),
        in_specs=[row, row, _resident((D, D)), pl.BlockSpec((1, D), lambda i: (0, 0)), _resident((D, NUP))],
        out_specs=[row, row, pl.BlockSpec((tm, NUP), lambda i: (i, 0))],
        out_shape=[jax.ShapeDtypeStruct((T, D), F32), jax.ShapeDtypeStruct((T, D), BF16),
                   jax.ShapeDtypeStruct((T, NUP), F32)],
        compiler_params=_params(("arbitrary",), 58))


FF_TM = 256
FF_HALO = 8
FF_RB = 64
FF_TILES = DFF // LANES


def _ff_taps(fw_ref, fb_ref, tile):
    cols = _lanes(tile)
    return [fw_ref[k:k + 1, cols] for k in range(FF_K)] + [fb_ref[:, cols]]


def _ff_conv(ext_ref, taps, tile, r0):
    base = FF_HALO + r0
    acc = taps[3] + taps[0] * ext_ref.at[tile][_sp(base - 2, FF_RB)]
    acc = acc + taps[1] * ext_ref.at[tile][_sp(base - 1, FF_RB)]
    return acc + taps[2] * ext_ref.at[tile][_sp(base, FF_RB)]


def _ffn(up, ffconv_w, ffconv_b, w_down, x1, target):
    tm, hl = FF_TM, FF_HALO
    per = tm // hl
    nt = T // tm
    tiles = 2 * FF_TILES

    def body(up_ref, uh_ref, fw_ref, fb_ref, wd_ref, x1_ref, tg_ref, act_ref, dy_ref, loss_ref, dup_ref, gff_ref,
             ext_ref, gv_ref, dact_ref, carry_ref):
        i = pl.program_id(0)

        @pl.when(i == 0)
        def _():
            gff_ref[...] = jnp.zeros_like(gff_ref)
            loss_ref[...] = jnp.zeros_like(loss_ref)
            carry_ref[...] = jnp.zeros_like(carry_ref)

        for j in range(tiles):
            ext_ref.at[j][_sp(0, hl)] = jnp.where(i < nt - 1, uh_ref[:, _lanes(j)], 0.0)
            for r0 in range(0, tm, FF_RB):
                ext_ref.at[j][_sp(hl + r0, FF_RB)] = up_ref[r0:r0 + FF_RB, _lanes(j)]
        for c in range(FF_TILES):
            gate_taps, val_taps = _ff_taps(fw_ref, fb_ref, c), _ff_taps(fw_ref, fb_ref, FF_TILES + c)
            for r0 in range(0, tm, FF_RB):
                rows = slice(r0, r0 + FF_RB)
                gate = _ff_conv(ext_ref, gate_taps, c, r0)
                val = _ff_conv(ext_ref, val_taps, FF_TILES + c, r0)
                gv_ref[rows, _lanes(c)] = gate
                gv_ref[rows, _lanes(FF_TILES + c)] = val
                act_ref[rows, _lanes(c)] = (gate * _sigmoid(gate) * val).astype(BF16)
        err = x1_ref[...] + jnp.dot(act_ref[...], wd_ref[...], preferred_element_type=F32) - tg_ref[...]
        dy_ref[...] = err * (1.0 / D)
        loss_ref[...] += jnp.sum(err * err)
        dact_ref[...] = _nt(dy_ref[...].astype(BF16), wd_ref[...])

        for c in range(FF_TILES):
            for r0 in range(0, tm, FF_RB):
                rows = slice(r0, r0 + FF_RB)
                gate, val = gv_ref[rows, _lanes(c)], gv_ref[rows, _lanes(FF_TILES + c)]
                sg = _sigmoid(gate)
                da = dact_ref[rows, _lanes(c)]
                ext_ref.at[c][_sp(r0, FF_RB)] = da * val * (sg + gate * sg * (1.0 - sg))
                ext_ref.at[FF_TILES + c][_sp(r0, FF_RB)] = da * gate * sg
        fold = lambda a: jnp.sum(a.reshape(FF_RB // 8, 8, LANES), axis=0)
        for c in range(tiles):
            cols = _lanes(c)
            ext_ref.at[c][_sp(tm, hl)] = carry_ref[c]
            taps = [fw_ref[k:k + 1, cols] for k in range(FF_K)]
            acc = [jnp.zeros((8, LANES), F32) for _ in range(FF_K + 1)]
            for r0 in range(0, tm, FF_RB):
                shifted = [ext_ref.at[c][_sp(r0 + k, FF_RB)] for k in range(FF_K)]
                u = up_ref[r0:r0 + FF_RB, cols]
                dup = taps[2] * shifted[0] + taps[1] * shifted[1] + taps[0] * shifted[2]
                dup_ref[r0:r0 + FF_RB, cols] = dup.astype(BF16)
                for k in range(FF_K):
                    acc[2 - k] = acc[2 - k] + fold(shifted[k] * u)
                acc[FF_K] = acc[FF_K] + fold(shifted[0])
            for k in range(FF_K + 1):
                gff_ref[k:k + 1, cols] += jnp.sum(acc[k], axis=0, keepdims=True)
            carry_ref[c] = ext_ref.at[c][_sp(0, hl)]

    rev = lambda i: (nt - 1 - i, 0)
    row = pl.BlockSpec((tm, D), rev)
    wide = pl.BlockSpec((tm, NUP), rev)
    return _call(
        body, up, up, ffconv_w, ffconv_b, w_down, x1, target, name="ffn", grid=(nt,),
        in_specs=[wide, pl.BlockSpec((hl, NUP), lambda i: (jnp.maximum((nt - 1 - i) * per - 1, 0), 0)),
                  pl.BlockSpec((FF_K, NUP), lambda i: (0, 0)), pl.BlockSpec((1, NUP), lambda i: (0, 0)),
                  _resident((DFF, D)), row, row],
        out_specs=[pl.BlockSpec((tm, DFF), rev), row, pl.BlockSpec((8, 128), lambda i: (0, 0)), wide,
                   pl.BlockSpec((8, NUP), lambda i: (0, 0))],
        out_shape=[jax.ShapeDtypeStruct((T, DFF), BF16), jax.ShapeDtypeStruct((T, D), F32),
                   jax.ShapeDtypeStruct((8, 128), F32), jax.ShapeDtypeStruct((T, NUP), BF16),
                   jax.ShapeDtypeStruct((8, NUP), F32)],
        scratch_shapes=[pltpu.VMEM((tiles, 2 * (tm + hl), LANES), F32), pltpu.VMEM((tm, NUP), F32),
                        pltpu.VMEM((tm, DFF), F32), pltpu.VMEM((tiles, hl, LANES), F32)],
        compiler_params=_params(("arbitrary",), 58))


def _weight_grad(a, g, bm, bn, tk, name):
    m, n = a.shape[1], g.shape[1]
    nk = T // tk

    def body(a_ref, g_ref, of_ref, ob_ref):
        k = pl.program_id(2)

        @pl.when(k == 0)
        def _():
            of_ref[...] = jnp.zeros_like(of_ref)
        of_ref[...] += _tn_dot(a_ref[...].astype(BF16), g_ref[...].astype(BF16))

        @pl.when(k == nk - 1)
        def _():
            ob_ref[...] = of_ref[...].astype(BF16)

    out = pl.BlockSpec((bm, bn), lambda i, j, k: (i, j))
    return _call(
        body, a, g, name=name, grid=(m // bm, n // bn, nk),
        in_specs=[pl.BlockSpec((tk, bm), lambda i, j, k: (k, i)), pl.BlockSpec((tk, bn), lambda i, j, k: (k, j))],
        out_specs=[out, out],
        out_shape=[jax.ShapeDtypeStruct((m, n), F32), jax.ShapeDtypeStruct((m, n), BF16)],
        compiler_params=_params(("parallel", "parallel", "arbitrary"), 56))


def _norm_bwd_mm(dz, w, xin, base, gain, name, rider, also_bf16=False):
    kdim = dz.shape[1]
    tm = 512

    def body(dz_ref, w_ref, x_ref, b_ref, g_ref, dx_ref, gg_ref, *dxb_ref):
        @pl.when(pl.program_id(0) == 0)
        def _():
            gg_ref[...] = jnp.zeros_like(gg_ref)

        xv = x_ref[...]
        dh = _nt(dz_ref[...], w_ref[...])
        r = lax.rsqrt(jnp.mean(xv * xv, axis=-1, keepdims=True) + EPS)
        t = dh * g_ref[...]
        dx_ref[...] = b_ref[...] + r * t - xv * (r * r * r) * jnp.mean(t * xv, axis=-1, keepdims=True)
        gg_ref[...] += jnp.sum(dh * xv * r, axis=0, keepdims=True)
        if also_bf16:
            dxb_ref[0][...] = dx_ref[...].astype(BF16)

    row = pl.BlockSpec((tm, D), lambda i: (i, 0))
    vec = pl.BlockSpec((1, D), lambda i: (0, 0))
    return _call(
        body, dz, w, xin, base, gain, rider=rider, name=name, grid=(T // tm,),
        in_specs=[pl.BlockSpec((tm, kdim), lambda i: (i, 0)), _resident((D, kdim)), row, row, vec],
        out_specs=[row, vec] + [row] * also_bf16,
        out_shape=[jax.ShapeDtypeStruct((T, D), F32), jax.ShapeDtypeStruct((1, D), F32)]
        + [jax.ShapeDtypeStruct((T, D), BF16)] * also_bf16,
        compiler_params=_params(("arbitrary",), 48))


def _outproj_bwd(dx1, w_out, o_f32, bd):
    tm = 512

    def body(d_ref, w_ref, o_ref, bd_ref, dc_ref, dl_ref):
        dc_ref[...] = _nt(d_ref[...].astype(BF16), w_ref[...])
        dl_ref[...] = _segsum(dc_ref[:, C:2 * C] * o_ref[...], bd_ref[...])

    row = pl.BlockSpec((tm, D), lambda i: (i, 0))
    blk = pl.BlockSpec((tm, C), lambda i: (i, 0))
    return _call(
        body, dx1, w_out, o_f32, bd, name="outproj_bwd", grid=(T // tm,),
        in_specs=[row, _resident((D, D)), blk, _resident((C, C))], out_specs=[row, blk],
        out_shape=[jax.ShapeDtypeStruct((T, D), F32), jax.ShapeDtypeStruct((T, C), F32)],
        compiler_params=_params(("arbitrary",), 32))


def _conv_bwd(dcat, cv, proj, conv_w, cn_g, cn_b, rider):
    tm, hl, rb, cb = CONV_TM, CONV_HALO, CONV_RB, CONV_CB
    per = tm // hl
    nt = T // tm
    tiles = C // LANES

    def body(du_ref, dun_ref, cv_ref, cvn_ref, av_ref, ag_ref, hv_ref, hg_ref, w_ref, g_ref, bb_ref,
             dp_ref, gv_ref, gw_ref, dsh_ref, gsh_ref):
        i = pl.program_id(0)

        @pl.when(i == 0)
        def _():
            gv_ref[...] = jnp.zeros_like(gv_ref)
            gw_ref[...] = jnp.zeros_like(gw_ref)

        def ln_bwd(du, cvv):
            mu = jnp.mean(cvv, axis=-1, keepdims=True)
            xc = cvv - mu
            rs = lax.rsqrt(jnp.mean(xc * xc, axis=-1, keepdims=True) + EPS)
            xh = xc * rs
            ln = xh * g_ref[...] + bb_ref[...]
            sg = _sigmoid(ln)
            dln = du * (sg + ln * sg * (1.0 - sg))
            dxh = dln * g_ref[...]
            dcv = rs * (dxh - jnp.mean(dxh, axis=-1, keepdims=True)
                        - xh * jnp.mean(dxh * xh, axis=-1, keepdims=True))
            return dcv, dln, xh

        for r0 in range(0, tm, rb):
            dcv, dln, xh = ln_bwd(du_ref[r0:r0 + rb, :], cv_ref[r0:r0 + rb, :])
            for j in range(tiles):
                dsh_ref.at[j][_sp(r0, rb)] = dcv[:, _lanes(j)]
            gv_ref[0:1, :] += jnp.sum(dln * xh, axis=0, keepdims=True)
            gv_ref[1:2, :] += jnp.sum(dln, axis=0, keepdims=True)
            gv_ref[2:3, :] += jnp.sum(dcv, axis=0, keepdims=True)
        dcv_n, _, _ = ln_bwd(dun_ref[...], cvn_ref[...])
        dcv_n = jnp.where(i < nt - 1, dcv_n, 0.0)
        for j in range(tiles):
            ln_ = _lanes(j)
            dsh_ref.at[j][_sp(tm, hl)] = dcv_n[:, ln_]
            glu_h = hv_ref[:, ln_] * _sigmoid(hg_ref[:, ln_])
            gsh_ref.at[j][_sp(0, hl)] = jnp.where(i > 0, glu_h, 0.0)
            for r0 in range(0, tm, cb):
                gsh_ref.at[j][_sp(hl + r0, cb)] = av_ref[r0:r0 + cb, ln_] * _sigmoid(ag_ref[r0:r0 + cb, ln_])

        for j in range(tiles):
            ln_ = _lanes(j)
            for r0 in range(0, tm, cb):
                dglu = jnp.zeros((cb, LANES), F32)
                for k in range(CONV_K):
                    dglu = dglu + w_ref[k:k + 1, ln_] * dsh_ref.at[j][_sp(r0 + (CONV_K - 1) - k, cb)]
                av = av_ref[r0:r0 + cb, ln_]
                sg = _sigmoid(ag_ref[r0:r0 + cb, ln_])
                dp_ref[r0:r0 + cb, ln_] = (dglu * sg).astype(BF16)
                dp_ref[r0:r0 + cb, _lanes(tiles + j)] = (dglu * av * sg * (1.0 - sg)).astype(BF16)
            for k in range(CONV_K):
                part = jnp.zeros((8, LANES), F32)
                for r0 in range(0, tm, cb):
                    prod = dsh_ref.at[j][_sp(r0, cb)] * gsh_ref.at[j][_sp(r0 + hl - (CONV_K - 1) + k, cb)]
                    part = part + jnp.sum(prod.reshape(cb // 8, 8, LANES), axis=0)
                gw_ref[k:k + 1, ln_] += jnp.sum(part, axis=0, keepdims=True)

    main = lambda col: pl.BlockSpec((tm, C), lambda i: (i, col))
    prev = lambda col: pl.BlockSpec((hl, C), lambda i: (jnp.maximum(i * per - 1, 0), col))
    nxt = pl.BlockSpec((hl, C), lambda i: (jnp.minimum((i + 1) * per, T // hl - 1), 0))
    vec = pl.BlockSpec((1, C), lambda i: (0, 0))
    return _call(
        body, dcat, dcat, cv, cv, proj, proj, proj, proj, conv_w, cn_g, cn_b, rider=rider, name="conv_bwd",
        grid=(nt,),
        in_specs=[main(0), nxt, main(0), nxt, main(0), main(1), prev(0), prev(1),
                  pl.BlockSpec((CONV_K, C), lambda i: (0, 0)), vec, vec],
        out_specs=[pl.BlockSpec((tm, 2 * C), lambda i: (i, 0)), pl.BlockSpec((8, C), lambda i: (0, 0)),
                   pl.BlockSpec((32, C), lambda i: (0, 0))],
        out_shape=[jax.ShapeDtypeStruct((T, NPROJ), BF16), jax.ShapeDtypeStruct((8, C), F32),
                   jax.ShapeDtypeStruct((32, C), F32)],
        scratch_shapes=[pltpu.VMEM((C // LANES, 2 * (tm + hl), LANES), F32)] * 2,
        compiler_params=_params(("arbitrary",), 48))


def _attn_bwd_unit(qs, dos, lgs, dls, rows, kc, vc, biasv, invalid_prev):
    qst, dost = _stack_heads(qs[rows, :]), _stack_heads(dos[rows, :])
    s = _nt(qst, kc) + biasv
    if invalid_prev is not None:
        col = lax.broadcasted_iota(jnp.int32, s.shape, 1)
        s = jnp.where((col < QB) & invalid_prev, NEG, s)
    p = jnp.exp(s - _stack_cols(lgs[rows, :]))
    ds = p * (_nt(dost, vc) - _stack_cols(dls[rows, :]))
    dsb = ds.astype(BF16)
    dq = _unstack_heads(jnp.dot(dsb, kc, preferred_element_type=F32))
    return dq, _tn_dot(dsb, qst), _tn_dot(p.astype(BF16), dost)


def _attn_bwd_lagged(qn, kn, proj, dcat, lg, dl, bias, d, earlier):
    assert QB * d == ATT_WIN
    n_win = T // ATT_WIN
    lanes = 2 * HEAD

    def body(q_ref, k_ref, v_ref, do_ref, lg_ref, dl_ref, kh_ref, vh_ref, bias_ref, eq_ref, ek_ref, ev_ref,
             dq_ref, dk_ref, dv_ref, qs, dos, lgs, dls, ks, vs, dqs, ck, cv, ok, ov, tmp):
        n = pl.program_id(1)
        block = lambda buf: (lambda r: buf[r * QB:(r + 1) * QB, :])

        @pl.when(n == 0)
        def _():
            ck[...] = jnp.zeros_like(ck)
            cv[...] = jnp.zeros_like(cv)

        @pl.when(n < n_win)
        def _():
            for dst, per, at, src, dt in ((qs, QB, 0, q_ref, BF16), (dos, QB, 0, do_ref, BF16),
                                          (lgs, QB, 0, lg_ref, F32), (dls, QB, 0, dl_ref, F32),
                                          (ks, 2 * QB, 0, kh_ref, BF16), (ks, 2 * QB, QB, k_ref, BF16),
                                          (vs, 2 * QB, 0, vh_ref, BF16), (vs, 2 * QB, QB, v_ref, BF16)):
                stream = _gather_streams(src, tmp, d)
                for r in range(d):
                    dst[r * per + at:r * per + at + QB, :] = stream(r, QB).astype(dt)
            for r in range(d):
                rows = slice(r * QB, (r + 1) * QB)
                keys = slice(2 * r * QB, (2 * r + 2) * QB)
                dq, dkc, dvc = _attn_bwd_unit(qs, dos, lgs, dls, rows, ks[keys, :], vs[keys, :], bias_ref[...], n == 0)
                dqs[rows, :] = dq
                ok[rows, :] = ck[rows, :] + dkc[:QB]
                ov[rows, :] = cv[rows, :] + dvc[:QB]
                ck[rows, :] = dkc[QB:]
                cv[rows, :] = dvc[QB:]
            _scatter_streams(dq_ref, tmp, d, block(dqs), QB, eq_ref)
            _scatter_streams(dk_ref, tmp, d, block(ok), QB, ek_ref)
            _scatter_streams(dv_ref, tmp, d, block(ov), QB, ev_ref)

        @pl.when(n == n_win)
        def _():
            _scatter_streams(dk_ref, tmp, d, block(ck), QB, ek_ref)
            _scatter_streams(dv_ref, tmp, d, block(cv), QB, ev_ref)

    cur = lambda off: pl.BlockSpec((ATT_WIN, lanes), lambda cb, n: (jnp.minimum(n, n_win - 1), off + cb))
    prev = lambda off: pl.BlockSpec(
        (ATT_WIN, lanes), lambda cb, n: (jnp.maximum(jnp.minimum(n, n_win - 1) - 1, 0), off + cb))
    late = pl.BlockSpec((ATT_WIN, lanes), lambda cb, n: (jnp.maximum(n - 1, 0), cb))
    buf = lambda rows, dt: pltpu.VMEM((rows, lanes), dt)
    return _call(
        body, qn, kn, proj, dcat, lg, dl, kn, proj, bias, *earlier, name=f"attn_bwd_d{d}",
        grid=(C // lanes, n_win + 1),
        in_specs=[cur(0), cur(0), cur(V_COL), cur(DO_COL), cur(0), cur(0), prev(0), prev(V_COL),
                  pl.BlockSpec((None, 2 * QB, 2 * QB), lambda cb, n: (cb, 0, 0)), cur(0), late, late],
        out_specs=[cur(0), late, late],
        out_shape=[jax.ShapeDtypeStruct((T, C), F32)] * 3,
        scratch_shapes=[buf(ATT_WIN, BF16), buf(ATT_WIN, BF16), buf(ATT_WIN, F32), buf(ATT_WIN, F32),
                        buf(2 * ATT_WIN, BF16), buf(2 * ATT_WIN, BF16)] + [buf(ATT_WIN, F32)] * 6,
        compiler_params=_params(("arbitrary", "arbitrary"), 48))


def _attn_bwd(qn, kn, proj, dcat, lg, dl, bias, d, earlier=None, rider=None):
    sl, nb, hr = _attn_geometry(d)
    slk = QB + sl
    slq = sl + QB
    n_win = T // ATT_WIN

    def body(q_ref, k_ref, v_ref, do_ref, lg_ref, dl_ref, kh_ref, vh_ref, qx_ref, dox_ref, lgx_ref, dlx_ref,
             bias_ref, *rest):
        sums = rest[:3] if earlier is not None else (None, None, None)
        dq_ref, dk_ref, dv_ref, qs, dos, lgs, dls, ks, vs, dqs, dks, dvs = rest[-12:]
        n = pl.program_id(1)
        for r in range(d):
            for dst, src, nx, dt in ((qs, q_ref, qx_ref, BF16), (dos, do_ref, dox_ref, BF16),
                                     (lgs, lg_ref, lgx_ref, F32), (dls, dl_ref, dlx_ref, F32)):
                dst[r * slq:r * slq + sl, :] = _stream(src, r, sl, d).astype(dt)
                dst[r * slq + sl:(r + 1) * slq, :] = _stream(nx, r, QB, d).astype(dt)
            for dst, halo, src in ((ks, kh_ref, k_ref), (vs, vh_ref, v_ref)):
                dst[r * slk:r * slk + QB, :] = _stream(halo, r, QB, d).astype(BF16)
                dst[r * slk + QB:(r + 1) * slk, :] = _stream(src, r, sl, d).astype(BF16)
        dks[...] = jnp.zeros_like(dks)
        dvs[...] = jnp.zeros_like(dvs)

        def unit(rows, kc, vc, biasv, invalid_prev):
            return _attn_bwd_unit(qs, dos, lgs, dls, rows, kc, vc, biasv, invalid_prev)

        for r in range(d):
            for b in range(nb):
                rows = slice(r * slq + b * QB, r * slq + (b + 1) * QB)
                keys = slice(r * slk + b * QB, r * slk + (b + 2) * QB)
                dq, dkc, dvc = unit(rows, ks[keys, :], vs[keys, :], bias_ref[...], (n == 0) if b == 0 else None)
                dqs[r * sl + b * QB:r * sl + (b + 1) * QB, :] = dq
                if b == 0:
                    dks[r * sl:r * sl + QB, :] += dkc[QB:]
                    dvs[r * sl:r * sl + QB, :] += dvc[QB:]
                else:
                    dks[r * sl + (b - 1) * QB:r * sl + (b + 1) * QB, :] += dkc
                    dvs[r * sl + (b - 1) * QB:r * sl + (b + 1) * QB, :] += dvc

        @pl.when(n < n_win - 1)
        def _():
            for r in range(d):
                rows = slice(r * slq + sl, (r + 1) * slq)
                keys = slice(r * slk + sl, (r + 1) * slk)
                _, dkc, dvc = unit(rows, ks[keys, :], vs[keys, :], bias_ref[:, 0:QB], None)
                dks[(r + 1) * sl - QB:(r + 1) * sl, :] += dkc
                dvs[(r + 1) * sl - QB:(r + 1) * sl, :] += dvc

        for dst, src, before in zip((dq_ref, dk_ref, dv_ref), (dqs, dks, dvs), sums):
            for r in range(d):
                pos = (pl.ds(r, sl, stride=d) if d > 1 else slice(None), slice(None))
                val = src[r * sl:(r + 1) * sl, :]
                dst[pos] = val if before is None else val + before[pos]

    main, prev, nxt, bias_spec = _attn_specs(d)
    lanes = 2 * HEAD
    return _call(
        body, qn, kn, proj, dcat, lg, dl, kn, proj, qn, dcat, lg, dl, bias, *(earlier or ()), rider=rider,
        name=f"attn_bwd_d{d}", grid=(C // lanes, n_win),
        in_specs=[main(0), main(0), main(V_COL), main(DO_COL), main(0), main(0), prev(0), prev(V_COL),
                  nxt(0), nxt(DO_COL), nxt(0), nxt(0), bias_spec] + ([main(0)] * 3 if earlier is not None else []),
        out_specs=[main(0)] * 3,
        out_shape=[jax.ShapeDtypeStruct((T, C), F32)] * 3,
        scratch_shapes=[pltpu.VMEM((ATT_WIN + hr, lanes), BF16), pltpu.VMEM((ATT_WIN + hr, lanes), BF16),
                        pltpu.VMEM((ATT_WIN + hr, lanes), F32), pltpu.VMEM((ATT_WIN + hr, lanes), F32),
                        pltpu.VMEM((ATT_WIN + hr, lanes), BF16), pltpu.VMEM((ATT_WIN + hr, lanes), BF16),
                        pltpu.VMEM((ATT_WIN, lanes), F32), pltpu.VMEM((ATT_WIN, lanes), F32),
                        pltpu.VMEM((ATT_WIN, lanes), F32)],
        compiler_params=_params(("arbitrary", "arbitrary"), 48))


def _qk_norm_bwd(dn_sum, proj, col, gain, bd, dproj, name):
    tm = 512

    def body(d0, x_ref, g_ref, bd_ref, dp_in, dp_ref, gg_ref):
        del dp_in

        @pl.when(pl.program_id(0) == 0)
        def _():
            gg_ref[...] = jnp.zeros_like(gg_ref)
        dn = d0[...]
        xv = x_ref[...]
        r = lax.rsqrt(_segsum(xv * xv, bd_ref[...]) * (1.0 / HEAD) + EPS)
        t = dn * g_ref[...]
        mean_tx = _segsum(t * xv, bd_ref[...]) * (1.0 / HEAD)
        dp_ref[...] = (r * t - xv * (r * r * r) * mean_tx).astype(BF16)
        gg_ref[...] += jnp.sum(dn * xv * r, axis=0, keepdims=True)

    blk = pl.BlockSpec((tm, C), lambda i: (i, 0))
    vec = pl.BlockSpec((1, C), lambda i: (0, 0))
    return _call(
        body, dn_sum, proj, gain, bd, dproj, name=name, grid=(T // tm,),
        in_specs=[blk, pl.BlockSpec((tm, C), lambda i: (i, col)), vec, _resident((C, C)), ANY],
        out_specs=[pl.BlockSpec((tm, C), lambda i: (i, col)), vec],
        out_shape=[jax.ShapeDtypeStruct((T, NPROJ), BF16), jax.ShapeDtypeStruct((1, C), F32)],
        input_output_aliases={4: 0},
        compiler_params=_params(("arbitrary",), 32))


def _v_bwd(dv_sum, dproj):
    tm = 512

    def body(d0, dp_in, dp_ref):
        del dp_in
        dp_ref[...] = d0[...].astype(BF16)

    blk = pl.BlockSpec((tm, C), lambda i: (i, 0))
    return _call(
        body, dv_sum, dproj, name="v_bwd", grid=(T // tm,),
        in_specs=[blk, ANY],
        out_specs=[pl.BlockSpec((tm, C), lambda i: (i, 4))],
        out_shape=[jax.ShapeDtypeStruct((T, NPROJ), BF16)],
        input_output_aliases={1: 0},
        compiler_params=_params(("parallel",), 32))[0]


def _adamw(w, g, m, v):
    m = ADAM_B1 * m + (1.0 - ADAM_B1) * g
    v = ADAM_B2 * v + (1.0 - ADAM_B2) * (g * g)
    m_hat = m / (1.0 - ADAM_B1 ** ADAM_STEP)
    v_hat = v / (1.0 - ADAM_B2 ** ADAM_STEP)
    delta = -ADAM_LR * (m_hat / (jnp.sqrt(v_hat) + ADAM_EPS) + ADAM_WD * w)
    return delta, m, v


def _row_block(shape):
    rows = shape[0]
    for cand in (256, 128, 64, 88, 32, 8):
        if rows % cand == 0 and cand * shape[1] * 4 <= (2 << 20):
            return cand
    return 8


def _partial_sum(own, recv, name):
    br = _row_block(own.shape)
    cols = own.shape[1]

    def body(o_ref, r_ref, p_ref):
        p_ref[...] = ((o_ref[...] + r_ref[0].astype(F32)) + r_ref[1].astype(F32)) + r_ref[2].astype(F32)

    blk = pl.BlockSpec((br, cols), lambda i: (i, 0))
    return _call(
        body, own, recv, name=name, grid=(own.shape[0] // br,),
        in_specs=[blk, pl.BlockSpec((3, br, cols), lambda i: (0, i, 0))], out_specs=[blk],
        out_shape=[jax.ShapeDtypeStruct(own.shape, F32)],
        compiler_params=_params(("parallel",), 32))[0]


def _adamw_mat(p_own, p_sib, w, m, v, name):
    br = _row_block(w.shape)
    cols = w.shape[1]

    def body(a_ref, b_ref, w_ref, m_ref, v_ref, g_ref, d_ref, nm_ref, nv_ref):
        g = a_ref[...] + b_ref[...]
        delta, nm, nv = _adamw(w_ref[...], g, m_ref[...], v_ref[...])
        g_ref[...] = g
        d_ref[...] = delta
        nm_ref[...] = nm
        nv_ref[...] = nv

    blk = pl.BlockSpec((br, cols), lambda i: (i, 0))
    return _call(
        body, p_own, p_sib, w, m, v, name=name, grid=(w.shape[0] // br,),
        in_specs=[blk] * 5, out_specs=[blk] * 4,
        out_shape=[jax.ShapeDtypeStruct(w.shape, F32)] * 4,
        compiler_params=_params(("parallel",), 40))


def _vec_reduce(vrecv):
    def body(v_ref, o_ref):
        acc = v_ref[0]
        for r in range(1, N_DEV):
            acc = acc + v_ref[r]
        o_ref[...] = acc

    return pl.pallas_call(
        body, name="vec_reduce",
        out_shape=jax.ShapeDtypeStruct((VPACK_ROWS, D), F32),
        compiler_params=_params((), 32),
    )(vrecv)


def _adamw_small(w, g, m, v):
    def body(w_ref, g_ref, m_ref, v_ref, d_ref, nm_ref, nv_ref):
        delta, nm, nv = _adamw(w_ref[...], g_ref[...], m_ref[...], v_ref[...])
        d_ref[...] = delta
        nm_ref[...] = nm
        nv_ref[...] = nv

    return pl.pallas_call(
        body, name="adamw_small",
        out_shape=[jax.ShapeDtypeStruct(w.shape, F32)] * 3,
        compiler_params=_params((), 32),
    )(w, g, m, v)


def _pack(parts, rows):
    flat = jnp.concatenate([p.reshape(-1) for p in parts])
    return jnp.pad(flat, (0, rows * D - flat.shape[0])).reshape(rows, D)


def _unpack(packed, shapes):
    flat = packed.reshape(-1)
    out, off = [], 0
    for shp in shapes:
        size = 1
        for s in shp:
            size *= s
        out.append(flat[off:off + size].reshape(shp))
        off += size
    return out


def _no_comm(shards, row_sharded, peers=(0, 1, 2), into=None):
    del row_sharded, peers, into
    return None, lambda res, n: (res, shards)


def _with_comm(shards, row_sharded, peers=(0, 1, 2), into=None):
    rider = _gather_rider(shards, row_sharded, peers, into)
    return rider, lambda res, n: (res[:n], res[n:])


def _local_step(x, target, norm1_g, conv_b, cn_g, cn_b, q_norm_g, k_norm_g, norm2_g, ffconv_b,
                first_weights, late_weights, comm=True):
    row = lambda a: a.reshape(1, -1)
    head_of = jnp.arange(C) // HEAD
    bd = (head_of[:, None] == head_of[None, :]).astype(BF16)
    qg = row(jnp.tile(q_norm_g, C // HEAD) * (HEAD ** -0.5))
    kg = row(jnp.tile(k_norm_g, C // HEAD))
    biases = [_alibi_tables(d) for d in PATTERN_DILATIONS]
    gather = _with_comm if comm else _no_comm
    grad_rider = (lambda g, rs: _grad_rider(g[1], g[0], rs)) if comm else (lambda g, rs: None)

    rider, split = gather(first_weights, (False, False, False))
    if comm:
        rider = _halved_gather_rider(first_weights, (True, False, False))
    (h,), (w_in, conv_w, ffconv_w) = split(_norm_fwd(x, row(norm1_g), rider), 1)
    rider, split = gather(late_weights[0:1], (True,))
    (proj, qn, kn), (w_out,) = split(_proj_fwd(h, w_in, qg, kg, bd, rider), 3)
    rider, split = gather(late_weights[1:2], (False,), (0, 1))
    (cat, cv), w_up_part = split(_conv_fwd(proj, conv_w, row(conv_b), row(cn_g), row(cn_b), rider), 2)
    fwd = [_attn_fwd(qn, kn, proj, biases[i], d) for i, d in enumerate(PATTERN_DILATIONS[:-1])]
    rider, split = gather(late_weights[1:2], (False,), (2,), w_up_part)
    merge = (fwd[0][0], fwd[0][1], fwd[1][0], fwd[1][1], cat)
    (cat, o_f32, lg), (w_up,) = split(
        _attn_fwd(qn, kn, proj, biases[-1], PATTERN_DILATIONS[-1], rider, merge), 3)
    rider, split = gather(late_weights[2:3], (True,))
    (x1, h2, up), (w_down,) = split(_out_up(x, cat, w_out, row(norm2_g), w_up, rider), 3)
    act, dy, loss_acc, dup, gff = _ffn(up, ffconv_w, row(ffconv_b), w_down, x1, target)
    gw_down = _weight_grad(act, dy, DFF // 2, D, 1024, "grad_w_down")
    res = _norm_bwd_mm(dup, w_up, x1, dy, row(norm2_g), "up_bwd", grad_rider(gw_down, True), also_bf16=True)
    (dx1, g_norm2, dx1_bf), ex_down = res[:3], res[3:]
    gw_up = _weight_grad(h2, dup, D, NUP // 4, 2048, "grad_w_up")
    dcat, dl = _outproj_bwd(dx1_bf, w_out, o_f32, bd)
    gw_out = _weight_grad(cat, dx1_bf, D, D, 2048, "grad_w_out")
    res = _conv_bwd(dcat, cv, proj, conv_w, row(cn_g), row(cn_b), grad_rider(gw_up, False))
    (dproj, gconv_vec, gconv_w), ex_up = res[:3], res[3:]
    sums, ex_out = None, []
    for i, d in enumerate(PATTERN_DILATIONS):
        if QB * d == ATT_WIN:
            res = _attn_bwd_lagged(qn, kn, proj, dcat, lg, dl, biases[i], d, sums)
        else:
            res = _attn_bwd(qn, kn, proj, dcat, lg, dl, biases[i], d, sums,
                            grad_rider(gw_out, True) if i == 0 else None)
        sums = res[:3]
        ex_out = res[3:] if i == 0 else ex_out
    dproj, gq_lane = _qk_norm_bwd(sums[0], proj, 2, qg, bd, dproj, "q_norm_bwd")
    dproj, gk_lane = _qk_norm_bwd(sums[1], proj, 3, kg, bd, dproj, "k_norm_bwd")
    dproj = _v_bwd(sums[2], dproj)
    gw_in = _weight_grad(h, dproj, D, NPROJ // 4, 2048, "grad_w_in")
    res = _norm_bwd_mm(dproj, w_in, x, dx1, row(norm1_g), "in_bwd", grad_rider(gw_in, False))
    (dx, g_norm1), ex_in = res[:2], res[2:]

    loss = loss_acc[0, 0] * (0.5 / D)
    g_qg = jnp.sum(gq_lane.reshape(C // HEAD, HEAD), axis=0) * (HEAD ** -0.5)
    g_kg = jnp.sum(gk_lane.reshape(C // HEAD, HEAD), axis=0)
    small = [g_norm1[0], gconv_vec[2], gconv_vec[0], gconv_vec[1], g_qg, g_kg, g_norm2[0], gff[3],
             gconv_w[:CONV_K], gff[:FF_K]]
    mats = [ex_in, ex_out, ex_up, ex_down] if comm else [gw_in, gw_out, gw_up, gw_down]
    return loss, dx, mats, small


def kernel(x, norm1_g, w_in, conv_w, conv_b, cn_g, cn_b, q_norm_g, k_norm_g, w_out, norm2_g, w_up, ffconv_w, ffconv_b, w_down, loss_target, m_norm1_g, m_w_in, m_conv_w, m_conv_b, m_cn_g, m_cn_b, m_q_norm_g, m_k_norm_g, m_w_out, m_norm2_g, m_w_up, m_ffconv_w, m_ffconv_b, m_w_down, v_norm1_g, v_w_in, v_conv_w, v_conv_b, v_cn_g, v_cn_b, v_q_norm_g, v_k_norm_g, v_w_out, v_norm2_g, v_w_up, v_ffconv_w, v_ffconv_b, v_w_down):
    chip = 2 * lax.axis_index("x") + lax.axis_index("y")

    loss, dx, mats, small = _local_step(
        x[0], loss_target[0], norm1_g, conv_b, cn_g, cn_b, q_norm_g, k_norm_g, norm2_g, ffconv_b,
        [w_in.astype(BF16), conv_w, ffconv_w], [w.astype(BF16) for w in (w_out, w_up, w_down)])

    names = ("w_in", "w_out", "w_up", "w_down")
    parts = [_partial_sum(own, recv, "partial_" + names[k]) for k, (recv, own) in enumerate(mats)]
    sib, vrecv = _final_exchange(parts, _pack(small + [loss.reshape(1)], VPACK_ROWS))
    ws = (w_in, w_out, w_up, w_down)
    ms = (m_w_in, m_w_out, m_w_up, m_w_down)
    vs = (v_w_in, v_w_out, v_w_up, v_w_down)
    mat = [_adamw_mat(parts[k], sib[k], ws[k], ms[k], vs[k], "adamw_" + names[k]) for k in range(4)]

    vsum = _vec_reduce(vrecv)
    vec_shapes = [(D,), (C,), (C,), (C,), (HEAD,), (HEAD,), (D,), (NUP,), (CONV_K, C), (FF_K, NUP), (1,)]
    gsmall = _unpack(vsum, vec_shapes)
    g_conv_w = lax.dynamic_slice_in_dim(gsmall[8], chip * (C // N_CHIPS), C // N_CHIPS, axis=1)
    g_ffconv_w = lax.dynamic_slice_in_dim(gsmall[9], chip * (NUP // N_CHIPS), NUP // N_CHIPS, axis=1)
    gs = gsmall[:8] + [g_conv_w, g_ffconv_w]
    w_s = [norm1_g, conv_b, cn_g, cn_b, q_norm_g, k_norm_g, norm2_g, ffconv_b, conv_w, ffconv_w]
    m_s = [m_norm1_g, m_conv_b, m_cn_g, m_cn_b, m_q_norm_g, m_k_norm_g, m_norm2_g, m_ffconv_b, m_conv_w, m_ffconv_w]
    v_s = [v_norm1_g, v_conv_b, v_cn_g, v_cn_b, v_q_norm_g, v_k_norm_g, v_norm2_g, v_ffconv_b, v_conv_w, v_ffconv_w]
    shapes_s = [a.shape for a in w_s]
    d_p, m_p, v_p = _adamw_small(_pack(w_s, SPACK_ROWS), _pack(gs, SPACK_ROWS), _pack(m_s, SPACK_ROWS),
                                 _pack(v_s, SPACK_ROWS))
    d_s, nm_s, nv_s = _unpack(d_p, shapes_s), _unpack(m_p, shapes_s), _unpack(v_p, shapes_s)

    def ordered(sm, mt):
        return [sm[0], mt[0], sm[8], sm[1], sm[2], sm[3], sm[4], sm[5], mt[1], sm[6], mt[2], sm[9], sm[7], mt[3]]

    loss_all = gsmall[10][0]
    grads = ordered(gs, [r[0] for r in mat])
    deltas = ordered(d_s, [r[1] for r in mat])
    new_m = ordered(nm_s, [r[2] for r in mat])
    new_v = ordered(nv_s, [r[3] for r in mat])
    return (loss_all, dx[None], *grads, *deltas, *new_m, *new_v)
```
